```python
import math
import jax, jax.numpy as jnp
from jax import lax
import numpy as np

D_MODEL = 1024
BATCH = 8
SEQ = 8192
DEPTH = 4

N_META = 16
MLA_HEADS = 8
Q_LORA = 384
KV_LORA = 256
QK_NOPE = 64
QK_ROPE = 32
V_HEAD = 64
ROPE_THETA = 10000.0
Q_BLOCK = 128
HG_HEADS = 4
HG_KDIM = 128
HG_VDIM = 128
HG_CHUNK = 64
D_FF = 2816
EPS = 1e-6
NEG_BIG = -1e30
F_MIN = 1e-20

MLA_WIDTH = MLA_HEADS * V_HEAD
HG_FWIDTH = HG_HEADS * HG_KDIM
HG_WIDTH = HG_HEADS * HG_VDIM
IN_SPLITS = (Q_LORA, KV_LORA, QK_ROPE, HG_FWIDTH, HG_FWIDTH, HG_WIDTH, HG_WIDTH, D_MODEL, D_MODEL)
D_IN = sum(IN_SPLITS)

kernel_name = "hybrid_mla_hgrn2_macaron_meta"


def rms_norm(x, w):
    xf = x.astype(jnp.float32)
    y = xf * lax.rsqrt(jnp.mean(xf * xf, axis=-1, keepdims=True) + EPS)
    return (y * w.astype(jnp.float32)).astype(x.dtype)


def split_cols(z, sizes):
    outs, start = [], 0
    for s in sizes:
        outs.append(z[..., start:start + s])
        start += s
    return outs


def swiglu(x, w_gu, w_down):
    gate, up = jnp.split(x @ w_gu, 2, axis=-1)
    return (jax.nn.silu(gate) * up) @ w_down


def rope(x, pos):
    half = x.shape[-1] // 2
    inv = ROPE_THETA ** (-jnp.arange(half, dtype=jnp.float32) / half)
    ang = pos.astype(jnp.float32)[:, None] * inv[None, :]
    cos = jnp.cos(ang)[:, None, :]
    sin = jnp.sin(ang)[:, None, :]
    x1 = x[..., :half].astype(jnp.float32)
    x2 = x[..., half:].astype(jnp.float32)
    return jnp.concatenate([x1 * cos - x2 * sin, x2 * cos + x1 * sin], axis=-1).astype(x.dtype)


def mla(c_q, c_kv, k_pe, pos, q_norm_w, kv_norm_w, w_uq, w_ukv):
    B, L, _ = c_q.shape
    H, DQK = MLA_HEADS, QK_NOPE + QK_ROPE
    q = (rms_norm(c_q, q_norm_w) @ w_uq).reshape(B, L, H, DQK)
    q = jnp.concatenate([q[..., :QK_NOPE], rope(q[..., QK_NOPE:], pos)], axis=-1)
    kv = (rms_norm(c_kv, kv_norm_w) @ w_ukv).reshape(B, L, H, QK_NOPE + V_HEAD)
    v = kv[..., QK_NOPE:]
    k_rot = rope(k_pe[:, :, None, :], pos)
    k = jnp.concatenate([kv[..., :QK_NOPE], jnp.broadcast_to(k_rot, (B, L, H, QK_ROPE))], axis=-1)
    scale = DQK ** -0.5
    n_blocks = -(-L // Q_BLOCK)
    pad = n_blocks * Q_BLOCK - L
    qb = jnp.pad(q, ((0, 0), (0, pad), (0, 0), (0, 0)))
    qb = qb.reshape(B, n_blocks, Q_BLOCK, H, DQK).transpose(1, 0, 2, 3, 4)
    k_pos = jnp.arange(L)

    def block(args):
        qi, blk = args
        s = jnp.einsum('bqhd,bkhd->bhqk', qi, k).astype(jnp.float32) * scale
        q_pos = blk * Q_BLOCK + jnp.arange(Q_BLOCK)
        s = jnp.where(k_pos[None, :] <= q_pos[:, None], s, NEG_BIG)
        p = jax.nn.softmax(s, axis=-1).astype(v.dtype)
        return jnp.einsum('bhqk,bkhd->bqhd', p, v)

    o = lax.map(block, (qb, jnp.arange(n_blocks)))
    o = o.transpose(1, 0, 2, 3, 4).reshape(B, n_blocks * Q_BLOCK, H * V_HEAD)
    return o[:, :L]


def hgrn2(q_in, f_in, i_in, g_in, lb, norm_w):
    B, L, _ = q_in.shape
    dt = q_in.dtype
    f32 = jnp.float32
    q = jax.nn.silu(q_in.astype(f32)).reshape(B, L, HG_HEADS, HG_KDIM)
    z = f_in.astype(f32).reshape(B, L, HG_HEADS, HG_KDIM)
    lbf = lb.astype(f32).reshape(HG_HEADS, HG_KDIM)
    f = lbf + (1.0 - lbf) * jax.nn.sigmoid(z)
    log_f = jnp.log(jnp.maximum(f, F_MIN))
    k = (1.0 - lbf) * jax.nn.sigmoid(-z)
    v = i_in.astype(f32).reshape(B, L, HG_HEADS, HG_VDIM)
    front = (-N_META) % HG_CHUNK

    def to_chunks(t):
        t = jnp.pad(t, ((0, 0), (front, 0), (0, 0), (0, 0)))
        n = t.shape[1] // HG_CHUNK
        return t.reshape(B, n, HG_CHUNK, HG_HEADS, t.shape[-1]).transpose(1, 0, 3, 2, 4)

    qc, kc, vc, gc = to_chunks(q), to_chunks(k), to_chunks(v), to_chunks(log_f)
    scale = HG_KDIM ** -0.5
    causal = jnp.tril(jnp.ones((HG_CHUNK, HG_CHUNK), dtype=bool))

    def step(S, inp):
        qt, kt, vt, gt = inp
        b = jnp.cumsum(gt, axis=2)
        diff = b[:, :, :, None, :] - b[:, :, None, :, :]
        decay = jnp.exp(jnp.where(causal[:, :, None], diff, NEG_BIG))
        attn = jnp.einsum('bhtk,bhtsk,bhsk->bhts', qt, decay, kt) * scale
        o = jnp.einsum('bhts,bhsv->bhtv', attn, vt) + \
            jnp.einsum('bhtk,bhkv->bhtv', qt * jnp.exp(b) * scale, S)
        b_last = b[:, :, -1:, :]
        S = jnp.exp(b_last[:, :, 0, :])[..., None] * S + \
            jnp.einsum('bhsk,bhsv->bhkv', kt * jnp.exp(b_last - b), vt)
        return S, o

    S0 = jnp.zeros((B, HG_HEADS, HG_KDIM, HG_VDIM), f32)
    _, o = lax.scan(step, S0, (qc, kc, vc, gc))
    n = o.shape[0]
    o = o.transpose(1, 0, 3, 2, 4).reshape(B, n * HG_CHUNK, HG_HEADS, HG_VDIM)[:, front:]
    o = o * lax.rsqrt(jnp.mean(o * o, axis=-1, keepdims=True) + EPS) * norm_w.astype(f32)
    gate = jax.nn.silu(g_in.astype(f32)).reshape(B, L, HG_HEADS, HG_VDIM)
    return (o * gate).reshape(B, L, HG_WIDTH).astype(dt)


def _fwd_setup_inputs(seed: int = 0) -> dict:
    key = jax.random.key(seed)
    ks = jax.random.split(key, 24)
    f32 = jnp.float32

    def nrm(k, shape, fan_in):
        return jax.random.normal(k, shape, f32) * (fan_in ** -0.5)

    def gain(k, shape):
        return 1.0 + 0.02 * jax.random.normal(k, shape, f32)

    return {
        "x": jax.random.normal(ks[0], (BATCH, SEQ, D_MODEL), f32),
        "meta_tokens": jax.random.normal(ks[1], (N_META, D_MODEL), f32),
        "ffn1_norm": gain(ks[2], (DEPTH, D_MODEL)),
        "ffn1_w_gu": nrm(ks[3], (DEPTH, D_MODEL, 2 * D_FF), D_MODEL),
        "ffn1_w_down": nrm(ks[4], (DEPTH, D_FF, D_MODEL), D_FF),
        "mix_norm": gain(ks[5], (DEPTH, D_MODEL)),
        "w_in": nrm(ks[6], (DEPTH, D_MODEL, D_IN), D_MODEL),
        "q_norm": gain(ks[7], (DEPTH, Q_LORA)),
        "kv_norm": gain(ks[8], (DEPTH, KV_LORA)),
        "w_uq": nrm(ks[9], (DEPTH, Q_LORA, MLA_HEADS * (QK_NOPE + QK_ROPE)), Q_LORA),
        "w_ukv": nrm(ks[10], (DEPTH, KV_LORA, MLA_HEADS * (QK_NOPE + V_HEAD)), KV_LORA),
        "hg_lb_raw": 0.5 * jax.random.normal(ks[11], (DEPTH, HG_FWIDTH), f32),
        "hg_norm": gain(ks[12], (DEPTH, HG_VDIM)),
        "w_proj_attn": nrm(ks[13], (DEPTH, MLA_WIDTH, D_MODEL), MLA_WIDTH),
        "w_proj_rec": nrm(ks[14], (DEPTH, HG_WIDTH, D_MODEL), HG_WIDTH),
        "w_out": nrm(ks[15], (DEPTH, D_MODEL, D_MODEL), D_MODEL),
        "ffn2_norm": gain(ks[16], (DEPTH, D_MODEL)),
        "ffn2_w_gu": nrm(ks[17], (DEPTH, D_MODEL, 2 * D_FF), D_MODEL),
        "ffn2_w_down": nrm(ks[18], (DEPTH, D_FF, D_MODEL), D_FF),
        "final_norm": gain(ks[19], (D_MODEL,)),
    }


def _fwd_reference(x, meta_tokens, ffn1_norm, ffn1_w_gu, ffn1_w_down, mix_norm, w_in, q_norm, kv_norm,
              w_uq, w_ukv, hg_lb_raw, hg_norm, w_proj_attn, w_proj_rec, w_out,
              ffn2_norm, ffn2_w_gu, ffn2_w_down, final_norm):
    B = x.shape[0]
    meta = jnp.broadcast_to(meta_tokens.astype(x.dtype)[None], (B, N_META, D_MODEL))
    h = jnp.concatenate([meta, x], axis=1)
    L = h.shape[1]
    pos = jnp.arange(L)
    p_lb = jax.nn.softmax(hg_lb_raw.astype(jnp.float32), axis=0)
    lbs = jnp.cumsum(p_lb, axis=0) - p_lb[0:1]
    for l in range(DEPTH):
        h = h + 0.5 * swiglu(rms_norm(h, ffn1_norm[l]), ffn1_w_gu[l], ffn1_w_down[l])
        u = rms_norm(h, mix_norm[l])
        c_q, c_kv, k_pe, hq, hf, hi, hg, ga, gb = split_cols(u @ w_in[l], IN_SPLITS)
        y_a = mla(c_q, c_kv, k_pe, pos, q_norm[l], kv_norm[l], w_uq[l], w_ukv[l]) @ w_proj_attn[l]
        y_b = hgrn2(hq, hf, hi, hg, lbs[l], hg_norm[l]) @ w_proj_rec[l]
        merged = jax.nn.sigmoid(ga) * y_a + jax.nn.sigmoid(gb) * y_b
        h = h + merged @ w_out[l]
        h = h + 0.5 * swiglu(rms_norm(h, ffn2_norm[l]), ffn2_w_gu[l], ffn2_w_down[l])
    return rms_norm(h, final_norm)[:, N_META:]


import jax as _jax
import jax.numpy as _jnp

TWIN_FORMAT = 'train_step'
FWD_PARAMS = ['x', 'meta_tokens', 'ffn1_norm', 'ffn1_w_gu', 'ffn1_w_down', 'mix_norm', 'w_in', 'q_norm', 'kv_norm', 'w_uq', 'w_ukv', 'hg_lb_raw', 'hg_norm', 'w_proj_attn', 'w_proj_rec', 'w_out', 'ffn2_norm', 'ffn2_w_gu', 'ffn2_w_down', 'final_norm']
TWIN_WEIGHTS = ['meta_tokens', 'ffn1_norm', 'ffn1_w_gu', 'ffn1_w_down', 'mix_norm', 'w_in', 'q_norm', 'kv_norm', 'w_uq', 'w_ukv', 'hg_lb_raw', 'hg_norm', 'w_proj_attn', 'w_proj_rec', 'w_out', 'ffn2_norm', 'ffn2_w_gu', 'ffn2_w_down', 'final_norm']
TWIN_DIFF_INPUT = 'x'
TWIN_INPUTS = ['x', 'meta_tokens', 'ffn1_norm', 'ffn1_w_gu', 'ffn1_w_down', 'mix_norm', 'w_in', 'q_norm', 'kv_norm', 'w_uq', 'w_ukv', 'hg_lb_raw', 'hg_norm', 'w_proj_attn', 'w_proj_rec', 'w_out', 'ffn2_norm', 'ffn2_w_gu', 'ffn2_w_down', 'final_norm', 'loss_target', 'm_meta_tokens', 'm_ffn1_norm', 'm_ffn1_w_gu', 'm_ffn1_w_down', 'm_mix_norm', 'm_w_in', 'm_q_norm', 'm_kv_norm', 'm_w_uq', 'm_w_ukv', 'm_hg_lb_raw', 'm_hg_norm', 'm_w_proj_attn', 'm_w_proj_rec', 'm_w_out', 'm_ffn2_norm', 'm_ffn2_w_gu', 'm_ffn2_w_down', 'm_final_norm', 'v_meta_tokens', 'v_ffn1_norm', 'v_ffn1_w_gu', 'v_ffn1_w_down', 'v_mix_norm', 'v_w_in', 'v_q_norm', 'v_kv_norm', 'v_w_uq', 'v_w_ukv', 'v_hg_lb_raw', 'v_hg_norm', 'v_w_proj_attn', 'v_w_proj_rec', 'v_w_out', 'v_ffn2_norm', 'v_ffn2_w_gu', 'v_ffn2_w_down', 'v_final_norm']
TWIN_OUTPUTS = ['loss', 'grad_x', 'grad_meta_tokens', 'grad_ffn1_norm', 'grad_ffn1_w_gu', 'grad_ffn1_w_down', 'grad_mix_norm', 'grad_w_in', 'grad_q_norm', 'grad_kv_norm', 'grad_w_uq', 'grad_w_ukv', 'grad_hg_lb_raw', 'grad_hg_norm', 'grad_w_proj_attn', 'grad_w_proj_rec', 'grad_w_out', 'grad_ffn2_norm', 'grad_ffn2_w_gu', 'grad_ffn2_w_down', 'grad_final_norm', 'delta_meta_tokens', 'delta_ffn1_norm', 'delta_ffn1_w_gu', 'delta_ffn1_w_down', 'delta_mix_norm', 'delta_w_in', 'delta_q_norm', 'delta_kv_norm', 'delta_w_uq', 'delta_w_ukv', 'delta_hg_lb_raw', 'delta_hg_norm', 'delta_w_proj_attn', 'delta_w_proj_rec', 'delta_w_out', 'delta_ffn2_norm', 'delta_ffn2_w_gu', 'delta_ffn2_w_down', 'delta_final_norm', 'new_m_meta_tokens', 'new_m_ffn1_norm', 'new_m_ffn1_w_gu', 'new_m_ffn1_w_down', 'new_m_mix_norm', 'new_m_w_in', 'new_m_q_norm', 'new_m_kv_norm', 'new_m_w_uq', 'new_m_w_ukv', 'new_m_hg_lb_raw', 'new_m_hg_norm', 'new_m_w_proj_attn', 'new_m_w_proj_rec', 'new_m_w_out', 'new_m_ffn2_norm', 'new_m_ffn2_w_gu', 'new_m_ffn2_w_down', 'new_m_final_norm', 'new_v_meta_tokens', 'new_v_ffn1_norm', 'new_v_ffn1_w_gu', 'new_v_ffn1_w_down', 'new_v_mix_norm', 'new_v_w_in', 'new_v_q_norm', 'new_v_kv_norm', 'new_v_w_uq', 'new_v_w_ukv', 'new_v_hg_lb_raw', 'new_v_hg_norm', 'new_v_w_proj_attn', 'new_v_w_proj_rec', 'new_v_w_out', 'new_v_ffn2_norm', 'new_v_ffn2_w_gu', 'new_v_ffn2_w_down', 'new_v_final_norm']
TWIN_LEAF_KINDS = {'loss': 'loss', 'grad_x': 'grad_x', 'grad_meta_tokens': 'grad_w', 'grad_ffn1_norm': 'grad_w', 'grad_ffn1_w_gu': 'grad_w', 'grad_ffn1_w_down': 'grad_w', 'grad_mix_norm': 'grad_w', 'grad_w_in': 'grad_w', 'grad_q_norm': 'grad_w', 'grad_kv_norm': 'grad_w', 'grad_w_uq': 'grad_w', 'grad_w_ukv': 'grad_w', 'grad_hg_lb_raw': 'grad_w', 'grad_hg_norm': 'grad_w', 'grad_w_proj_attn': 'grad_w', 'grad_w_proj_rec': 'grad_w', 'grad_w_out': 'grad_w', 'grad_ffn2_norm': 'grad_w', 'grad_ffn2_w_gu': 'grad_w', 'grad_ffn2_w_down': 'grad_w', 'grad_final_norm': 'grad_w', 'delta_meta_tokens': 'delta_w', 'delta_ffn1_norm': 'delta_w', 'delta_ffn1_w_gu': 'delta_w', 'delta_ffn1_w_down': 'delta_w', 'delta_mix_norm': 'delta_w', 'delta_w_in': 'delta_w', 'delta_q_norm': 'delta_w', 'delta_kv_norm': 'delta_w', 'delta_w_uq': 'delta_w', 'delta_w_ukv': 'delta_w', 'delta_hg_lb_raw': 'delta_w', 'delta_hg_norm': 'delta_w', 'delta_w_proj_attn': 'delta_w', 'delta_w_proj_rec': 'delta_w', 'delta_w_out': 'delta_w', 'delta_ffn2_norm': 'delta_w', 'delta_ffn2_w_gu': 'delta_w', 'delta_ffn2_w_down': 'delta_w', 'delta_final_norm': 'delta_w', 'new_m_meta_tokens': 'new_m', 'new_m_ffn1_norm': 'new_m', 'new_m_ffn1_w_gu': 'new_m', 'new_m_ffn1_w_down': 'new_m', 'new_m_mix_norm': 'new_m', 'new_m_w_in': 'new_m', 'new_m_q_norm': 'new_m', 'new_m_kv_norm': 'new_m', 'new_m_w_uq': 'new_m', 'new_m_w_ukv': 'new_m', 'new_m_hg_lb_raw': 'new_m', 'new_m_hg_norm': 'new_m', 'new_m_w_proj_attn': 'new_m', 'new_m_w_proj_rec': 'new_m', 'new_m_w_out': 'new_m', 'new_m_ffn2_norm': 'new_m', 'new_m_ffn2_w_gu': 'new_m', 'new_m_ffn2_w_down': 'new_m', 'new_m_final_norm': 'new_m', 'new_v_meta_tokens': 'new_v', 'new_v_ffn1_norm': 'new_v', 'new_v_ffn1_w_gu': 'new_v', 'new_v_ffn1_w_down': 'new_v', 'new_v_mix_norm': 'new_v', 'new_v_w_in': 'new_v', 'new_v_q_norm': 'new_v', 'new_v_kv_norm': 'new_v', 'new_v_w_uq': 'new_v', 'new_v_w_ukv': 'new_v', 'new_v_hg_lb_raw': 'new_v', 'new_v_hg_norm': 'new_v', 'new_v_w_proj_attn': 'new_v', 'new_v_w_proj_rec': 'new_v', 'new_v_w_out': 'new_v', 'new_v_ffn2_norm': 'new_v', 'new_v_ffn2_w_gu': 'new_v', 'new_v_ffn2_w_down': 'new_v', 'new_v_final_norm': 'new_v'}


def _forward(args):
    return _fwd_reference(*[args[k] for k in FWD_PARAMS])


def _output_shape():
    def fwd():
        inp = _fwd_setup_inputs(0)
        return _fwd_reference(*[inp[k] for k in FWD_PARAMS])
    out = _jax.eval_shape(fwd)
    return out.shape, out.dtype

N_MICROBATCH = 1
ADAM_LR = 0.001
ADAM_B1 = 0.9
ADAM_B2 = 0.999
ADAM_EPS = 1e-08
ADAM_WD = 0.01
ADAM_STEP = 10
PER_EXAMPLE_BATCH_AXIS = {'x': 0, 'loss_target': 0}
SHARED_INPUTS = []
_WEIGHT_DTYPES = {'meta_tokens': _jnp.float32, 'ffn1_norm': _jnp.float32, 'ffn1_w_gu': _jnp.float32, 'ffn1_w_down': _jnp.float32, 'mix_norm': _jnp.float32, 'w_in': _jnp.float32, 'q_norm': _jnp.float32, 'kv_norm': _jnp.float32, 'w_uq': _jnp.float32, 'w_ukv': _jnp.float32, 'hg_lb_raw': _jnp.float32, 'hg_norm': _jnp.float32, 'w_proj_attn': _jnp.float32, 'w_proj_rec': _jnp.float32, 'w_out': _jnp.float32, 'ffn2_norm': _jnp.float32, 'ffn2_w_gu': _jnp.float32, 'ffn2_w_down': _jnp.float32, 'final_norm': _jnp.float32}
MOMENT_SCALE = {'meta_tokens': 1.160026e-02, 'ffn1_norm': 1.134633e-01, 'ffn1_w_gu': 4.842040e-02, 'ffn1_w_down': 7.900345e-02, 'mix_norm': 1.256447e-01, 'w_in': 5.795538e-02, 'q_norm': 3.545188e-02, 'kv_norm': 7.075202e-02, 'w_uq': 2.493254e-02, 'w_ukv': 3.308876e-02, 'hg_lb_raw': 8.817008e-03, 'hg_norm': 2.290766e-01, 'w_proj_attn': 2.777283e-02, 'w_proj_rec': 7.909516e-02, 'w_out': 8.341991e-02, 'ffn2_norm': 9.816118e-02, 'ffn2_w_gu': 4.186705e-02, 'ffn2_w_down': 6.837654e-02, 'final_norm': 6.398679e+01}


def _to_microbatches(a, axis):
    t = _jnp.moveaxis(a, axis, 0)
    t = t.reshape((N_MICROBATCH, t.shape[0] // N_MICROBATCH) + t.shape[1:])
    return _jnp.moveaxis(t, 1, axis + 1)


def setup_inputs(seed: int = 0) -> dict:
    inp = _fwd_setup_inputs(seed)
    key = _jax.random.fold_in(_jax.random.key(seed), 7919)
    shape, _ = _output_shape()
    out = dict(inp)
    out["loss_target"] = _jax.random.normal(_jax.random.fold_in(key, 0), shape, _jnp.float32)
    for i, name in enumerate(TWIN_WEIGHTS):
        w = inp[name].astype(_jnp.float32)
        if MOMENT_SCALE is None:
            s = _jnp.sqrt(_jnp.mean(_jnp.square(w)) + 1e-30)
        else:
            s = MOMENT_SCALE[name]
        km, kv = _jax.random.split(_jax.random.fold_in(key, i + 1))
        out[name] = w
        out["m_" + name] = s * _jax.random.normal(km, w.shape, _jnp.float32)
        out["v_" + name] = (s * s) * _jax.random.uniform(kv, w.shape, _jnp.float32, 0.5, 1.5)
    if N_MICROBATCH > 1:
        for name, axis in PER_EXAMPLE_BATCH_AXIS.items():
            out[name] = _to_microbatches(out[name], axis)
    return {'x': out['x'], 'meta_tokens': out['meta_tokens'], 'ffn1_norm': out['ffn1_norm'], 'ffn1_w_gu': out['ffn1_w_gu'], 'ffn1_w_down': out['ffn1_w_down'], 'mix_norm': out['mix_norm'], 'w_in': out['w_in'], 'q_norm': out['q_norm'], 'kv_norm': out['kv_norm'], 'w_uq': out['w_uq'], 'w_ukv': out['w_ukv'], 'hg_lb_raw': out['hg_lb_raw'], 'hg_norm': out['hg_norm'], 'w_proj_attn': out['w_proj_attn'], 'w_proj_rec': out['w_proj_rec'], 'w_out': out['w_out'], 'ffn2_norm': out['ffn2_norm'], 'ffn2_w_gu': out['ffn2_w_gu'], 'ffn2_w_down': out['ffn2_w_down'], 'final_norm': out['final_norm'], 'loss_target': out['loss_target'], 'm_meta_tokens': out['m_meta_tokens'], 'm_ffn1_norm': out['m_ffn1_norm'], 'm_ffn1_w_gu': out['m_ffn1_w_gu'], 'm_ffn1_w_down': out['m_ffn1_w_down'], 'm_mix_norm': out['m_mix_norm'], 'm_w_in': out['m_w_in'], 'm_q_norm': out['m_q_norm'], 'm_kv_norm': out['m_kv_norm'], 'm_w_uq': out['m_w_uq'], 'm_w_ukv': out['m_w_ukv'], 'm_hg_lb_raw': out['m_hg_lb_raw'], 'm_hg_norm': out['m_hg_norm'], 'm_w_proj_attn': out['m_w_proj_attn'], 'm_w_proj_rec': out['m_w_proj_rec'], 'm_w_out': out['m_w_out'], 'm_ffn2_norm': out['m_ffn2_norm'], 'm_ffn2_w_gu': out['m_ffn2_w_gu'], 'm_ffn2_w_down': out['m_ffn2_w_down'], 'm_final_norm': out['m_final_norm'], 'v_meta_tokens': out['v_meta_tokens'], 'v_ffn1_norm': out['v_ffn1_norm'], 'v_ffn1_w_gu': out['v_ffn1_w_gu'], 'v_ffn1_w_down': out['v_ffn1_w_down'], 'v_mix_norm': out['v_mix_norm'], 'v_w_in': out['v_w_in'], 'v_q_norm': out['v_q_norm'], 'v_kv_norm': out['v_kv_norm'], 'v_w_uq': out['v_w_uq'], 'v_w_ukv': out['v_w_ukv'], 'v_hg_lb_raw': out['v_hg_lb_raw'], 'v_hg_norm': out['v_hg_norm'], 'v_w_proj_attn': out['v_w_proj_attn'], 'v_w_proj_rec': out['v_w_proj_rec'], 'v_w_out': out['v_w_out'], 'v_ffn2_norm': out['v_ffn2_norm'], 'v_ffn2_w_gu': out['v_ffn2_w_gu'], 'v_ffn2_w_down': out['v_ffn2_w_down'], 'v_final_norm': out['v_final_norm']}


def _loss(weights, diff, rest, loss_target):
    with _jax.named_scope("forward"):
        args = {**rest, TWIN_DIFF_INPUT: diff, **{k: w.astype(_WEIGHT_DTYPES[k]) for k, w in weights.items()}}
        y = _forward(args)
    with _jax.named_scope("loss_head"):
        err = _jnp.square(y.astype(_jnp.float32) - loss_target)
        return 0.5 * _jnp.sum(_jnp.mean(err, axis=-1)) if err.ndim else 0.5 * err


def _adamw(w, g, m, v):
    m = ADAM_B1 * m + (1.0 - ADAM_B1) * g
    v = ADAM_B2 * v + (1.0 - ADAM_B2) * _jnp.square(g)
    m_hat = m / (1.0 - ADAM_B1 ** ADAM_STEP)
    v_hat = v / (1.0 - ADAM_B2 ** ADAM_STEP)
    delta = -ADAM_LR * (m_hat / (_jnp.sqrt(v_hat) + ADAM_EPS) + ADAM_WD * w)
    return delta, m, v


def reference(x, meta_tokens, ffn1_norm, ffn1_w_gu, ffn1_w_down, mix_norm, w_in, q_norm, kv_norm, w_uq, w_ukv, hg_lb_raw, hg_norm, w_proj_attn, w_proj_rec, w_out, ffn2_norm, ffn2_w_gu, ffn2_w_down, final_norm, loss_target, m_meta_tokens, m_ffn1_norm, m_ffn1_w_gu, m_ffn1_w_down, m_mix_norm, m_w_in, m_q_norm, m_kv_norm, m_w_uq, m_w_ukv, m_hg_lb_raw, m_hg_norm, m_w_proj_attn, m_w_proj_rec, m_w_out, m_ffn2_norm, m_ffn2_w_gu, m_ffn2_w_down, m_final_norm, v_meta_tokens, v_ffn1_norm, v_ffn1_w_gu, v_ffn1_w_down, v_mix_norm, v_w_in, v_q_norm, v_kv_norm, v_w_uq, v_w_ukv, v_hg_lb_raw, v_hg_norm, v_w_proj_attn, v_w_proj_rec, v_w_out, v_ffn2_norm, v_ffn2_w_gu, v_ffn2_w_down, v_final_norm):
    given = dict(x=x, meta_tokens=meta_tokens, ffn1_norm=ffn1_norm, ffn1_w_gu=ffn1_w_gu, ffn1_w_down=ffn1_w_down, mix_norm=mix_norm, w_in=w_in, q_norm=q_norm, kv_norm=kv_norm, w_uq=w_uq, w_ukv=w_ukv, hg_lb_raw=hg_lb_raw, hg_norm=hg_norm, w_proj_attn=w_proj_attn, w_proj_rec=w_proj_rec, w_out=w_out, ffn2_norm=ffn2_norm, ffn2_w_gu=ffn2_w_gu, ffn2_w_down=ffn2_w_down, final_norm=final_norm, loss_target=loss_target, m_meta_tokens=m_meta_tokens, m_ffn1_norm=m_ffn1_norm, m_ffn1_w_gu=m_ffn1_w_gu, m_ffn1_w_down=m_ffn1_w_down, m_mix_norm=m_mix_norm, m_w_in=m_w_in, m_q_norm=m_q_norm, m_kv_norm=m_kv_norm, m_w_uq=m_w_uq, m_w_ukv=m_w_ukv, m_hg_lb_raw=m_hg_lb_raw, m_hg_norm=m_hg_norm, m_w_proj_attn=m_w_proj_attn, m_w_proj_rec=m_w_proj_rec, m_w_out=m_w_out, m_ffn2_norm=m_ffn2_norm, m_ffn2_w_gu=m_ffn2_w_gu, m_ffn2_w_down=m_ffn2_w_down, m_final_norm=m_final_norm, v_meta_tokens=v_meta_tokens, v_ffn1_norm=v_ffn1_norm, v_ffn1_w_gu=v_ffn1_w_gu, v_ffn1_w_down=v_ffn1_w_down, v_mix_norm=v_mix_norm, v_w_in=v_w_in, v_q_norm=v_q_norm, v_kv_norm=v_kv_norm, v_w_uq=v_w_uq, v_w_ukv=v_w_ukv, v_hg_lb_raw=v_hg_lb_raw, v_hg_norm=v_hg_norm, v_w_proj_attn=v_w_proj_attn, v_w_proj_rec=v_w_proj_rec, v_w_out=v_w_out, v_ffn2_norm=v_ffn2_norm, v_ffn2_w_gu=v_ffn2_w_gu, v_ffn2_w_down=v_ffn2_w_down, v_final_norm=v_final_norm)
    weights = {n: given[n] for n in TWIN_WEIGHTS}
    shared = {n: given[n] for n in SHARED_INPUTS}
    per_example = {n: given[n] for n in ['x']}
    grad_fn = _jax.value_and_grad(_loss, argnums=(0, 1))

    def one_microbatch(ex, loss_target):
        ex = dict(ex)
        diff = ex.pop(TWIN_DIFF_INPUT)
        return grad_fn(weights, diff, {**shared, **ex}, loss_target)

    if N_MICROBATCH == 1:
        loss, (grad_w, grad_x) = one_microbatch(per_example, given["loss_target"])
    else:
        def body(carry, xs):
            loss_sum, grad_sum = carry
            l_k, (gw_k, gx_k) = one_microbatch(xs[0], xs[1])
            with _jax.named_scope("update"):
                return (loss_sum + l_k, _jax.tree.map(_jnp.add, grad_sum, gw_k)), gx_k

        init = (_jnp.zeros((), _jnp.float32), _jax.tree.map(_jnp.zeros_like, weights))
        (loss, grad_w), grad_x = _jax.lax.scan(body, init, (per_example, given["loss_target"]))
    with _jax.named_scope("update"):
        delta_w, new_m, new_v = {}, {}, {}
        for n in TWIN_WEIGHTS:
            delta_w[n], new_m[n], new_v[n] = _adamw(weights[n], grad_w[n], given["m_" + n], given["v_" + n])
    return (loss, grad_x, *[grad_w[n] for n in TWIN_WEIGHTS], *[delta_w[n] for n in TWIN_WEIGHTS],
            *[new_m[n] for n in TWIN_WEIGHTS], *[new_v[n] for n in TWIN_WEIGHTS])
```

```python
import functools
import math

import jax
import jax.numpy as jnp
from jax import lax
from jax.experimental import pallas as pl
from jax.experimental.pallas import tpu as pltpu

F32 = jnp.float32
BF16 = jnp.bfloat16

N_DEV = 8
N_META = 16
MLA_HEADS = 8
Q_LORA = 384
KV_LORA = 256
QK_NOPE = 64
QK_ROPE = 32
V_HEAD = 64
ROPE_THETA = 10000.0
HG_HEADS = 4
HG_DIM = 128
EPS = 1e-6
NEG_BIG = -1e30
F_MIN = 1e-20
ADAM_LR = 0.001
ADAM_B1 = 0.9
ADAM_B2 = 0.999
ADAM_EPS = 1e-08
ADAM_WD = 0.01
ADAM_STEP = 10

LANE = 128
ROW_ALIGN = 256
ATTN_BLOCK = 256
HG_CHUNK = 128
HG_CHUNKS_PER_STEP = 2
HG_LEVELS = (64, 32, 16, 8, 4, 2, 1)
VMEM_LIMIT = 48 * 1024 * 1024

Z_CQ, Z_CKV, Z_KPE, Z_KPESW, Z_HQ, Z_HF, Z_HI, Z_HG, Z_GA, Z_GB, Z_W = 0, 384, 640, 768, 1024, 1536, 2048, 2560, 3072, 4096, 5120
ATTN_SCALE = float((QK_NOPE + QK_ROPE) ** -0.5)
HG_SCALE = float(HG_DIM ** -0.5)


def _cparams(sem):
    return pltpu.CompilerParams(dimension_semantics=sem, vmem_limit_bytes=VMEM_LIMIT)


def _tile(n, cap):
    if n <= cap:
        return n
    best = None
    for t in range(LANE, cap + 1, LANE):
        if n % t == 0:
            best = t
    assert best is not None, (n, cap)
    return best


def _row_tile(m):
    for t in (384, 256, 128):
        if m % t == 0:
            return t
    raise ValueError(m)


def _bf(x):
    return x.astype(BF16)


def _dot(a, b):
    return jnp.dot(a, b, preferred_element_type=F32)


def _dot_nt(a, b):
    return lax.dot_general(a, b, (((1,), (1,)), ((), ())), preferred_element_type=F32)


def _dot_tn(a, b):
    return lax.dot_general(a, b, (((0,), (0,)), ((), ())), preferred_element_type=F32)


def _sigmoid(x):
    return 1.0 / (1.0 + jnp.exp(-x))


def _mm(name, a_list, pairs, extras, outs, fn, *, tm, tn, n):
    m = a_list[0].shape[0]
    na, nb, ne, no = len(a_list), len(pairs), len(extras), len(outs)

    def body(*refs):
        a_refs = refs[:na]
        b_refs = refs[na:na + nb]
        e_refs = refs[na + nb:na + nb + ne]
        o_refs = refs[na + nb + ne:]
        a_vals = [_bf(r[...]) for r in a_refs]
        accs = [_dot(a_vals[ai], b_refs[k][...]) for k, (ai, _, _) in enumerate(pairs)]
        res = fn(accs, [r[...] for r in e_refs])
        for r, v in zip(o_refs, res):
            r[...] = v.astype(r.dtype)

    in_specs = [pl.BlockSpec((tm, a.shape[1]), lambda i, j: (i, 0)) for a in a_list]
    for _, b, off in pairs:
        in_specs.append(pl.BlockSpec((b.shape[0], tn), functools.partial(lambda i, j, off: (0, j + off), off=off)))
    in_specs += [pl.BlockSpec(bs, im) for _, bs, im in extras]
    return pl.pallas_call(
        body, name=name, grid=(m // tm, n // tn),
        in_specs=in_specs,
        out_specs=[pl.BlockSpec(bs, im) for _, _, bs, im in outs],
        out_shape=[jax.ShapeDtypeStruct(s, d) for s, d, _, _ in outs],
        compiler_params=_cparams(("parallel", "arbitrary")),
    )(*a_list, *[b for _, b, _ in pairs], *[e for e, _, _ in extras])


def _tile_spec(tm, tn, col_off=0):
    return (tm, tn), functools.partial(lambda i, j, off: (i, j + off), off=col_off)


def _mm_tn(name, a, b, *, alpha, out_dtype):
    t, k = a.shape
    n = b.shape[1]
    tk, tn, tt = _tile(k, 1408), _tile(n, 1408), _row_tile(t)
    nt = t // tt

    def body(a_ref, b_ref, o_ref, acc_ref):
        s = pl.program_id(2)

        @pl.when(s == 0)
        def _():
            acc_ref[...] = jnp.zeros_like(acc_ref)

        acc_ref[...] += _dot_tn(_bf(a_ref[...]), _bf(b_ref[...]))

        @pl.when(s == nt - 1)
        def _():
            o_ref[...] = (alpha * acc_ref[...]).astype(o_ref.dtype)

    return pl.pallas_call(
        body, name=name, grid=(k // tk, n // tn, nt),
        in_specs=[pl.BlockSpec((tt, tk), lambda i, j, s: (s, i)), pl.BlockSpec((tt, tn), lambda i, j, s: (s, j))],
        out_specs=pl.BlockSpec((tk, tn), lambda i, j, s: (i, j)),
        out_shape=jax.ShapeDtypeStruct((k, n), out_dtype),
        scratch_shapes=[pltpu.VMEM((tk, tn), F32)],
        compiler_params=_cparams(("parallel", "parallel", "arbitrary")),
    )(a, b)


def _rms_parts(x):
    r = lax.rsqrt(jnp.mean(x * x, axis=-1, keepdims=True) + EPS)
    return r, x * r


def _rms_bwd_math(x, w, dxn):
    r, xhat = _rms_parts(x)
    t = dxn * w
    dx = r * (t - xhat * jnp.mean(t * xhat, axis=-1, keepdims=True))
    dw = jnp.sum(dxn * xhat, axis=0, keepdims=True)
    return dx, dw


def _rms_fwd(name, h, w):
    t, d = h.shape
    tm = _row_tile(t)

    def body(h_ref, w_ref, o_ref):
        _, xhat = _rms_parts(h_ref[...])
        o_ref[...] = (xhat * w_ref[...]).astype(o_ref.dtype)

    return pl.pallas_call(
        body, name=name, grid=(t // tm,),
        in_specs=[pl.BlockSpec((tm, d), lambda i: (i, 0)), pl.BlockSpec((1, d), lambda i: (0, 0))],
        out_specs=pl.BlockSpec((tm, d), lambda i: (i, 0)),
        out_shape=jax.ShapeDtypeStruct((t, d), BF16),
        compiler_params=_cparams(("parallel",)),
    )(h, w)


def _rms_bwd(name, h, w, dxn, dh_in):
    t, d = h.shape
    tm = _row_tile(t)

    def body(h_ref, w_ref, dxn_ref, dh_ref, o_ref, dw_ref):
        dx, dw = _rms_bwd_math(h_ref[...], w_ref[...], dxn_ref[...])
        o_ref[...] = dh_ref[...] + dx

        @pl.when(pl.program_id(0) == 0)
        def _():
            dw_ref[...] = jnp.zeros_like(dw_ref)

        dw_ref[...] += dw

    row = pl.BlockSpec((tm, d), lambda i: (i, 0))
    vec = pl.BlockSpec((1, d), lambda i: (0, 0))
    return pl.pallas_call(
        body, name=name, grid=(t // tm,),
        in_specs=[row, vec, row, row],
        out_specs=[row, vec],
        out_shape=[jax.ShapeDtypeStruct((t, d), F32), jax.ShapeDtypeStruct((1, d), F32)],
        compiler_params=_cparams(("arbitrary",)),
    )(h, w, dxn, dh_in)


def _loss_head(h, w, tgt, n_real):
    t, d = h.shape
    tm = _row_tile(t)

    def body(h_ref, w_ref, t_ref, dh_ref, dw_ref, loss_ref):
        i = pl.program_id(0)
        x = h_ref[...]
        wv = w_ref[...]
        _, xhat = _rms_parts(x)
        rows = i * tm + lax.broadcasted_iota(jnp.int32, (tm, 1), 0)
        valid = (rows >= N_META) & (rows < n_real)
        e = jnp.where(valid, xhat * wv - t_ref[...], 0.0)
        dx, dw = _rms_bwd_math(x, wv, e * (1.0 / d))
        dh_ref[...] = dx

        @pl.when(i == 0)
        def _():
            dw_ref[...] = jnp.zeros_like(dw_ref)
            loss_ref[...] = jnp.zeros_like(loss_ref)

        dw_ref[...] += dw
        loss_ref[...] += (0.5 / d) * jnp.sum(jnp.sum(e * e, axis=-1, keepdims=True), axis=0, keepdims=True)

    row = pl.BlockSpec((tm, d), lambda i: (i, 0))
    vec = pl.BlockSpec((1, d), lambda i: (0, 0))
    return pl.pallas_call(
        body, name="loss_head", grid=(t // tm,),
        in_specs=[row, vec, row],
        out_specs=[row, vec, pl.BlockSpec((1, 1), lambda i: (0, 0))],
        out_shape=[jax.ShapeDtypeStruct((t, d), F32), jax.ShapeDtypeStruct((1, d), F32), jax.ShapeDtypeStruct((1, 1), F32)],
        compiler_params=_cparams(("arbitrary",)),
    )(h, w, tgt)


def _mla_prep(z, qn_w, kvn_w, ck, sk):
    t = z.shape[0]
    tm = _row_tile(t)

    def body(z_ref, qw_ref, kw_ref, ck_ref, sk_ref, cq_ref, ckv_ref, kr_ref):
        zz = z_ref[...]
        _, qhat = _rms_parts(zz[:, Z_CQ:Z_CKV])
        _, khat = _rms_parts(zz[:, Z_CKV:Z_KPE])
        cq_ref[...] = (qhat * qw_ref[...]).astype(BF16)
        ckv_ref[...] = (khat * kw_ref[...]).astype(BF16)
        kr_ref[...] = zz[:, Z_KPE:Z_KPESW] * ck_ref[...] + zz[:, Z_KPESW:Z_KPESW + LANE] * sk_ref[...]

    def rows(wd):
        return pl.BlockSpec((tm, wd), lambda i: (i, 0))

    def vec(wd):
        return pl.BlockSpec((1, wd), lambda i: (0, 0))

    return pl.pallas_call(
        body, name="mla_prep", grid=(t // tm,),
        in_specs=[rows(Z_HQ), vec(Q_LORA), vec(KV_LORA), rows(LANE), rows(LANE)],
        out_specs=[rows(Q_LORA), rows(KV_LORA), rows(LANE)],
        out_shape=[jax.ShapeDtypeStruct((t, Q_LORA), BF16), jax.ShapeDtypeStruct((t, KV_LORA), BF16),
                   jax.ShapeDtypeStruct((t, LANE), F32)],
        compiler_params=_cparams(("parallel",)),
    )(z, qn_w, kvn_w, ck, sk)


def _mla_prep_bwd(z, qn_w, kvn_w, ck, sk, dcq, dckv, dkv):
    t = z.shape[0]
    tm = _row_tile(t)

    def body(z_ref, qw_ref, kw_ref, ck_ref, sk_ref, dcq_ref, dckv_ref, dkv_ref, dz_ref, dqw_ref, dkw_ref):
        zz = z_ref[...]
        dq, dqw = _rms_bwd_math(zz[:, Z_CQ:Z_CKV], qw_ref[...], dcq_ref[...])
        dk, dkw = _rms_bwd_math(zz[:, Z_CKV:Z_KPE], kw_ref[...], dckv_ref[...])
        dkv_v = dkv_ref[...]
        dkr = jnp.zeros((tm, LANE), F32)
        for hd in range(MLA_HEADS):
            dkr = dkr + dkv_v[:, 2 * LANE * hd:2 * LANE * hd + LANE].astype(F32)
        dz_ref[...] = jnp.concatenate(
            [dq, dk, dkr * ck_ref[...], dkr * sk_ref[...], jnp.zeros((tm, Z_HQ - Z_KPESW - LANE), F32)], axis=1
        ).astype(BF16)

        @pl.when(pl.program_id(0) == 0)
        def _():
            dqw_ref[...] = jnp.zeros_like(dqw_ref)
            dkw_ref[...] = jnp.zeros_like(dkw_ref)

        dqw_ref[...] += dqw
        dkw_ref[...] += dkw

    def rows(wd):
        return pl.BlockSpec((tm, wd), lambda i: (i, 0))

    def vec(wd):
        return pl.BlockSpec((1, wd), lambda i: (0, 0))

    return pl.pallas_call(
        body, name="mla_prep_bwd", grid=(t // tm,),
        in_specs=[rows(Z_HQ), vec(Q_LORA), vec(KV_LORA), rows(LANE), rows(LANE), rows(Q_LORA), rows(KV_LORA),
                  rows(2 * LANE * MLA_HEADS)],
        out_specs=[rows(Z_HQ), vec(Q_LORA), vec(KV_LORA)],
        out_shape=[jax.ShapeDtypeStruct((t, Z_HQ), BF16), jax.ShapeDtypeStruct((1, Q_LORA), F32),
                   jax.ShapeDtypeStruct((1, KV_LORA), F32)],
        compiler_params=_cparams(("arbitrary",)),
    )(z, qn_w, kvn_w, ck, sk, dcq, dckv, dkv)


def _rope_bwd_q(dq, cq, sq):
    t, wq = dq.shape
    tm = _row_tile(t)

    def body(dq_ref, c_ref, s_ref, o_ref):
        d = dq_ref[...]
        c8 = jnp.concatenate([c_ref[...]] * MLA_HEADS, axis=1)
        s8 = jnp.concatenate([s_ref[...]] * MLA_HEADS, axis=1)
        o_ref[...] = jnp.concatenate([d * c8, d * s8], axis=1).astype(BF16)

    return pl.pallas_call(
        body, name="rope_bwd_q", grid=(t // tm,),
        in_specs=[pl.BlockSpec((tm, wq), lambda i: (i, 0)), pl.BlockSpec((tm, LANE), lambda i: (i, 0)),
                  pl.BlockSpec((tm, LANE), lambda i: (i, 0))],
        out_specs=pl.BlockSpec((tm, 2 * wq), lambda i: (i, 0)),
        out_shape=jax.ShapeDtypeStruct((t, 2 * wq), BF16),
        compiler_params=_cparams(("parallel",)),
    )(dq, cq, sq)


def _row_vector(col):
    return jnp.broadcast_to(col, (col.shape[0], LANE)).T[0:8, :]


def _attn_fwd(q, kv):
    t = q.shape[0]
    bq = ATTN_BLOCK
    nq = t // bq

    def body(q_ref, k_ref, v_ref, o_ref, lse_ref):
        i = pl.program_id(1)
        qv = q_ref[...]
        qpos = i * bq + lax.broadcasted_iota(jnp.int32, (bq, 1), 0)

        def step(j, carry):
            m, l, acc = carry
            kb = k_ref[pl.ds(pl.multiple_of(j * bq, bq), bq), :]
            vb = v_ref[pl.ds(pl.multiple_of(j * bq, bq), bq), :]
            s = _dot_nt(qv, kb) * ATTN_SCALE
            kpos = j * bq + lax.broadcasted_iota(jnp.int32, (1, bq), 1)
            s = jnp.where(kpos <= qpos, s, NEG_BIG)
            m_new = jnp.maximum(m, jnp.max(s, axis=1, keepdims=True))
            p = jnp.exp(s - m_new)
            a = jnp.exp(m - m_new)
            return m_new, a * l + jnp.sum(p, axis=1, keepdims=True), a * acc + _dot(_bf(p), vb)

        init = (jnp.full((bq, 1), NEG_BIG, F32), jnp.zeros((bq, 1), F32), jnp.zeros((bq, LANE), F32))
        m, l, acc = lax.fori_loop(0, i + 1, step, init)
        o_ref[...] = (acc / l).astype(o_ref.dtype)
        lse_ref[0, 0] = _row_vector(m + jnp.log(l))

    return pl.pallas_call(
        body, name="attn_fwd", grid=(MLA_HEADS, nq),
        in_specs=[pl.BlockSpec((bq, LANE), lambda h, i: (i, h)),
                  pl.BlockSpec((t, LANE), lambda h, i: (0, 2 * h)),
                  pl.BlockSpec((t, LANE), lambda h, i: (0, 2 * h + 1))],
        out_specs=[pl.BlockSpec((bq, LANE), lambda h, i: (i, h)),
                   pl.BlockSpec((1, 1, 8, bq), lambda h, i: (h, i, 0, 0))],
        out_shape=[jax.ShapeDtypeStruct((t, MLA_HEADS * LANE), BF16), jax.ShapeDtypeStruct((MLA_HEADS, nq, 8, bq), F32)],
        compiler_params=_cparams(("parallel", "arbitrary")),
    )(q, kv, kv)


def _attn_delta(o, do):
    t = o.shape[0]
    bq = ATTN_BLOCK
    nq = t // bq

    def body(o_ref, do_ref, d_ref):
        d = jnp.sum(o_ref[...].astype(F32) * do_ref[...].astype(F32), axis=1, keepdims=True)
        d_ref[0, 0] = _row_vector(d)

    blk = pl.BlockSpec((bq, LANE), lambda h, i: (i, h))
    return pl.pallas_call(
        body, name="attn_delta", grid=(MLA_HEADS, nq),
        in_specs=[blk, blk],
        out_specs=pl.BlockSpec((1, 1, 8, bq), lambda h, i: (h, i, 0, 0)),
        out_shape=jax.ShapeDtypeStruct((MLA_HEADS, nq, 8, bq), F32),
        compiler_params=_cparams(("parallel", "parallel")),
    )(o, do)


def _attn_bwd(q, kv, do, lse, delta):
    t = q.shape[0]
    bq = ATTN_BLOCK
    nq = t // bq

    def body(q_ref, do_ref, k_ref, v_ref, lse_ref, dl_ref, dq_ref, dkv_ref):
        j = pl.program_id(1)

        @pl.when(j == 0)
        def _():
            dq_ref[...] = jnp.zeros_like(dq_ref)

        kb = k_ref[...]
        vb = v_ref[...]
        kpos = j * bq + lax.broadcasted_iota(jnp.int32, (bq, 1), 0)

        def step(i, carry):
            dk, dv = carry
            rows = pl.ds(pl.multiple_of(i * bq, bq), bq)
            qb = q_ref[rows, :]
            dob = do_ref[rows, :]
            st = _dot_nt(kb, qb) * ATTN_SCALE
            qpos = i * bq + lax.broadcasted_iota(jnp.int32, (1, bq), 1)
            pt = jnp.where(kpos <= qpos, jnp.exp(st - lse_ref[0, i, 0:1, :]), 0.0)
            dv = dv + _dot(_bf(pt), dob)
            dst = _bf(pt * (_dot_nt(vb, dob) - dl_ref[0, i, 0:1, :]) * ATTN_SCALE)
            dk = dk + _dot(dst, qb)
            dq_ref[rows, :] += _dot_tn(dst, kb)
            return dk, dv

        dk, dv = lax.fori_loop(j, nq, step, (jnp.zeros((bq, LANE), F32), jnp.zeros((bq, LANE), F32)))
        dkv_ref[...] = jnp.concatenate([dk, dv], axis=1).astype(dkv_ref.dtype)

    head_rows = lambda h, j: (0, h)
    stats = pl.BlockSpec((1, nq, 8, bq), lambda h, j: (h, 0, 0, 0))
    return pl.pallas_call(
        body, name="attn_bwd", grid=(MLA_HEADS, nq),
        in_specs=[pl.BlockSpec((t, LANE), head_rows), pl.BlockSpec((t, LANE), head_rows),
                  pl.BlockSpec((bq, LANE), lambda h, j: (j, 2 * h)), pl.BlockSpec((bq, LANE), lambda h, j: (j, 2 * h + 1)),
                  stats, stats],
        out_specs=[pl.BlockSpec((t, LANE), head_rows), pl.BlockSpec((bq, 2 * LANE), lambda h, j: (j, h))],
        out_shape=[jax.ShapeDtypeStruct((t, MLA_HEADS * LANE), F32), jax.ShapeDtypeStruct((t, 2 * MLA_HEADS * LANE), BF16)],
        compiler_params=_cparams(("parallel", "arbitrary")),
    )(q, do, kv, kv, lse, delta)


def _hg_tables():
    c = HG_CHUNK
    tri = (jnp.arange(c)[:, None] >= jnp.arange(c)[None, :]).astype(F32)
    mats = [tri]
    for m in HG_LEVELS:
        ref = (jnp.arange(c) // (2 * m)) * (2 * m) + m - 1
        mats.append(tri[ref])
    call = jnp.concatenate(mats, axis=0)
    return call, call.T


def _hg_gates(hq, hf, lb):
    sg = _sigmoid(hf)
    sn = _sigmoid(-hf)
    f = lb + (1.0 - lb) * sg
    q = hq * _sigmoid(hq)
    g = jnp.log(jnp.maximum(f, F_MIN))
    k = (1.0 - lb) * sn
    return q, k, g, f, sg, sn


def _hg_level_masks(m):
    c = HG_CHUNK
    row = lax.broadcasted_iota(jnp.int32, (c, 1), 0)
    col = lax.broadcasted_iota(jnp.int32, (1, c), 1)
    shift = (2 * m).bit_length() - 1
    up = (row & m) != 0
    same = lax.shift_right_logical(row, shift) == lax.shift_right_logical(col, shift)
    return up, same


def _hg_level_factors(b, bref, up):
    lo = jnp.logical_not(up)
    eq = jnp.where(up, jnp.exp(jnp.where(up, b - bref, 0.0)), 0.0)
    ek = jnp.where(lo, jnp.exp(jnp.where(lo, bref - b, 0.0)), 0.0)
    return eq, ek


def _hg_intra(q, k, ball):
    c = HG_CHUNK
    b = ball[0:c]
    row = lax.broadcasted_iota(jnp.int32, (c, 1), 0)
    col = lax.broadcasted_iota(jnp.int32, (1, c), 1)
    a = jnp.where(row == col, _dot_nt(_bf(q), _bf(k)), 0.0)
    parts = []
    for lv, m in enumerate(HG_LEVELS):
        up, same = _hg_level_masks(m)
        eq, ek = _hg_level_factors(b, ball[(lv + 1) * c:(lv + 2) * c], up)
        qt, kt = q * eq, k * ek
        a = a + jnp.where(same, _dot_nt(_bf(qt), _bf(kt)), 0.0)
        parts.append((eq, ek, qt, kt))
    return a, parts


def _hg_chunk_fwd(hq, hf, hi, hg, lb, nw, st, call):
    q, k, g, _, _, _ = _hg_gates(hq, hf, lb)
    ball = jnp.dot(call, g, precision=lax.Precision.HIGHEST, preferred_element_type=F32)
    b = ball[0:HG_CHUNK]
    a, _ = _hg_intra(q, k, ball)
    v16 = _bf(hi)
    o = _dot(_bf(a * HG_SCALE), v16) + _dot_nt(_bf(q * jnp.exp(b) * HG_SCALE), _bf(st))
    bl = b[HG_CHUNK - 1:HG_CHUNK]
    ke = k * jnp.exp(bl - b)
    st_new = st * jnp.exp(bl) + _dot(_bf(hi.T), _bf(ke))
    r = lax.rsqrt(jnp.mean(o * o, axis=-1, keepdims=True) + EPS)
    y = o * r * nw * (hg * _sigmoid(hg))
    return y, st_new


def _hg_chunk_bwd(hq, hf, hi, hg, lb, nw, st, call, call_t, dy, dst_new):
    c = HG_CHUNK
    q, k, g, f, sg, sn = _hg_gates(hq, hf, lb)
    ball = jnp.dot(call, g, precision=lax.Precision.HIGHEST, preferred_element_type=F32)
    b = ball[0:c]
    a, parts = _hg_intra(q, k, ball)
    v16 = _bf(hi)
    st16 = _bf(st)
    eb = jnp.exp(b)
    qe = q * eb * HG_SCALE
    a16 = _bf(a * HG_SCALE)
    o = _dot(a16, v16) + _dot_nt(_bf(qe), st16)
    bl = b[c - 1:c]
    el = jnp.exp(bl)
    x = jnp.exp(bl - b)
    ke = k * x
    r = lax.rsqrt(jnp.mean(o * o, axis=-1, keepdims=True) + EPS)
    shg = _sigmoid(hg)
    gate = hg * shg
    ohat = o * r
    don = dy * gate
    dhg = dy * ohat * nw * (shg * (1.0 + hg * (1.0 - shg)))
    dnw = jnp.sum(don * ohat, axis=0, keepdims=True)
    tt = don * nw
    do = r * (tt - ohat * jnp.mean(tt * ohat, axis=-1, keepdims=True))
    do16 = _bf(do)
    dst16 = _bf(dst_new)
    da = _dot_nt(do16, v16) * HG_SCALE
    dv = _dot(_bf(a16.astype(F32).T), do16) + _dot_nt(_bf(ke), dst16)
    dqe = _dot(do16, st16)
    dke = _dot(v16, dst16)
    dst = dst_new * el + _dot(_bf(do.T), _bf(qe))
    dbl = jnp.sum(dst_new * st, axis=0, keepdims=True) * el
    dk = dke * x
    dxa = dke * ke
    db = dqe * qe - dxa
    dbl = dbl + jnp.sum(dxa, axis=0, keepdims=True)
    dq = dqe * eb * HG_SCALE
    row = lax.broadcasted_iota(jnp.int32, (c, 1), 0)
    col = lax.broadcasted_iota(jnp.int32, (1, c), 1)
    ddiag = jnp.sum(jnp.where(row == col, da, 0.0), axis=1, keepdims=True)
    dq = dq + ddiag * k
    dk = dk + ddiag * q
    dball = []
    for (eq, ek, qt, kt), m in zip(parts, HG_LEVELS):
        _, same = _hg_level_masks(m)
        gm = jnp.where(same, da, 0.0)
        dqt = _dot(_bf(gm), _bf(kt))
        dkt = _dot(_bf(gm.T), _bf(qt))
        dq = dq + dqt * eq
        dk = dk + dkt * ek
        darg = dqt * qt - dkt * kt
        db = db + darg
        dball.append(-darg)
    db = db + jnp.where(row == c - 1, dbl, 0.0)
    dg = jnp.dot(call_t, jnp.concatenate([db] + dball, axis=0), precision=lax.Precision.HIGHEST, preferred_element_type=F32)
    shq = _sigmoid(hq)
    dhq = dq * (shq * (1.0 + hq * (1.0 - shq)))
    df = jnp.where(f > F_MIN, dg / jnp.maximum(f, F_MIN), 0.0)
    dlb = jnp.sum(df * (1.0 - sg) - dk * sn, axis=0, keepdims=True)
    dhf = df * (1.0 - lb) * sg * (1.0 - sg) - dk * (1.0 - lb) * sn * (1.0 - sn)
    return dhq, dhf, dv, dhg, dlb, dnw, dst


def _hg_col(group, h):
    return group // LANE + h


def _hgrn_fwd(z, lb, nw, call):
    t = z.shape[0]
    c, cs = HG_CHUNK, HG_CHUNKS_PER_STEP
    rows = c * cs
    nsteps = t // rows

    def body(hq_ref, hf_ref, hi_ref, hg_ref, lb_ref, nw_ref, call_ref, y_ref, sv_ref, st_ref):
        @pl.when(pl.program_id(1) == 0)
        def _():
            st_ref[...] = jnp.zeros_like(st_ref)

        for u in range(cs):
            sl = slice(u * c, (u + 1) * c)
            st = st_ref[...]
            sv_ref[0, u] = st
            y, st_new = _hg_chunk_fwd(hq_ref[sl, :], hf_ref[sl, :], hi_ref[sl, :], hg_ref[sl, :], lb_ref[...], nw_ref[...],
                                      st, call_ref[...])
            y_ref[sl, :] = y.astype(y_ref.dtype)
            st_ref[...] = st_new

    def zcol(group):
        return pl.BlockSpec((rows, LANE), functools.partial(lambda h, i, g: (i, _hg_col(g, h)), g=group))

    ncall = call.shape[0]
    return pl.pallas_call(
        body, name="hgrn_fwd", grid=(HG_HEADS, nsteps),
        in_specs=[zcol(Z_HQ), zcol(Z_HF), zcol(Z_HI), zcol(Z_HG),
                  pl.BlockSpec((1, LANE), lambda h, i: (0, h)), pl.BlockSpec((1, LANE), lambda h, i: (0, 0)),
                  pl.BlockSpec((ncall, c), lambda h, i: (0, 0))],
        out_specs=[pl.BlockSpec((rows, LANE), lambda h, i: (i, h)),
                   pl.BlockSpec((1, cs, c, c), lambda h, i: (h, i, 0, 0))],
        out_shape=[jax.ShapeDtypeStruct((t, HG_HEADS * LANE), BF16), jax.ShapeDtypeStruct((HG_HEADS, t // c, c, c), F32)],
        scratch_shapes=[pltpu.VMEM((c, c), F32)],
        compiler_params=_cparams(("parallel", "arbitrary")),
    )(z, z, z, z, lb, nw, call)


def _hgrn_bwd(z, lb, nw, call, call_t, saved, dy):
    t = z.shape[0]
    c, cs = HG_CHUNK, HG_CHUNKS_PER_STEP
    rows = c * cs
    nsteps = t // rows

    def body(hq_ref, hf_ref, hi_ref, hg_ref, lb_ref, nw_ref, call_ref, callt_ref, sv_ref, dy_ref,
             dhq_ref, dhf_ref, dhi_ref, dhg_ref, dlb_ref, dnw_ref, dst_ref):
        h, i = pl.program_id(0), pl.program_id(1)

        @pl.when(i == 0)
        def _():
            dst_ref[...] = jnp.zeros_like(dst_ref)
            dlb_ref[...] = jnp.zeros_like(dlb_ref)

        @pl.when((i == 0) & (h == 0))
        def _():
            dnw_ref[...] = jnp.zeros_like(dnw_ref)

        for u in reversed(range(cs)):
            sl = slice(u * c, (u + 1) * c)
            dhq, dhf, dhi, dhg, dlb, dnw, dst = _hg_chunk_bwd(
                hq_ref[sl, :], hf_ref[sl, :], hi_ref[sl, :], hg_ref[sl, :], lb_ref[...], nw_ref[...], sv_ref[0, u],
                call_ref[...], callt_ref[...], dy_ref[sl, :].astype(F32), dst_ref[...])
            dhq_ref[sl, :] = dhq.astype(BF16)
            dhf_ref[sl, :] = dhf.astype(BF16)
            dhi_ref[sl, :] = dhi.astype(BF16)
            dhg_ref[sl, :] = dhg.astype(BF16)
            dlb_ref[...] += dlb
            dnw_ref[...] += dnw
            dst_ref[...] = dst

    def zcol(group):
        return pl.BlockSpec((rows, LANE), functools.partial(lambda h, i, g: (nsteps - 1 - i, _hg_col(g, h)), g=group))

    head_rows = pl.BlockSpec((rows, LANE), lambda h, i: (nsteps - 1 - i, h))
    ncall = call.shape[0]
    piece = jax.ShapeDtypeStruct((t, HG_HEADS * LANE), BF16)
    return pl.pallas_call(
        body, name="hgrn_bwd", grid=(HG_HEADS, nsteps),
        in_specs=[zcol(Z_HQ), zcol(Z_HF), zcol(Z_HI), zcol(Z_HG),
                  pl.BlockSpec((1, LANE), lambda h, i: (0, h)), pl.BlockSpec((1, LANE), lambda h, i: (0, 0)),
                  pl.BlockSpec((ncall, c), lambda h, i: (0, 0)), pl.BlockSpec((c, ncall), lambda h, i: (0, 0)),
                  pl.BlockSpec((1, cs, c, c), lambda h, i: (h, nsteps - 1 - i, 0, 0)), head_rows],
        out_specs=[head_rows, head_rows, head_rows, head_rows,
                   pl.BlockSpec((1, LANE), lambda h, i: (0, h)), pl.BlockSpec((1, LANE), lambda h, i: (0, 0))],
        out_shape=[piece, piece, piece, piece,
                   jax.ShapeDtypeStruct((1, HG_HEADS * LANE), F32), jax.ShapeDtypeStruct((1, LANE), F32)],
        scratch_shapes=[pltpu.VMEM((c, c), F32)],
        compiler_params=_cparams(("arbitrary", "arbitrary")),
    )(z, z, z, z, lb, nw, call, call_t, saved, dy)


def _lb_fwd(raw):
    nl = raw.shape[0]

    def body(r_ref, o_ref):
        x = r_ref[...]
        e = jnp.exp(x - jnp.max(x, axis=0, keepdims=True))
        p = e / jnp.sum(e, axis=0, keepdims=True)
        acc = jnp.zeros_like(p[0:1])
        for l in range(nl):
            if l > 0:
                acc = acc + p[l:l + 1]
            o_ref[l:l + 1, :] = acc

    return pl.pallas_call(body, name="lb_fwd", out_shape=jax.ShapeDtypeStruct(raw.shape, F32))(raw)


def _lb_bwd(raw, dlbs):
    nl = raw.shape[0]

    def body(r_ref, d_ref, o_ref):
        x = r_ref[...]
        e = jnp.exp(x - jnp.max(x, axis=0, keepdims=True))
        p = e / jnp.sum(e, axis=0, keepdims=True)
        d = d_ref[...]
        dps = [jnp.zeros_like(d[0:1])]
        for i in range(1, nl):
            acc = d[i:i + 1]
            for l in range(i + 1, nl):
                acc = acc + d[l:l + 1]
            dps.append(acc)
        dot = dps[0] * p[0:1]
        for i in range(1, nl):
            dot = dot + dps[i] * p[i:i + 1]
        for i in range(nl):
            o_ref[i:i + 1, :] = p[i:i + 1] * (dps[i] - dot)

    return pl.pallas_call(body, name="lb_bwd", out_shape=jax.ShapeDtypeStruct(raw.shape, F32))(raw, dlbs)


def _exchange(name, srcs, gather):
    nk = len(srcs)

    def body(*refs):
        src_refs, out_refs = refs[:nk], refs[nk:2 * nk]
        send_sems, recv_sems, local_sems = refs[2 * nk:]
        me = 4 * lax.axis_index("x") + 2 * lax.axis_index("y") + lax.axis_index("c")
        local = []
        for k in range(nk):
            own = src_refs[k] if gather else src_refs[k].at[me]
            cp = pltpu.make_async_copy(own, out_refs[k].at[me], local_sems.at[k])
            cp.start()
            local.append(cp)
        copies = []
        for r in range(1, N_DEV):
            to = (me + r) % N_DEV
            for k in range(nk):
                cp = pltpu.make_async_remote_copy(
                    src_ref=src_refs[k] if gather else src_refs[k].at[to],
                    dst_ref=out_refs[k].at[me],
                    send_sem=send_sems.at[k * N_DEV + r], recv_sem=recv_sems.at[k * N_DEV + r],
                    device_id=(to // 4, (to // 2) % 2, to % 2), device_id_type=pl.DeviceIdType.MESH)
                cp.start()
                copies.append(cp)
        for cp in copies:
            cp.wait()
        for cp in local:
            cp.wait()

    any_spec = pl.BlockSpec(memory_space=pl.ANY)
    out_shape = [jax.ShapeDtypeStruct(((N_DEV,) + s.shape) if gather else s.shape, s.dtype) for s in srcs]
    return pl.pallas_call(
        body, name=name,
        in_specs=[any_spec] * nk, out_specs=[any_spec] * nk, out_shape=out_shape,
        scratch_shapes=[pltpu.SemaphoreType.DMA((nk * N_DEV,)), pltpu.SemaphoreType.DMA((nk * N_DEV,)),
                        pltpu.SemaphoreType.DMA((nk,))],
    )(*srcs)


def _adam_math(g, w, m, v):
    m2 = ADAM_B1 * m + (1.0 - ADAM_B1) * g
    v2 = ADAM_B2 * v + (1.0 - ADAM_B2) * (g * g)
    m_hat = m2 / (1.0 - ADAM_B1 ** ADAM_STEP)
    v_hat = v2 / (1.0 - ADAM_B2 ** ADAM_STEP)
    return -ADAM_LR * (m_hat / (jnp.sqrt(v_hat) + ADAM_EPS) + ADAM_WD * w), m2, v2


def _sum_slots(ref):
    g = ref[0].astype(F32)
    for s in range(1, N_DEV):
        g = g + ref[s].astype(F32)
    return g


def _adam_sharded(name, slots, w, m, v):
    nl, a, b = w.shape
    ta = a
    for cand in range(8, 257, 8):
        if a % cand == 0:
            ta = cand

    def body(s_ref, w_ref, m_ref, v_ref, g_ref, d_ref, m2_ref, v2_ref):
        g = _sum_slots(s_ref)
        d, m2, v2 = _adam_math(g, w_ref[...], m_ref[...], v_ref[...])
        g_ref[...] = g
        d_ref[...] = d
        m2_ref[...] = m2
        v2_ref[...] = v2

    blk = pl.BlockSpec((1, ta, b), lambda l, i: (l, i, 0))
    sds = jax.ShapeDtypeStruct(w.shape, F32)
    return pl.pallas_call(
        body, name=name, grid=(nl, a // ta),
        in_specs=[pl.BlockSpec((N_DEV, 1, ta, b), lambda l, i: (0, l, i, 0)), blk, blk, blk],
        out_specs=[blk] * 4, out_shape=[sds] * 4,
        compiler_params=_cparams(("parallel", "parallel")),
    )(slots, w, m, v)


def _sum_replicated(slots):
    def body(s_ref, g_ref):
        g_ref[...] = _sum_slots(s_ref)

    return pl.pallas_call(body, name="sum_small", out_shape=jax.ShapeDtypeStruct(slots.shape[1:], F32))(slots)


def _adam_small(name, g, w, m, v):
    def body(g_ref, w_ref, m_ref, v_ref, d_ref, m2_ref, v2_ref):
        d, m2, v2 = _adam_math(g_ref[...], w_ref[...], m_ref[...], v_ref[...])
        d_ref[...] = d
        m2_ref[...] = m2
        v2_ref[...] = v2

    sds = jax.ShapeDtypeStruct(w.shape, F32)
    return pl.pallas_call(body, name=name, out_shape=[sds] * 3)(g, w, m, v)


def _cols_full(g):
    return jnp.transpose(g, (1, 0, 2)).reshape(g.shape[1], -1)


def _cols_shards(w):
    k = w.shape[0]
    return jnp.transpose(w.reshape(k, N_DEV, -1), (1, 0, 2))


def _swap_halves(x):
    half = x.shape[-1] // 2
    return jnp.concatenate([x[..., half:], x[..., :half]], axis=-1)


def _zeros_like_cols(x, n):
    return jnp.zeros(x.shape[:-1] + (n,), x.dtype)


def _w_in_internal(w):
    d = w.shape[0]
    kpe = w[:, 640:672]
    z64, z32 = jnp.zeros((d, 64), w.dtype), jnp.zeros((d, 32), w.dtype)
    return jnp.concatenate(
        [w[:, 0:640], z64, kpe, z32, z64, _swap_halves(kpe), z32, jnp.zeros((d, Z_HQ - Z_KPESW - LANE), w.dtype),
         w[:, 672:2720], w[:, 2720:4768]], axis=1)


def _w_in_grad(g):
    kpe = g[:, Z_KPE + 64:Z_KPE + 96] + _swap_halves(g[:, Z_KPESW + 64:Z_KPESW + 96])
    return jnp.concatenate([g[:, 0:640], kpe, g[:, Z_HQ:Z_W]], axis=1)


def _w_uq_internal(w):
    k = w.shape[0]
    w3 = w.reshape(k, MLA_HEADS, QK_NOPE + QK_ROPE)
    nope, rope = w3[..., :QK_NOPE], w3[..., QK_NOPE:]
    plain = jnp.concatenate([nope, rope, _zeros_like_cols(rope, 32)], axis=-1).reshape(k, -1)
    swapped = jnp.concatenate([_zeros_like_cols(nope, 64), _swap_halves(rope), _zeros_like_cols(rope, 32)], axis=-1).reshape(k, -1)
    return jnp.concatenate([plain, swapped], axis=1)


def _w_uq_grad(g):
    k = g.shape[0]
    half = MLA_HEADS * LANE
    g1, g2 = g[:, :half].reshape(k, MLA_HEADS, LANE), g[:, half:].reshape(k, MLA_HEADS, LANE)
    rope = g1[..., 64:96] + _swap_halves(g2[..., 64:96])
    return jnp.concatenate([g1[..., :64], rope], axis=-1).reshape(k, -1)


def _w_ukv_internal(w):
    k = w.shape[0]
    w3 = w.reshape(k, MLA_HEADS, QK_NOPE + V_HEAD)
    kn, vv = w3[..., :QK_NOPE], w3[..., QK_NOPE:]
    z = _zeros_like_cols(kn, 64)
    return jnp.concatenate([kn, z, vv, z], axis=-1).reshape(k, -1)


def _w_ukv_grad(g):
    k = g.shape[0]
    g3 = g.reshape(k, MLA_HEADS, 2 * LANE)
    return jnp.concatenate([g3[..., 0:64], g3[..., LANE:LANE + 64]], axis=-1).reshape(k, -1)


def _w_pa_internal(w):
    n = w.shape[1]
    w3 = w.reshape(MLA_HEADS, V_HEAD, n)
    return jnp.concatenate([w3, jnp.zeros_like(w3)], axis=1).reshape(-1, n)


def _w_pa_grad(g):
    n = g.shape[1]
    return g.reshape(MLA_HEADS, 2 * V_HEAD, n)[:, :V_HEAD].reshape(-1, n)


def _rope_tables(t):
    half = QK_ROPE // 2
    inv = ROPE_THETA ** (-jnp.arange(half, dtype=F32) / half)
    ang = jnp.arange(t, dtype=F32)[:, None] * inv[None, :]
    cos, sin = jnp.cos(ang), jnp.sin(ang)
    one, zero = jnp.ones((t, 64), F32), jnp.zeros((t, 64), F32)
    z32 = jnp.zeros((t, 32), F32)
    cq = jnp.concatenate([one, cos, cos, z32], axis=1)
    ck = jnp.concatenate([zero, cos, cos, z32], axis=1)
    sq = jnp.concatenate([zero, -sin, sin, z32], axis=1)
    return cq, ck, sq


def _ffn_fwd(tag, h, nw, w_gu, w_down):
    t, d = h.shape
    dff = w_down.shape[0]
    tm, tn = _row_tile(t), _tile(dff, 1408)
    xn = _rms_fwd(tag + "_norm", h, nw)

    def act_fn(accs, _):
        g, u = accs
        return g, u, g * _sigmoid(g) * u

    spec = _tile_spec(tm, tn)
    g, u, act = _mm(tag + "_gu", [xn], [(0, w_gu, 0), (0, w_gu, dff // tn)], [],
                    [((t, dff), BF16) + spec] * 3, act_fn, tm=tm, tn=tn, n=dff)
    tn2 = _tile(d, 1024)
    h2, = _mm(tag + "_down", [act], [(0, w_down, 0)], [(h,) + _tile_spec(tm, tn2)],
              [((t, d), F32) + _tile_spec(tm, tn2)], lambda accs, ex: (ex[0] + 0.5 * accs[0],), tm=tm, tn=tn2, n=d)
    return h2, (h, xn, g, u, act)


def _ffn_bwd(tag, dh2, saved, nw, w_gu_t, w_down_t):
    h, xn, g, u, act = saved
    t, d = h.shape
    dff = act.shape[1]
    tm, tn = _row_tile(t), _tile(dff, 1408)

    def dact_fn(accs, ex):
        gg, uu = ex[0].astype(F32), ex[1].astype(F32)
        da = 0.5 * accs[0]
        sg = _sigmoid(gg)
        return da * uu * (sg * (1.0 + gg * (1.0 - sg))), da * (gg * sg)

    spec = _tile_spec(tm, tn)
    dg, du = _mm(tag + "_dact", [dh2], [(0, w_down_t, 0)], [(g,) + spec, (u,) + spec],
                 [((t, dff), BF16) + spec] * 2, dact_fn, tm=tm, tn=tn, n=dff)
    dw_down = _mm_tn(tag + "_dwdown", act, dh2, alpha=0.5, out_dtype=BF16)
    tn2 = _tile(d, 512)
    dxn, = _mm(tag + "_dxn", [dg, du], [(0, w_gu_t[:dff], 0), (1, w_gu_t[dff:], 0)], [],
               [((t, d), F32) + _tile_spec(tm, tn2)], lambda accs, _: (accs[0] + accs[1],), tm=tm, tn=tn2, n=d)
    dw_gu = jnp.concatenate([_mm_tn(tag + "_dwg", xn, dg, alpha=1.0, out_dtype=BF16),
                             _mm_tn(tag + "_dwu", xn, du, alpha=1.0, out_dtype=BF16)], axis=1)
    dh, dnw = _rms_bwd(tag + "_dnorm", h, nw, dxn, dh2)
    return dh, dnw, dw_gu, dw_down


def _kv_pattern(kr):
    z = jnp.zeros_like(kr)
    return jnp.concatenate([kr, z] * MLA_HEADS, axis=1)


def _mix_fwd(h, p, tabs, lb, call):
    t, d = h.shape
    tm = _row_tile(t)
    cq_t, ck_t, sq_t = tabs
    u = _rms_fwd("mix_norm", h, p["mix_norm"])
    tnz = _tile(Z_W, 1024)
    z, = _mm("mix_in", [u], [(0, p["w_in"], 0)], [], [((t, Z_W), F32) + _tile_spec(tm, tnz)], lambda a, _: (a[0],),
             tm=tm, tn=tnz, n=Z_W)
    cqn, ckvn, krot = _mla_prep(z, p["q_norm"], p["kv_norm"], ck_t, sq_t)
    wq = MLA_HEADS * LANE
    lane_rows = lambda i, j: (i, 0)

    def q_fn(accs, ex):
        c8 = jnp.concatenate([ex[0]] * MLA_HEADS, axis=1)
        s8 = jnp.concatenate([ex[1]] * MLA_HEADS, axis=1)
        return (accs[0] * c8 + accs[1] * s8,)

    q, = _mm("mla_q", [cqn], [(0, p["w_uq"], 0), (0, p["w_uq"], 1)],
             [(cq_t, (tm, LANE), lane_rows), (sq_t, (tm, LANE), lane_rows)],
             [((t, wq), BF16) + _tile_spec(tm, wq)], q_fn, tm=tm, tn=wq, n=wq)
    kv, = _mm("mla_kv", [ckvn], [(0, p["w_ukv"], 0)], [(krot, (tm, LANE), lane_rows)],
              [((t, 2 * wq), BF16) + _tile_spec(tm, 2 * wq)], lambda a, ex: (a[0] + _kv_pattern(ex[0]),),
              tm=tm, tn=2 * wq, n=2 * wq)
    o_a, lse = _attn_fwd(q, kv)
    o_b, st_saved = _hgrn_fwd(z, lb, p["hg_norm"], call)
    tn = _tile(d, 512)

    def merge_fn(accs, ex):
        ya, yb = accs
        return ya, yb, _sigmoid(ex[0]) * ya + _sigmoid(ex[1]) * yb

    spec = _tile_spec(tm, tn)
    ya, yb, merged = _mm("mix_merge", [o_a, o_b], [(0, p["w_pa"], 0), (1, p["w_pr"], 0)],
                         [(z,) + _tile_spec(tm, tn, Z_GA // tn), (z,) + _tile_spec(tm, tn, Z_GB // tn)],
                         [((t, d), BF16) + spec] * 3, merge_fn, tm=tm, tn=tn, n=d)
    tn2 = _tile(d, 1024)
    h2, = _mm("mix_out", [merged], [(0, p["w_out"], 0)], [(h,) + _tile_spec(tm, tn2)],
              [((t, d), F32) + _tile_spec(tm, tn2)], lambda a, ex: (ex[0] + a[0],), tm=tm, tn=tn2, n=d)
    return h2, (h, u, z, cqn, ckvn, q, kv, o_a, lse, o_b, st_saved, ya, yb, merged)


def _mix_bwd(dh2, saved, p, tabs, lb, call, call_t):
    h, u, z, cqn, ckvn, q, kv, o_a, lse, o_b, st_saved, ya, yb, merged = saved
    t, d = h.shape
    tm = _row_tile(t)
    cq_t, ck_t, sq_t = tabs
    tn = _tile(d, 512)
    spec = _tile_spec(tm, tn)

    def dmerge_fn(accs, ex):
        dm = accs[0]
        yav, ybv = ex[0].astype(F32), ex[1].astype(F32)
        sa, sb = _sigmoid(ex[2]), _sigmoid(ex[3])
        return dm * sa, dm * sb, dm * yav * sa * (1.0 - sa), dm * ybv * sb * (1.0 - sb)

    dya, dyb, dga, dgb = _mm("mix_dmerge", [dh2], [(0, p["w_out_t"], 0)],
                             [(ya,) + spec, (yb,) + spec, (z,) + _tile_spec(tm, tn, Z_GA // tn),
                              (z,) + _tile_spec(tm, tn, Z_GB // tn)],
                             [((t, d), BF16) + spec] * 4, dmerge_fn, tm=tm, tn=tn, n=d)
    dw_out = _mm_tn("mix_dwout", merged, dh2, alpha=1.0, out_dtype=BF16)
    wq = MLA_HEADS * LANE
    do_a, = _mm("mix_doa", [dya], [(0, p["w_pa_t"], 0)], [], [((t, wq), BF16) + _tile_spec(tm, wq)], lambda a, _: (a[0],),
                tm=tm, tn=wq, n=wq)
    wr = HG_HEADS * LANE
    do_b, = _mm("mix_dob", [dyb], [(0, p["w_pr_t"], 0)], [], [((t, wr), BF16) + _tile_spec(tm, wr)], lambda a, _: (a[0],),
                tm=tm, tn=wr, n=wr)
    dw_pa = _mm_tn("mix_dwpa", o_a, dya, alpha=1.0, out_dtype=F32)
    dw_pr = _mm_tn("mix_dwpr", o_b, dyb, alpha=1.0, out_dtype=BF16)
    delta = _attn_delta(o_a, do_a)
    dq, dkv = _attn_bwd(q, kv, do_a, lse, delta)
    dqq = _rope_bwd_q(dq, cq_t, sq_t)
    dcq, = _mm("mla_dcq", [dqq], [(0, p["w_uq_t"], 0)], [], [((t, Q_LORA), F32) + _tile_spec(tm, Q_LORA)],
               lambda a, _: (a[0],), tm=tm, tn=Q_LORA, n=Q_LORA)
    dckv, = _mm("mla_dckv", [dkv], [(0, p["w_ukv_t"], 0)], [], [((t, KV_LORA), F32) + _tile_spec(tm, KV_LORA)],
                lambda a, _: (a[0],), tm=tm, tn=KV_LORA, n=KV_LORA)
    dw_uq = _mm_tn("mla_dwuq", cqn, dqq, alpha=1.0, out_dtype=F32)
    dw_ukv = _mm_tn("mla_dwukv", ckvn, dkv, alpha=1.0, out_dtype=F32)
    dz_mla, dqn, dkvn = _mla_prep_bwd(z, p["q_norm"], p["kv_norm"], ck_t, sq_t, dcq, dckv, dkv)
    dhq, dhf, dhi, dhg, dlb, dhgn = _hgrn_bwd(z, lb, p["hg_norm"], call, call_t, st_saved, do_b)
    dz = jnp.concatenate([dz_mla, dhq, dhf, dhi, dhg, dga, dgb], axis=1)
    du, = _mm("mix_du", [dz], [(0, p["w_in_t"], 0)], [], [((t, d), F32) + spec], lambda a, _: (a[0],), tm=tm, tn=tn, n=d)
    dw_in = _mm_tn("mix_dwin", u, dz, alpha=1.0, out_dtype=F32)
    dh, dmn = _rms_bwd("mix_dnorm", h, p["mix_norm"], du, dh2)
    grads = dict(mix_norm=dmn, q_norm=dqn, kv_norm=dkvn, hg_norm=dhgn, lb=dlb, w_in=_w_in_grad(dw_in), w_uq=_w_uq_grad(dw_uq),
                 w_ukv=_w_ukv_grad(dw_ukv), w_proj_attn=_w_pa_grad(dw_pa), w_proj_rec=dw_pr, w_out=dw_out)
    return dh, grads


SHARDED = ("ffn1_w_gu", "ffn1_w_down", "w_in", "w_uq", "w_ukv", "w_proj_attn", "w_proj_rec", "w_out", "ffn2_w_gu", "ffn2_w_down")
ROW_SHARDED = ("ffn1_w_down", "w_out", "ffn2_w_down")
SMALL = ("ffn1_norm", "mix_norm", "q_norm", "kv_norm", "hg_lb_raw", "hg_norm", "ffn2_norm", "final_norm")
WEIGHTS = ("meta_tokens", "ffn1_norm", "ffn1_w_gu", "ffn1_w_down", "mix_norm", "w_in", "q_norm", "kv_norm", "w_uq", "w_ukv",
           "hg_lb_raw", "hg_norm", "w_proj_attn", "w_proj_rec", "w_out", "ffn2_norm", "ffn2_w_gu", "ffn2_w_down", "final_norm")


def _pack_small(vals):
    flat = jnp.concatenate([vals[n].reshape(-1) for n in SMALL])
    return flat.reshape(-1, LANE)


def _unpack_small(packed, like):
    flat = packed.reshape(-1)
    out, off = {}, 0
    for n in SMALL:
        size = math.prod(like[n].shape)
        out[n] = flat[off:off + size].reshape(like[n].shape)
        off += size
    return out


def kernel(x, meta_tokens, ffn1_norm, ffn1_w_gu, ffn1_w_down, mix_norm, w_in, q_norm, kv_norm, w_uq, w_ukv, hg_lb_raw, hg_norm, w_proj_attn, w_proj_rec, w_out, ffn2_norm, ffn2_w_gu, ffn2_w_down, final_norm, loss_target, m_meta_tokens, m_ffn1_norm, m_ffn1_w_gu, m_ffn1_w_down, m_mix_norm, m_w_in, m_q_norm, m_kv_norm, m_w_uq, m_w_ukv, m_hg_lb_raw, m_hg_norm, m_w_proj_attn, m_w_proj_rec, m_w_out, m_ffn2_norm, m_ffn2_w_gu, m_ffn2_w_down, m_final_norm, v_meta_tokens, v_ffn1_norm, v_ffn1_w_gu, v_ffn1_w_down, v_mix_norm, v_w_in, v_q_norm, v_kv_norm, v_w_uq, v_w_ukv, v_hg_lb_raw, v_hg_norm, v_w_proj_attn, v_w_proj_rec, v_w_out, v_ffn2_norm, v_ffn2_w_gu, v_ffn2_w_down, v_final_norm):
    w = dict(meta_tokens=meta_tokens, ffn1_norm=ffn1_norm, ffn1_w_gu=ffn1_w_gu, ffn1_w_down=ffn1_w_down, mix_norm=mix_norm,
             w_in=w_in, q_norm=q_norm, kv_norm=kv_norm, w_uq=w_uq, w_ukv=w_ukv, hg_lb_raw=hg_lb_raw, hg_norm=hg_norm,
             w_proj_attn=w_proj_attn, w_proj_rec=w_proj_rec, w_out=w_out, ffn2_norm=ffn2_norm, ffn2_w_gu=ffn2_w_gu,
             ffn2_w_down=ffn2_w_down, final_norm=final_norm)
    mom = dict(meta_tokens=m_meta_tokens, ffn1_norm=m_ffn1_norm, ffn1_w_gu=m_ffn1_w_gu, ffn1_w_down=m_ffn1_w_down,
               mix_norm=m_mix_norm, w_in=m_w_in, q_norm=m_q_norm, kv_norm=m_kv_norm, w_uq=m_w_uq, w_ukv=m_w_ukv,
               hg_lb_raw=m_hg_lb_raw, hg_norm=m_hg_norm, w_proj_attn=m_w_proj_attn, w_proj_rec=m_w_proj_rec, w_out=m_w_out,
               ffn2_norm=m_ffn2_norm, ffn2_w_gu=m_ffn2_w_gu, ffn2_w_down=m_ffn2_w_down, final_norm=m_final_norm)
    var = dict(meta_tokens=v_meta_tokens, ffn1_norm=v_ffn1_norm, ffn1_w_gu=v_ffn1_w_gu, ffn1_w_down=v_ffn1_w_down,
               mix_norm=v_mix_norm, w_in=v_w_in, q_norm=v_q_norm, kv_norm=v_kv_norm, w_uq=v_w_uq, w_ukv=v_w_ukv,
               hg_lb_raw=v_hg_lb_raw, hg_norm=v_hg_norm, w_proj_attn=v_w_proj_attn, w_proj_rec=v_w_proj_rec, w_out=v_w_out,
               ffn2_norm=v_ffn2_norm, ffn2_w_gu=v_ffn2_w_gu, ffn2_w_down=v_ffn2_w_down, final_norm=v_final_norm)
    nl = ffn1_norm.shape[0]
    seq, d = x.shape[1], x.shape[2]
    n_real = N_META + seq
    t = -(-n_real // ROW_ALIGN) * ROW_ALIGN
    me = 4 * lax.axis_index("x") + 2 * lax.axis_index("y") + lax.axis_index("c")

    gathered = _exchange("gather_weights", [w[n].astype(BF16) for n in SHARDED] + [meta_tokens], gather=True)
    full = dict(zip(SHARDED, gathered[:-1]))
    meta_full = _cols_full(gathered[-1])

    def layer_params(l):
        def mat(n):
            g = full[n][:, l]
            return g.reshape(-1, g.shape[-1]) if n in ROW_SHARDED else _cols_full(g)

        p = {}
        for tag in ("ffn1", "ffn2"):
            p[tag + "_w_gu"] = mat(tag + "_w_gu")
            p[tag + "_w_down"] = mat(tag + "_w_down")
            p[tag + "_w_gu_t"] = p[tag + "_w_gu"].T
            p[tag + "_w_down_t"] = p[tag + "_w_down"].T
            p[tag + "_norm"] = w[tag + "_norm"][l:l + 1]
        p["w_in"] = _w_in_internal(mat("w_in"))
        p["w_uq"] = _w_uq_internal(mat("w_uq"))
        p["w_ukv"] = _w_ukv_internal(mat("w_ukv"))
        p["w_pa"] = _w_pa_internal(mat("w_proj_attn"))
        p["w_pr"] = mat("w_proj_rec")
        p["w_out"] = mat("w_out")
        for n in ("w_in", "w_uq", "w_ukv", "w_pa", "w_pr", "w_out"):
            p[n + "_t"] = p[n].T
        for n in ("mix_norm", "q_norm", "kv_norm", "hg_norm"):
            p[n] = w[n][l:l + 1]
        return p

    params = [layer_params(l) for l in range(nl)]
    tabs = _rope_tables(t)
    call, call_t = _hg_tables()
    lbs = _lb_fwd(hg_lb_raw)

    pad = jnp.zeros((t - n_real, d), F32)
    h = jnp.concatenate([meta_full, x[0], pad], axis=0)
    tgt = jnp.concatenate([jnp.zeros((N_META, d), F32), loss_target[0], pad], axis=0)
    saved = []
    for l in range(nl):
        p = params[l]
        h, s1 = _ffn_fwd("ffn1", h, p["ffn1_norm"], p["ffn1_w_gu"], p["ffn1_w_down"])
        h, s2 = _mix_fwd(h, p, tabs, lbs[l:l + 1], call)
        h, s3 = _ffn_fwd("ffn2", h, p["ffn2_norm"], p["ffn2_w_gu"], p["ffn2_w_down"])
        saved.append((s1, s2, s3))
    dh, d_final, loss_part = _loss_head(h, final_norm.reshape(1, d), tgt, n_real)
    loss = lax.psum(loss_part[0, 0], ("x", "y", "c"))

    per_layer = []
    for l in reversed(range(nl)):
        p = params[l]
        s1, s2, s3 = saved[l]
        dh, dn2, dgu2, ddown2 = _ffn_bwd("ffn2", dh, s3, p["ffn2_norm"], p["ffn2_w_gu_t"], p["ffn2_w_down_t"])
        dh, gm = _mix_bwd(dh, s2, p, tabs, lbs[l:l + 1], call, call_t)
        dh, dn1, dgu1, ddown1 = _ffn_bwd("ffn1", dh, s1, p["ffn1_norm"], p["ffn1_w_gu_t"], p["ffn1_w_down_t"])
        gm.update(ffn1_norm=dn1, ffn2_norm=dn2, ffn1_w_gu=dgu1, ffn2_w_gu=dgu2, ffn1_w_down=ddown1, ffn2_w_down=ddown2)
        per_layer.append(gm)
    per_layer.reverse()
    grad_x = dh[N_META:n_real][None]

    def shards(n):
        per = []
        for gm in per_layer:
            g = gm[n].astype(BF16)
            per.append(g.reshape(N_DEV, -1, g.shape[-1]) if n in ROW_SHARDED else _cols_shards(g))
        return jnp.stack(per, axis=1)

    slots = _exchange("scatter_grads", [shards(n) for n in SHARDED], gather=False)
    grads, delta, new_m, new_v = {}, {}, {}, {}
    for n, s in zip(SHARDED, slots):
        grads[n], delta[n], new_m[n], new_v[n] = _adam_sharded("adam_" + n, s, w[n], mom[n], var[n])

    small = {n: jnp.concatenate([gm[n] for gm in per_layer], axis=0) for n in SMALL if n not in ("hg_lb_raw", "final_norm")}
    small["hg_lb_raw"] = _lb_bwd(hg_lb_raw, jnp.concatenate([gm["lb"] for gm in per_layer], axis=0))
    small["final_norm"] = d_final
    packed = jnp.concatenate([_pack_small(small), dh[:N_META].reshape(-1, LANE)], axis=0)
    summed = _sum_replicated(_exchange("gather_small", [packed], gather=True)[0])
    n_small = packed.shape[0] - N_META * d // LANE
    sd, sm, sv = _adam_small("adam_small", summed[:n_small], _pack_small(w), _pack_small(mom), _pack_small(var))
    grads.update(_unpack_small(summed[:n_small], w))
    delta.update(_unpack_small(sd, w))
    new_m.update(_unpack_small(sm, w))
    new_v.update(_unpack_small(sv, w))
    dmeta = lax.dynamic_slice_in_dim(summed[n_small:].reshape(N_META, d), me * (d // N_DEV), d // N_DEV, axis=1)
    grads["meta_tokens"] = dmeta
    delta["meta_tokens"], new_m["meta_tokens"], new_v["meta_tokens"] = _adam_small(
        "adam_meta", dmeta, meta_tokens, m_meta_tokens, v_meta_tokens)

    return (loss, grad_x, *[grads[n] for n in WEIGHTS], *[delta[n] for n in WEIGHTS], *[new_m[n] for n in WEIGHTS],
            *[new_v[n] for n in WEIGHTS])
```

```python
import functools
import math

import jax
import jax.numpy as jnp
from jax import lax
from jax.experimental import pallas as pl
from jax.experimental.pallas import tpu as pltpu

F32 = jnp.float32
BF16 = jnp.bfloat16

N_DEV = 8
N_META = 16
MLA_HEADS = 8
Q_LORA = 384
KV_LORA = 256
QK_NOPE = 64
QK_ROPE = 32
V_HEAD = 64
ROPE_THETA = 10000.0
HG_HEADS = 4
HG_DIM = 128
EPS = 1e-6
NEG_BIG = -1e30
F_MIN = 1e-20
ADAM_LR = 0.001
ADAM_B1 = 0.9
ADAM_B2 = 0.999
ADAM_EPS = 1e-08
ADAM_WD = 0.01
ADAM_STEP = 10

LANE = 128
ROW_ALIGN = 256
ATTN_BLOCK = 256
HG_CHUNK = 128
HG_CHUNKS_PER_STEP = 2
HG_LEVELS = (64, 32, 16, 8, 4, 2, 1)
VMEM_LIMIT = 48 * 1024 * 1024

Z_CQ, Z_CKV, Z_KPE, Z_KPESW, Z_HQ, Z_HF, Z_HI, Z_HG, Z_GA, Z_GB, Z_W = 0, 384, 640, 768, 1024, 1536, 2048, 2560, 3072, 4096, 5120
ATTN_SCALE = float((QK_NOPE + QK_ROPE) ** -0.5)
LOG2E = 1.4426950408889634
ATTN_C2 = ATTN_SCALE * LOG2E
HG_SCALE = float(HG_DIM ** -0.5)


def _cparams(sem):
    return pltpu.CompilerParams(dimension_semantics=sem, vmem_limit_bytes=VMEM_LIMIT)


def _tile(n, cap):
    if n <= cap:
        return n
    best = None
    for t in range(LANE, cap + 1, LANE):
        if n % t == 0:
            best = t
    assert best is not None, (n, cap)
    return best


def _row_tile(m):
    for t in (384, 256, 128):
        if m % t == 0:
            return t
    raise ValueError(m)


def _bf(x):
    return x.astype(BF16)


def _dot(a, b):
    return jnp.dot(a, b, preferred_element_type=F32)


def _dot_nt(a, b):
    return lax.dot_general(a, b, (((1,), (1,)), ((), ())), preferred_element_type=F32)


def _dot_tn(a, b):
    return lax.dot_general(a, b, (((0,), (0,)), ((), ())), preferred_element_type=F32)


def _sigmoid(x):
    return 1.0 / (1.0 + jnp.exp(-x))


def _mm(name, a_list, pairs, extras, outs, fn, *, tm, tn, n):
    m = a_list[0].shape[0]
    na, nb, ne, no = len(a_list), len(pairs), len(extras), len(outs)

    def body(*refs):
        a_refs = refs[:na]
        b_refs = refs[na:na + nb]
        e_refs = refs[na + nb:na + nb + ne]
        o_refs = refs[na + nb + ne:]
        a_vals = [_bf(r[...]) for r in a_refs]
        accs = [_dot(a_vals[ai], b_refs[k][...]) for k, (ai, _, _) in enumerate(pairs)]
        res = fn(accs, [r[...] for r in e_refs])
        for r, v in zip(o_refs, res):
            r[...] = v.astype(r.dtype)

    in_specs = [pl.BlockSpec((tm, a.shape[1]), lambda i, j: (i, 0)) for a in a_list]
    for _, b, off in pairs:
        in_specs.append(pl.BlockSpec((b.shape[0], tn), functools.partial(lambda i, j, off: (0, j + off), off=off)))
    in_specs += [pl.BlockSpec(bs, im) for _, bs, im in extras]
    return pl.pallas_call(
        body, name=name, grid=(m // tm, n // tn),
        in_specs=in_specs,
        out_specs=[pl.BlockSpec(bs, im) for _, _, bs, im in outs],
        out_shape=[jax.ShapeDtypeStruct(s, d) for s, d, _, _ in outs],
        compiler_params=_cparams(("parallel", "arbitrary")),
    )(*a_list, *[b for _, b, _ in pairs], *[e for e, _, _ in extras])


def _tile_spec(tm, tn, col_off=0):
    return (tm, tn), functools.partial(lambda i, j, off: (i, j + off), off=col_off)


def _mm_tn(name, a, b, *, alpha, out_dtype):
    t, k = a.shape
    n = b.shape[1]
    tk, tn, tt = _tile(k, 1408), _tile(n, 1408), _row_tile(t)
    nt = t // tt

    def body(a_ref, b_ref, o_ref, acc_ref):
        s = pl.program_id(2)

        @pl.when(s == 0)
        def _():
            acc_ref[...] = jnp.zeros_like(acc_ref)

        acc_ref[...] += _dot_tn(_bf(a_ref[...]), _bf(b_ref[...]))

        @pl.when(s == nt - 1)
        def _():
            o_ref[...] = (alpha * acc_ref[...]).astype(o_ref.dtype)

    return pl.pallas_call(
        body, name=name, grid=(k // tk, n // tn, nt),
        in_specs=[pl.BlockSpec((tt, tk), lambda i, j, s: (s, i)), pl.BlockSpec((tt, tn), lambda i, j, s: (s, j))],
        out_specs=pl.BlockSpec((tk, tn), lambda i, j, s: (i, j)),
        out_shape=jax.ShapeDtypeStruct((k, n), out_dtype),
        scratch_shapes=[pltpu.VMEM((tk, tn), F32)],
        compiler_params=_cparams(("parallel", "parallel", "arbitrary")),
    )(a, b)


def _rms_parts(x):
    r = lax.rsqrt(jnp.mean(x * x, axis=-1, keepdims=True) + EPS)
    return r, x * r


def _rms_bwd_math(x, w, dxn):
    r, xhat = _rms_parts(x)
    t = dxn * w
    dx = r * (t - xhat * jnp.mean(t * xhat, axis=-1, keepdims=True))
    dw = jnp.sum(dxn * xhat, axis=0, keepdims=True)
    return dx, dw


def _rms_fwd(name, h, w):
    t, d = h.shape
    tm = _row_tile(t)

    def body(h_ref, w_ref, o_ref):
        _, xhat = _rms_parts(h_ref[...])
        o_ref[...] = (xhat * w_ref[...]).astype(o_ref.dtype)

    return pl.pallas_call(
        body, name=name, grid=(t // tm,),
        in_specs=[pl.BlockSpec((tm, d), lambda i: (i, 0)), pl.BlockSpec((1, d), lambda i: (0, 0))],
        out_specs=pl.BlockSpec((tm, d), lambda i: (i, 0)),
        out_shape=jax.ShapeDtypeStruct((t, d), BF16),
        compiler_params=_cparams(("parallel",)),
    )(h, w)


def _rms_bwd(name, h, w, dxn, dh_in):
    t, d = h.shape
    tm = _row_tile(t)

    def body(h_ref, w_ref, dxn_ref, dh_ref, o_ref, dw_ref):
        dx, dw = _rms_bwd_math(h_ref[...], w_ref[...], dxn_ref[...])
        o_ref[...] = dh_ref[...] + dx

        @pl.when(pl.program_id(0) == 0)
        def _():
            dw_ref[...] = jnp.zeros_like(dw_ref)

        dw_ref[...] += dw

    row = pl.BlockSpec((tm, d), lambda i: (i, 0))
    vec = pl.BlockSpec((1, d), lambda i: (0, 0))
    return pl.pallas_call(
        body, name=name, grid=(t // tm,),
        in_specs=[row, vec, row, row],
        out_specs=[row, vec],
        out_shape=[jax.ShapeDtypeStruct((t, d), F32), jax.ShapeDtypeStruct((1, d), F32)],
        compiler_params=_cparams(("arbitrary",)),
    )(h, w, dxn, dh_in)


def _loss_head(h, w, tgt, n_real):
    t, d = h.shape
    tm = _row_tile(t)

    def body(h_ref, w_ref, t_ref, dh_ref, dw_ref, loss_ref):
        i = pl.program_id(0)
        x = h_ref[...]
        wv = w_ref[...]
        _, xhat = _rms_parts(x)
        rows = i * tm + lax.broadcasted_iota(jnp.int32, (tm, 1), 0)
        valid = (rows >= N_META) & (rows < n_real)
        e = jnp.where(valid, xhat * wv - t_ref[...], 0.0)
        dx, dw = _rms_bwd_math(x, wv, e * (1.0 / d))
        dh_ref[...] = dx

        @pl.when(i == 0)
        def _():
            dw_ref[...] = jnp.zeros_like(dw_ref)
            loss_ref[...] = jnp.zeros_like(loss_ref)

        dw_ref[...] += dw
        loss_ref[...] += (0.5 / d) * jnp.sum(jnp.sum(e * e, axis=-1, keepdims=True), axis=0, keepdims=True)

    row = pl.BlockSpec((tm, d), lambda i: (i, 0))
    vec = pl.BlockSpec((1, d), lambda i: (0, 0))
    return pl.pallas_call(
        body, name="loss_head", grid=(t // tm,),
        in_specs=[row, vec, row],
        out_specs=[row, vec, pl.BlockSpec((1, 1), lambda i: (0, 0))],
        out_shape=[jax.ShapeDtypeStruct((t, d), F32), jax.ShapeDtypeStruct((1, d), F32), jax.ShapeDtypeStruct((1, 1), F32)],
        compiler_params=_cparams(("arbitrary",)),
    )(h, w, tgt)


def _mla_prep(z, qn_w, kvn_w, ck, sk):
    t = z.shape[0]
    tm = _row_tile(t)

    def body(z_ref, qw_ref, kw_ref, ck_ref, sk_ref, cq_ref, ckv_ref, kr_ref):
        zz = z_ref[...]
        _, qhat = _rms_parts(zz[:, Z_CQ:Z_CKV])
        _, khat = _rms_parts(zz[:, Z_CKV:Z_KPE])
        cq_ref[...] = (qhat * qw_ref[...]).astype(BF16)
        ckv_ref[...] = (khat * kw_ref[...]).astype(BF16)
        kr_ref[...] = zz[:, Z_KPE:Z_KPESW] * ck_ref[...] + zz[:, Z_KPESW:Z_KPESW + LANE] * sk_ref[...]

    def rows(wd):
        return pl.BlockSpec((tm, wd), lambda i: (i, 0))

    def vec(wd):
        return pl.BlockSpec((1, wd), lambda i: (0, 0))

    return pl.pallas_call(
        body, name="mla_prep", grid=(t // tm,),
        in_specs=[rows(Z_HQ), vec(Q_LORA), vec(KV_LORA), rows(LANE), rows(LANE)],
        out_specs=[rows(Q_LORA), rows(KV_LORA), rows(LANE)],
        out_shape=[jax.ShapeDtypeStruct((t, Q_LORA), BF16), jax.ShapeDtypeStruct((t, KV_LORA), BF16),
                   jax.ShapeDtypeStruct((t, LANE), F32)],
        compiler_params=_cparams(("parallel",)),
    )(z, qn_w, kvn_w, ck, sk)


def _mla_prep_bwd(z, qn_w, kvn_w, ck, sk, dcq, dckv, dkv):
    t = z.shape[0]
    tm = _row_tile(t)

    def body(z_ref, qw_ref, kw_ref, ck_ref, sk_ref, dcq_ref, dckv_ref, dkv_ref, dz_ref, dqw_ref, dkw_ref):
        zz = z_ref[...]
        dq, dqw = _rms_bwd_math(zz[:, Z_CQ:Z_CKV], qw_ref[...], dcq_ref[...])
        dk, dkw = _rms_bwd_math(zz[:, Z_CKV:Z_KPE], kw_ref[...], dckv_ref[...])
        dkv_v = dkv_ref[...]
        dkr = jnp.zeros((tm, LANE), F32)
        for hd in range(MLA_HEADS):
            dkr = dkr + dkv_v[:, 2 * LANE * hd:2 * LANE * hd + LANE].astype(F32)
        dz_ref[...] = jnp.concatenate(
            [dq, dk, dkr * ck_ref[...], dkr * sk_ref[...], jnp.zeros((tm, Z_HQ - Z_KPESW - LANE), F32)], axis=1
        ).astype(BF16)

        @pl.when(pl.program_id(0) == 0)
        def _():
            dqw_ref[...] = jnp.zeros_like(dqw_ref)
            dkw_ref[...] = jnp.zeros_like(dkw_ref)

        dqw_ref[...] += dqw
        dkw_ref[...] += dkw

    def rows(wd):
        return pl.BlockSpec((tm, wd), lambda i: (i, 0))

    def vec(wd):
        return pl.BlockSpec((1, wd), lambda i: (0, 0))

    return pl.pallas_call(
        body, name="mla_prep_bwd", grid=(t // tm,),
        in_specs=[rows(Z_HQ), vec(Q_LORA), vec(KV_LORA), rows(LANE), rows(LANE), rows(Q_LORA), rows(KV_LORA),
                  rows(2 * LANE * MLA_HEADS)],
        out_specs=[rows(Z_HQ), vec(Q_LORA), vec(KV_LORA)],
        out_shape=[jax.ShapeDtypeStruct((t, Z_HQ), BF16), jax.ShapeDtypeStruct((1, Q_LORA), F32),
                   jax.ShapeDtypeStruct((1, KV_LORA), F32)],
        compiler_params=_cparams(("arbitrary",)),
    )(z, qn_w, kvn_w, ck, sk, dcq, dckv, dkv)


def _rope_bwd_q(dq, cq, sq):
    t, wq = dq.shape
    tm = _row_tile(t)

    def body(dq_ref, c_ref, s_ref, o_ref):
        d = dq_ref[...] * ATTN_SCALE
        c8 = jnp.concatenate([c_ref[...]] * MLA_HEADS, axis=1)
        s8 = jnp.concatenate([s_ref[...]] * MLA_HEADS, axis=1)
        o_ref[...] = jnp.concatenate([d * c8, d * s8], axis=1).astype(BF16)

    return pl.pallas_call(
        body, name="rope_bwd_q", grid=(t // tm,),
        in_specs=[pl.BlockSpec((tm, wq), lambda i: (i, 0)), pl.BlockSpec((tm, LANE), lambda i: (i, 0)),
                  pl.BlockSpec((tm, LANE), lambda i: (i, 0))],
        out_specs=pl.BlockSpec((tm, 2 * wq), lambda i: (i, 0)),
        out_shape=jax.ShapeDtypeStruct((t, 2 * wq), BF16),
        compiler_params=_cparams(("parallel",)),
    )(dq, cq, sq)


def _row_vector(col):
    return jnp.broadcast_to(col, (col.shape[0], LANE)).T[0:8, :]


def _wide_block(t):
    for b in (768, 512, 256):
        if t % b == 0:
            return b
    raise ValueError(t)


def _attn_fwd(q, kv):
    t = q.shape[0]
    bq, bk = ATTN_BLOCK, _wide_block(t)
    nq = t // bq

    def body(q_ref, k_ref, v_ref, o_ref, lse_ref):
        i = pl.program_id(1)
        qv = q_ref[...]
        qpos = i * bq + lax.broadcasted_iota(jnp.int32, (bq, 1), 0)
        n = (bq * (i + 1) + bk - 1) // bk

        def block(j, carry, masked):
            m, l, acc = carry
            rows = pl.ds(pl.multiple_of(j * bk, bk), bk)
            s = _dot_nt(qv, k_ref[rows, :])
            if masked:
                kpos = j * bk + lax.broadcasted_iota(jnp.int32, (1, bk), 1)
                s = jnp.where(kpos <= qpos, s, NEG_BIG)
            m_new = jnp.maximum(m, jnp.max(s, axis=1, keepdims=True))
            p = jnp.exp2((s - m_new) * ATTN_C2)
            a = jnp.exp2((m - m_new) * ATTN_C2)
            return m_new, a * l + jnp.sum(p, axis=1, keepdims=True), a * acc + _dot(_bf(p), v_ref[rows, :])

        init = (jnp.full((bq, 1), NEG_BIG, F32), jnp.zeros((bq, 1), F32), jnp.zeros((bq, LANE), F32))
        carry = lax.fori_loop(0, n - 1, functools.partial(block, masked=False), init)
        m, l, acc = block(n - 1, carry, True)
        o_ref[...] = (acc / l).astype(o_ref.dtype)
        lse_ref[0] = _row_vector(m * ATTN_C2 + jnp.log(l) * LOG2E)

    return pl.pallas_call(
        body, name="attn_fwd", grid=(MLA_HEADS, nq),
        in_specs=[pl.BlockSpec((bq, LANE), lambda h, i: (i, h)),
                  pl.BlockSpec((t, LANE), lambda h, i: (0, 2 * h)),
                  pl.BlockSpec((t, LANE), lambda h, i: (0, 2 * h + 1))],
        out_specs=[pl.BlockSpec((bq, LANE), lambda h, i: (i, h)),
                   pl.BlockSpec((1, 8, bq), lambda h, i: (h, 0, i))],
        out_shape=[jax.ShapeDtypeStruct((t, MLA_HEADS * LANE), BF16), jax.ShapeDtypeStruct((MLA_HEADS, 8, t), F32)],
        compiler_params=_cparams(("parallel", "arbitrary")),
    )(q, kv, kv)


def _attn_delta(o, do):
    t = o.shape[0]
    bq = ATTN_BLOCK
    nq = t // bq

    def body(o_ref, do_ref, d_ref):
        d = jnp.sum(o_ref[...].astype(F32) * do_ref[...].astype(F32), axis=1, keepdims=True)
        d_ref[0] = _row_vector(d)

    blk = pl.BlockSpec((bq, LANE), lambda h, i: (i, h))
    return pl.pallas_call(
        body, name="attn_delta", grid=(MLA_HEADS, nq),
        in_specs=[blk, blk],
        out_specs=pl.BlockSpec((1, 8, bq), lambda h, i: (h, 0, i)),
        out_shape=jax.ShapeDtypeStruct((MLA_HEADS, 8, t), F32),
        compiler_params=_cparams(("parallel", "parallel")),
    )(o, do)


def _stat_rows(s, bw):
    return s[:, 0, :].reshape(s.shape[0], -1, 1, bw)


def _attn_bwd(q, kv, do, lse, delta):
    t = q.shape[0]
    bk, bw = ATTN_BLOCK, _wide_block(t)
    nk, nw = t // bk, t // bw
    lse, delta = _stat_rows(lse, bw), _stat_rows(delta, bw)

    def body(q_ref, do_ref, k_ref, v_ref, lse_ref, dl_ref, dq_ref, dkv_ref):
        j = pl.program_id(1)

        @pl.when(j == 0)
        def _():
            dq_ref[...] = jnp.zeros_like(dq_ref)

        kb = k_ref[...]
        vb = v_ref[...]
        kpos = j * bk + lax.broadcasted_iota(jnp.int32, (bk, 1), 0)

        def block(i, carry, masked):
            dk, dv = carry
            rows = pl.ds(pl.multiple_of(i * bw, bw), bw)
            qb = q_ref[rows, :]
            dob = do_ref[rows, :]
            pt = jnp.exp2(_dot_nt(kb, qb) * ATTN_C2 - lse_ref[0, i])
            if masked:
                qpos = i * bw + lax.broadcasted_iota(jnp.int32, (1, bw), 1)
                pt = jnp.where(kpos <= qpos, pt, 0.0)
            dv = dv + _dot(_bf(pt), dob)
            dst = _bf(pt * (_dot_nt(vb, dob) - dl_ref[0, i]))
            dk = dk + _dot(dst, qb)
            dq_ref[rows, :] += _dot_tn(dst, kb)
            return dk, dv

        i0 = (bk * j) // bw
        carry = block(i0, (jnp.zeros((bk, LANE), F32), jnp.zeros((bk, LANE), F32)), True)
        dk, dv = lax.fori_loop(i0 + 1, nw, functools.partial(block, masked=False), carry)
        dkv_ref[...] = jnp.concatenate([dk * ATTN_SCALE, dv], axis=1).astype(dkv_ref.dtype)

    head_rows = lambda h, j: (0, h)
    stats = pl.BlockSpec((1, nw, 1, bw), lambda h, j: (h, 0, 0, 0))
    return pl.pallas_call(
        body, name="attn_bwd", grid=(MLA_HEADS, nk),
        in_specs=[pl.BlockSpec((t, LANE), head_rows), pl.BlockSpec((t, LANE), head_rows),
                  pl.BlockSpec((bk, LANE), lambda h, j: (j, 2 * h)), pl.BlockSpec((bk, LANE), lambda h, j: (j, 2 * h + 1)),
                  stats, stats],
        out_specs=[pl.BlockSpec((t, LANE), head_rows), pl.BlockSpec((bk, 2 * LANE), lambda h, j: (j, h))],
        out_shape=[jax.ShapeDtypeStruct((t, MLA_HEADS * LANE), F32), jax.ShapeDtypeStruct((t, 2 * MLA_HEADS * LANE), BF16)],
        compiler_params=_cparams(("parallel", "arbitrary")),
    )(q, do, kv, kv, lse, delta)


def _hg_tables():
    c = HG_CHUNK
    tri = (jnp.arange(c)[:, None] >= jnp.arange(c)[None, :]).astype(F32)
    mats = [tri]
    for m in HG_LEVELS:
        ref = (jnp.arange(c) // (2 * m)) * (2 * m) + m - 1
        mats.append((ref[:, None] >= jnp.arange(c)[None, :]).astype(F32))
    call = jnp.concatenate(mats, axis=0)
    return call, call.T


def _hg_gates(hq, hf, lb):
    sg = _sigmoid(hf)
    sn = _sigmoid(-hf)
    f = lb + (1.0 - lb) * sg
    q = hq * _sigmoid(hq)
    g = jnp.log(jnp.maximum(f, F_MIN))
    k = (1.0 - lb) * sn
    return q, k, g, f, sg, sn


def _hg_level_masks(m):
    c = HG_CHUNK
    row = lax.broadcasted_iota(jnp.int32, (c, 1), 0)
    col = lax.broadcasted_iota(jnp.int32, (1, c), 1)
    shift = (2 * m).bit_length() - 1
    up = (row & m) != 0
    same = lax.shift_right_logical(row, shift) == lax.shift_right_logical(col, shift)
    return up, same


def _hg_level_factors(b, bref, up):
    lo = jnp.logical_not(up)
    eq = jnp.where(up, jnp.exp(jnp.where(up, b - bref, 0.0)), 0.0)
    ek = jnp.where(lo, jnp.exp(jnp.where(lo, bref - b, 0.0)), 0.0)
    return eq, ek


def _hg_intra(q, k, ball):
    c = HG_CHUNK
    b = ball[0:c]
    row = lax.broadcasted_iota(jnp.int32, (c, 1), 0)
    col = lax.broadcasted_iota(jnp.int32, (1, c), 1)
    a = jnp.where(row == col, _dot_nt(_bf(q), _bf(k)), 0.0)
    parts = []
    for lv, m in enumerate(HG_LEVELS):
        up, same = _hg_level_masks(m)
        eq, ek = _hg_level_factors(b, ball[(lv + 1) * c:(lv + 2) * c], up)
        qt, kt = q * eq, k * ek
        a = a + jnp.where(same, _dot_nt(_bf(qt), _bf(kt)), 0.0)
        parts.append((eq, ek, qt, kt))
    return a, parts


def _hg_chunk_fwd(hq, hf, hi, hg, lb, nw, st, call):
    q, k, g, _, _, _ = _hg_gates(hq, hf, lb)
    ball = jnp.dot(call, g, precision=lax.Precision.HIGHEST, preferred_element_type=F32)
    b = ball[0:HG_CHUNK]
    a, _ = _hg_intra(q, k, ball)
    v16 = _bf(hi)
    o = _dot(_bf(a * HG_SCALE), v16) + _dot_nt(_bf(q * jnp.exp(b) * HG_SCALE), _bf(st))
    bl = b[HG_CHUNK - 1:HG_CHUNK]
    ke = k * jnp.exp(bl - b)
    st_new = st * jnp.exp(bl) + _dot(_bf(hi.T), _bf(ke))
    r = lax.rsqrt(jnp.mean(o * o, axis=-1, keepdims=True) + EPS)
    y = o * r * nw * (hg * _sigmoid(hg))
    return y, st_new


def _hg_chunk_bwd(hq, hf, hi, hg, lb, nw, st, call, call_t, dy, dst_new):
    c = HG_CHUNK
    q, k, g, f, sg, sn = _hg_gates(hq, hf, lb)
    ball = jnp.dot(call, g, precision=lax.Precision.HIGHEST, preferred_element_type=F32)
    b = ball[0:c]
    a, parts = _hg_intra(q, k, ball)
    v16 = _bf(hi)
    st16 = _bf(st)
    eb = jnp.exp(b)
    qe = q * eb * HG_SCALE
    a16 = _bf(a * HG_SCALE)
    o = _dot(a16, v16) + _dot_nt(_bf(qe), st16)
    bl = b[c - 1:c]
    el = jnp.exp(bl)
    x = jnp.exp(bl - b)
    ke = k * x
    r = lax.rsqrt(jnp.mean(o * o, axis=-1, keepdims=True) + EPS)
    shg = _sigmoid(hg)
    gate = hg * shg
    ohat = o * r
    don = dy * gate
    dhg = dy * ohat * nw * (shg * (1.0 + hg * (1.0 - shg)))
    dnw = jnp.sum(don * ohat, axis=0, keepdims=True)
    tt = don * nw
    do = r * (tt - ohat * jnp.mean(tt * ohat, axis=-1, keepdims=True))
    do16 = _bf(do)
    dst16 = _bf(dst_new)
    da = _dot_nt(do16, v16) * HG_SCALE
    dv = _dot(_bf(a16.astype(F32).T), do16) + _dot_nt(_bf(ke), dst16)
    dqe = _dot(do16, st16)
    dke = _dot(v16, dst16)
    dst = dst_new * el + _dot(_bf(do.T), _bf(qe))
    dbl = jnp.sum(dst_new * st, axis=0, keepdims=True) * el
    dk = dke * x
    dxa = dke * ke
    db = dqe * qe - dxa
    dbl = dbl + jnp.sum(dxa, axis=0, keepdims=True)
    dq = dqe * eb * HG_SCALE
    row = lax.broadcasted_iota(jnp.int32, (c, 1), 0)
    col = lax.broadcasted_iota(jnp.int32, (1, c), 1)
    ddiag = jnp.sum(jnp.where(row == col, da, 0.0), axis=1, keepdims=True)
    dq = dq + ddiag * k
    dk = dk + ddiag * q
    dball = []
    for (eq, ek, qt, kt), m in zip(parts, HG_LEVELS):
        _, same = _hg_level_masks(m)
        gm = jnp.where(same, da, 0.0)
        dqt = _dot(_bf(gm), _bf(kt))
        dkt = _dot(_bf(gm.T), _bf(qt))
        dq = dq + dqt * eq
        dk = dk + dkt * ek
        darg = dqt * qt - dkt * kt
        db = db + darg
        dball.append(-darg)
    db = db + jnp.where(row == c - 1, dbl, 0.0)
    dg = jnp.dot(call_t, jnp.concatenate([db] + dball, axis=0), precision=lax.Precision.HIGHEST, preferred_element_type=F32)
    shq = _sigmoid(hq)
    dhq = dq * (shq * (1.0 + hq * (1.0 - shq)))
    df = jnp.where(f > F_MIN, dg / jnp.maximum(f, F_MIN), 0.0)
    dlb = jnp.sum(df * (1.0 - sg) - dk * sn, axis=0, keepdims=True)
    dhf = df * (1.0 - lb) * sg * (1.0 - sg) - dk * (1.0 - lb) * sn * (1.0 - sn)
    return dhq, dhf, dv, dhg, dlb, dnw, dst


def _hg_col(group, h):
    return group // LANE + h


def _hgrn_fwd(z, lb, nw, call):
    t = z.shape[0]
    c, cs = HG_CHUNK, HG_CHUNKS_PER_STEP
    rows = c * cs
    nsteps = t // rows

    def body(hq_ref, hf_ref, hi_ref, hg_ref, lb_ref, nw_ref, call_ref, y_ref, sv_ref, st_ref):
        @pl.when(pl.program_id(1) == 0)
        def _():
            st_ref[...] = jnp.zeros_like(st_ref)

        for u in range(cs):
            sl = slice(u * c, (u + 1) * c)
            st = st_ref[...]
            sv_ref[0, u] = st
            y, st_new = _hg_chunk_fwd(hq_ref[sl, :], hf_ref[sl, :], hi_ref[sl, :], hg_ref[sl, :], lb_ref[...], nw_ref[...],
                                      st, call_ref[...])
            y_ref[sl, :] = y.astype(y_ref.dtype)
            st_ref[...] = st_new

    def zcol(group):
        return pl.BlockSpec((rows, LANE), functools.partial(lambda h, i, g: (i, _hg_col(g, h)), g=group))

    ncall = call.shape[0]
    return pl.pallas_call(
        body, name="hgrn_fwd", grid=(HG_HEADS, nsteps),
        in_specs=[zcol(Z_HQ), zcol(Z_HF), zcol(Z_HI), zcol(Z_HG),
                  pl.BlockSpec((1, LANE), lambda h, i: (0, h)), pl.BlockSpec((1, LANE), lambda h, i: (0, 0)),
                  pl.BlockSpec((ncall, c), lambda h, i: (0, 0))],
        out_specs=[pl.BlockSpec((rows, LANE), lambda h, i: (i, h)),
                   pl.BlockSpec((1, cs, c, c), lambda h, i: (h, i, 0, 0))],
        out_shape=[jax.ShapeDtypeStruct((t, HG_HEADS * LANE), BF16), jax.ShapeDtypeStruct((HG_HEADS, t // c, c, c), F32)],
        scratch_shapes=[pltpu.VMEM((c, c), F32)],
        compiler_params=_cparams(("parallel", "arbitrary")),
    )(z, z, z, z, lb, nw, call)


def _hgrn_bwd(z, lb, nw, call, call_t, saved, dy):
    t = z.shape[0]
    c, cs = HG_CHUNK, HG_CHUNKS_PER_STEP
    rows = c * cs
    nsteps = t // rows

    def body(hq_ref, hf_ref, hi_ref, hg_ref, lb_ref, nw_ref, call_ref, callt_ref, sv_ref, dy_ref,
             dhq_ref, dhf_ref, dhi_ref, dhg_ref, dlb_ref, dnw_ref, dst_ref):
        h, i = pl.program_id(0), pl.program_id(1)

        @pl.when(i == 0)
        def _():
            dst_ref[...] = jnp.zeros_like(dst_ref)
            dlb_ref[...] = jnp.zeros_like(dlb_ref)

        @pl.when((i == 0) & (h == 0))
        def _():
            dnw_ref[...] = jnp.zeros_like(dnw_ref)

        for u in reversed(range(cs)):
            sl = slice(u * c, (u + 1) * c)
            dhq, dhf, dhi, dhg, dlb, dnw, dst = _hg_chunk_bwd(
                hq_ref[sl, :], hf_ref[sl, :], hi_ref[sl, :], hg_ref[sl, :], lb_ref[...], nw_ref[...], sv_ref[0, u],
                call_ref[...], callt_ref[...], dy_ref[sl, :].astype(F32), dst_ref[...])
            dhq_ref[sl, :] = dhq.astype(BF16)
            dhf_ref[sl, :] = dhf.astype(BF16)
            dhi_ref[sl, :] = dhi.astype(BF16)
            dhg_ref[sl, :] = dhg.astype(BF16)
            dlb_ref[...] += dlb
            dnw_ref[...] += dnw
            dst_ref[...] = dst

    def zcol(group):
        return pl.BlockSpec((rows, LANE), functools.partial(lambda h, i, g: (nsteps - 1 - i, _hg_col(g, h)), g=group))

    head_rows = pl.BlockSpec((rows, LANE), lambda h, i: (nsteps - 1 - i, h))
    ncall = call.shape[0]
    piece = jax.ShapeDtypeStruct((t, HG_HEADS * LANE), BF16)
    return pl.pallas_call(
        body, name="hgrn_bwd", grid=(HG_HEADS, nsteps),
        in_specs=[zcol(Z_HQ), zcol(Z_HF), zcol(Z_HI), zcol(Z_HG),
                  pl.BlockSpec((1, LANE), lambda h, i: (0, h)), pl.BlockSpec((1, LANE), lambda h, i: (0, 0)),
                  pl.BlockSpec((ncall, c), lambda h, i: (0, 0)), pl.BlockSpec((c, ncall), lambda h, i: (0, 0)),
                  pl.BlockSpec((1, cs, c, c), lambda h, i: (h, nsteps - 1 - i, 0, 0)), head_rows],
        out_specs=[head_rows, head_rows, head_rows, head_rows,
                   pl.BlockSpec((1, LANE), lambda h, i: (0, h)), pl.BlockSpec((1, LANE), lambda h, i: (0, 0))],
        out_shape=[piece, piece, piece, piece,
                   jax.ShapeDtypeStruct((1, HG_HEADS * LANE), F32), jax.ShapeDtypeStruct((1, LANE), F32)],
        scratch_shapes=[pltpu.VMEM((c, c), F32)],
        compiler_params=_cparams(("arbitrary", "arbitrary")),
    )(z, z, z, z, lb, nw, call, call_t, saved, dy)


def _lb_fwd(raw):
    nl = raw.shape[0]

    def body(r_ref, o_ref):
        x = r_ref[...]
        e = jnp.exp(x - jnp.max(x, axis=0, keepdims=True))
        p = e / jnp.sum(e, axis=0, keepdims=True)
        acc = jnp.zeros_like(p[0:1])
        for l in range(nl):
            if l > 0:
                acc = acc + p[l:l + 1]
            o_ref[l:l + 1, :] = acc

    return pl.pallas_call(body, name="lb_fwd", out_shape=jax.ShapeDtypeStruct(raw.shape, F32))(raw)


def _lb_bwd(raw, dlbs):
    nl = raw.shape[0]

    def body(r_ref, d_ref, o_ref):
        x = r_ref[...]
        e = jnp.exp(x - jnp.max(x, axis=0, keepdims=True))
        p = e / jnp.sum(e, axis=0, keepdims=True)
        d = d_ref[...]
        dps = [jnp.zeros_like(d[0:1])]
        for i in range(1, nl):
            acc = d[i:i + 1]
            for l in range(i + 1, nl):
                acc = acc + d[l:l + 1]
            dps.append(acc)
        dot = dps[0] * p[0:1]
        for i in range(1, nl):
            dot = dot + dps[i] * p[i:i + 1]
        for i in range(nl):
            o_ref[i:i + 1, :] = p[i:i + 1] * (dps[i] - dot)

    return pl.pallas_call(body, name="lb_bwd", out_shape=jax.ShapeDtypeStruct(raw.shape, F32))(raw, dlbs)


def _exchange(name, srcs, gather):
    nk = len(srcs)

    def body(*refs):
        src_refs, out_refs = refs[:nk], refs[nk:2 * nk]
        send_sems, recv_sems, local_sems = refs[2 * nk:]
        me = 4 * lax.axis_index("x") + 2 * lax.axis_index("y") + lax.axis_index("c")
        local = []
        for k in range(nk):
            own = src_refs[k] if gather else src_refs[k].at[me]
            cp = pltpu.make_async_copy(own, out_refs[k].at[me], local_sems.at[k])
            cp.start()
            local.append(cp)
        copies = []
        for r in range(1, N_DEV):
            to = (me + r) % N_DEV
            for k in range(nk):
                cp = pltpu.make_async_remote_copy(
                    src_ref=src_refs[k] if gather else src_refs[k].at[to],
                    dst_ref=out_refs[k].at[me],
                    send_sem=send_sems.at[k * N_DEV + r], recv_sem=recv_sems.at[k * N_DEV + r],
                    device_id=(to // 4, (to // 2) % 2, to % 2), device_id_type=pl.DeviceIdType.MESH)
                cp.start()
                copies.append(cp)
        for cp in copies:
            cp.wait()
        for cp in local:
            cp.wait()

    any_spec = pl.BlockSpec(memory_space=pl.ANY)
    out_shape = [jax.ShapeDtypeStruct(((N_DEV,) + s.shape) if gather else s.shape, s.dtype) for s in srcs]
    return pl.pallas_call(
        body, name=name,
        in_specs=[any_spec] * nk, out_specs=[any_spec] * nk, out_shape=out_shape,
        scratch_shapes=[pltpu.SemaphoreType.DMA((nk * N_DEV,)), pltpu.SemaphoreType.DMA((nk * N_DEV,)),
                        pltpu.SemaphoreType.DMA((nk,))],
    )(*srcs)


def _adam_math(g, w, m, v):
    m2 = ADAM_B1 * m + (1.0 - ADAM_B1) * g
    v2 = ADAM_B2 * v + (1.0 - ADAM_B2) * (g * g)
    m_hat = m2 / (1.0 - ADAM_B1 ** ADAM_STEP)
    v_hat = v2 / (1.0 - ADAM_B2 ** ADAM_STEP)
    return -ADAM_LR * (m_hat / (jnp.sqrt(v_hat) + ADAM_EPS) + ADAM_WD * w), m2, v2


def _sum_slots(ref):
    g = ref[0].astype(F32)
    for s in range(1, N_DEV):
        g = g + ref[s].astype(F32)
    return g


def _adam_sharded(name, slots, w, m, v):
    nl, a, b = w.shape
    ta = a
    for cand in range(8, 257, 8):
        if a % cand == 0:
            ta = cand

    def body(s_ref, w_ref, m_ref, v_ref, g_ref, d_ref, m2_ref, v2_ref):
        g = _sum_slots(s_ref)
        d, m2, v2 = _adam_math(g, w_ref[...], m_ref[...], v_ref[...])
        g_ref[...] = g
        d_ref[...] = d
        m2_ref[...] = m2
        v2_ref[...] = v2

    blk = pl.BlockSpec((1, ta, b), lambda l, i: (l, i, 0))
    sds = jax.ShapeDtypeStruct(w.shape, F32)
    return pl.pallas_call(
        body, name=name, grid=(nl, a // ta),
        in_specs=[pl.BlockSpec((N_DEV, 1, ta, b), lambda l, i: (0, l, i, 0)), blk, blk, blk],
        out_specs=[blk] * 4, out_shape=[sds] * 4,
        compiler_params=_cparams(("parallel", "parallel")),
    )(slots, w, m, v)


def _sum_replicated(slots):
    def body(s_ref, g_ref):
        g_ref[...] = _sum_slots(s_ref)

    return pl.pallas_call(body, name="sum_small", out_shape=jax.ShapeDtypeStruct(slots.shape[1:], F32))(slots)


def _adam_small(name, g, w, m, v):
    def body(g_ref, w_ref, m_ref, v_ref, d_ref, m2_ref, v2_ref):
        d, m2, v2 = _adam_math(g_ref[...], w_ref[...], m_ref[...], v_ref[...])
        d_ref[...] = d
        m2_ref[...] = m2
        v2_ref[...] = v2

    sds = jax.ShapeDtypeStruct(w.shape, F32)
    return pl.pallas_call(body, name=name, out_shape=[sds] * 3)(g, w, m, v)


def _cols_full(g):
    return jnp.transpose(g, (1, 0, 2)).reshape(g.shape[1], -1)


def _cols_shards(w):
    k = w.shape[0]
    return jnp.transpose(w.reshape(k, N_DEV, -1), (1, 0, 2))


def _swap_halves(x):
    half = x.shape[-1] // 2
    return jnp.concatenate([x[..., half:], x[..., :half]], axis=-1)


def _zeros_like_cols(x, n):
    return jnp.zeros(x.shape[:-1] + (n,), x.dtype)


def _w_in_internal(w):
    d = w.shape[0]
    kpe = w[:, 640:672]
    z64, z32 = jnp.zeros((d, 64), w.dtype), jnp.zeros((d, 32), w.dtype)
    return jnp.concatenate(
        [w[:, 0:640], z64, kpe, z32, z64, _swap_halves(kpe), z32, jnp.zeros((d, Z_HQ - Z_KPESW - LANE), w.dtype),
         w[:, 672:2720], w[:, 2720:4768]], axis=1)


def _w_in_grad(g):
    kpe = g[:, Z_KPE + 64:Z_KPE + 96] + _swap_halves(g[:, Z_KPESW + 64:Z_KPESW + 96])
    return jnp.concatenate([g[:, 0:640], kpe, g[:, Z_HQ:Z_W]], axis=1)


def _w_uq_internal(w):
    k = w.shape[0]
    w3 = w.reshape(k, MLA_HEADS, QK_NOPE + QK_ROPE)
    nope, rope = w3[..., :QK_NOPE], w3[..., QK_NOPE:]
    plain = jnp.concatenate([nope, rope, _zeros_like_cols(rope, 32)], axis=-1).reshape(k, -1)
    swapped = jnp.concatenate([_zeros_like_cols(nope, 64), _swap_halves(rope), _zeros_like_cols(rope, 32)], axis=-1).reshape(k, -1)
    return jnp.concatenate([plain, swapped], axis=1)


def _w_uq_grad(g):
    k = g.shape[0]
    half = MLA_HEADS * LANE
    g1, g2 = g[:, :half].reshape(k, MLA_HEADS, LANE), g[:, half:].reshape(k, MLA_HEADS, LANE)
    rope = g1[..., 64:96] + _swap_halves(g2[..., 64:96])
    return jnp.concatenate([g1[..., :64], rope], axis=-1).reshape(k, -1)


def _w_ukv_internal(w):
    k = w.shape[0]
    w3 = w.reshape(k, MLA_HEADS, QK_NOPE + V_HEAD)
    kn, vv = w3[..., :QK_NOPE], w3[..., QK_NOPE:]
    z = _zeros_like_cols(kn, 64)
    return jnp.concatenate([kn, z, vv, z], axis=-1).reshape(k, -1)


def _w_ukv_grad(g):
    k = g.shape[0]
    g3 = g.reshape(k, MLA_HEADS, 2 * LANE)
    return jnp.concatenate([g3[..., 0:64], g3[..., LANE:LANE + 64]], axis=-1).reshape(k, -1)


def _w_pa_internal(w):
    n = w.shape[1]
    w3 = w.reshape(MLA_HEADS, V_HEAD, n)
    return jnp.concatenate([w3, jnp.zeros_like(w3)], axis=1).reshape(-1, n)


def _w_pa_grad(g):
    n = g.shape[1]
    return g.reshape(MLA_HEADS, 2 * V_HEAD, n)[:, :V_HEAD].reshape(-1, n)


def _rope_tables(t):
    half = QK_ROPE // 2
    inv = ROPE_THETA ** (-jnp.arange(half, dtype=F32) / half)
    ang = jnp.arange(t, dtype=F32)[:, None] * inv[None, :]
    cos, sin = jnp.cos(ang), jnp.sin(ang)
    one, zero = jnp.ones((t, 64), F32), jnp.zeros((t, 64), F32)
    z32 = jnp.zeros((t, 32), F32)
    cq = jnp.concatenate([one, cos, cos, z32], axis=1)
    ck = jnp.concatenate([zero, cos, cos, z32], axis=1)
    sq = jnp.concatenate([zero, -sin, sin, z32], axis=1)
    return cq, ck, sq


def _ffn_fwd(tag, h, nw, w_gu, w_down):
    t, d = h.shape
    dff = w_down.shape[0]
    tm, tn = _row_tile(t), _tile(dff, 1408)
    xn = _rms_fwd(tag + "_norm", h, nw)

    def act_fn(accs, _):
        g, u = accs
        return g, u, g * _sigmoid(g) * u

    spec = _tile_spec(tm, tn)
    g, u, act = _mm(tag + "_gu", [xn], [(0, w_gu, 0), (0, w_gu, dff // tn)], [],
                    [((t, dff), BF16) + spec] * 3, act_fn, tm=tm, tn=tn, n=dff)
    tn2 = _tile(d, 1024)
    h2, = _mm(tag + "_down", [act], [(0, w_down, 0)], [(h,) + _tile_spec(tm, tn2)],
              [((t, d), F32) + _tile_spec(tm, tn2)], lambda accs, ex: (ex[0] + 0.5 * accs[0],), tm=tm, tn=tn2, n=d)
    return h2, (h, xn, g, u, act)


def _ffn_bwd(tag, dh2, saved, nw, w_gu_t, w_down_t):
    h, xn, g, u, act = saved
    t, d = h.shape
    dff = act.shape[1]
    tm, tn = _row_tile(t), _tile(dff, 1408)

    def dact_fn(accs, ex):
        gg, uu = ex[0].astype(F32), ex[1].astype(F32)
        da = 0.5 * accs[0]
        sg = _sigmoid(gg)
        return da * uu * (sg * (1.0 + gg * (1.0 - sg))), da * (gg * sg)

    spec = _tile_spec(tm, tn)
    dg, du = _mm(tag + "_dact", [dh2], [(0, w_down_t, 0)], [(g,) + spec, (u,) + spec],
                 [((t, dff), BF16) + spec] * 2, dact_fn, tm=tm, tn=tn, n=dff)
    dw_down = _mm_tn(tag + "_dwdown", act, dh2, alpha=0.5, out_dtype=BF16)
    tn2 = _tile(d, 512)
    dxn, = _mm(tag + "_dxn", [dg, du], [(0, w_gu_t[:dff], 0), (1, w_gu_t[dff:], 0)], [],
               [((t, d), F32) + _tile_spec(tm, tn2)], lambda accs, _: (accs[0] + accs[1],), tm=tm, tn=tn2, n=d)
    dw_gu = jnp.concatenate([_mm_tn(tag + "_dwg", xn, dg, alpha=1.0, out_dtype=BF16),
                             _mm_tn(tag + "_dwu", xn, du, alpha=1.0, out_dtype=BF16)], axis=1)
    dh, dnw = _rms_bwd(tag + "_dnorm", h, nw, dxn, dh2)
    return dh, dnw, dw_gu, dw_down


def _kv_pattern(kr):
    z = jnp.zeros_like(kr)
    return jnp.concatenate([kr, z] * MLA_HEADS, axis=1)


def _mix_fwd(h, p, tabs, lb, call):
    t, d = h.shape
    tm = _row_tile(t)
    cq_t, ck_t, sq_t = tabs
    u = _rms_fwd("mix_norm", h, p["mix_norm"])
    tnz = _tile(Z_W, 1024)
    z, = _mm("mix_in", [u], [(0, p["w_in"], 0)], [], [((t, Z_W), F32) + _tile_spec(tm, tnz)], lambda a, _: (a[0],),
             tm=tm, tn=tnz, n=Z_W)
    cqn, ckvn, krot = _mla_prep(z, p["q_norm"], p["kv_norm"], ck_t, sq_t)
    wq = MLA_HEADS * LANE
    lane_rows = lambda i, j: (i, 0)

    def q_fn(accs, ex):
        c8 = jnp.concatenate([ex[0]] * MLA_HEADS, axis=1)
        s8 = jnp.concatenate([ex[1]] * MLA_HEADS, axis=1)
        return (accs[0] * c8 + accs[1] * s8,)

    q, = _mm("mla_q", [cqn], [(0, p["w_uq"], 0), (0, p["w_uq"], 1)],
             [(cq_t, (tm, LANE), lane_rows), (sq_t, (tm, LANE), lane_rows)],
             [((t, wq), BF16) + _tile_spec(tm, wq)], q_fn, tm=tm, tn=wq, n=wq)
    kv, = _mm("mla_kv", [ckvn], [(0, p["w_ukv"], 0)], [(krot, (tm, LANE), lane_rows)],
              [((t, 2 * wq), BF16) + _tile_spec(tm, 2 * wq)], lambda a, ex: (a[0] + _kv_pattern(ex[0]),),
              tm=tm, tn=2 * wq, n=2 * wq)
    o_a, lse = _attn_fwd(q, kv)
    o_b, st_saved = _hgrn_fwd(z, lb, p["hg_norm"], call)
    tn = _tile(d, 512)

    def merge_fn(accs, ex):
        ya, yb = accs
        return ya, yb, _sigmoid(ex[0]) * ya + _sigmoid(ex[1]) * yb

    spec = _tile_spec(tm, tn)
    ya, yb, merged = _mm("mix_merge", [o_a, o_b], [(0, p["w_pa"], 0), (1, p["w_pr"], 0)],
                         [(z,) + _tile_spec(tm, tn, Z_GA // tn), (z,) + _tile_spec(tm, tn, Z_GB // tn)],
                         [((t, d), BF16) + spec] * 3, merge_fn, tm=tm, tn=tn, n=d)
    tn2 = _tile(d, 1024)
    h2, = _mm("mix_out", [merged], [(0, p["w_out"], 0)], [(h,) + _tile_spec(tm, tn2)],
              [((t, d), F32) + _tile_spec(tm, tn2)], lambda a, ex: (ex[0] + a[0],), tm=tm, tn=tn2, n=d)
    return h2, (h, u, z, cqn, ckvn, q, kv, o_a, lse, o_b, st_saved, ya, yb, merged)


def _mix_bwd(dh2, saved, p, tabs, lb, call, call_t):
    h, u, z, cqn, ckvn, q, kv, o_a, lse, o_b, st_saved, ya, yb, merged = saved
    t, d = h.shape
    tm = _row_tile(t)
    cq_t, ck_t, sq_t = tabs
    tn = _tile(d, 512)
    spec = _tile_spec(tm, tn)

    def dmerge_fn(accs, ex):
        dm = accs[0]
        yav, ybv = ex[0].astype(F32), ex[1].astype(F32)
        sa, sb = _sigmoid(ex[2]), _sigmoid(ex[3])
        return dm * sa, dm * sb, dm * yav * sa * (1.0 - sa), dm * ybv * sb * (1.0 - sb)

    dya, dyb, dga, dgb = _mm("mix_dmerge", [dh2], [(0, p["w_out_t"], 0)],
                             [(ya,) + spec, (yb,) + spec, (z,) + _tile_spec(tm, tn, Z_GA // tn),
                              (z,) + _tile_spec(tm, tn, Z_GB // tn)],
                             [((t, d), BF16) + spec] * 4, dmerge_fn, tm=tm, tn=tn, n=d)
    dw_out = _mm_tn("mix_dwout", merged, dh2, alpha=1.0, out_dtype=BF16)
    wq = MLA_HEADS * LANE
    do_a, = _mm("mix_doa", [dya], [(0, p["w_pa_t"], 0)], [], [((t, wq), BF16) + _tile_spec(tm, wq)], lambda a, _: (a[0],),
                tm=tm, tn=wq, n=wq)
    wr = HG_HEADS * LANE
    do_b, = _mm("mix_dob", [dyb], [(0, p["w_pr_t"], 0)], [], [((t, wr), BF16) + _tile_spec(tm, wr)], lambda a, _: (a[0],),
                tm=tm, tn=wr, n=wr)
    dw_pa = _mm_tn("mix_dwpa", o_a, dya, alpha=1.0, out_dtype=F32)
    dw_pr = _mm_tn("mix_dwpr", o_b, dyb, alpha=1.0, out_dtype=BF16)
    delta = _attn_delta(o_a, do_a)
    dq, dkv = _attn_bwd(q, kv, do_a, lse, delta)
    dqq = _rope_bwd_q(dq, cq_t, sq_t)
    dcq, = _mm("mla_dcq", [dqq], [(0, p["w_uq_t"], 0)], [], [((t, Q_LORA), F32) + _tile_spec(tm, Q_LORA)],
               lambda a, _: (a[0],), tm=tm, tn=Q_LORA, n=Q_LORA)
    dckv, = _mm("mla_dckv", [dkv], [(0, p["w_ukv_t"], 0)], [], [((t, KV_LORA), F32) + _tile_spec(tm, KV_LORA)],
                lambda a, _: (a[0],), tm=tm, tn=KV_LORA, n=KV_LORA)
    dw_uq = _mm_tn("mla_dwuq", cqn, dqq, alpha=1.0, out_dtype=F32)
    dw_ukv = _mm_tn("mla_dwukv", ckvn, dkv, alpha=1.0, out_dtype=F32)
    dz_mla, dqn, dkvn = _mla_prep_bwd(z, p["q_norm"], p["kv_norm"], ck_t, sq_t, dcq, dckv, dkv)
    dhq, dhf, dhi, dhg, dlb, dhgn = _hgrn_bwd(z, lb, p["hg_norm"], call, call_t, st_saved, do_b)
    dz = jnp.concatenate([dz_mla, dhq, dhf, dhi, dhg, dga, dgb], axis=1)
    du, = _mm("mix_du", [dz], [(0, p["w_in_t"], 0)], [], [((t, d), F32) + spec], lambda a, _: (a[0],), tm=tm, tn=tn, n=d)
    dw_in = _mm_tn("mix_dwin", u, dz, alpha=1.0, out_dtype=F32)
    dh, dmn = _rms_bwd("mix_dnorm", h, p["mix_norm"], du, dh2)
    grads = dict(mix_norm=dmn, q_norm=dqn, kv_norm=dkvn, hg_norm=dhgn, lb=dlb, w_in=_w_in_grad(dw_in), w_uq=_w_uq_grad(dw_uq),
                 w_ukv=_w_ukv_grad(dw_ukv), w_proj_attn=_w_pa_grad(dw_pa), w_proj_rec=dw_pr, w_out=dw_out)
    return dh, grads


SHARDED = ("ffn1_w_gu", "ffn1_w_down", "w_in", "w_uq", "w_ukv", "w_proj_attn", "w_proj_rec", "w_out", "ffn2_w_gu", "ffn2_w_down")
ROW_SHARDED = ("ffn1_w_down", "w_out", "ffn2_w_down")
SMALL = ("ffn1_norm", "mix_norm", "q_norm", "kv_norm", "hg_lb_raw", "hg_norm", "ffn2_norm", "final_norm")
WEIGHTS = ("meta_tokens", "ffn1_norm", "ffn1_w_gu", "ffn1_w_down", "mix_norm", "w_in", "q_norm", "kv_norm", "w_uq", "w_ukv",
           "hg_lb_raw", "hg_norm", "w_proj_attn", "w_proj_rec", "w_out", "ffn2_norm", "ffn2_w_gu", "ffn2_w_down", "final_norm")


def _pack_small(vals):
    flat = jnp.concatenate([vals[n].reshape(-1) for n in SMALL])
    return flat.reshape(-1, LANE)


def _unpack_small(packed, like):
    flat = packed.reshape(-1)
    out, off = {}, 0
    for n in SMALL:
        size = math.prod(like[n].shape)
        out[n] = flat[off:off + size].reshape(like[n].shape)
        off += size
    return out


def kernel(x, meta_tokens, ffn1_norm, ffn1_w_gu, ffn1_w_down, mix_norm, w_in, q_norm, kv_norm, w_uq, w_ukv, hg_lb_raw, hg_norm, w_proj_attn, w_proj_rec, w_out, ffn2_norm, ffn2_w_gu, ffn2_w_down, final_norm, loss_target, m_meta_tokens, m_ffn1_norm, m_ffn1_w_gu, m_ffn1_w_down, m_mix_norm, m_w_in, m_q_norm, m_kv_norm, m_w_uq, m_w_ukv, m_hg_lb_raw, m_hg_norm, m_w_proj_attn, m_w_proj_rec, m_w_out, m_ffn2_norm, m_ffn2_w_gu, m_ffn2_w_down, m_final_norm, v_meta_tokens, v_ffn1_norm, v_ffn1_w_gu, v_ffn1_w_down, v_mix_norm, v_w_in, v_q_norm, v_kv_norm, v_w_uq, v_w_ukv, v_hg_lb_raw, v_hg_norm, v_w_proj_attn, v_w_proj_rec, v_w_out, v_ffn2_norm, v_ffn2_w_gu, v_ffn2_w_down, v_final_norm):
    w = dict(meta_tokens=meta_tokens, ffn1_norm=ffn1_norm, ffn1_w_gu=ffn1_w_gu, ffn1_w_down=ffn1_w_down, mix_norm=mix_norm,
             w_in=w_in, q_norm=q_norm, kv_norm=kv_norm, w_uq=w_uq, w_ukv=w_ukv, hg_lb_raw=hg_lb_raw, hg_norm=hg_norm,
             w_proj_attn=w_proj_attn, w_proj_rec=w_proj_rec, w_out=w_out, ffn2_norm=ffn2_norm, ffn2_w_gu=ffn2_w_gu,
             ffn2_w_down=ffn2_w_down, final_norm=final_norm)
    mom = dict(meta_tokens=m_meta_tokens, ffn1_norm=m_ffn1_norm, ffn1_w_gu=m_ffn1_w_gu, ffn1_w_down=m_ffn1_w_down,
               mix_norm=m_mix_norm, w_in=m_w_in, q_norm=m_q_norm, kv_norm=m_kv_norm, w_uq=m_w_uq, w_ukv=m_w_ukv,
               hg_lb_raw=m_hg_lb_raw, hg_norm=m_hg_norm, w_proj_attn=m_w_proj_attn, w_proj_rec=m_w_proj_rec, w_out=m_w_out,
               ffn2_norm=m_ffn2_norm, ffn2_w_gu=m_ffn2_w_gu, ffn2_w_down=m_ffn2_w_down, final_norm=m_final_norm)
    var = dict(meta_tokens=v_meta_tokens, ffn1_norm=v_ffn1_norm, ffn1_w_gu=v_ffn1_w_gu, ffn1_w_down=v_ffn1_w_down,
               mix_norm=v_mix_norm, w_in=v_w_in, q_norm=v_q_norm, kv_norm=v_kv_norm, w_uq=v_w_uq, w_ukv=v_w_ukv,
               hg_lb_raw=v_hg_lb_raw, hg_norm=v_hg_norm, w_proj_attn=v_w_proj_attn, w_proj_rec=v_w_proj_rec, w_out=v_w_out,
               ffn2_norm=v_ffn2_norm, ffn2_w_gu=v_ffn2_w_gu, ffn2_w_down=v_ffn2_w_down, final_norm=v_final_norm)
    nl = ffn1_norm.shape[0]
    seq, d = x.shape[1], x.shape[2]
    n_real = N_META + seq
    t = -(-n_real // ROW_ALIGN) * ROW_ALIGN
    me = 4 * lax.axis_index("x") + 2 * lax.axis_index("y") + lax.axis_index("c")

    gathered = _exchange("gather_weights", [w[n].astype(BF16) for n in SHARDED] + [meta_tokens], gather=True)
    full = dict(zip(SHARDED, gathered[:-1]))
    meta_full = _cols_full(gathered[-1])

    def layer_params(l):
        def mat(n):
            g = full[n][:, l]
            return g.reshape(-1, g.shape[-1]) if n in ROW_SHARDED else _cols_full(g)

        p = {}
        for tag in ("ffn1", "ffn2"):
            p[tag + "_w_gu"] = mat(tag + "_w_gu")
            p[tag + "_w_down"] = mat(tag + "_w_down")
            p[tag + "_w_gu_t"] = p[tag + "_w_gu"].T
            p[tag + "_w_down_t"] = p[tag + "_w_down"].T
            p[tag + "_norm"] = w[tag + "_norm"][l:l + 1]
        p["w_in"] = _w_in_internal(mat("w_in"))
        p["w_uq"] = _w_uq_internal(mat("w_uq"))
        p["w_ukv"] = _w_ukv_internal(mat("w_ukv"))
        p["w_pa"] = _w_pa_internal(mat("w_proj_attn"))
        p["w_pr"] = mat("w_proj_rec")
        p["w_out"] = mat("w_out")
        for n in ("w_in", "w_uq", "w_ukv", "w_pa", "w_pr", "w_out"):
            p[n + "_t"] = p[n].T
        for n in ("mix_norm", "q_norm", "kv_norm", "hg_norm"):
            p[n] = w[n][l:l + 1]
        return p

    params = [layer_params(l) for l in range(nl)]
    tabs = _rope_tables(t)
    call, call_t = _hg_tables()
    lbs = _lb_fwd(hg_lb_raw)

    pad = jnp.zeros((t - n_real, d), F32)
    h = jnp.concatenate([meta_full, x[0], pad], axis=0)
    tgt = jnp.concatenate([jnp.zeros((N_META, d), F32), loss_target[0], pad], axis=0)
    saved = []
    for l in range(nl):
        p = params[l]
        h, s1 = _ffn_fwd("ffn1", h, p["ffn1_norm"], p["ffn1_w_gu"], p["ffn1_w_down"])
        h, s2 = _mix_fwd(h, p, tabs, lbs[l:l + 1], call)
        h, s3 = _ffn_fwd("ffn2", h, p["ffn2_norm"], p["ffn2_w_gu"], p["ffn2_w_down"])
        saved.append((s1, s2, s3))
    dh, d_final, loss_part = _loss_head(h, final_norm.reshape(1, d), tgt, n_real)
    loss = lax.psum(loss_part[0, 0], ("x", "y", "c"))

    per_layer = []
    for l in reversed(range(nl)):
        p = params[l]
        s1, s2, s3 = saved[l]
        dh, dn2, dgu2, ddown2 = _ffn_bwd("ffn2", dh, s3, p["ffn2_norm"], p["ffn2_w_gu_t"], p["ffn2_w_down_t"])
        dh, gm = _mix_bwd(dh, s2, p, tabs, lbs[l:l + 1], call, call_t)
        dh, dn1, dgu1, ddown1 = _ffn_bwd("ffn1", dh, s1, p["ffn1_norm"], p["ffn1_w_gu_t"], p["ffn1_w_down_t"])
        gm.update(ffn1_norm=dn1, ffn2_norm=dn2, ffn1_w_gu=dgu1, ffn2_w_gu=dgu2, ffn1_w_down=ddown1, ffn2_w_down=ddown2)
        per_layer.append(gm)
    per_layer.reverse()
    grad_x = dh[N_META:n_real][None]

    def shards(n):
        per = []
        for gm in per_layer:
            g = gm[n].astype(BF16)
            per.append(g.reshape(N_DEV, -1, g.shape[-1]) if n in ROW_SHARDED else _cols_shards(g))
        return jnp.stack(per, axis=1)

    slots = _exchange("scatter_grads", [shards(n) for n in SHARDED], gather=False)
    grads, delta, new_m, new_v = {}, {}, {}, {}
    for n, s in zip(SHARDED, slots):
        grads[n], delta[n], new_m[n], new_v[n] = _adam_sharded("adam_" + n, s, w[n], mom[n], var[n])

    small = {n: jnp.concatenate([gm[n] for gm in per_layer], axis=0) for n in SMALL if n not in ("hg_lb_raw", "final_norm")}
    small["hg_lb_raw"] = _lb_bwd(hg_lb_raw, jnp.concatenate([gm["lb"] for gm in per_layer], axis=0))
    small["final_norm"] = d_final
    packed = jnp.concatenate([_pack_small(small), dh[:N_META].reshape(-1, LANE)], axis=0)
    summed = _sum_replicated(_exchange("gather_small", [packed], gather=True)[0])
    n_small = packed.shape[0] - N_META * d // LANE
    sd, sm, sv = _adam_small("adam_small", summed[:n_small], _pack_small(w), _pack_small(mom), _pack_small(var))
    grads.update(_unpack_small(summed[:n_small], w))
    delta.update(_unpack_small(sd, w))
    new_m.update(_unpack_small(sm, w))
    new_v.update(_unpack_small(sv, w))
    dmeta = lax.dynamic_slice_in_dim(summed[n_small:].reshape(N_META, d), me * (d // N_DEV), d // N_DEV, axis=1)
    grads["meta_tokens"] = dmeta
    delta["meta_tokens"], new_m["meta_tokens"], new_v["meta_tokens"] = _adam_small(
        "adam_meta", dmeta, meta_tokens, m_meta_tokens, v_meta_tokens)

    return (loss, grad_x, *[grads[n] for n in WEIGHTS], *[delta[n] for n in WEIGHTS], *[new_m[n] for n in WEIGHTS],
            *[new_v[n] for n in WEIGHTS])
```

```python
import functools
import math

import jax
import jax.numpy as jnp
from jax import lax
from jax.experimental import pallas as pl
from jax.experimental.pallas import tpu as pltpu

F32 = jnp.float32
BF16 = jnp.bfloat16

N_DEV = 8
N_META = 16
MLA_HEADS = 8
Q_LORA = 384
KV_LORA = 256
QK_NOPE = 64
QK_ROPE = 32
V_HEAD = 64
ROPE_THETA = 10000.0
HG_HEADS = 4
HG_DIM = 128
EPS = 1e-6
NEG_BIG = -1e30
F_MIN = 1e-20
ADAM_LR = 0.001
ADAM_B1 = 0.9
ADAM_B2 = 0.999
ADAM_EPS = 1e-08
ADAM_WD = 0.01
ADAM_STEP = 10

LANE = 128
ROW_ALIGN = 256
ATTN_BLOCK = 256
HG_CHUNK = 128
HG_CHUNKS_PER_STEP = 2
HG_LEVELS = (64, 32, 16, 8, 4, 2, 1)
VMEM_LIMIT = 48 * 1024 * 1024

Z_CQ, Z_CKV, Z_KPE, Z_KPESW, Z_HQ, Z_HF, Z_HI, Z_HG, Z_GA, Z_GB, Z_W = 0, 384, 640, 768, 1024, 1536, 2048, 2560, 3072, 4096, 5120
ATTN_SCALE = float((QK_NOPE + QK_ROPE) ** -0.5)
LOG2E = 1.4426950408889634
ATTN_C2 = ATTN_SCALE * LOG2E
HG_SCALE = float(HG_DIM ** -0.5)


def _cparams(sem):
    return pltpu.CompilerParams(dimension_semantics=sem, vmem_limit_bytes=VMEM_LIMIT)


def _tile(n, cap):
    if n <= cap:
        return n
    best = None
    for t in range(LANE, cap + 1, LANE):
        if n % t == 0:
            best = t
    assert best is not None, (n, cap)
    return best


def _row_tile(m):
    for t in (384, 256, 128):
        if m % t == 0:
            return t
    raise ValueError(m)


def _bf(x):
    return x.astype(BF16)


def _dot(a, b):
    return jnp.dot(a, b, preferred_element_type=F32)


def _dot_nt(a, b):
    return lax.dot_general(a, b, (((1,), (1,)), ((), ())), preferred_element_type=F32)


def _dot_tn(a, b):
    return lax.dot_general(a, b, (((0,), (0,)), ((), ())), preferred_element_type=F32)


def _sigmoid(x):
    return 1.0 / (1.0 + jnp.exp(-x))


def _mm(name, a_list, pairs, extras, outs, fn, *, tm, tn, n):
    m = a_list[0].shape[0]
    na, nb, ne, no = len(a_list), len(pairs), len(extras), len(outs)

    def body(*refs):
        a_refs = refs[:na]
        b_refs = refs[na:na + nb]
        e_refs = refs[na + nb:na + nb + ne]
        o_refs = refs[na + nb + ne:]
        a_vals = [_bf(r[...]) for r in a_refs]
        accs = [_dot(a_vals[ai], b_refs[k][...]) for k, (ai, _, _) in enumerate(pairs)]
        res = fn(accs, [r[...] for r in e_refs])
        for r, v in zip(o_refs, res):
            r[...] = v.astype(r.dtype)

    def spec(block_shape, index_map):
        return pl.BlockSpec(block_shape, functools.partial(lambda j, i, im: im(i, j), im=index_map))

    in_specs = [spec((tm, a.shape[1]), lambda i, j: (i, 0)) for a in a_list]
    for _, b, off in pairs:
        in_specs.append(spec((b.shape[0], tn), functools.partial(lambda i, j, off: (0, j + off), off=off)))
    in_specs += [spec(bs, im) for _, bs, im in extras]
    return pl.pallas_call(
        body, name=name, grid=(n // tn, m // tm),
        in_specs=in_specs,
        out_specs=[spec(bs, im) for _, _, bs, im in outs],
        out_shape=[jax.ShapeDtypeStruct(s, d) for s, d, _, _ in outs],
        compiler_params=_cparams(("parallel", "parallel")),
    )(*a_list, *[b for _, b, _ in pairs], *[e for e, _, _ in extras])


def _tile_spec(tm, tn, col_off=0):
    return (tm, tn), functools.partial(lambda i, j, off: (i, j + off), off=col_off)


def _mm_tn(name, a, b, *, alpha, out_dtype):
    t, k = a.shape
    n = b.shape[1]
    tk, tn, tt = _tile(k, 1408), _tile(n, 1408), _row_tile(t)
    nt = t // tt

    def body(a_ref, b_ref, o_ref, acc_ref):
        s = pl.program_id(2)

        @pl.when(s == 0)
        def _():
            acc_ref[...] = jnp.zeros_like(acc_ref)

        acc_ref[...] += _dot_tn(_bf(a_ref[...]), _bf(b_ref[...]))

        @pl.when(s == nt - 1)
        def _():
            o_ref[...] = (alpha * acc_ref[...]).astype(o_ref.dtype)

    return pl.pallas_call(
        body, name=name, grid=(k // tk, n // tn, nt),
        in_specs=[pl.BlockSpec((tt, tk), lambda i, j, s: (s, i)), pl.BlockSpec((tt, tn), lambda i, j, s: (s, j))],
        out_specs=pl.BlockSpec((tk, tn), lambda i, j, s: (i, j)),
        out_shape=jax.ShapeDtypeStruct((k, n), out_dtype),
        scratch_shapes=[pltpu.VMEM((tk, tn), F32)],
        compiler_params=_cparams(("parallel", "parallel", "arbitrary")),
    )(a, b)


def _rms_parts(x):
    r = lax.rsqrt(jnp.mean(x * x, axis=-1, keepdims=True) + EPS)
    return r, x * r


def _rms_bwd_math(x, w, dxn):
    r, xhat = _rms_parts(x)
    t = dxn * w
    dx = r * (t - xhat * jnp.mean(t * xhat, axis=-1, keepdims=True))
    dw = jnp.sum(dxn * xhat, axis=0, keepdims=True)
    return dx, dw


def _rms_fwd(name, h, w):
    t, d = h.shape
    tm = _row_tile(t)

    def body(h_ref, w_ref, o_ref):
        _, xhat = _rms_parts(h_ref[...])
        o_ref[...] = (xhat * w_ref[...]).astype(o_ref.dtype)

    return pl.pallas_call(
        body, name=name, grid=(t // tm,),
        in_specs=[pl.BlockSpec((tm, d), lambda i: (i, 0)), pl.BlockSpec((1, d), lambda i: (0, 0))],
        out_specs=pl.BlockSpec((tm, d), lambda i: (i, 0)),
        out_shape=jax.ShapeDtypeStruct((t, d), BF16),
        compiler_params=_cparams(("parallel",)),
    )(h, w)


def _rms_bwd(name, h, w, dxn, dh_in):
    t, d = h.shape
    tm = _row_tile(t)

    def body(h_ref, w_ref, dxn_ref, dh_ref, o_ref, dw_ref):
        dx, dw = _rms_bwd_math(h_ref[...], w_ref[...], dxn_ref[...])
        o_ref[...] = dh_ref[...] + dx

        @pl.when(pl.program_id(0) == 0)
        def _():
            dw_ref[...] = jnp.zeros_like(dw_ref)

        dw_ref[...] += dw

    row = pl.BlockSpec((tm, d), lambda i: (i, 0))
    vec = pl.BlockSpec((1, d), lambda i: (0, 0))
    return pl.pallas_call(
        body, name=name, grid=(t // tm,),
        in_specs=[row, vec, row, row],
        out_specs=[row, vec],
        out_shape=[jax.ShapeDtypeStruct((t, d), F32), jax.ShapeDtypeStruct((1, d), F32)],
        compiler_params=_cparams(("arbitrary",)),
    )(h, w, dxn, dh_in)


def _loss_head(h, w, tgt, n_real):
    t, d = h.shape
    tm = _row_tile(t)

    def body(h_ref, w_ref, t_ref, dh_ref, dw_ref, loss_ref):
        i = pl.program_id(0)
        x = h_ref[...]
        wv = w_ref[...]
        _, xhat = _rms_parts(x)
        rows = i * tm + lax.broadcasted_iota(jnp.int32, (tm, 1), 0)
        valid = (rows >= N_META) & (rows < n_real)
        e = jnp.where(valid, xhat * wv - t_ref[...], 0.0)
        dx, dw = _rms_bwd_math(x, wv, e * (1.0 / d))
        dh_ref[...] = dx

        @pl.when(i == 0)
        def _():
            dw_ref[...] = jnp.zeros_like(dw_ref)
            loss_ref[...] = jnp.zeros_like(loss_ref)

        dw_ref[...] += dw
        loss_ref[...] += (0.5 / d) * jnp.sum(jnp.sum(e * e, axis=-1, keepdims=True), axis=0, keepdims=True)

    row = pl.BlockSpec((tm, d), lambda i: (i, 0))
    vec = pl.BlockSpec((1, d), lambda i: (0, 0))
    return pl.pallas_call(
        body, name="loss_head", grid=(t // tm,),
        in_specs=[row, vec, row],
        out_specs=[row, vec, pl.BlockSpec((1, 1), lambda i: (0, 0))],
        out_shape=[jax.ShapeDtypeStruct((t, d), F32), jax.ShapeDtypeStruct((1, d), F32), jax.ShapeDtypeStruct((1, 1), F32)],
        compiler_params=_cparams(("arbitrary",)),
    )(h, w, tgt)


def _mla_prep(z, qn_w, kvn_w, ck, sk):
    t = z.shape[0]
    tm = _row_tile(t)

    def body(z_ref, qw_ref, kw_ref, ck_ref, sk_ref, cq_ref, ckv_ref, kr_ref):
        zz = z_ref[...]
        _, qhat = _rms_parts(zz[:, Z_CQ:Z_CKV])
        _, khat = _rms_parts(zz[:, Z_CKV:Z_KPE])
        cq_ref[...] = (qhat * qw_ref[...]).astype(BF16)
        ckv_ref[...] = (khat * kw_ref[...]).astype(BF16)
        kr_ref[...] = zz[:, Z_KPE:Z_KPESW] * ck_ref[...] + zz[:, Z_KPESW:Z_KPESW + LANE] * sk_ref[...]

    def rows(wd):
        return pl.BlockSpec((tm, wd), lambda i: (i, 0))

    def vec(wd):
        return pl.BlockSpec((1, wd), lambda i: (0, 0))

    return pl.pallas_call(
        body, name="mla_prep", grid=(t // tm,),
        in_specs=[rows(Z_HQ), vec(Q_LORA), vec(KV_LORA), rows(LANE), rows(LANE)],
        out_specs=[rows(Q_LORA), rows(KV_LORA), rows(LANE)],
        out_shape=[jax.ShapeDtypeStruct((t, Q_LORA), BF16), jax.ShapeDtypeStruct((t, KV_LORA), BF16),
                   jax.ShapeDtypeStruct((t, LANE), F32)],
        compiler_params=_cparams(("parallel",)),
    )(z, qn_w, kvn_w, ck, sk)


def _mla_prep_bwd(z, qn_w, kvn_w, ck, sk, dcq, dckv, dkv):
    t = z.shape[0]
    tm = _row_tile(t)

    def body(z_ref, qw_ref, kw_ref, ck_ref, sk_ref, dcq_ref, dckv_ref, dkv_ref, dz_ref, dqw_ref, dkw_ref):
        zz = z_ref[...]
        dq, dqw = _rms_bwd_math(zz[:, Z_CQ:Z_CKV], qw_ref[...], dcq_ref[...])
        dk, dkw = _rms_bwd_math(zz[:, Z_CKV:Z_KPE], kw_ref[...], dckv_ref[...])
        dkv_v = dkv_ref[...]
        dkr = jnp.zeros((tm, LANE), F32)
        for hd in range(MLA_HEADS):
            dkr = dkr + dkv_v[:, 2 * LANE * hd:2 * LANE * hd + LANE].astype(F32)
        dz_ref[...] = jnp.concatenate(
            [dq, dk, dkr * ck_ref[...], dkr * sk_ref[...], jnp.zeros((tm, Z_HQ - Z_KPESW - LANE), F32)], axis=1
        ).astype(BF16)

        @pl.when(pl.program_id(0) == 0)
        def _():
            dqw_ref[...] = jnp.zeros_like(dqw_ref)
            dkw_ref[...] = jnp.zeros_like(dkw_ref)

        dqw_ref[...] += dqw
        dkw_ref[...] += dkw

    def rows(wd):
        return pl.BlockSpec((tm, wd), lambda i: (i, 0))

    def vec(wd):
        return pl.BlockSpec((1, wd), lambda i: (0, 0))

    return pl.pallas_call(
        body, name="mla_prep_bwd", grid=(t // tm,),
        in_specs=[rows(Z_HQ), vec(Q_LORA), vec(KV_LORA), rows(LANE), rows(LANE), rows(Q_LORA), rows(KV_LORA),
                  rows(2 * LANE * MLA_HEADS)],
        out_specs=[rows(Z_HQ), vec(Q_LORA), vec(KV_LORA)],
        out_shape=[jax.ShapeDtypeStruct((t, Z_HQ), BF16), jax.ShapeDtypeStruct((1, Q_LORA), F32),
                   jax.ShapeDtypeStruct((1, KV_LORA), F32)],
        compiler_params=_cparams(("arbitrary",)),
    )(z, qn_w, kvn_w, ck, sk, dcq, dckv, dkv)


def _rope_bwd_q(dq, cq, sq):
    t, wq = dq.shape
    tm = _row_tile(t)

    def body(dq_ref, c_ref, s_ref, o_ref):
        d = dq_ref[...] * ATTN_SCALE
        c8 = jnp.concatenate([c_ref[...]] * MLA_HEADS, axis=1)
        s8 = jnp.concatenate([s_ref[...]] * MLA_HEADS, axis=1)
        o_ref[...] = jnp.concatenate([d * c8, d * s8], axis=1).astype(BF16)

    return pl.pallas_call(
        body, name="rope_bwd_q", grid=(t // tm,),
        in_specs=[pl.BlockSpec((tm, wq), lambda i: (i, 0)), pl.BlockSpec((tm, LANE), lambda i: (i, 0)),
                  pl.BlockSpec((tm, LANE), lambda i: (i, 0))],
        out_specs=pl.BlockSpec((tm, 2 * wq), lambda i: (i, 0)),
        out_shape=jax.ShapeDtypeStruct((t, 2 * wq), BF16),
        compiler_params=_cparams(("parallel",)),
    )(dq, cq, sq)


def _row_vector(col):
    return jnp.broadcast_to(col, (col.shape[0], LANE)).T[0:8, :]


def _wide_block(t):
    for b in (768, 512, 256):
        if t % b == 0:
            return b
    raise ValueError(t)


def _attn_fwd(q, kv):
    t = q.shape[0]
    bq, bk = ATTN_BLOCK, _wide_block(t)
    nq = t // bq

    def body(q_ref, k_ref, v_ref, o_ref, lse_ref):
        i = pl.program_id(1)
        qv = q_ref[...]
        qpos = i * bq + lax.broadcasted_iota(jnp.int32, (bq, 1), 0)
        n = (bq * (i + 1) + bk - 1) // bk

        def block(j, carry, masked):
            m, l, acc = carry
            rows = pl.ds(pl.multiple_of(j * bk, bk), bk)
            s = _dot_nt(qv, k_ref[rows, :])
            if masked:
                kpos = j * bk + lax.broadcasted_iota(jnp.int32, (1, bk), 1)
                s = jnp.where(kpos <= qpos, s, NEG_BIG)
            m_new = jnp.maximum(m, jnp.max(s, axis=1, keepdims=True))
            p = jnp.exp2((s - m_new) * ATTN_C2)
            a = jnp.exp2((m - m_new) * ATTN_C2)
            return m_new, a * l + jnp.sum(p, axis=1, keepdims=True), a * acc + _dot(_bf(p), v_ref[rows, :])

        init = (jnp.full((bq, 1), NEG_BIG, F32), jnp.zeros((bq, 1), F32), jnp.zeros((bq, LANE), F32))
        carry = lax.fori_loop(0, n - 1, functools.partial(block, masked=False), init)
        m, l, acc = block(n - 1, carry, True)
        o_ref[...] = (acc / l).astype(o_ref.dtype)
        lse_ref[0] = _row_vector(m * ATTN_C2 + jnp.log(l) * LOG2E)

    return pl.pallas_call(
        body, name="attn_fwd", grid=(MLA_HEADS, nq),
        in_specs=[pl.BlockSpec((bq, LANE), lambda h, i: (i, h)),
                  pl.BlockSpec((t, LANE), lambda h, i: (0, 2 * h)),
                  pl.BlockSpec((t, LANE), lambda h, i: (0, 2 * h + 1))],
        out_specs=[pl.BlockSpec((bq, LANE), lambda h, i: (i, h)),
                   pl.BlockSpec((1, 8, bq), lambda h, i: (h, 0, i))],
        out_shape=[jax.ShapeDtypeStruct((t, MLA_HEADS * LANE), BF16), jax.ShapeDtypeStruct((MLA_HEADS, 8, t), F32)],
        compiler_params=_cparams(("parallel", "arbitrary")),
    )(q, kv, kv)


def _attn_bwd(q, kv, o, do, lse):
    t = q.shape[0]
    bk, bw = ATTN_BLOCK, _wide_block(t)
    nk, nw = t // bk, t // bw
    lse = lse[:, 0, :].reshape(MLA_HEADS, nw, 1, bw)

    def body(q_ref, o_ref, do_ref, k_ref, v_ref, lse_ref, dq_ref, dkv_ref, dl_ref):
        j = pl.program_id(1)

        @pl.when(j == 0)
        def _():
            dq_ref[...] = jnp.zeros_like(dq_ref)
            for i in range(nw):
                rows = slice(i * bw, (i + 1) * bw)
                d = jnp.sum(o_ref[rows, :].astype(F32) * do_ref[rows, :].astype(F32), axis=1, keepdims=True)
                dl_ref[i] = jnp.broadcast_to(d, (bw, LANE)).T[0:1, :]

        kb = k_ref[...]
        vb = v_ref[...]
        kpos = j * bk + lax.broadcasted_iota(jnp.int32, (bk, 1), 0)

        def block(i, carry, masked):
            dk, dv = carry
            rows = pl.ds(pl.multiple_of(i * bw, bw), bw)
            qb = q_ref[rows, :]
            dob = do_ref[rows, :]
            pt = jnp.exp2(_dot_nt(kb, qb) * ATTN_C2 - lse_ref[0, i])
            if masked:
                qpos = i * bw + lax.broadcasted_iota(jnp.int32, (1, bw), 1)
                pt = jnp.where(kpos <= qpos, pt, 0.0)
            dv = dv + _dot(_bf(pt), dob)
            dst = _bf(pt * (_dot_nt(vb, dob) - dl_ref[i]))
            dk = dk + _dot(dst, qb)
            dq_ref[rows, :] += _dot_tn(dst, kb)
            return dk, dv

        i0 = (bk * j) // bw
        carry = block(i0, (jnp.zeros((bk, LANE), F32), jnp.zeros((bk, LANE), F32)), True)
        dk, dv = lax.fori_loop(i0 + 1, nw, functools.partial(block, masked=False), carry)
        dkv_ref[...] = jnp.concatenate([dk * ATTN_SCALE, dv], axis=1).astype(dkv_ref.dtype)

    head_rows = pl.BlockSpec((t, LANE), lambda h, j: (0, h))
    return pl.pallas_call(
        body, name="attn_bwd", grid=(MLA_HEADS, nk),
        in_specs=[head_rows, head_rows, head_rows,
                  pl.BlockSpec((bk, LANE), lambda h, j: (j, 2 * h)), pl.BlockSpec((bk, LANE), lambda h, j: (j, 2 * h + 1)),
                  pl.BlockSpec((1, nw, 1, bw), lambda h, j: (h, 0, 0, 0))],
        out_specs=[head_rows, pl.BlockSpec((bk, 2 * LANE), lambda h, j: (j, h))],
        out_shape=[jax.ShapeDtypeStruct((t, MLA_HEADS * LANE), F32), jax.ShapeDtypeStruct((t, 2 * MLA_HEADS * LANE), BF16)],
        scratch_shapes=[pltpu.VMEM((nw, 1, bw), F32)],
        compiler_params=_cparams(("parallel", "arbitrary")),
    )(q, o, do, kv, kv, lse)


def _hg_tables():
    c = HG_CHUNK
    tri = (jnp.arange(c)[:, None] >= jnp.arange(c)[None, :]).astype(F32)
    mats = [tri]
    for m in HG_LEVELS:
        ref = (jnp.arange(c) // (2 * m)) * (2 * m) + m - 1
        mats.append((ref[:, None] >= jnp.arange(c)[None, :]).astype(F32))
    call = jnp.concatenate(mats, axis=0)
    return call.astype(BF16), call.T.astype(BF16)


def _table_dot(table, x):
    hi = _bf(x)
    rest = x - hi.astype(F32)
    mid = _bf(rest)
    lo = _bf(rest - mid.astype(F32))
    out = _dot(table, jnp.concatenate([hi, mid, lo], axis=1))
    n = x.shape[1]
    return out[:, 0:n] + out[:, n:2 * n] + out[:, 2 * n:3 * n]


def _hg_gates(hq, hf, lb):
    sg = _sigmoid(hf)
    sn = _sigmoid(-hf)
    f = lb + (1.0 - lb) * sg
    q = hq * _sigmoid(hq)
    g = jnp.log(jnp.maximum(f, F_MIN))
    k = (1.0 - lb) * sn
    return q, k, g, f, sg, sn


def _hg_level_masks(m):
    c = HG_CHUNK
    row = lax.broadcasted_iota(jnp.int32, (c, 1), 0)
    col = lax.broadcasted_iota(jnp.int32, (1, c), 1)
    shift = (2 * m).bit_length() - 1
    up = (row & m) != 0
    same = lax.shift_right_logical(row, shift) == lax.shift_right_logical(col, shift)
    return up, same


def _hg_level_factors(b, bref, up):
    lo = jnp.logical_not(up)
    eq = jnp.where(up, jnp.exp(jnp.where(up, b - bref, 0.0)), 0.0)
    ek = jnp.where(lo, jnp.exp(jnp.where(lo, bref - b, 0.0)), 0.0)
    return eq, ek


def _hg_intra(q, k, ball):
    c = HG_CHUNK
    b = ball[0:c]
    row = lax.broadcasted_iota(jnp.int32, (c, 1), 0)
    col = lax.broadcasted_iota(jnp.int32, (1, c), 1)
    a = jnp.where(row == col, _dot_nt(_bf(q), _bf(k)), 0.0)
    parts = []
    for lv, m in enumerate(HG_LEVELS):
        up, same = _hg_level_masks(m)
        eq, ek = _hg_level_factors(b, ball[(lv + 1) * c:(lv + 2) * c], up)
        qt, kt = q * eq, k * ek
        a = a + jnp.where(same, _dot_nt(_bf(qt), _bf(kt)), 0.0)
        parts.append((eq, ek, qt, kt))
    return a, parts


def _hg_chunk_fwd(hq, hf, hi, hg, lb, nw, st, call):
    q, k, g, _, _, _ = _hg_gates(hq, hf, lb)
    ball = _table_dot(call, g)
    b = ball[0:HG_CHUNK]
    a, _ = _hg_intra(q, k, ball)
    v16 = _bf(hi)
    o = _dot(_bf(a * HG_SCALE), v16) + _dot_nt(_bf(q * jnp.exp(b) * HG_SCALE), _bf(st))
    bl = b[HG_CHUNK - 1:HG_CHUNK]
    ke = k * jnp.exp(bl - b)
    st_new = st * jnp.exp(bl) + _dot(_bf(hi.T), _bf(ke))
    r = lax.rsqrt(jnp.mean(o * o, axis=-1, keepdims=True) + EPS)
    y = o * r * nw * (hg * _sigmoid(hg))
    return y, st_new


def _hg_chunk_bwd(hq, hf, hi, hg, lb, nw, st, call, call_t, dy, dst_new):
    c = HG_CHUNK
    q, k, g, f, sg, sn = _hg_gates(hq, hf, lb)
    ball = _table_dot(call, g)
    b = ball[0:c]
    a, parts = _hg_intra(q, k, ball)
    v16 = _bf(hi)
    st16 = _bf(st)
    eb = jnp.exp(b)
    qe = q * eb * HG_SCALE
    a16 = _bf(a * HG_SCALE)
    o = _dot(a16, v16) + _dot_nt(_bf(qe), st16)
    bl = b[c - 1:c]
    el = jnp.exp(bl)
    x = jnp.exp(bl - b)
    ke = k * x
    r = lax.rsqrt(jnp.mean(o * o, axis=-1, keepdims=True) + EPS)
    shg = _sigmoid(hg)
    gate = hg * shg
    ohat = o * r
    don = dy * gate
    dhg = dy * ohat * nw * (shg * (1.0 + hg * (1.0 - shg)))
    dnw = jnp.sum(don * ohat, axis=0, keepdims=True)
    tt = don * nw
    do = r * (tt - ohat * jnp.mean(tt * ohat, axis=-1, keepdims=True))
    do16 = _bf(do)
    dst16 = _bf(dst_new)
    da = _dot_nt(do16, v16) * HG_SCALE
    dv = _dot(_bf(a16.astype(F32).T), do16) + _dot_nt(_bf(ke), dst16)
    dqe = _dot(do16, st16)
    dke = _dot(v16, dst16)
    dst = dst_new * el + _dot(_bf(do.T), _bf(qe))
    dbl = jnp.sum(dst_new * st, axis=0, keepdims=True) * el
    dk = dke * x
    dxa = dke * ke
    db = dqe * qe - dxa
    dbl = dbl + jnp.sum(dxa, axis=0, keepdims=True)
    dq = dqe * eb * HG_SCALE
    row = lax.broadcasted_iota(jnp.int32, (c, 1), 0)
    col = lax.broadcasted_iota(jnp.int32, (1, c), 1)
    ddiag = jnp.sum(jnp.where(row == col, da, 0.0), axis=1, keepdims=True)
    dq = dq + ddiag * k
    dk = dk + ddiag * q
    dball = []
    for (eq, ek, qt, kt), m in zip(parts, HG_LEVELS):
        _, same = _hg_level_masks(m)
        gm = jnp.where(same, da, 0.0)
        dqt = _dot(_bf(gm), _bf(kt))
        dkt = _dot(_bf(gm.T), _bf(qt))
        dq = dq + dqt * eq
        dk = dk + dkt * ek
        darg = dqt * qt - dkt * kt
        db = db + darg
        dball.append(-darg)
    db = db + jnp.where(row == c - 1, dbl, 0.0)
    dg = _table_dot(call_t, jnp.concatenate([db] + dball, axis=0))
    shq = _sigmoid(hq)
    dhq = dq * (shq * (1.0 + hq * (1.0 - shq)))
    df = jnp.where(f > F_MIN, dg / jnp.maximum(f, F_MIN), 0.0)
    dlb = jnp.sum(df * (1.0 - sg) - dk * sn, axis=0, keepdims=True)
    dhf = df * (1.0 - lb) * sg * (1.0 - sg) - dk * (1.0 - lb) * sn * (1.0 - sn)
    return dhq, dhf, dv, dhg, dlb, dnw, dst


def _hg_col(group, h):
    return group // LANE + h


def _hgrn_fwd(z, lb, nw, call):
    t = z.shape[0]
    c, cs = HG_CHUNK, HG_CHUNKS_PER_STEP
    rows = c * cs
    nsteps = t // rows

    def body(hq_ref, hf_ref, hi_ref, hg_ref, lb_ref, nw_ref, call_ref, y_ref, sv_ref, st_ref):
        @pl.when(pl.program_id(1) == 0)
        def _():
            st_ref[...] = jnp.zeros_like(st_ref)

        for u in range(cs):
            sl = slice(u * c, (u + 1) * c)
            st = st_ref[...]
            sv_ref[0, u] = st
            y, st_new = _hg_chunk_fwd(hq_ref[sl, :], hf_ref[sl, :], hi_ref[sl, :], hg_ref[sl, :], lb_ref[...], nw_ref[...],
                                      st, call_ref[...])
            y_ref[sl, :] = y.astype(y_ref.dtype)
            st_ref[...] = st_new

    def zcol(group):
        return pl.BlockSpec((rows, LANE), functools.partial(lambda h, i, g: (i, _hg_col(g, h)), g=group))

    ncall = call.shape[0]
    return pl.pallas_call(
        body, name="hgrn_fwd", grid=(HG_HEADS, nsteps),
        in_specs=[zcol(Z_HQ), zcol(Z_HF), zcol(Z_HI), zcol(Z_HG),
                  pl.BlockSpec((1, LANE), lambda h, i: (0, h)), pl.BlockSpec((1, LANE), lambda h, i: (0, 0)),
                  pl.BlockSpec((ncall, c), lambda h, i: (0, 0))],
        out_specs=[pl.BlockSpec((rows, LANE), lambda h, i: (i, h)),
                   pl.BlockSpec((1, cs, c, c), lambda h, i: (h, i, 0, 0))],
        out_shape=[jax.ShapeDtypeStruct((t, HG_HEADS * LANE), BF16), jax.ShapeDtypeStruct((HG_HEADS, t // c, c, c), F32)],
        scratch_shapes=[pltpu.VMEM((c, c), F32)],
        compiler_params=_cparams(("parallel", "arbitrary")),
    )(z, z, z, z, lb, nw, call)


def _hgrn_bwd(z, lb, nw, call, call_t, saved, dy):
    t = z.shape[0]
    c, cs = HG_CHUNK, HG_CHUNKS_PER_STEP
    rows = c * cs
    nsteps = t // rows

    def body(hq_ref, hf_ref, hi_ref, hg_ref, lb_ref, nw_ref, call_ref, callt_ref, sv_ref, dy_ref,
             dhq_ref, dhf_ref, dhi_ref, dhg_ref, dlb_ref, dnw_ref, dst_ref):
        h, i = pl.program_id(0), pl.program_id(1)

        @pl.when(i == 0)
        def _():
            dst_ref[...] = jnp.zeros_like(dst_ref)
            dlb_ref[...] = jnp.zeros_like(dlb_ref)

        @pl.when((i == 0) & (h == 0))
        def _():
            dnw_ref[...] = jnp.zeros_like(dnw_ref)

        for u in reversed(range(cs)):
            sl = slice(u * c, (u + 1) * c)
            dhq, dhf, dhi, dhg, dlb, dnw, dst = _hg_chunk_bwd(
                hq_ref[sl, :], hf_ref[sl, :], hi_ref[sl, :], hg_ref[sl, :], lb_ref[...], nw_ref[...], sv_ref[0, u],
                call_ref[...], callt_ref[...], dy_ref[sl, :].astype(F32), dst_ref[...])
            dhq_ref[sl, :] = dhq.astype(BF16)
            dhf_ref[sl, :] = dhf.astype(BF16)
            dhi_ref[sl, :] = dhi.astype(BF16)
            dhg_ref[sl, :] = dhg.astype(BF16)
            dlb_ref[...] += dlb
            dnw_ref[...] += dnw
            dst_ref[...] = dst

    def zcol(group):
        return pl.BlockSpec((rows, LANE), functools.partial(lambda h, i, g: (nsteps - 1 - i, _hg_col(g, h)), g=group))

    head_rows = pl.BlockSpec((rows, LANE), lambda h, i: (nsteps - 1 - i, h))
    ncall = call.shape[0]
    piece = jax.ShapeDtypeStruct((t, HG_HEADS * LANE), BF16)
    return pl.pallas_call(
        body, name="hgrn_bwd", grid=(HG_HEADS, nsteps),
        in_specs=[zcol(Z_HQ), zcol(Z_HF), zcol(Z_HI), zcol(Z_HG),
                  pl.BlockSpec((1, LANE), lambda h, i: (0, h)), pl.BlockSpec((1, LANE), lambda h, i: (0, 0)),
                  pl.BlockSpec((ncall, c), lambda h, i: (0, 0)), pl.BlockSpec((c, ncall), lambda h, i: (0, 0)),
                  pl.BlockSpec((1, cs, c, c), lambda h, i: (h, nsteps - 1 - i, 0, 0)), head_rows],
        out_specs=[head_rows, head_rows, head_rows, head_rows,
                   pl.BlockSpec((1, LANE), lambda h, i: (0, h)), pl.BlockSpec((1, LANE), lambda h, i: (0, 0))],
        out_shape=[piece, piece, piece, piece,
                   jax.ShapeDtypeStruct((1, HG_HEADS * LANE), F32), jax.ShapeDtypeStruct((1, LANE), F32)],
        scratch_shapes=[pltpu.VMEM((c, c), F32)],
        compiler_params=_cparams(("arbitrary", "arbitrary")),
    )(z, z, z, z, lb, nw, call, call_t, saved, dy)


def _lb_fwd(raw):
    nl = raw.shape[0]

    def body(r_ref, o_ref):
        x = r_ref[...]
        e = jnp.exp(x - jnp.max(x, axis=0, keepdims=True))
        p = e / jnp.sum(e, axis=0, keepdims=True)
        acc = jnp.zeros_like(p[0:1])
        for l in range(nl):
            if l > 0:
                acc = acc + p[l:l + 1]
            o_ref[l:l + 1, :] = acc

    return pl.pallas_call(body, name="lb_fwd", out_shape=jax.ShapeDtypeStruct(raw.shape, F32))(raw)


def _lb_bwd(raw, dlbs):
    nl = raw.shape[0]

    def body(r_ref, d_ref, o_ref):
        x = r_ref[...]
        e = jnp.exp(x - jnp.max(x, axis=0, keepdims=True))
        p = e / jnp.sum(e, axis=0, keepdims=True)
        d = d_ref[...]
        dps = [jnp.zeros_like(d[0:1])]
        for i in range(1, nl):
            acc = d[i:i + 1]
            for l in range(i + 1, nl):
                acc = acc + d[l:l + 1]
            dps.append(acc)
        dot = dps[0] * p[0:1]
        for i in range(1, nl):
            dot = dot + dps[i] * p[i:i + 1]
        for i in range(nl):
            o_ref[i:i + 1, :] = p[i:i + 1] * (dps[i] - dot)

    return pl.pallas_call(body, name="lb_bwd", out_shape=jax.ShapeDtypeStruct(raw.shape, F32))(raw, dlbs)


def _exchange(name, srcs, gather):
    nk = len(srcs)

    def body(*refs):
        src_refs, out_refs = refs[:nk], refs[nk:2 * nk]
        send_sems, recv_sems, local_sems = refs[2 * nk:]
        me = 4 * lax.axis_index("x") + 2 * lax.axis_index("y") + lax.axis_index("c")
        local = []
        for k in range(nk):
            own = src_refs[k] if gather else src_refs[k].at[me]
            cp = pltpu.make_async_copy(own, out_refs[k].at[me], local_sems.at[k])
            cp.start()
            local.append(cp)
        copies = []
        for r in range(1, N_DEV):
            to = (me + r) % N_DEV
            for k in range(nk):
                cp = pltpu.make_async_remote_copy(
                    src_ref=src_refs[k] if gather else src_refs[k].at[to],
                    dst_ref=out_refs[k].at[me],
                    send_sem=send_sems.at[k * N_DEV + r], recv_sem=recv_sems.at[k * N_DEV + r],
                    device_id=(to // 4, (to // 2) % 2, to % 2), device_id_type=pl.DeviceIdType.MESH)
                cp.start()
                copies.append(cp)
        for cp in copies:
            cp.wait()
        for cp in local:
            cp.wait()

    any_spec = pl.BlockSpec(memory_space=pl.ANY)
    out_shape = [jax.ShapeDtypeStruct(((N_DEV,) + s.shape) if gather else s.shape, s.dtype) for s in srcs]
    return pl.pallas_call(
        body, name=name,
        in_specs=[any_spec] * nk, out_specs=[any_spec] * nk, out_shape=out_shape,
        scratch_shapes=[pltpu.SemaphoreType.DMA((nk * N_DEV,)), pltpu.SemaphoreType.DMA((nk * N_DEV,)),
                        pltpu.SemaphoreType.DMA((nk,))],
    )(*srcs)


def _adam_math(g, w, m, v):
    m2 = ADAM_B1 * m + (1.0 - ADAM_B1) * g
    v2 = ADAM_B2 * v + (1.0 - ADAM_B2) * (g * g)
    m_hat = m2 / (1.0 - ADAM_B1 ** ADAM_STEP)
    v_hat = v2 / (1.0 - ADAM_B2 ** ADAM_STEP)
    return -ADAM_LR * (m_hat / (jnp.sqrt(v_hat) + ADAM_EPS) + ADAM_WD * w), m2, v2


def _sum_slots(ref):
    g = ref[0].astype(F32)
    for s in range(1, N_DEV):
        g = g + ref[s].astype(F32)
    return g


def _adam_sharded(name, slots, w, m, v):
    nl, a, b = w.shape
    ta = a
    for cand in range(8, 257, 8):
        if a % cand == 0:
            ta = cand

    def body(s_ref, w_ref, m_ref, v_ref, g_ref, d_ref, m2_ref, v2_ref):
        g = _sum_slots(s_ref)
        d, m2, v2 = _adam_math(g, w_ref[...], m_ref[...], v_ref[...])
        g_ref[...] = g
        d_ref[...] = d
        m2_ref[...] = m2
        v2_ref[...] = v2

    blk = pl.BlockSpec((1, ta, b), lambda l, i: (l, i, 0))
    sds = jax.ShapeDtypeStruct(w.shape, F32)
    return pl.pallas_call(
        body, name=name, grid=(nl, a // ta),
        in_specs=[pl.BlockSpec((N_DEV, 1, ta, b), lambda l, i: (0, l, i, 0)), blk, blk, blk],
        out_specs=[blk] * 4, out_shape=[sds] * 4,
        compiler_params=_cparams(("parallel", "parallel")),
    )(slots, w, m, v)


def _sum_replicated(slots):
    def body(s_ref, g_ref):
        g_ref[...] = _sum_slots(s_ref)

    return pl.pallas_call(body, name="sum_small", out_shape=jax.ShapeDtypeStruct(slots.shape[1:], F32))(slots)


def _adam_small(name, g, w, m, v):
    def body(g_ref, w_ref, m_ref, v_ref, d_ref, m2_ref, v2_ref):
        d, m2, v2 = _adam_math(g_ref[...], w_ref[...], m_ref[...], v_ref[...])
        d_ref[...] = d
        m2_ref[...] = m2
        v2_ref[...] = v2

    sds = jax.ShapeDtypeStruct(w.shape, F32)
    return pl.pallas_call(body, name=name, out_shape=[sds] * 3)(g, w, m, v)


def _cols_full(g):
    return jnp.transpose(g, (1, 0, 2)).reshape(g.shape[1], -1)


def _cols_shards(w):
    k = w.shape[0]
    return jnp.transpose(w.reshape(k, N_DEV, -1), (1, 0, 2))


def _swap_halves(x):
    half = x.shape[-1] // 2
    return jnp.concatenate([x[..., half:], x[..., :half]], axis=-1)


def _zeros_like_cols(x, n):
    return jnp.zeros(x.shape[:-1] + (n,), x.dtype)


def _w_in_internal(w):
    d = w.shape[0]
    kpe = w[:, 640:672]
    z64, z32 = jnp.zeros((d, 64), w.dtype), jnp.zeros((d, 32), w.dtype)
    return jnp.concatenate(
        [w[:, 0:640], z64, kpe, z32, z64, _swap_halves(kpe), z32, jnp.zeros((d, Z_HQ - Z_KPESW - LANE), w.dtype),
         w[:, 672:2720], w[:, 2720:4768]], axis=1)


def _w_in_grad(g):
    kpe = g[:, Z_KPE + 64:Z_KPE + 96] + _swap_halves(g[:, Z_KPESW + 64:Z_KPESW + 96])
    return jnp.concatenate([g[:, 0:640], kpe, g[:, Z_HQ:Z_W]], axis=1)


def _w_uq_internal(w):
    k = w.shape[0]
    w3 = w.reshape(k, MLA_HEADS, QK_NOPE + QK_ROPE)
    nope, rope = w3[..., :QK_NOPE], w3[..., QK_NOPE:]
    plain = jnp.concatenate([nope, rope, _zeros_like_cols(rope, 32)], axis=-1).reshape(k, -1)
    swapped = jnp.concatenate([_zeros_like_cols(nope, 64), _swap_halves(rope), _zeros_like_cols(rope, 32)], axis=-1).reshape(k, -1)
    return jnp.concatenate([plain, swapped], axis=1)


def _w_uq_grad(g):
    k = g.shape[0]
    half = MLA_HEADS * LANE
    g1, g2 = g[:, :half].reshape(k, MLA_HEADS, LANE), g[:, half:].reshape(k, MLA_HEADS, LANE)
    rope = g1[..., 64:96] + _swap_halves(g2[..., 64:96])
    return jnp.concatenate([g1[..., :64], rope], axis=-1).reshape(k, -1)


def _w_ukv_internal(w):
    k = w.shape[0]
    w3 = w.reshape(k, MLA_HEADS, QK_NOPE + V_HEAD)
    kn, vv = w3[..., :QK_NOPE], w3[..., QK_NOPE:]
    z = _zeros_like_cols(kn, 64)
    return jnp.concatenate([kn, z, vv, z], axis=-1).reshape(k, -1)


def _w_ukv_grad(g):
    k = g.shape[0]
    g3 = g.reshape(k, MLA_HEADS, 2 * LANE)
    return jnp.concatenate([g3[..., 0:64], g3[..., LANE:LANE + 64]], axis=-1).reshape(k, -1)


def _w_pa_internal(w):
    n = w.shape[1]
    w3 = w.reshape(MLA_HEADS, V_HEAD, n)
    return jnp.concatenate([w3, jnp.zeros_like(w3)], axis=1).reshape(-1, n)


def _w_pa_grad(g):
    n = g.shape[1]
    return g.reshape(MLA_HEADS, 2 * V_HEAD, n)[:, :V_HEAD].reshape(-1, n)


def _rope_tables(t):
    half = QK_ROPE // 2
    inv = ROPE_THETA ** (-jnp.arange(half, dtype=F32) / half)
    ang = jnp.arange(t, dtype=F32)[:, None] * inv[None, :]
    cos, sin = jnp.cos(ang), jnp.sin(ang)
    one, zero = jnp.ones((t, 64), F32), jnp.zeros((t, 64), F32)
    z32 = jnp.zeros((t, 32), F32)
    cq = jnp.concatenate([one, cos, cos, z32], axis=1)
    ck = jnp.concatenate([zero, cos, cos, z32], axis=1)
    sq = jnp.concatenate([zero, -sin, sin, z32], axis=1)
    return cq, ck, sq


def _ffn_fwd(tag, h, nw, w_gu, w_down):
    t, d = h.shape
    dff = w_down.shape[0]
    tm, tn = _row_tile(t), _tile(dff, 1408)
    xn = _rms_fwd(tag + "_norm", h, nw)

    def act_fn(accs, _):
        g, u = accs
        return g, u, g * _sigmoid(g) * u

    spec = _tile_spec(tm, tn)
    g, u, act = _mm(tag + "_gu", [xn], [(0, w_gu, 0), (0, w_gu, dff // tn)], [],
                    [((t, dff), BF16) + spec] * 3, act_fn, tm=tm, tn=tn, n=dff)
    tn2 = _tile(d, 1024)
    h2, = _mm(tag + "_down", [act], [(0, w_down, 0)], [(h,) + _tile_spec(tm, tn2)],
              [((t, d), F32) + _tile_spec(tm, tn2)], lambda accs, ex: (ex[0] + 0.5 * accs[0],), tm=tm, tn=tn2, n=d)
    return h2, (h, xn, g, u, act)


def _ffn_bwd(tag, dh2, saved, nw, w_gu_t, w_down_t):
    h, xn, g, u, act = saved
    t, d = h.shape
    dff = act.shape[1]
    tm, tn = _row_tile(t), _tile(dff, 1408)

    def dact_fn(accs, ex):
        gg, uu = ex[0].astype(F32), ex[1].astype(F32)
        da = 0.5 * accs[0]
        sg = _sigmoid(gg)
        return da * uu * (sg * (1.0 + gg * (1.0 - sg))), da * (gg * sg)

    spec = _tile_spec(tm, tn)
    dg, du = _mm(tag + "_dact", [dh2], [(0, w_down_t, 0)], [(g,) + spec, (u,) + spec],
                 [((t, dff), BF16) + spec] * 2, dact_fn, tm=tm, tn=tn, n=dff)
    dw_down = _mm_tn(tag + "_dwdown", act, dh2, alpha=0.5, out_dtype=BF16)
    tn2 = _tile(d, 512)
    dxn, = _mm(tag + "_dxn", [dg, du], [(0, w_gu_t[:dff], 0), (1, w_gu_t[dff:], 0)], [],
               [((t, d), F32) + _tile_spec(tm, tn2)], lambda accs, _: (accs[0] + accs[1],), tm=tm, tn=tn2, n=d)
    dw_gu = jnp.concatenate([_mm_tn(tag + "_dwg", xn, dg, alpha=1.0, out_dtype=BF16),
                             _mm_tn(tag + "_dwu", xn, du, alpha=1.0, out_dtype=BF16)], axis=1)
    dh, dnw = _rms_bwd(tag + "_dnorm", h, nw, dxn, dh2)
    return dh, dnw, dw_gu, dw_down


def _kv_pattern(kr):
    z = jnp.zeros_like(kr)
    return jnp.concatenate([kr, z] * MLA_HEADS, axis=1)


def _mix_fwd(h, p, tabs, lb, call):
    t, d = h.shape
    tm = _row_tile(t)
    cq_t, ck_t, sq_t = tabs
    u = _rms_fwd("mix_norm", h, p["mix_norm"])
    tnz = _tile(Z_W, 1024)
    z, = _mm("mix_in", [u], [(0, p["w_in"], 0)], [], [((t, Z_W), F32) + _tile_spec(tm, tnz)], lambda a, _: (a[0],),
             tm=tm, tn=tnz, n=Z_W)
    cqn, ckvn, krot = _mla_prep(z, p["q_norm"], p["kv_norm"], ck_t, sq_t)
    wq = MLA_HEADS * LANE
    lane_rows = lambda i, j: (i, 0)

    def q_fn(accs, ex):
        c8 = jnp.concatenate([ex[0]] * MLA_HEADS, axis=1)
        s8 = jnp.concatenate([ex[1]] * MLA_HEADS, axis=1)
        return (accs[0] * c8 + accs[1] * s8,)

    q, = _mm("mla_q", [cqn], [(0, p["w_uq"], 0), (0, p["w_uq"], 1)],
             [(cq_t, (tm, LANE), lane_rows), (sq_t, (tm, LANE), lane_rows)],
             [((t, wq), BF16) + _tile_spec(tm, wq)], q_fn, tm=tm, tn=wq, n=wq)
    kv, = _mm("mla_kv", [ckvn], [(0, p["w_ukv"], 0)], [(krot, (tm, LANE), lane_rows)],
              [((t, 2 * wq), BF16) + _tile_spec(tm, 2 * wq)], lambda a, ex: (a[0] + _kv_pattern(ex[0]),),
              tm=tm, tn=2 * wq, n=2 * wq)
    o_a, lse = _attn_fwd(q, kv)
    o_b, st_saved = _hgrn_fwd(z, lb, p["hg_norm"], call)
    tn = _tile(d, 512)

    def merge_fn(accs, ex):
        ya, yb = accs
        return ya, yb, _sigmoid(ex[0]) * ya + _sigmoid(ex[1]) * yb

    spec = _tile_spec(tm, tn)
    ya, yb, merged = _mm("mix_merge", [o_a, o_b], [(0, p["w_pa"], 0), (1, p["w_pr"], 0)],
                         [(z,) + _tile_spec(tm, tn, Z_GA // tn), (z,) + _tile_spec(tm, tn, Z_GB // tn)],
                         [((t, d), BF16) + spec] * 3, merge_fn, tm=tm, tn=tn, n=d)
    tn2 = _tile(d, 1024)
    h2, = _mm("mix_out", [merged], [(0, p["w_out"], 0)], [(h,) + _tile_spec(tm, tn2)],
              [((t, d), F32) + _tile_spec(tm, tn2)], lambda a, ex: (ex[0] + a[0],), tm=tm, tn=tn2, n=d)
    return h2, (h, u, z, cqn, ckvn, q, kv, o_a, lse, o_b, st_saved, ya, yb, merged)


def _mix_bwd(dh2, saved, p, tabs, lb, call, call_t):
    h, u, z, cqn, ckvn, q, kv, o_a, lse, o_b, st_saved, ya, yb, merged = saved
    t, d = h.shape
    tm = _row_tile(t)
    cq_t, ck_t, sq_t = tabs
    tn = _tile(d, 512)
    spec = _tile_spec(tm, tn)

    def dmerge_fn(accs, ex):
        dm = accs[0]
        yav, ybv = ex[0].astype(F32), ex[1].astype(F32)
        sa, sb = _sigmoid(ex[2]), _sigmoid(ex[3])
        return dm * sa, dm * sb, dm * yav * sa * (1.0 - sa), dm * ybv * sb * (1.0 - sb)

    dya, dyb, dga, dgb = _mm("mix_dmerge", [dh2], [(0, p["w_out_t"], 0)],
                             [(ya,) + spec, (yb,) + spec, (z,) + _tile_spec(tm, tn, Z_GA // tn),
                              (z,) + _tile_spec(tm, tn, Z_GB // tn)],
                             [((t, d), BF16) + spec] * 4, dmerge_fn, tm=tm, tn=tn, n=d)
    dw_out = _mm_tn("mix_dwout", merged, dh2, alpha=1.0, out_dtype=BF16)
    wq = MLA_HEADS * LANE
    do_a, = _mm("mix_doa", [dya], [(0, p["w_pa_t"], 0)], [], [((t, wq), BF16) + _tile_spec(tm, wq)], lambda a, _: (a[0],),
                tm=tm, tn=wq, n=wq)
    wr = HG_HEADS * LANE
    do_b, = _mm("mix_dob", [dyb], [(0, p["w_pr_t"], 0)], [], [((t, wr), BF16) + _tile_spec(tm, wr)], lambda a, _: (a[0],),
                tm=tm, tn=wr, n=wr)
    dw_pa = _mm_tn("mix_dwpa", o_a, dya, alpha=1.0, out_dtype=F32)
    dw_pr = _mm_tn("mix_dwpr", o_b, dyb, alpha=1.0, out_dtype=BF16)
    dq, dkv = _attn_bwd(q, kv, o_a, do_a, lse)
    dqq = _rope_bwd_q(dq, cq_t, sq_t)
    dcq, = _mm("mla_dcq", [dqq], [(0, p["w_uq_t"], 0)], [], [((t, Q_LORA), F32) + _tile_spec(tm, Q_LORA)],
               lambda a, _: (a[0],), tm=tm, tn=Q_LORA, n=Q_LORA)
    dckv, = _mm("mla_dckv", [dkv], [(0, p["w_ukv_t"], 0)], [], [((t, KV_LORA), F32) + _tile_spec(tm, KV_LORA)],
                lambda a, _: (a[0],), tm=tm, tn=KV_LORA, n=KV_LORA)
    dw_uq = _mm_tn("mla_dwuq", cqn, dqq, alpha=1.0, out_dtype=F32)
    dw_ukv = _mm_tn("mla_dwukv", ckvn, dkv, alpha=1.0, out_dtype=F32)
    dz_mla, dqn, dkvn = _mla_prep_bwd(z, p["q_norm"], p["kv_norm"], ck_t, sq_t, dcq, dckv, dkv)
    dhq, dhf, dhi, dhg, dlb, dhgn = _hgrn_bwd(z, lb, p["hg_norm"], call, call_t, st_saved, do_b)
    dz = jnp.concatenate([dz_mla, dhq, dhf, dhi, dhg, dga, dgb], axis=1)
    du, = _mm("mix_du", [dz], [(0, p["w_in_t"], 0)], [], [((t, d), F32) + spec], lambda a, _: (a[0],), tm=tm, tn=tn, n=d)
    dw_in = _mm_tn("mix_dwin", u, dz, alpha=1.0, out_dtype=F32)
    dh, dmn = _rms_bwd("mix_dnorm", h, p["mix_norm"], du, dh2)
    grads = dict(mix_norm=dmn, q_norm=dqn, kv_norm=dkvn, hg_norm=dhgn, lb=dlb, w_in=_w_in_grad(dw_in), w_uq=_w_uq_grad(dw_uq),
                 w_ukv=_w_ukv_grad(dw_ukv), w_proj_attn=_w_pa_grad(dw_pa), w_proj_rec=dw_pr, w_out=dw_out)
    return dh, grads


SHARDED = ("ffn1_w_gu", "ffn1_w_down", "w_in", "w_uq", "w_ukv", "w_proj_attn", "w_proj_rec", "w_out", "ffn2_w_gu", "ffn2_w_down")
ROW_SHARDED = ("ffn1_w_down", "w_out", "ffn2_w_down")
SMALL = ("ffn1_norm", "mix_norm", "q_norm", "kv_norm", "hg_lb_raw", "hg_norm", "ffn2_norm", "final_norm")
WEIGHTS = ("meta_tokens", "ffn1_norm", "ffn1_w_gu", "ffn1_w_down", "mix_norm", "w_in", "q_norm", "kv_norm", "w_uq", "w_ukv",
           "hg_lb_raw", "hg_norm", "w_proj_attn", "w_proj_rec", "w_out", "ffn2_norm", "ffn2_w_gu", "ffn2_w_down", "final_norm")


def _pack_small(vals):
    flat = jnp.concatenate([vals[n].reshape(-1) for n in SMALL])
    return flat.reshape(-1, LANE)


def _unpack_small(packed, like):
    flat = packed.reshape(-1)
    out, off = {}, 0
    for n in SMALL:
        size = math.prod(like[n].shape)
        out[n] = flat[off:off + size].reshape(like[n].shape)
        off += size
    return out


def kernel(x, meta_tokens, ffn1_norm, ffn1_w_gu, ffn1_w_down, mix_norm, w_in, q_norm, kv_norm, w_uq, w_ukv, hg_lb_raw, hg_norm, w_proj_attn, w_proj_rec, w_out, ffn2_norm, ffn2_w_gu, ffn2_w_down, final_norm, loss_target, m_meta_tokens, m_ffn1_norm, m_ffn1_w_gu, m_ffn1_w_down, m_mix_norm, m_w_in, m_q_norm, m_kv_norm, m_w_uq, m_w_ukv, m_hg_lb_raw, m_hg_norm, m_w_proj_attn, m_w_proj_rec, m_w_out, m_ffn2_norm, m_ffn2_w_gu, m_ffn2_w_down, m_final_norm, v_meta_tokens, v_ffn1_norm, v_ffn1_w_gu, v_ffn1_w_down, v_mix_norm, v_w_in, v_q_norm, v_kv_norm, v_w_uq, v_w_ukv, v_hg_lb_raw, v_hg_norm, v_w_proj_attn, v_w_proj_rec, v_w_out, v_ffn2_norm, v_ffn2_w_gu, v_ffn2_w_down, v_final_norm):
    w = dict(meta_tokens=meta_tokens, ffn1_norm=ffn1_norm, ffn1_w_gu=ffn1_w_gu, ffn1_w_down=ffn1_w_down, mix_norm=mix_norm,
             w_in=w_in, q_norm=q_norm, kv_norm=kv_norm, w_uq=w_uq, w_ukv=w_ukv, hg_lb_raw=hg_lb_raw, hg_norm=hg_norm,
             w_proj_attn=w_proj_attn, w_proj_rec=w_proj_rec, w_out=w_out, ffn2_norm=ffn2_norm, ffn2_w_gu=ffn2_w_gu,
             ffn2_w_down=ffn2_w_down, final_norm=final_norm)
    mom = dict(meta_tokens=m_meta_tokens, ffn1_norm=m_ffn1_norm, ffn1_w_gu=m_ffn1_w_gu, ffn1_w_down=m_ffn1_w_down,
               mix_norm=m_mix_norm, w_in=m_w_in, q_norm=m_q_norm, kv_norm=m_kv_norm, w_uq=m_w_uq, w_ukv=m_w_ukv,
               hg_lb_raw=m_hg_lb_raw, hg_norm=m_hg_norm, w_proj_attn=m_w_proj_attn, w_proj_rec=m_w_proj_rec, w_out=m_w_out,
               ffn2_norm=m_ffn2_norm, ffn2_w_gu=m_ffn2_w_gu, ffn2_w_down=m_ffn2_w_down, final_norm=m_final_norm)
    var = dict(meta_tokens=v_meta_tokens, ffn1_norm=v_ffn1_norm, ffn1_w_gu=v_ffn1_w_gu, ffn1_w_down=v_ffn1_w_down,
               mix_norm=v_mix_norm, w_in=v_w_in, q_norm=v_q_norm, kv_norm=v_kv_norm, w_uq=v_w_uq, w_ukv=v_w_ukv,
               hg_lb_raw=v_hg_lb_raw, hg_norm=v_hg_norm, w_proj_attn=v_w_proj_attn, w_proj_rec=v_w_proj_rec, w_out=v_w_out,
               ffn2_norm=v_ffn2_norm, ffn2_w_gu=v_ffn2_w_gu, ffn2_w_down=v_ffn2_w_down, final_norm=v_final_norm)
    nl = ffn1_norm.shape[0]
    seq, d = x.shape[1], x.shape[2]
    n_real = N_META + seq
    t = -(-n_real // ROW_ALIGN) * ROW_ALIGN
    me = 4 * lax.axis_index("x") + 2 * lax.axis_index("y") + lax.axis_index("c")

    gathered = _exchange("gather_weights", [w[n].astype(BF16) for n in SHARDED] + [meta_tokens], gather=True)
    full = dict(zip(SHARDED, gathered[:-1]))
    meta_full = _cols_full(gathered[-1])

    def layer_params(l):
        def mat(n):
            g = full[n][:, l]
            return g.reshape(-1, g.shape[-1]) if n in ROW_SHARDED else _cols_full(g)

        p = {}
        for tag in ("ffn1", "ffn2"):
            p[tag + "_w_gu"] = mat(tag + "_w_gu")
            p[tag + "_w_down"] = mat(tag + "_w_down")
            p[tag + "_w_gu_t"] = p[tag + "_w_gu"].T
            p[tag + "_w_down_t"] = p[tag + "_w_down"].T
            p[tag + "_norm"] = w[tag + "_norm"][l:l + 1]
        p["w_in"] = _w_in_internal(mat("w_in"))
        p["w_uq"] = _w_uq_internal(mat("w_uq"))
        p["w_ukv"] = _w_ukv_internal(mat("w_ukv"))
        p["w_pa"] = _w_pa_internal(mat("w_proj_attn"))
        p["w_pr"] = mat("w_proj_rec")
        p["w_out"] = mat("w_out")
        for n in ("w_in", "w_uq", "w_ukv", "w_pa", "w_pr", "w_out"):
            p[n + "_t"] = p[n].T
        for n in ("mix_norm", "q_norm", "kv_norm", "hg_norm"):
            p[n] = w[n][l:l + 1]
        return p

    params = [layer_params(l) for l in range(nl)]
    tabs = _rope_tables(t)
    call, call_t = _hg_tables()
    lbs = _lb_fwd(hg_lb_raw)

    pad = jnp.zeros((t - n_real, d), F32)
    h = jnp.concatenate([meta_full, x[0], pad], axis=0)
    tgt = jnp.concatenate([jnp.zeros((N_META, d), F32), loss_target[0], pad], axis=0)
    saved = []
    for l in range(nl):
        p = params[l]
        h, s1 = _ffn_fwd("ffn1", h, p["ffn1_norm"], p["ffn1_w_gu"], p["ffn1_w_down"])
        h, s2 = _mix_fwd(h, p, tabs, lbs[l:l + 1], call)
        h, s3 = _ffn_fwd("ffn2", h, p["ffn2_norm"], p["ffn2_w_gu"], p["ffn2_w_down"])
        saved.append((s1, s2, s3))
    dh, d_final, loss_part = _loss_head(h, final_norm.reshape(1, d), tgt, n_real)
    loss = lax.psum(loss_part[0, 0], ("x", "y", "c"))

    per_layer = []
    for l in reversed(range(nl)):
        p = params[l]
        s1, s2, s3 = saved[l]
        dh, dn2, dgu2, ddown2 = _ffn_bwd("ffn2", dh, s3, p["ffn2_norm"], p["ffn2_w_gu_t"], p["ffn2_w_down_t"])
        dh, gm = _mix_bwd(dh, s2, p, tabs, lbs[l:l + 1], call, call_t)
        dh, dn1, dgu1, ddown1 = _ffn_bwd("ffn1", dh, s1, p["ffn1_norm"], p["ffn1_w_gu_t"], p["ffn1_w_down_t"])
        gm.update(ffn1_norm=dn1, ffn2_norm=dn2, ffn1_w_gu=dgu1, ffn2_w_gu=dgu2, ffn1_w_down=ddown1, ffn2_w_down=ddown2)
        per_layer.append(gm)
    per_layer.reverse()
    grad_x = dh[N_META:n_real][None]

    def shards(n):
        per = []
        for gm in per_layer:
            g = gm[n].astype(BF16)
            per.append(g.reshape(N_DEV, -1, g.shape[-1]) if n in ROW_SHARDED else _cols_shards(g))
        return jnp.stack(per, axis=1)

    slots = _exchange("scatter_grads", [shards(n) for n in SHARDED], gather=False)
    grads, delta, new_m, new_v = {}, {}, {}, {}
    for n, s in zip(SHARDED, slots):
        grads[n], delta[n], new_m[n], new_v[n] = _adam_sharded("adam_" + n, s, w[n], mom[n], var[n])

    small = {n: jnp.concatenate([gm[n] for gm in per_layer], axis=0) for n in SMALL if n not in ("hg_lb_raw", "final_norm")}
    small["hg_lb_raw"] = _lb_bwd(hg_lb_raw, jnp.concatenate([gm["lb"] for gm in per_layer], axis=0))
    small["final_norm"] = d_final
    packed = jnp.concatenate([_pack_small(small), dh[:N_META].reshape(-1, LANE)], axis=0)
    summed = _sum_replicated(_exchange("gather_small", [packed], gather=True)[0])
    n_small = packed.shape[0] - N_META * d // LANE
    sd, sm, sv = _adam_small("adam_small", summed[:n_small], _pack_small(w), _pack_small(mom), _pack_small(var))
    grads.update(_unpack_small(summed[:n_small], w))
    delta.update(_unpack_small(sd, w))
    new_m.update(_unpack_small(sm, w))
    new_v.update(_unpack_small(sv, w))
    dmeta = lax.dynamic_slice_in_dim(summed[n_small:].reshape(N_META, d), me * (d // N_DEV), d // N_DEV, axis=1)
    grads["meta_tokens"] = dmeta
    delta["meta_tokens"], new_m["meta_tokens"], new_v["meta_tokens"] = _adam_small(
        "adam_meta", dmeta, meta_tokens, m_meta_tokens, v_meta_tokens)

    return (loss, grad_x, *[grads[n] for n in WEIGHTS], *[delta[n] for n in WEIGHTS], *[new_m[n] for n in WEIGHTS],
            *[new_v[n] for n in WEIGHTS])
```

```python
import functools
import math

import jax
import jax.numpy as jnp
from jax import lax
from jax.experimental import pallas as pl
from jax.experimental.pallas import tpu as pltpu

F32 = jnp.float32
BF16 = jnp.bfloat16

N_DEV = 8
N_META = 16
MLA_HEADS = 8
Q_LORA = 384
KV_LORA = 256
QK_NOPE = 64
QK_ROPE = 32
V_HEAD = 64
ROPE_THETA = 10000.0
HG_HEADS = 4
HG_DIM = 128
EPS = 1e-6
NEG_BIG = -1e30
F_MIN = 1e-20
ADAM_LR = 0.001
ADAM_B1 = 0.9
ADAM_B2 = 0.999
ADAM_EPS = 1e-08
ADAM_WD = 0.01
ADAM_STEP = 10

LANE = 128
ROW_ALIGN = 256
ATTN_BLOCK = 256
HG_CHUNK = 128
HG_CHUNKS_PER_STEP = 2
HG_LEVELS = (64, 32, 16, 8, 4, 2, 1)
VMEM_LIMIT = 48 * 1024 * 1024

Z_CQ, Z_CKV, Z_KPE, Z_KPESW, Z_HQ, Z_HF, Z_HI, Z_HG, Z_GA, Z_GB, Z_W = 0, 384, 640, 768, 1024, 1536, 2048, 2560, 3072, 4096, 5120
ATTN_SCALE = float((QK_NOPE + QK_ROPE) ** -0.5)
LOG2E = 1.4426950408889634
ATTN_C2 = ATTN_SCALE * LOG2E
HG_SCALE = float(HG_DIM ** -0.5)


def _cparams(sem):
    return pltpu.CompilerParams(dimension_semantics=sem, vmem_limit_bytes=VMEM_LIMIT)


def _tile(n, cap):
    if n <= cap:
        return n
    best = None
    for t in range(LANE, cap + 1, LANE):
        if n % t == 0:
            best = t
    assert best is not None, (n, cap)
    return best


def _row_tile(m):
    for t in (384, 256, 128):
        if m % t == 0:
            return t
    raise ValueError(m)


def _bf(x):
    return x.astype(BF16)


def _dot(a, b):
    return jnp.dot(a, b, preferred_element_type=F32)


def _dot_nt(a, b):
    return lax.dot_general(a, b, (((1,), (1,)), ((), ())), preferred_element_type=F32)


def _dot_tn(a, b):
    return lax.dot_general(a, b, (((0,), (0,)), ((), ())), preferred_element_type=F32)


def _sigmoid(x):
    return 1.0 / (1.0 + jnp.exp(-x))


def _mm(name, a_list, pairs, extras, outs, fn, *, tm, tn, n):
    m = a_list[0].shape[0]
    na, nb, ne, no = len(a_list), len(pairs), len(extras), len(outs)

    def body(*refs):
        a_refs = refs[:na]
        b_refs = refs[na:na + nb]
        e_refs = refs[na + nb:na + nb + ne]
        o_refs = refs[na + nb + ne:]
        a_vals = [_bf(r[...]) for r in a_refs]
        accs = [_dot(a_vals[ai], b_refs[k][...]) for k, (ai, _, _) in enumerate(pairs)]
        res = fn(accs, [r[...] for r in e_refs])
        for r, v in zip(o_refs, res):
            r[...] = v.astype(r.dtype)

    def spec(block_shape, index_map):
        return pl.BlockSpec(block_shape, functools.partial(lambda j, i, im: im(i, j), im=index_map))

    in_specs = [spec((tm, a.shape[1]), lambda i, j: (i, 0)) for a in a_list]
    for _, b, off in pairs:
        in_specs.append(spec((b.shape[0], tn), functools.partial(lambda i, j, off: (0, j + off), off=off)))
    in_specs += [spec(bs, im) for _, bs, im in extras]
    return pl.pallas_call(
        body, name=name, grid=(n // tn, m // tm),
        in_specs=in_specs,
        out_specs=[spec(bs, im) for _, _, bs, im in outs],
        out_shape=[jax.ShapeDtypeStruct(s, d) for s, d, _, _ in outs],
        compiler_params=_cparams(("parallel", "parallel")),
    )(*a_list, *[b for _, b, _ in pairs], *[e for e, _, _ in extras])


def _tile_spec(tm, tn, col_off=0):
    return (tm, tn), functools.partial(lambda i, j, off: (i, j + off), off=col_off)


def _mm_tn(name, a, b, *, alpha, out_dtype):
    t, k = a.shape
    n = b.shape[1]
    tk, tn, tt = _tile(k, 1408), _tile(n, 1408), _row_tile(t)
    nt = t // tt

    def body(a_ref, b_ref, o_ref, acc_ref):
        s = pl.program_id(2)

        @pl.when(s == 0)
        def _():
            acc_ref[...] = jnp.zeros_like(acc_ref)

        acc_ref[...] += _dot_tn(_bf(a_ref[...]), _bf(b_ref[...]))

        @pl.when(s == nt - 1)
        def _():
            o_ref[...] = (alpha * acc_ref[...]).astype(o_ref.dtype)

    return pl.pallas_call(
        body, name=name, grid=(k // tk, n // tn, nt),
        in_specs=[pl.BlockSpec((tt, tk), lambda i, j, s: (s, i)), pl.BlockSpec((tt, tn), lambda i, j, s: (s, j))],
        out_specs=pl.BlockSpec((tk, tn), lambda i, j, s: (i, j)),
        out_shape=jax.ShapeDtypeStruct((k, n), out_dtype),
        scratch_shapes=[pltpu.VMEM((tk, tn), F32)],
        compiler_params=_cparams(("parallel", "parallel", "arbitrary")),
    )(a, b)


def _rms_parts(x):
    r = lax.rsqrt(jnp.mean(x * x, axis=-1, keepdims=True) + EPS)
    return r, x * r


def _rms_bwd_math(x, w, dxn):
    r, xhat = _rms_parts(x)
    t = dxn * w
    dx = r * (t - xhat * jnp.mean(t * xhat, axis=-1, keepdims=True))
    dw = jnp.sum(dxn * xhat, axis=0, keepdims=True)
    return dx, dw


def _rms_fwd(name, h, w):
    t, d = h.shape
    tm = _row_tile(t)

    def body(h_ref, w_ref, o_ref):
        _, xhat = _rms_parts(h_ref[...])
        o_ref[...] = (xhat * w_ref[...]).astype(o_ref.dtype)

    return pl.pallas_call(
        body, name=name, grid=(t // tm,),
        in_specs=[pl.BlockSpec((tm, d), lambda i: (i, 0)), pl.BlockSpec((1, d), lambda i: (0, 0))],
        out_specs=pl.BlockSpec((tm, d), lambda i: (i, 0)),
        out_shape=jax.ShapeDtypeStruct((t, d), BF16),
        compiler_params=_cparams(("parallel",)),
    )(h, w)


def _rms_bwd(name, h, w, dxn, dh_in):
    t, d = h.shape
    tm = _row_tile(t)

    def body(h_ref, w_ref, dxn_ref, dh_ref, o_ref, dw_ref):
        dx, dw = _rms_bwd_math(h_ref[...], w_ref[...], dxn_ref[...])
        o_ref[...] = dh_ref[...] + dx

        @pl.when(pl.program_id(0) == 0)
        def _():
            dw_ref[...] = jnp.zeros_like(dw_ref)

        dw_ref[...] += dw

    row = pl.BlockSpec((tm, d), lambda i: (i, 0))
    vec = pl.BlockSpec((1, d), lambda i: (0, 0))
    return pl.pallas_call(
        body, name=name, grid=(t // tm,),
        in_specs=[row, vec, row, row],
        out_specs=[row, vec],
        out_shape=[jax.ShapeDtypeStruct((t, d), F32), jax.ShapeDtypeStruct((1, d), F32)],
        compiler_params=_cparams(("arbitrary",)),
    )(h, w, dxn, dh_in)


def _loss_head(h, w, tgt, n_real):
    t, d = h.shape
    tm = _row_tile(t)

    def body(h_ref, w_ref, t_ref, dh_ref, dw_ref, loss_ref):
        i = pl.program_id(0)
        x = h_ref[...]
        wv = w_ref[...]
        _, xhat = _rms_parts(x)
        rows = i * tm + lax.broadcasted_iota(jnp.int32, (tm, 1), 0)
        valid = (rows >= N_META) & (rows < n_real)
        e = jnp.where(valid, xhat * wv - t_ref[...], 0.0)
        dx, dw = _rms_bwd_math(x, wv, e * (1.0 / d))
        dh_ref[...] = dx

        @pl.when(i == 0)
        def _():
            dw_ref[...] = jnp.zeros_like(dw_ref)
            loss_ref[...] = jnp.zeros_like(loss_ref)

        dw_ref[...] += dw
        loss_ref[...] += (0.5 / d) * jnp.sum(jnp.sum(e * e, axis=-1, keepdims=True), axis=0, keepdims=True)

    row = pl.BlockSpec((tm, d), lambda i: (i, 0))
    vec = pl.BlockSpec((1, d), lambda i: (0, 0))
    return pl.pallas_call(
        body, name="loss_head", grid=(t // tm,),
        in_specs=[row, vec, row],
        out_specs=[row, vec, pl.BlockSpec((1, 1), lambda i: (0, 0))],
        out_shape=[jax.ShapeDtypeStruct((t, d), F32), jax.ShapeDtypeStruct((1, d), F32), jax.ShapeDtypeStruct((1, 1), F32)],
        compiler_params=_cparams(("arbitrary",)),
    )(h, w, tgt)


def _mla_prep(z, qn_w, kvn_w, ck, sk):
    t = z.shape[0]
    tm = _row_tile(t)

    def body(z_ref, qw_ref, kw_ref, ck_ref, sk_ref, cq_ref, ckv_ref, kr_ref):
        zz = z_ref[...]
        _, qhat = _rms_parts(zz[:, Z_CQ:Z_CKV])
        _, khat = _rms_parts(zz[:, Z_CKV:Z_KPE])
        cq_ref[...] = (qhat * qw_ref[...]).astype(BF16)
        ckv_ref[...] = (khat * kw_ref[...]).astype(BF16)
        kr_ref[...] = zz[:, Z_KPE:Z_KPESW] * ck_ref[...] + zz[:, Z_KPESW:Z_KPESW + LANE] * sk_ref[...]

    def rows(wd):
        return pl.BlockSpec((tm, wd), lambda i: (i, 0))

    def vec(wd):
        return pl.BlockSpec((1, wd), lambda i: (0, 0))

    return pl.pallas_call(
        body, name="mla_prep", grid=(t // tm,),
        in_specs=[rows(Z_HQ), vec(Q_LORA), vec(KV_LORA), rows(LANE), rows(LANE)],
        out_specs=[rows(Q_LORA), rows(KV_LORA), rows(LANE)],
        out_shape=[jax.ShapeDtypeStruct((t, Q_LORA), BF16), jax.ShapeDtypeStruct((t, KV_LORA), BF16),
                   jax.ShapeDtypeStruct((t, LANE), F32)],
        compiler_params=_cparams(("parallel",)),
    )(z, qn_w, kvn_w, ck, sk)


def _mla_prep_bwd(z, qn_w, kvn_w, ck, sk, dcq, dckv, dkv):
    t = z.shape[0]
    tm = _row_tile(t)

    def body(z_ref, qw_ref, kw_ref, ck_ref, sk_ref, dcq_ref, dckv_ref, dkv_ref, dz_ref, dqw_ref, dkw_ref):
        zz = z_ref[...]
        dq, dqw = _rms_bwd_math(zz[:, Z_CQ:Z_CKV], qw_ref[...], dcq_ref[...])
        dk, dkw = _rms_bwd_math(zz[:, Z_CKV:Z_KPE], kw_ref[...], dckv_ref[...])
        dkv_v = dkv_ref[...]
        dkr = jnp.zeros((tm, LANE), F32)
        for hd in range(MLA_HEADS):
            dkr = dkr + dkv_v[:, 2 * LANE * hd:2 * LANE * hd + LANE].astype(F32)
        dz_ref[...] = jnp.concatenate(
            [dq, dk, dkr * ck_ref[...], dkr * sk_ref[...], jnp.zeros((tm, Z_HQ - Z_KPESW - LANE), F32)], axis=1
        ).astype(BF16)

        @pl.when(pl.program_id(0) == 0)
        def _():
            dqw_ref[...] = jnp.zeros_like(dqw_ref)
            dkw_ref[...] = jnp.zeros_like(dkw_ref)

        dqw_ref[...] += dqw
        dkw_ref[...] += dkw

    def rows(wd):
        return pl.BlockSpec((tm, wd), lambda i: (i, 0))

    def vec(wd):
        return pl.BlockSpec((1, wd), lambda i: (0, 0))

    return pl.pallas_call(
        body, name="mla_prep_bwd", grid=(t // tm,),
        in_specs=[rows(Z_HQ), vec(Q_LORA), vec(KV_LORA), rows(LANE), rows(LANE), rows(Q_LORA), rows(KV_LORA),
                  rows(2 * LANE * MLA_HEADS)],
        out_specs=[rows(Z_HQ), vec(Q_LORA), vec(KV_LORA)],
        out_shape=[jax.ShapeDtypeStruct((t, Z_HQ), BF16), jax.ShapeDtypeStruct((1, Q_LORA), F32),
                   jax.ShapeDtypeStruct((1, KV_LORA), F32)],
        compiler_params=_cparams(("arbitrary",)),
    )(z, qn_w, kvn_w, ck, sk, dcq, dckv, dkv)


def _rope_bwd_q(dq, cq, sq):
    t, wq = dq.shape
    tm = _row_tile(t)

    def body(dq_ref, c_ref, s_ref, o_ref):
        d = dq_ref[...] * ATTN_SCALE
        c8 = jnp.concatenate([c_ref[...]] * MLA_HEADS, axis=1)
        s8 = jnp.concatenate([s_ref[...]] * MLA_HEADS, axis=1)
        o_ref[...] = jnp.concatenate([d * c8, d * s8], axis=1).astype(BF16)

    return pl.pallas_call(
        body, name="rope_bwd_q", grid=(t // tm,),
        in_specs=[pl.BlockSpec((tm, wq), lambda i: (i, 0)), pl.BlockSpec((tm, LANE), lambda i: (i, 0)),
                  pl.BlockSpec((tm, LANE), lambda i: (i, 0))],
        out_specs=pl.BlockSpec((tm, 2 * wq), lambda i: (i, 0)),
        out_shape=jax.ShapeDtypeStruct((t, 2 * wq), BF16),
        compiler_params=_cparams(("parallel",)),
    )(dq, cq, sq)


def _row_vector(col):
    return jnp.broadcast_to(col, (col.shape[0], LANE)).T[0:8, :]


def _wide_block(t):
    for b in (768, 512, 256):
        if t % b == 0:
            return b
    raise ValueError(t)


def _call_carrying_push(body, push, *, name, grid, in_specs, out_specs, out_shape, scratch_shapes, args):
    if push is None:
        outs = pl.pallas_call(body, name=name, grid=grid, in_specs=in_specs, out_specs=out_specs, out_shape=out_shape,
                              scratch_shapes=scratch_shapes, compiler_params=_cparams(("parallel", "arbitrary")))(*args)
        return outs, None
    n_in, n_out, n_scr, n_pin, nk = len(in_specs), len(out_specs), len(scratch_shapes), len(push["ins"]), push["nk"]

    def carrying(*refs):
        o0 = n_in + n_pin
        s0 = o0 + n_out + nk
        pins, pouts, sems = refs[n_in:o0], refs[o0 + n_out:s0], refs[s0 + n_scr:]
        a, b = pl.program_id(0), pl.program_id(1)

        @pl.when((a == 0) & (b == 0))
        def _():
            for cp in _push_copies(push, pins, pouts, sems):
                cp.start()

        body(*refs[:n_in], *refs[o0:o0 + n_out], *refs[s0:s0 + n_scr])

        @pl.when((a == grid[0] - 1) & (b == grid[1] - 1))
        def _():
            for cp in _push_copies(push, pins, pouts, sems):
                cp.wait()

    outs = pl.pallas_call(
        carrying, name=name + "_push", grid=grid, in_specs=list(in_specs) + push["in_specs"],
        out_specs=list(out_specs) + push["out_specs"], out_shape=list(out_shape) + push["outs"],
        scratch_shapes=list(scratch_shapes) + push["sems"], input_output_aliases=_push_aliases(push, n_in, n_out),
        compiler_params=_cparams(("arbitrary", "arbitrary")))(*args, *push["ins"])
    return outs[:n_out], outs[n_out:]


def _attn_fwd(q, kv, push=None):
    t = q.shape[0]
    bq, bk = ATTN_BLOCK, _wide_block(t)
    nq = t // bq

    def body(q_ref, k_ref, v_ref, o_ref, lse_ref):
        i = pl.program_id(1)
        qv = q_ref[...]
        qpos = i * bq + lax.broadcasted_iota(jnp.int32, (bq, 1), 0)
        n = (bq * (i + 1) + bk - 1) // bk

        def block(j, carry, masked):
            m, l, acc = carry
            rows = pl.ds(pl.multiple_of(j * bk, bk), bk)
            s = _dot_nt(qv, k_ref[rows, :])
            if masked:
                kpos = j * bk + lax.broadcasted_iota(jnp.int32, (1, bk), 1)
                s = jnp.where(kpos <= qpos, s, NEG_BIG)
            m_new = jnp.maximum(m, jnp.max(s, axis=1, keepdims=True))
            p = jnp.exp2((s - m_new) * ATTN_C2)
            a = jnp.exp2((m - m_new) * ATTN_C2)
            return m_new, a * l + jnp.sum(p, axis=1, keepdims=True), a * acc + _dot(_bf(p), v_ref[rows, :])

        init = (jnp.full((bq, 1), NEG_BIG, F32), jnp.zeros((bq, 1), F32), jnp.zeros((bq, LANE), F32))
        carry = lax.fori_loop(0, n - 1, functools.partial(block, masked=False), init)
        m, l, acc = block(n - 1, carry, True)
        o_ref[...] = (acc / l).astype(o_ref.dtype)
        lse_ref[0] = _row_vector(m * ATTN_C2 + jnp.log(l) * LOG2E)

    return _call_carrying_push(
        body, push, name="attn_fwd", grid=(MLA_HEADS, nq),
        in_specs=[pl.BlockSpec((bq, LANE), lambda h, i: (i, h)),
                  pl.BlockSpec((t, LANE), lambda h, i: (0, 2 * h)),
                  pl.BlockSpec((t, LANE), lambda h, i: (0, 2 * h + 1))],
        out_specs=[pl.BlockSpec((bq, LANE), lambda h, i: (i, h)),
                   pl.BlockSpec((1, 8, bq), lambda h, i: (h, 0, i))],
        out_shape=[jax.ShapeDtypeStruct((t, MLA_HEADS * LANE), BF16), jax.ShapeDtypeStruct((MLA_HEADS, 8, t), F32)],
        scratch_shapes=[], args=(q, kv, kv))


def _attn_bwd(q, kv, o, do, lse, push=None):
    t = q.shape[0]
    bk, bw = ATTN_BLOCK, _wide_block(t)
    nk, nw = t // bk, t // bw
    lse = lse[:, 0, :].reshape(MLA_HEADS, nw, 1, bw)

    def body(q_ref, o_ref, do_ref, k_ref, v_ref, lse_ref, dq_ref, dkv_ref, dl_ref):
        j = pl.program_id(1)

        @pl.when(j == 0)
        def _():
            dq_ref[...] = jnp.zeros_like(dq_ref)
            for i in range(nw):
                rows = slice(i * bw, (i + 1) * bw)
                d = jnp.sum(o_ref[rows, :].astype(F32) * do_ref[rows, :].astype(F32), axis=1, keepdims=True)
                dl_ref[i] = jnp.broadcast_to(d, (bw, LANE)).T[0:1, :]

        kb = k_ref[...]
        vb = v_ref[...]
        kpos = j * bk + lax.broadcasted_iota(jnp.int32, (bk, 1), 0)

        def block(i, carry, masked):
            dk, dv = carry
            rows = pl.ds(pl.multiple_of(i * bw, bw), bw)
            qb = q_ref[rows, :]
            dob = do_ref[rows, :]
            pt = jnp.exp2(_dot_nt(kb, qb) * ATTN_C2 - lse_ref[0, i])
            if masked:
                qpos = i * bw + lax.broadcasted_iota(jnp.int32, (1, bw), 1)
                pt = jnp.where(kpos <= qpos, pt, 0.0)
            dv = dv + _dot(_bf(pt), dob)
            dst = _bf(pt * (_dot_nt(vb, dob) - dl_ref[i]))
            dk = dk + _dot(dst, qb)
            dq_ref[rows, :] += _dot_tn(dst, kb)
            return dk, dv

        i0 = (bk * j) // bw
        carry = block(i0, (jnp.zeros((bk, LANE), F32), jnp.zeros((bk, LANE), F32)), True)
        dk, dv = lax.fori_loop(i0 + 1, nw, functools.partial(block, masked=False), carry)
        dkv_ref[...] = jnp.concatenate([dk * ATTN_SCALE, dv], axis=1).astype(dkv_ref.dtype)

    head_rows = pl.BlockSpec((t, LANE), lambda h, j: (0, h))
    return _call_carrying_push(
        body, push, name="attn_bwd", grid=(MLA_HEADS, nk),
        in_specs=[head_rows, head_rows, head_rows,
                  pl.BlockSpec((bk, LANE), lambda h, j: (j, 2 * h)), pl.BlockSpec((bk, LANE), lambda h, j: (j, 2 * h + 1)),
                  pl.BlockSpec((1, nw, 1, bw), lambda h, j: (h, 0, 0, 0))],
        out_specs=[head_rows, pl.BlockSpec((bk, 2 * LANE), lambda h, j: (j, h))],
        out_shape=[jax.ShapeDtypeStruct((t, MLA_HEADS * LANE), F32), jax.ShapeDtypeStruct((t, 2 * MLA_HEADS * LANE), BF16)],
        scratch_shapes=[pltpu.VMEM((nw, 1, bw), F32)], args=(q, o, do, kv, kv, lse))


def _hg_tables():
    c = HG_CHUNK
    tri = (jnp.arange(c)[:, None] >= jnp.arange(c)[None, :]).astype(F32)
    mats = [tri]
    for m in HG_LEVELS:
        ref = (jnp.arange(c) // (2 * m)) * (2 * m) + m - 1
        mats.append((ref[:, None] >= jnp.arange(c)[None, :]).astype(F32))
    call = jnp.concatenate(mats, axis=0)
    return call.astype(BF16), call.T.astype(BF16)


def _table_dot(table, x):
    hi = _bf(x)
    rest = x - hi.astype(F32)
    mid = _bf(rest)
    lo = _bf(rest - mid.astype(F32))
    out = _dot(table, jnp.concatenate([hi, mid, lo], axis=1))
    n = x.shape[1]
    return out[:, 0:n] + out[:, n:2 * n] + out[:, 2 * n:3 * n]


def _hg_gates(hq, hf, lb):
    sg = _sigmoid(hf)
    sn = _sigmoid(-hf)
    f = lb + (1.0 - lb) * sg
    q = hq * _sigmoid(hq)
    g = jnp.log(jnp.maximum(f, F_MIN))
    k = (1.0 - lb) * sn
    return q, k, g, f, sg, sn


def _hg_level_masks(m):
    c = HG_CHUNK
    row = lax.broadcasted_iota(jnp.int32, (c, 1), 0)
    col = lax.broadcasted_iota(jnp.int32, (1, c), 1)
    shift = (2 * m).bit_length() - 1
    up = (row & m) != 0
    same = lax.shift_right_logical(row, shift) == lax.shift_right_logical(col, shift)
    return up, same


def _hg_level_factors(b, bref, up):
    lo = jnp.logical_not(up)
    eq = jnp.where(up, jnp.exp(jnp.where(up, b - bref, 0.0)), 0.0)
    ek = jnp.where(lo, jnp.exp(jnp.where(lo, bref - b, 0.0)), 0.0)
    return eq, ek


def _hg_intra(q, k, ball):
    c = HG_CHUNK
    b = ball[0:c]
    row = lax.broadcasted_iota(jnp.int32, (c, 1), 0)
    col = lax.broadcasted_iota(jnp.int32, (1, c), 1)
    a = jnp.where(row == col, _dot_nt(_bf(q), _bf(k)), 0.0)
    parts = []
    for lv, m in enumerate(HG_LEVELS):
        up, same = _hg_level_masks(m)
        eq, ek = _hg_level_factors(b, ball[(lv + 1) * c:(lv + 2) * c], up)
        qt, kt = q * eq, k * ek
        a = a + jnp.where(same, _dot_nt(_bf(qt), _bf(kt)), 0.0)
        parts.append((eq, ek, qt, kt))
    return a, parts


def _hg_chunk_fwd(hq, hf, hi, hg, lb, nw, st, call):
    q, k, g, _, _, _ = _hg_gates(hq, hf, lb)
    ball = _table_dot(call, g)
    b = ball[0:HG_CHUNK]
    a, _ = _hg_intra(q, k, ball)
    v16 = _bf(hi)
    o = _dot(_bf(a * HG_SCALE), v16) + _dot_nt(_bf(q * jnp.exp(b) * HG_SCALE), _bf(st))
    bl = b[HG_CHUNK - 1:HG_CHUNK]
    ke = k * jnp.exp(bl - b)
    st_new = st * jnp.exp(bl) + _dot(_bf(hi.T), _bf(ke))
    r = lax.rsqrt(jnp.mean(o * o, axis=-1, keepdims=True) + EPS)
    y = o * r * nw * (hg * _sigmoid(hg))
    return y, st_new


def _hg_chunk_bwd(hq, hf, hi, hg, lb, nw, st, call, call_t, dy, dst_new):
    c = HG_CHUNK
    q, k, g, f, sg, sn = _hg_gates(hq, hf, lb)
    ball = _table_dot(call, g)
    b = ball[0:c]
    a, parts = _hg_intra(q, k, ball)
    v16 = _bf(hi)
    st16 = _bf(st)
    eb = jnp.exp(b)
    qe = q * eb * HG_SCALE
    a16 = _bf(a * HG_SCALE)
    o = _dot(a16, v16) + _dot_nt(_bf(qe), st16)
    bl = b[c - 1:c]
    el = jnp.exp(bl)
    x = jnp.exp(bl - b)
    ke = k * x
    r = lax.rsqrt(jnp.mean(o * o, axis=-1, keepdims=True) + EPS)
    shg = _sigmoid(hg)
    gate = hg * shg
    ohat = o * r
    don = dy * gate
    dhg = dy * ohat * nw * (shg * (1.0 + hg * (1.0 - shg)))
    dnw = jnp.sum(don * ohat, axis=0, keepdims=True)
    tt = don * nw
    do = r * (tt - ohat * jnp.mean(tt * ohat, axis=-1, keepdims=True))
    do16 = _bf(do)
    dst16 = _bf(dst_new)
    da = _dot_nt(do16, v16) * HG_SCALE
    dv = _dot(_bf(a16.astype(F32).T), do16) + _dot_nt(_bf(ke), dst16)
    dqe = _dot(do16, st16)
    dke = _dot(v16, dst16)
    dst = dst_new * el + _dot(_bf(do.T), _bf(qe))
    dbl = jnp.sum(dst_new * st, axis=0, keepdims=True) * el
    dk = dke * x
    dxa = dke * ke
    db = dqe * qe - dxa
    dbl = dbl + jnp.sum(dxa, axis=0, keepdims=True)
    dq = dqe * eb * HG_SCALE
    row = lax.broadcasted_iota(jnp.int32, (c, 1), 0)
    col = lax.broadcasted_iota(jnp.int32, (1, c), 1)
    ddiag = jnp.sum(jnp.where(row == col, da, 0.0), axis=1, keepdims=True)
    dq = dq + ddiag * k
    dk = dk + ddiag * q
    dball = []
    for (eq, ek, qt, kt), m in zip(parts, HG_LEVELS):
        _, same = _hg_level_masks(m)
        gm = jnp.where(same, da, 0.0)
        dqt = _dot(_bf(gm), _bf(kt))
        dkt = _dot(_bf(gm.T), _bf(qt))
        dq = dq + dqt * eq
        dk = dk + dkt * ek
        darg = dqt * qt - dkt * kt
        db = db + darg
        dball.append(-darg)
    db = db + jnp.where(row == c - 1, dbl, 0.0)
    dg = _table_dot(call_t, jnp.concatenate([db] + dball, axis=0))
    shq = _sigmoid(hq)
    dhq = dq * (shq * (1.0 + hq * (1.0 - shq)))
    df = jnp.where(f > F_MIN, dg / jnp.maximum(f, F_MIN), 0.0)
    dlb = jnp.sum(df * (1.0 - sg) - dk * sn, axis=0, keepdims=True)
    dhf = df * (1.0 - lb) * sg * (1.0 - sg) - dk * (1.0 - lb) * sn * (1.0 - sn)
    return dhq, dhf, dv, dhg, dlb, dnw, dst


def _hg_col(group, h):
    return group // LANE + h


def _hgrn_fwd(z, lb, nw, call):
    t = z.shape[0]
    c, cs = HG_CHUNK, HG_CHUNKS_PER_STEP
    rows = c * cs
    nsteps = t // rows

    def body(hq_ref, hf_ref, hi_ref, hg_ref, lb_ref, nw_ref, call_ref, y_ref, sv_ref, st_ref):
        @pl.when(pl.program_id(1) == 0)
        def _():
            st_ref[...] = jnp.zeros_like(st_ref)

        for u in range(cs):
            sl = slice(u * c, (u + 1) * c)
            st = st_ref[...]
            sv_ref[0, u] = st
            y, st_new = _hg_chunk_fwd(hq_ref[sl, :], hf_ref[sl, :], hi_ref[sl, :], hg_ref[sl, :], lb_ref[...], nw_ref[...],
                                      st, call_ref[...])
            y_ref[sl, :] = y.astype(y_ref.dtype)
            st_ref[...] = st_new

    def zcol(group):
        return pl.BlockSpec((rows, LANE), functools.partial(lambda h, i, g: (i, _hg_col(g, h)), g=group))

    ncall = call.shape[0]
    return pl.pallas_call(
        body, name="hgrn_fwd", grid=(HG_HEADS, nsteps),
        in_specs=[zcol(Z_HQ), zcol(Z_HF), zcol(Z_HI), zcol(Z_HG),
                  pl.BlockSpec((1, LANE), lambda h, i: (0, h)), pl.BlockSpec((1, LANE), lambda h, i: (0, 0)),
                  pl.BlockSpec((ncall, c), lambda h, i: (0, 0))],
        out_specs=[pl.BlockSpec((rows, LANE), lambda h, i: (i, h)),
                   pl.BlockSpec((1, cs, c, c), lambda h, i: (h, i, 0, 0))],
        out_shape=[jax.ShapeDtypeStruct((t, HG_HEADS * LANE), BF16), jax.ShapeDtypeStruct((HG_HEADS, t // c, c, c), F32)],
        scratch_shapes=[pltpu.VMEM((c, c), F32)],
        compiler_params=_cparams(("parallel", "arbitrary")),
    )(z, z, z, z, lb, nw, call)


def _hgrn_bwd(z, lb, nw, call, call_t, saved, dy):
    t = z.shape[0]
    c, cs = HG_CHUNK, HG_CHUNKS_PER_STEP
    rows = c * cs
    nsteps = t // rows

    def body(hq_ref, hf_ref, hi_ref, hg_ref, lb_ref, nw_ref, call_ref, callt_ref, sv_ref, dy_ref,
             dhq_ref, dhf_ref, dhi_ref, dhg_ref, dlb_ref, dnw_ref, dst_ref):
        h, i = pl.program_id(0), pl.program_id(1)

        @pl.when(i == 0)
        def _():
            dst_ref[...] = jnp.zeros_like(dst_ref)
            dlb_ref[...] = jnp.zeros_like(dlb_ref)

        @pl.when((i == 0) & (h == 0))
        def _():
            dnw_ref[...] = jnp.zeros_like(dnw_ref)

        for u in reversed(range(cs)):
            sl = slice(u * c, (u + 1) * c)
            dhq, dhf, dhi, dhg, dlb, dnw, dst = _hg_chunk_bwd(
                hq_ref[sl, :], hf_ref[sl, :], hi_ref[sl, :], hg_ref[sl, :], lb_ref[...], nw_ref[...], sv_ref[0, u],
                call_ref[...], callt_ref[...], dy_ref[sl, :].astype(F32), dst_ref[...])
            dhq_ref[sl, :] = dhq.astype(BF16)
            dhf_ref[sl, :] = dhf.astype(BF16)
            dhi_ref[sl, :] = dhi.astype(BF16)
            dhg_ref[sl, :] = dhg.astype(BF16)
            dlb_ref[...] += dlb
            dnw_ref[...] += dnw
            dst_ref[...] = dst

    def zcol(group):
        return pl.BlockSpec((rows, LANE), functools.partial(lambda h, i, g: (nsteps - 1 - i, _hg_col(g, h)), g=group))

    head_rows = pl.BlockSpec((rows, LANE), lambda h, i: (nsteps - 1 - i, h))
    ncall = call.shape[0]
    piece = jax.ShapeDtypeStruct((t, HG_HEADS * LANE), BF16)
    return pl.pallas_call(
        body, name="hgrn_bwd", grid=(HG_HEADS, nsteps),
        in_specs=[zcol(Z_HQ), zcol(Z_HF), zcol(Z_HI), zcol(Z_HG),
                  pl.BlockSpec((1, LANE), lambda h, i: (0, h)), pl.BlockSpec((1, LANE), lambda h, i: (0, 0)),
                  pl.BlockSpec((ncall, c), lambda h, i: (0, 0)), pl.BlockSpec((c, ncall), lambda h, i: (0, 0)),
                  pl.BlockSpec((1, cs, c, c), lambda h, i: (h, nsteps - 1 - i, 0, 0)), head_rows],
        out_specs=[head_rows, head_rows, head_rows, head_rows,
                   pl.BlockSpec((1, LANE), lambda h, i: (0, h)), pl.BlockSpec((1, LANE), lambda h, i: (0, 0))],
        out_shape=[piece, piece, piece, piece,
                   jax.ShapeDtypeStruct((1, HG_HEADS * LANE), F32), jax.ShapeDtypeStruct((1, LANE), F32)],
        scratch_shapes=[pltpu.VMEM((c, c), F32)],
        compiler_params=_cparams(("arbitrary", "arbitrary")),
    )(z, z, z, z, lb, nw, call, call_t, saved, dy)


def _lb_fwd(raw):
    nl = raw.shape[0]

    def body(r_ref, o_ref):
        x = r_ref[...]
        e = jnp.exp(x - jnp.max(x, axis=0, keepdims=True))
        p = e / jnp.sum(e, axis=0, keepdims=True)
        acc = jnp.zeros_like(p[0:1])
        for l in range(nl):
            if l > 0:
                acc = acc + p[l:l + 1]
            o_ref[l:l + 1, :] = acc

    return pl.pallas_call(body, name="lb_fwd", out_shape=jax.ShapeDtypeStruct(raw.shape, F32))(raw)


def _lb_bwd(raw, dlbs):
    nl = raw.shape[0]

    def body(r_ref, d_ref, o_ref):
        x = r_ref[...]
        e = jnp.exp(x - jnp.max(x, axis=0, keepdims=True))
        p = e / jnp.sum(e, axis=0, keepdims=True)
        d = d_ref[...]
        dps = [jnp.zeros_like(d[0:1])]
        for i in range(1, nl):
            acc = d[i:i + 1]
            for l in range(i + 1, nl):
                acc = acc + d[l:l + 1]
            dps.append(acc)
        dot = dps[0] * p[0:1]
        for i in range(1, nl):
            dot = dot + dps[i] * p[i:i + 1]
        for i in range(nl):
            o_ref[i:i + 1, :] = p[i:i + 1] * (dps[i] - dot)

    return pl.pallas_call(body, name="lb_bwd", out_shape=jax.ShapeDtypeStruct(raw.shape, F32))(raw, dlbs)


def _push_plan(srcs, gather, bufs=None, layer=None):
    nk = len(srcs)
    any_spec = pl.BlockSpec(memory_space=pl.ANY)
    if bufs is None:
        ins = list(srcs)
        outs = [jax.ShapeDtypeStruct(((N_DEV,) + s.shape) if gather else s.shape, s.dtype) for s in srcs]
    else:
        ins = list(srcs) + list(bufs)
        outs = [jax.ShapeDtypeStruct(b.shape, b.dtype) for b in bufs]
    sems = [pltpu.SemaphoreType.DMA((nk * N_DEV,)), pltpu.SemaphoreType.DMA((nk * N_DEV,)), pltpu.SemaphoreType.DMA((nk,))]
    return dict(nk=nk, gather=gather, layer=layer, ins=ins, in_specs=[any_spec] * len(ins), outs=outs,
                out_specs=[any_spec] * nk, sems=sems, alias_from=None if bufs is None else nk)


def _push_aliases(plan, first_in, first_out):
    if plan is None or plan["alias_from"] is None:
        return {}
    return {first_in + plan["alias_from"] + k: first_out + k for k in range(plan["nk"])}


def _push_copies(plan, in_refs, out_refs, sems):
    nk, gather, layer = plan["nk"], plan["gather"], plan["layer"]
    send_sems, recv_sems, local_sems = sems
    me = 4 * lax.axis_index("x") + 2 * lax.axis_index("y") + lax.axis_index("c")

    def landing(k):
        return out_refs[k].at[me] if layer is None else out_refs[k].at[me, layer]

    copies = [pltpu.make_async_copy(in_refs[k] if gather else in_refs[k].at[me], landing(k), local_sems.at[k])
              for k in range(nk)]
    for r in range(1, N_DEV):
        to = (me + r) % N_DEV
        for k in range(nk):
            copies.append(pltpu.make_async_remote_copy(
                src_ref=in_refs[k] if gather else in_refs[k].at[to], dst_ref=landing(k),
                send_sem=send_sems.at[k * N_DEV + r], recv_sem=recv_sems.at[k * N_DEV + r],
                device_id=(to // 4, (to // 2) % 2, to % 2), device_id_type=pl.DeviceIdType.MESH))
    return copies


def _exchange(name, srcs, gather, bufs=None, layer=None):
    plan = _push_plan(srcs, gather, bufs, layer)
    nin, nk = len(plan["ins"]), plan["nk"]

    def body(*refs):
        copies = _push_copies(plan, refs[:nin], refs[nin:nin + nk], refs[nin + nk:])
        for cp in copies:
            cp.start()
        for cp in copies:
            cp.wait()

    return pl.pallas_call(
        body, name=name, in_specs=plan["in_specs"], out_specs=plan["out_specs"], out_shape=plan["outs"],
        scratch_shapes=plan["sems"], input_output_aliases=_push_aliases(plan, 0, 0),
    )(*plan["ins"])


def _adam_math(g, w, m, v):
    m2 = ADAM_B1 * m + (1.0 - ADAM_B1) * g
    v2 = ADAM_B2 * v + (1.0 - ADAM_B2) * (g * g)
    m_hat = m2 / (1.0 - ADAM_B1 ** ADAM_STEP)
    v_hat = v2 / (1.0 - ADAM_B2 ** ADAM_STEP)
    return -ADAM_LR * (m_hat / (jnp.sqrt(v_hat) + ADAM_EPS) + ADAM_WD * w), m2, v2


def _sum_slots(ref):
    g = ref[0].astype(F32)
    for s in range(1, N_DEV):
        g = g + ref[s].astype(F32)
    return g


def _adam_sharded(name, slots, w, m, v):
    nl, a, b = w.shape
    ta = a
    for cand in range(8, 257, 8):
        if a % cand == 0:
            ta = cand

    def body(s_ref, w_ref, m_ref, v_ref, g_ref, d_ref, m2_ref, v2_ref):
        g = _sum_slots(s_ref)
        d, m2, v2 = _adam_math(g, w_ref[...], m_ref[...], v_ref[...])
        g_ref[...] = g
        d_ref[...] = d
        m2_ref[...] = m2
        v2_ref[...] = v2

    blk = pl.BlockSpec((1, ta, b), lambda l, i: (l, i, 0))
    sds = jax.ShapeDtypeStruct(w.shape, F32)
    return pl.pallas_call(
        body, name=name, grid=(nl, a // ta),
        in_specs=[pl.BlockSpec((N_DEV, 1, ta, b), lambda l, i: (0, l, i, 0)), blk, blk, blk],
        out_specs=[blk] * 4, out_shape=[sds] * 4,
        compiler_params=_cparams(("parallel", "parallel")),
    )(slots, w, m, v)


def _sum_replicated(slots):
    def body(s_ref, g_ref):
        g_ref[...] = _sum_slots(s_ref)

    return pl.pallas_call(body, name="sum_small", out_shape=jax.ShapeDtypeStruct(slots.shape[1:], F32))(slots)


def _adam_small(name, g, w, m, v):
    def body(g_ref, w_ref, m_ref, v_ref, d_ref, m2_ref, v2_ref):
        d, m2, v2 = _adam_math(g_ref[...], w_ref[...], m_ref[...], v_ref[...])
        d_ref[...] = d
        m2_ref[...] = m2
        v2_ref[...] = v2

    sds = jax.ShapeDtypeStruct(w.shape, F32)
    return pl.pallas_call(body, name=name, out_shape=[sds] * 3)(g, w, m, v)


def _cols_full(g):
    return jnp.transpose(g, (1, 0, 2)).reshape(g.shape[1], -1)


def _cols_shards(w):
    k = w.shape[0]
    return jnp.transpose(w.reshape(k, N_DEV, -1), (1, 0, 2))


def _swap_halves(x):
    half = x.shape[-1] // 2
    return jnp.concatenate([x[..., half:], x[..., :half]], axis=-1)


def _zeros_like_cols(x, n):
    return jnp.zeros(x.shape[:-1] + (n,), x.dtype)


def _w_in_internal(w):
    d = w.shape[0]
    kpe = w[:, 640:672]
    z64, z32 = jnp.zeros((d, 64), w.dtype), jnp.zeros((d, 32), w.dtype)
    return jnp.concatenate(
        [w[:, 0:640], z64, kpe, z32, z64, _swap_halves(kpe), z32, jnp.zeros((d, Z_HQ - Z_KPESW - LANE), w.dtype),
         w[:, 672:2720], w[:, 2720:4768]], axis=1)


def _w_in_grad(g):
    kpe = g[:, Z_KPE + 64:Z_KPE + 96] + _swap_halves(g[:, Z_KPESW + 64:Z_KPESW + 96])
    return jnp.concatenate([g[:, 0:640], kpe, g[:, Z_HQ:Z_W]], axis=1)


def _w_uq_internal(w):
    k = w.shape[0]
    w3 = w.reshape(k, MLA_HEADS, QK_NOPE + QK_ROPE)
    nope, rope = w3[..., :QK_NOPE], w3[..., QK_NOPE:]
    plain = jnp.concatenate([nope, rope, _zeros_like_cols(rope, 32)], axis=-1).reshape(k, -1)
    swapped = jnp.concatenate([_zeros_like_cols(nope, 64), _swap_halves(rope), _zeros_like_cols(rope, 32)], axis=-1).reshape(k, -1)
    return jnp.concatenate([plain, swapped], axis=1)


def _w_uq_grad(g):
    k = g.shape[0]
    half = MLA_HEADS * LANE
    g1, g2 = g[:, :half].reshape(k, MLA_HEADS, LANE), g[:, half:].reshape(k, MLA_HEADS, LANE)
    rope = g1[..., 64:96] + _swap_halves(g2[..., 64:96])
    return jnp.concatenate([g1[..., :64], rope], axis=-1).reshape(k, -1)


def _w_ukv_internal(w):
    k = w.shape[0]
    w3 = w.reshape(k, MLA_HEADS, QK_NOPE + V_HEAD)
    kn, vv = w3[..., :QK_NOPE], w3[..., QK_NOPE:]
    z = _zeros_like_cols(kn, 64)
    return jnp.concatenate([kn, z, vv, z], axis=-1).reshape(k, -1)


def _w_ukv_grad(g):
    k = g.shape[0]
    g3 = g.reshape(k, MLA_HEADS, 2 * LANE)
    return jnp.concatenate([g3[..., 0:64], g3[..., LANE:LANE + 64]], axis=-1).reshape(k, -1)


def _w_pa_internal(w):
    n = w.shape[1]
    w3 = w.reshape(MLA_HEADS, V_HEAD, n)
    return jnp.concatenate([w3, jnp.zeros_like(w3)], axis=1).reshape(-1, n)


def _w_pa_grad(g):
    n = g.shape[1]
    return g.reshape(MLA_HEADS, 2 * V_HEAD, n)[:, :V_HEAD].reshape(-1, n)


def _rope_tables(t):
    half = QK_ROPE // 2
    inv = ROPE_THETA ** (-jnp.arange(half, dtype=F32) / half)
    ang = jnp.arange(t, dtype=F32)[:, None] * inv[None, :]
    cos, sin = jnp.cos(ang), jnp.sin(ang)
    one, zero = jnp.ones((t, 64), F32), jnp.zeros((t, 64), F32)
    z32 = jnp.zeros((t, 32), F32)
    cq = jnp.concatenate([one, cos, cos, z32], axis=1)
    ck = jnp.concatenate([zero, cos, cos, z32], axis=1)
    sq = jnp.concatenate([zero, -sin, sin, z32], axis=1)
    return cq, ck, sq


def _ffn_fwd(tag, h, nw, w_gu, w_down):
    t, d = h.shape
    dff = w_down.shape[0]
    tm, tn = _row_tile(t), _tile(dff, 1408)
    xn = _rms_fwd(tag + "_norm", h, nw)

    def act_fn(accs, _):
        g, u = accs
        return g, u, g * _sigmoid(g) * u

    spec = _tile_spec(tm, tn)
    g, u, act = _mm(tag + "_gu", [xn], [(0, w_gu, 0), (0, w_gu, dff // tn)], [],
                    [((t, dff), BF16) + spec] * 3, act_fn, tm=tm, tn=tn, n=dff)
    tn2 = _tile(d, 1024)
    h2, = _mm(tag + "_down", [act], [(0, w_down, 0)], [(h,) + _tile_spec(tm, tn2)],
              [((t, d), F32) + _tile_spec(tm, tn2)], lambda accs, ex: (ex[0] + 0.5 * accs[0],), tm=tm, tn=tn2, n=d)
    return h2, (h, xn, g, u, act)


def _ffn_bwd(tag, dh2, saved, nw, w_gu_t, w_down_t):
    h, xn, g, u, act = saved
    t, d = h.shape
    dff = act.shape[1]
    tm, tn = _row_tile(t), _tile(dff, 1408)

    def dact_fn(accs, ex):
        gg, uu = ex[0].astype(F32), ex[1].astype(F32)
        da = 0.5 * accs[0]
        sg = _sigmoid(gg)
        return da * uu * (sg * (1.0 + gg * (1.0 - sg))), da * (gg * sg)

    spec = _tile_spec(tm, tn)
    dg, du = _mm(tag + "_dact", [dh2], [(0, w_down_t, 0)], [(g,) + spec, (u,) + spec],
                 [((t, dff), BF16) + spec] * 2, dact_fn, tm=tm, tn=tn, n=dff)
    dw_down = _mm_tn(tag + "_dwdown", act, dh2, alpha=0.5, out_dtype=BF16)
    tn2 = _tile(d, 512)
    dxn, = _mm(tag + "_dxn", [dg, du], [(0, w_gu_t[:dff], 0), (1, w_gu_t[dff:], 0)], [],
               [((t, d), F32) + _tile_spec(tm, tn2)], lambda accs, _: (accs[0] + accs[1],), tm=tm, tn=tn2, n=d)
    dw_gu = jnp.concatenate([_mm_tn(tag + "_dwg", xn, dg, alpha=1.0, out_dtype=BF16),
                             _mm_tn(tag + "_dwu", xn, du, alpha=1.0, out_dtype=BF16)], axis=1)
    dh, dnw = _rms_bwd(tag + "_dnorm", h, nw, dxn, dh2)
    return dh, dnw, dw_gu, dw_down


def _kv_pattern(kr):
    z = jnp.zeros_like(kr)
    return jnp.concatenate([kr, z] * MLA_HEADS, axis=1)


def _mix_fwd(h, p, tabs, lb, call, push):
    t, d = h.shape
    tm = _row_tile(t)
    cq_t, ck_t, sq_t = tabs
    u = _rms_fwd("mix_norm", h, p["mix_norm"])
    tnz = _tile(Z_W, 1024)
    z, = _mm("mix_in", [u], [(0, p["w_in"], 0)], [], [((t, Z_W), F32) + _tile_spec(tm, tnz)], lambda a, _: (a[0],),
             tm=tm, tn=tnz, n=Z_W)
    cqn, ckvn, krot = _mla_prep(z, p["q_norm"], p["kv_norm"], ck_t, sq_t)
    wq = MLA_HEADS * LANE
    lane_rows = lambda i, j: (i, 0)

    def q_fn(accs, ex):
        c8 = jnp.concatenate([ex[0]] * MLA_HEADS, axis=1)
        s8 = jnp.concatenate([ex[1]] * MLA_HEADS, axis=1)
        return (accs[0] * c8 + accs[1] * s8,)

    q, = _mm("mla_q", [cqn], [(0, p["w_uq"], 0), (0, p["w_uq"], 1)],
             [(cq_t, (tm, LANE), lane_rows), (sq_t, (tm, LANE), lane_rows)],
             [((t, wq), BF16) + _tile_spec(tm, wq)], q_fn, tm=tm, tn=wq, n=wq)
    kv, = _mm("mla_kv", [ckvn], [(0, p["w_ukv"], 0)], [(krot, (tm, LANE), lane_rows)],
              [((t, 2 * wq), BF16) + _tile_spec(tm, 2 * wq)], lambda a, ex: (a[0] + _kv_pattern(ex[0]),),
              tm=tm, tn=2 * wq, n=2 * wq)
    (o_a, lse), pushed = _attn_fwd(q, kv, push)
    o_b, st_saved = _hgrn_fwd(z, lb, p["hg_norm"], call)
    tn = _tile(d, 512)

    def merge_fn(accs, ex):
        ya, yb = accs
        return ya, yb, _sigmoid(ex[0]) * ya + _sigmoid(ex[1]) * yb

    spec = _tile_spec(tm, tn)
    ya, yb, merged = _mm("mix_merge", [o_a, o_b], [(0, p["w_pa"], 0), (1, p["w_pr"], 0)],
                         [(z,) + _tile_spec(tm, tn, Z_GA // tn), (z,) + _tile_spec(tm, tn, Z_GB // tn)],
                         [((t, d), BF16) + spec] * 3, merge_fn, tm=tm, tn=tn, n=d)
    tn2 = _tile(d, 1024)
    h2, = _mm("mix_out", [merged], [(0, p["w_out"], 0)], [(h,) + _tile_spec(tm, tn2)],
              [((t, d), F32) + _tile_spec(tm, tn2)], lambda a, ex: (ex[0] + a[0],), tm=tm, tn=tn2, n=d)
    return h2, (h, u, z, cqn, ckvn, q, kv, o_a, lse, o_b, st_saved, ya, yb, merged), pushed


def _mix_bwd(dh2, saved, p, tabs, lb, call, call_t, push):
    h, u, z, cqn, ckvn, q, kv, o_a, lse, o_b, st_saved, ya, yb, merged = saved
    t, d = h.shape
    tm = _row_tile(t)
    cq_t, ck_t, sq_t = tabs
    tn = _tile(d, 512)
    spec = _tile_spec(tm, tn)

    def dmerge_fn(accs, ex):
        dm = accs[0]
        yav, ybv = ex[0].astype(F32), ex[1].astype(F32)
        sa, sb = _sigmoid(ex[2]), _sigmoid(ex[3])
        return dm * sa, dm * sb, dm * yav * sa * (1.0 - sa), dm * ybv * sb * (1.0 - sb)

    dya, dyb, dga, dgb = _mm("mix_dmerge", [dh2], [(0, p["w_out_t"], 0)],
                             [(ya,) + spec, (yb,) + spec, (z,) + _tile_spec(tm, tn, Z_GA // tn),
                              (z,) + _tile_spec(tm, tn, Z_GB // tn)],
                             [((t, d), BF16) + spec] * 4, dmerge_fn, tm=tm, tn=tn, n=d)
    dw_out = _mm_tn("mix_dwout", merged, dh2, alpha=1.0, out_dtype=BF16)
    wq = MLA_HEADS * LANE
    do_a, = _mm("mix_doa", [dya], [(0, p["w_pa_t"], 0)], [], [((t, wq), BF16) + _tile_spec(tm, wq)], lambda a, _: (a[0],),
                tm=tm, tn=wq, n=wq)
    wr = HG_HEADS * LANE
    do_b, = _mm("mix_dob", [dyb], [(0, p["w_pr_t"], 0)], [], [((t, wr), BF16) + _tile_spec(tm, wr)], lambda a, _: (a[0],),
                tm=tm, tn=wr, n=wr)
    dw_pa = _mm_tn("mix_dwpa", o_a, dya, alpha=1.0, out_dtype=F32)
    dw_pr = _mm_tn("mix_dwpr", o_b, dyb, alpha=1.0, out_dtype=BF16)
    (dq, dkv), pushed = _attn_bwd(q, kv, o_a, do_a, lse, push)
    dqq = _rope_bwd_q(dq, cq_t, sq_t)
    dcq, = _mm("mla_dcq", [dqq], [(0, p["w_uq_t"], 0)], [], [((t, Q_LORA), F32) + _tile_spec(tm, Q_LORA)],
               lambda a, _: (a[0],), tm=tm, tn=Q_LORA, n=Q_LORA)
    dckv, = _mm("mla_dckv", [dkv], [(0, p["w_ukv_t"], 0)], [], [((t, KV_LORA), F32) + _tile_spec(tm, KV_LORA)],
                lambda a, _: (a[0],), tm=tm, tn=KV_LORA, n=KV_LORA)
    dw_uq = _mm_tn("mla_dwuq", cqn, dqq, alpha=1.0, out_dtype=F32)
    dw_ukv = _mm_tn("mla_dwukv", ckvn, dkv, alpha=1.0, out_dtype=F32)
    dz_mla, dqn, dkvn = _mla_prep_bwd(z, p["q_norm"], p["kv_norm"], ck_t, sq_t, dcq, dckv, dkv)
    dhq, dhf, dhi, dhg, dlb, dhgn = _hgrn_bwd(z, lb, p["hg_norm"], call, call_t, st_saved, do_b)
    dz = jnp.concatenate([dz_mla, dhq, dhf, dhi, dhg, dga, dgb], axis=1)
    du, = _mm("mix_du", [dz], [(0, p["w_in_t"], 0)], [], [((t, d), F32) + spec], lambda a, _: (a[0],), tm=tm, tn=tn, n=d)
    dw_in = _mm_tn("mix_dwin", u, dz, alpha=1.0, out_dtype=F32)
    dh, dmn = _rms_bwd("mix_dnorm", h, p["mix_norm"], du, dh2)
    grads = dict(mix_norm=dmn, q_norm=dqn, kv_norm=dkvn, hg_norm=dhgn, lb=dlb, w_in=_w_in_grad(dw_in), w_uq=_w_uq_grad(dw_uq),
                 w_ukv=_w_ukv_grad(dw_ukv), w_proj_attn=_w_pa_grad(dw_pa), w_proj_rec=dw_pr, w_out=dw_out)
    return dh, grads, pushed


SHARDED = ("ffn1_w_gu", "ffn1_w_down", "w_in", "w_uq", "w_ukv", "w_proj_attn", "w_proj_rec", "w_out", "ffn2_w_gu", "ffn2_w_down")
ROW_SHARDED = ("ffn1_w_down", "w_out", "ffn2_w_down")
SMALL = ("ffn1_norm", "mix_norm", "q_norm", "kv_norm", "hg_lb_raw", "hg_norm", "ffn2_norm", "final_norm")
WEIGHTS = ("meta_tokens", "ffn1_norm", "ffn1_w_gu", "ffn1_w_down", "mix_norm", "w_in", "q_norm", "kv_norm", "w_uq", "w_ukv",
           "hg_lb_raw", "hg_norm", "w_proj_attn", "w_proj_rec", "w_out", "ffn2_norm", "ffn2_w_gu", "ffn2_w_down", "final_norm")


def _pack_small(vals):
    flat = jnp.concatenate([vals[n].reshape(-1) for n in SMALL])
    return flat.reshape(-1, LANE)


def _unpack_small(packed, like):
    flat = packed.reshape(-1)
    out, off = {}, 0
    for n in SMALL:
        size = math.prod(like[n].shape)
        out[n] = flat[off:off + size].reshape(like[n].shape)
        off += size
    return out


def kernel(x, meta_tokens, ffn1_norm, ffn1_w_gu, ffn1_w_down, mix_norm, w_in, q_norm, kv_norm, w_uq, w_ukv, hg_lb_raw, hg_norm, w_proj_attn, w_proj_rec, w_out, ffn2_norm, ffn2_w_gu, ffn2_w_down, final_norm, loss_target, m_meta_tokens, m_ffn1_norm, m_ffn1_w_gu, m_ffn1_w_down, m_mix_norm, m_w_in, m_q_norm, m_kv_norm, m_w_uq, m_w_ukv, m_hg_lb_raw, m_hg_norm, m_w_proj_attn, m_w_proj_rec, m_w_out, m_ffn2_norm, m_ffn2_w_gu, m_ffn2_w_down, m_final_norm, v_meta_tokens, v_ffn1_norm, v_ffn1_w_gu, v_ffn1_w_down, v_mix_norm, v_w_in, v_q_norm, v_kv_norm, v_w_uq, v_w_ukv, v_hg_lb_raw, v_hg_norm, v_w_proj_attn, v_w_proj_rec, v_w_out, v_ffn2_norm, v_ffn2_w_gu, v_ffn2_w_down, v_final_norm):
    w = dict(meta_tokens=meta_tokens, ffn1_norm=ffn1_norm, ffn1_w_gu=ffn1_w_gu, ffn1_w_down=ffn1_w_down, mix_norm=mix_norm,
             w_in=w_in, q_norm=q_norm, kv_norm=kv_norm, w_uq=w_uq, w_ukv=w_ukv, hg_lb_raw=hg_lb_raw, hg_norm=hg_norm,
             w_proj_attn=w_proj_attn, w_proj_rec=w_proj_rec, w_out=w_out, ffn2_norm=ffn2_norm, ffn2_w_gu=ffn2_w_gu,
             ffn2_w_down=ffn2_w_down, final_norm=final_norm)
    mom = dict(meta_tokens=m_meta_tokens, ffn1_norm=m_ffn1_norm, ffn1_w_gu=m_ffn1_w_gu, ffn1_w_down=m_ffn1_w_down,
               mix_norm=m_mix_norm, w_in=m_w_in, q_norm=m_q_norm, kv_norm=m_kv_norm, w_uq=m_w_uq, w_ukv=m_w_ukv,
               hg_lb_raw=m_hg_lb_raw, hg_norm=m_hg_norm, w_proj_attn=m_w_proj_attn, w_proj_rec=m_w_proj_rec, w_out=m_w_out,
               ffn2_norm=m_ffn2_norm, ffn2_w_gu=m_ffn2_w_gu, ffn2_w_down=m_ffn2_w_down, final_norm=m_final_norm)
    var = dict(meta_tokens=v_meta_tokens, ffn1_norm=v_ffn1_norm, ffn1_w_gu=v_ffn1_w_gu, ffn1_w_down=v_ffn1_w_down,
               mix_norm=v_mix_norm, w_in=v_w_in, q_norm=v_q_norm, kv_norm=v_kv_norm, w_uq=v_w_uq, w_ukv=v_w_ukv,
               hg_lb_raw=v_hg_lb_raw, hg_norm=v_hg_norm, w_proj_attn=v_w_proj_attn, w_proj_rec=v_w_proj_rec, w_out=v_w_out,
               ffn2_norm=v_ffn2_norm, ffn2_w_gu=v_ffn2_w_gu, ffn2_w_down=v_ffn2_w_down, final_norm=v_final_norm)
    nl = ffn1_norm.shape[0]
    seq, d = x.shape[1], x.shape[2]
    n_real = N_META + seq
    t = -(-n_real // ROW_ALIGN) * ROW_ALIGN
    me = 4 * lax.axis_index("x") + 2 * lax.axis_index("y") + lax.axis_index("c")

    def own_shards(l):
        return [w[n][l].astype(BF16) for n in SHARDED]

    gathered = _exchange("gather_weights", own_shards(0) + [meta_tokens], gather=True)
    meta_full = _cols_full(gathered[-1])

    def layer_params(l, full):
        def mat(n):
            g = full[n]
            return g.reshape(-1, g.shape[-1]) if n in ROW_SHARDED else _cols_full(g)

        p = {}
        for tag in ("ffn1", "ffn2"):
            p[tag + "_w_gu"] = mat(tag + "_w_gu")
            p[tag + "_w_down"] = mat(tag + "_w_down")
            p[tag + "_w_gu_t"] = p[tag + "_w_gu"].T
            p[tag + "_w_down_t"] = p[tag + "_w_down"].T
            p[tag + "_norm"] = w[tag + "_norm"][l:l + 1]
        p["w_in"] = _w_in_internal(mat("w_in"))
        p["w_uq"] = _w_uq_internal(mat("w_uq"))
        p["w_ukv"] = _w_ukv_internal(mat("w_ukv"))
        p["w_pa"] = _w_pa_internal(mat("w_proj_attn"))
        p["w_pr"] = mat("w_proj_rec")
        p["w_out"] = mat("w_out")
        for n in ("w_in", "w_uq", "w_ukv", "w_pa", "w_pr", "w_out"):
            p[n + "_t"] = p[n].T
        for n in ("mix_norm", "q_norm", "kv_norm", "hg_norm"):
            p[n] = w[n][l:l + 1]
        return p

    tabs = _rope_tables(t)
    call, call_t = _hg_tables()
    lbs = _lb_fwd(hg_lb_raw)

    pad = jnp.zeros((t - n_real, d), F32)
    h = jnp.concatenate([meta_full, x[0], pad], axis=0)
    tgt = jnp.concatenate([jnp.zeros((N_META, d), F32), loss_target[0], pad], axis=0)
    saved, params = [], []
    full = dict(zip(SHARDED, gathered[:-1]))
    for l in range(nl):
        p = layer_params(l, full)
        params.append(p)
        h, s1 = _ffn_fwd("ffn1", h, p["ffn1_norm"], p["ffn1_w_gu"], p["ffn1_w_down"])
        push = _push_plan(own_shards(l + 1), gather=True) if l + 1 < nl else None
        h, s2, pushed = _mix_fwd(h, p, tabs, lbs[l:l + 1], call, push)
        if pushed is not None:
            full = dict(zip(SHARDED, pushed))
        h, s3 = _ffn_fwd("ffn2", h, p["ffn2_norm"], p["ffn2_w_gu"], p["ffn2_w_down"])
        saved.append((s1, s2, s3))
    dh, d_final, loss_part = _loss_head(h, final_norm.reshape(1, d), tgt, n_real)
    loss = lax.psum(loss_part[0, 0], ("x", "y", "c"))

    def shards(gm):
        out = []
        for n in SHARDED:
            g = gm[n].astype(BF16)
            out.append(g.reshape(N_DEV, -1, g.shape[-1]) if n in ROW_SHARDED else _cols_shards(g))
        return out

    per_layer = []
    slots = [jnp.zeros((N_DEV,) + w[n].shape, BF16) for n in SHARDED]
    ready = None
    for l in reversed(range(nl)):
        p = params[l]
        s1, s2, s3 = saved[l]
        dh, dn2, dgu2, ddown2 = _ffn_bwd("ffn2", dh, s3, p["ffn2_norm"], p["ffn2_w_gu_t"], p["ffn2_w_down_t"])
        push = None if ready is None else _push_plan(ready, gather=False, bufs=slots, layer=l + 1)
        dh, gm, pushed = _mix_bwd(dh, s2, p, tabs, lbs[l:l + 1], call, call_t, push)
        if pushed is not None:
            slots = list(pushed)
        dh, dn1, dgu1, ddown1 = _ffn_bwd("ffn1", dh, s1, p["ffn1_norm"], p["ffn1_w_gu_t"], p["ffn1_w_down_t"])
        gm.update(ffn1_norm=dn1, ffn2_norm=dn2, ffn1_w_gu=dgu1, ffn2_w_gu=dgu2, ffn1_w_down=ddown1, ffn2_w_down=ddown2)
        per_layer.append(gm)
        ready = shards(gm)
    per_layer.reverse()
    grad_x = dh[N_META:n_real][None]

    slots = _exchange("scatter_grads", ready, gather=False, bufs=slots, layer=0)
    grads, delta, new_m, new_v = {}, {}, {}, {}
    for n, s in zip(SHARDED, slots):
        grads[n], delta[n], new_m[n], new_v[n] = _adam_sharded("adam_" + n, s, w[n], mom[n], var[n])

    small = {n: jnp.concatenate([gm[n] for gm in per_layer], axis=0) for n in SMALL if n not in ("hg_lb_raw", "final_norm")}
    small["hg_lb_raw"] = _lb_bwd(hg_lb_raw, jnp.concatenate([gm["lb"] for gm in per_layer], axis=0))
    small["final_norm"] = d_final
    packed = jnp.concatenate([_pack_small(small), dh[:N_META].reshape(-1, LANE)], axis=0)
    summed = _sum_replicated(_exchange("gather_small", [packed], gather=True)[0])
    n_small = packed.shape[0] - N_META * d // LANE
    sd, sm, sv = _adam_small("adam_small", summed[:n_small], _pack_small(w), _pack_small(mom), _pack_small(var))
    grads.update(_unpack_small(summed[:n_small], w))
    delta.update(_unpack_small(sd, w))
    new_m.update(_unpack_small(sm, w))
    new_v.update(_unpack_small(sv, w))
    dmeta = lax.dynamic_slice_in_dim(summed[n_small:].reshape(N_META, d), me * (d // N_DEV), d // N_DEV, axis=1)
    grads["meta_tokens"] = dmeta
    delta["meta_tokens"], new_m["meta_tokens"], new_v["meta_tokens"] = _adam_small(
        "adam_meta", dmeta, meta_tokens, m_meta_tokens, v_meta_tokens)

    return (loss, grad_x, *[grads[n] for n in WEIGHTS], *[delta[n] for n in WEIGHTS], *[new_m[n] for n in WEIGHTS],
            *[new_v[n] for n in WEIGHTS])
```

```python
import functools
import math

import jax
import jax.numpy as jnp
from jax import lax
from jax.experimental import pallas as pl
from jax.experimental.pallas import tpu as pltpu

F32 = jnp.float32
BF16 = jnp.bfloat16

N_DEV = 8
N_META = 16
MLA_HEADS = 8
Q_LORA = 384
KV_LORA = 256
QK_NOPE = 64
QK_ROPE = 32
V_HEAD = 64
ROPE_THETA = 10000.0
HG_HEADS = 4
HG_DIM = 128
EPS = 1e-6
NEG_BIG = -1e30
F_MIN = 1e-20
ADAM_LR = 0.001
ADAM_B1 = 0.9
ADAM_B2 = 0.999
ADAM_EPS = 1e-08
ADAM_WD = 0.01
ADAM_STEP = 10

LANE = 128
ROW_ALIGN = 256
HG_CHUNK = 128
HG_CHUNKS_PER_STEP = 2
HG_LEVELS = (64, 32, 16, 8, 4, 2, 1)
VMEM_LIMIT = 48 * 1024 * 1024

Z_CQ, Z_CKV, Z_KPE, Z_KPESW, Z_HQ, Z_HF, Z_HI, Z_HG, Z_GA, Z_GB, Z_W = 0, 384, 640, 768, 1024, 1536, 2048, 2560, 3072, 4096, 5120
ATTN_SCALE = float((QK_NOPE + QK_ROPE) ** -0.5)
LOG2E = 1.4426950408889634
ATTN_C2 = ATTN_SCALE * LOG2E
HG_SCALE = float(HG_DIM ** -0.5)


def _cparams(sem):
    return pltpu.CompilerParams(dimension_semantics=sem, vmem_limit_bytes=VMEM_LIMIT)


def _tile(n, cap):
    if n <= cap:
        return n
    best = None
    for t in range(LANE, cap + 1, LANE):
        if n % t == 0:
            best = t
    assert best is not None, (n, cap)
    return best


def _row_tile(m):
    for t in (384, 256, 128):
        if m % t == 0:
            return t
    raise ValueError(m)


def _bf(x):
    return x.astype(BF16)


def _dot(a, b):
    return jnp.dot(a, b, preferred_element_type=F32)


def _dot_nt(a, b):
    return lax.dot_general(a, b, (((1,), (1,)), ((), ())), preferred_element_type=F32)


def _dot_tn(a, b):
    return lax.dot_general(a, b, (((0,), (0,)), ((), ())), preferred_element_type=F32)


def _sigmoid(x):
    return 1.0 / (1.0 + jnp.exp(-x))


def _mm(name, a_list, pairs, extras, outs, fn, *, tm, tn, n):
    m = a_list[0].shape[0]
    na, nb, ne, no = len(a_list), len(pairs), len(extras), len(outs)

    def body(*refs):
        a_refs = refs[:na]
        b_refs = refs[na:na + nb]
        e_refs = refs[na + nb:na + nb + ne]
        o_refs = refs[na + nb + ne:]
        a_vals = [_bf(r[...]) for r in a_refs]
        accs = [_dot(a_vals[ai], b_refs[k][...]) for k, (ai, _, _) in enumerate(pairs)]
        res = fn(accs, [r[...] for r in e_refs])
        for r, v in zip(o_refs, res):
            r[...] = v.astype(r.dtype)

    def spec(block_shape, index_map):
        return pl.BlockSpec(block_shape, functools.partial(lambda j, i, im: im(i, j), im=index_map))

    in_specs = [spec((tm, a.shape[1]), lambda i, j: (i, 0)) for a in a_list]
    for _, b, off in pairs:
        in_specs.append(spec((b.shape[0], tn), functools.partial(lambda i, j, off: (0, j + off), off=off)))
    in_specs += [spec(bs, im) for _, bs, im in extras]
    return pl.pallas_call(
        body, name=name, grid=(n // tn, m // tm),
        in_specs=in_specs,
        out_specs=[spec(bs, im) for _, _, bs, im in outs],
        out_shape=[jax.ShapeDtypeStruct(s, d) for s, d, _, _ in outs],
        compiler_params=_cparams(("parallel", "parallel")),
    )(*a_list, *[b for _, b, _ in pairs], *[e for e, _, _ in extras])


def _tile_spec(tm, tn, col_off=0):
    return (tm, tn), functools.partial(lambda i, j, off: (i, j + off), off=col_off)


def _mm_tn(name, a, b, *, alpha, out_dtype):
    t, k = a.shape
    n = b.shape[1]
    tk, tn = _tile(k, 1408), _tile(n, 1408)
    tt = next(c for c in (768, 512, 256) if t % c == 0)
    nt = t // tt

    def body(a_ref, b_ref, o_ref, acc_ref):
        s = pl.program_id(2)

        @pl.when(s == 0)
        def _():
            acc_ref[...] = jnp.zeros_like(acc_ref)

        acc_ref[...] += _dot_tn(_bf(a_ref[...]), _bf(b_ref[...]))

        @pl.when(s == nt - 1)
        def _():
            o_ref[...] = (alpha * acc_ref[...]).astype(o_ref.dtype)

    return pl.pallas_call(
        body, name=name, grid=(k // tk, n // tn, nt),
        in_specs=[pl.BlockSpec((tt, tk), lambda i, j, s: (s, i)), pl.BlockSpec((tt, tn), lambda i, j, s: (s, j))],
        out_specs=pl.BlockSpec((tk, tn), lambda i, j, s: (i, j)),
        out_shape=jax.ShapeDtypeStruct((k, n), out_dtype),
        scratch_shapes=[pltpu.VMEM((tk, tn), F32)],
        compiler_params=_cparams(("parallel", "parallel", "arbitrary")),
    )(a, b)


def _rms_parts(x):
    r = lax.rsqrt(jnp.mean(x * x, axis=-1, keepdims=True) + EPS)
    return r, x * r


def _rms_bwd_math(x, w, dxn):
    r, xhat = _rms_parts(x)
    t = dxn * w
    dx = r * (t - xhat * jnp.mean(t * xhat, axis=-1, keepdims=True))
    dw = jnp.sum(dxn * xhat, axis=0, keepdims=True)
    return dx, dw


def _rms_fwd(name, h, w):
    t, d = h.shape
    tm = _row_tile(t)

    def body(h_ref, w_ref, o_ref):
        _, xhat = _rms_parts(h_ref[...])
        o_ref[...] = (xhat * w_ref[...]).astype(o_ref.dtype)

    return pl.pallas_call(
        body, name=name, grid=(t // tm,),
        in_specs=[pl.BlockSpec((tm, d), lambda i: (i, 0)), pl.BlockSpec((1, d), lambda i: (0, 0))],
        out_specs=pl.BlockSpec((tm, d), lambda i: (i, 0)),
        out_shape=jax.ShapeDtypeStruct((t, d), BF16),
        compiler_params=_cparams(("parallel",)),
    )(h, w)


def _rms_bwd(name, h, w, dxn, dh_in):
    t, d = h.shape
    tm = _row_tile(t)

    def body(h_ref, w_ref, dxn_ref, dh_ref, o_ref, dw_ref):
        dx, dw = _rms_bwd_math(h_ref[...], w_ref[...], dxn_ref[...])
        o_ref[...] = dh_ref[...] + dx

        @pl.when(pl.program_id(0) == 0)
        def _():
            dw_ref[...] = jnp.zeros_like(dw_ref)

        dw_ref[...] += dw

    row = pl.BlockSpec((tm, d), lambda i: (i, 0))
    vec = pl.BlockSpec((1, d), lambda i: (0, 0))
    return pl.pallas_call(
        body, name=name, grid=(t // tm,),
        in_specs=[row, vec, row, row],
        out_specs=[row, vec],
        out_shape=[jax.ShapeDtypeStruct((t, d), F32), jax.ShapeDtypeStruct((1, d), F32)],
        compiler_params=_cparams(("arbitrary",)),
    )(h, w, dxn, dh_in)


def _loss_head(h, w, tgt, n_real):
    t, d = h.shape
    tm = _row_tile(t)

    def body(h_ref, w_ref, t_ref, dh_ref, dw_ref, loss_ref):
        i = pl.program_id(0)
        x = h_ref[...]
        wv = w_ref[...]
        _, xhat = _rms_parts(x)
        rows = i * tm + lax.broadcasted_iota(jnp.int32, (tm, 1), 0)
        valid = (rows >= N_META) & (rows < n_real)
        e = jnp.where(valid, xhat * wv - t_ref[...], 0.0)
        dx, dw = _rms_bwd_math(x, wv, e * (1.0 / d))
        dh_ref[...] = dx

        @pl.when(i == 0)
        def _():
            dw_ref[...] = jnp.zeros_like(dw_ref)
            loss_ref[...] = jnp.zeros_like(loss_ref)

        dw_ref[...] += dw
        loss_ref[...] += (0.5 / d) * jnp.sum(jnp.sum(e * e, axis=-1, keepdims=True), axis=0, keepdims=True)

    row = pl.BlockSpec((tm, d), lambda i: (i, 0))
    vec = pl.BlockSpec((1, d), lambda i: (0, 0))
    return pl.pallas_call(
        body, name="loss_head", grid=(t // tm,),
        in_specs=[row, vec, row],
        out_specs=[row, vec, pl.BlockSpec((1, 1), lambda i: (0, 0))],
        out_shape=[jax.ShapeDtypeStruct((t, d), F32), jax.ShapeDtypeStruct((1, d), F32), jax.ShapeDtypeStruct((1, 1), F32)],
        compiler_params=_cparams(("arbitrary",)),
    )(h, w, tgt)


def _mla_prep(z, qn_w, kvn_w, ck, sk):
    t = z.shape[0]
    tm = _row_tile(t)

    def body(z_ref, qw_ref, kw_ref, ck_ref, sk_ref, cq_ref, ckv_ref, kr_ref):
        zz = z_ref[...]
        _, qhat = _rms_parts(zz[:, Z_CQ:Z_CKV])
        _, khat = _rms_parts(zz[:, Z_CKV:Z_KPE])
        cq_ref[...] = (qhat * qw_ref[...]).astype(BF16)
        ckv_ref[...] = (khat * kw_ref[...]).astype(BF16)
        kr_ref[...] = zz[:, Z_KPE:Z_KPESW] * ck_ref[...] + zz[:, Z_KPESW:Z_KPESW + LANE] * sk_ref[...]

    def rows(wd):
        return pl.BlockSpec((tm, wd), lambda i: (i, 0))

    def vec(wd):
        return pl.BlockSpec((1, wd), lambda i: (0, 0))

    return pl.pallas_call(
        body, name="mla_prep", grid=(t // tm,),
        in_specs=[rows(Z_HQ), vec(Q_LORA), vec(KV_LORA), rows(LANE), rows(LANE)],
        out_specs=[rows(Q_LORA), rows(KV_LORA), rows(LANE)],
        out_shape=[jax.ShapeDtypeStruct((t, Q_LORA), BF16), jax.ShapeDtypeStruct((t, KV_LORA), BF16),
                   jax.ShapeDtypeStruct((t, LANE), F32)],
        compiler_params=_cparams(("parallel",)),
    )(z, qn_w, kvn_w, ck, sk)


def _mla_prep_bwd(z, qn_w, kvn_w, ck, sk, dcq, dckv, dkv):
    t = z.shape[0]
    tm = _row_tile(t)

    def body(z_ref, qw_ref, kw_ref, ck_ref, sk_ref, dcq_ref, dckv_ref, dkv_ref, dz_ref, dqw_ref, dkw_ref):
        zz = z_ref[...]
        dq, dqw = _rms_bwd_math(zz[:, Z_CQ:Z_CKV], qw_ref[...], dcq_ref[...])
        dk, dkw = _rms_bwd_math(zz[:, Z_CKV:Z_KPE], kw_ref[...], dckv_ref[...])
        dkv_v = dkv_ref[...]
        dkr = jnp.zeros((tm, LANE), F32)
        for hd in range(MLA_HEADS):
            dkr = dkr + dkv_v[:, 2 * LANE * hd:2 * LANE * hd + LANE].astype(F32)
        dz_ref[...] = jnp.concatenate(
            [dq, dk, dkr * ck_ref[...], dkr * sk_ref[...], jnp.zeros((tm, Z_HQ - Z_KPESW - LANE), F32)], axis=1
        ).astype(BF16)

        @pl.when(pl.program_id(0) == 0)
        def _():
            dqw_ref[...] = jnp.zeros_like(dqw_ref)
            dkw_ref[...] = jnp.zeros_like(dkw_ref)

        dqw_ref[...] += dqw
        dkw_ref[...] += dkw

    def rows(wd):
        return pl.BlockSpec((tm, wd), lambda i: (i, 0))

    def vec(wd):
        return pl.BlockSpec((1, wd), lambda i: (0, 0))

    return pl.pallas_call(
        body, name="mla_prep_bwd", grid=(t // tm,),
        in_specs=[rows(Z_HQ), vec(Q_LORA), vec(KV_LORA), rows(LANE), rows(LANE), rows(Q_LORA), rows(KV_LORA),
                  rows(2 * LANE * MLA_HEADS)],
        out_specs=[rows(Z_HQ), vec(Q_LORA), vec(KV_LORA)],
        out_shape=[jax.ShapeDtypeStruct((t, Z_HQ), BF16), jax.ShapeDtypeStruct((1, Q_LORA), F32),
                   jax.ShapeDtypeStruct((1, KV_LORA), F32)],
        compiler_params=_cparams(("arbitrary",)),
    )(z, qn_w, kvn_w, ck, sk, dcq, dckv, dkv)


def _rope_bwd_q(dq, cq, sq):
    t, wq = dq.shape
    tm = _row_tile(t)

    def body(dq_ref, c_ref, s_ref, o_ref):
        d = dq_ref[...] * ATTN_SCALE
        c8 = jnp.concatenate([c_ref[...]] * MLA_HEADS, axis=1)
        s8 = jnp.concatenate([s_ref[...]] * MLA_HEADS, axis=1)
        o_ref[...] = jnp.concatenate([d * c8, d * s8], axis=1).astype(BF16)

    return pl.pallas_call(
        body, name="rope_bwd_q", grid=(t // tm,),
        in_specs=[pl.BlockSpec((tm, wq), lambda i: (i, 0)), pl.BlockSpec((tm, LANE), lambda i: (i, 0)),
                  pl.BlockSpec((tm, LANE), lambda i: (i, 0))],
        out_specs=pl.BlockSpec((tm, 2 * wq), lambda i: (i, 0)),
        out_shape=jax.ShapeDtypeStruct((t, 2 * wq), BF16),
        compiler_params=_cparams(("parallel",)),
    )(dq, cq, sq)


def _row_vector(col):
    return jnp.broadcast_to(col, (col.shape[0], LANE)).T[0:8, :]


def _attn_block(t):
    for b in (768, 512, 256):
        if t % b == 0:
            return b
    raise ValueError(t)


def _call_carrying_push(body, push, *, name, grid, in_specs, out_specs, out_shape, scratch_shapes, args):
    if push is None:
        outs = pl.pallas_call(body, name=name, grid=grid, in_specs=in_specs, out_specs=out_specs, out_shape=out_shape,
                              scratch_shapes=scratch_shapes, compiler_params=_cparams(("parallel", "arbitrary")))(*args)
        return outs, None
    n_in, n_out, n_scr, n_pin, nk = len(in_specs), len(out_specs), len(scratch_shapes), len(push["ins"]), push["nk"]

    def carrying(*refs):
        o0 = n_in + n_pin
        s0 = o0 + n_out + nk
        pins, pouts, sems = refs[n_in:o0], refs[o0 + n_out:s0], refs[s0 + n_scr:]
        a, b = pl.program_id(0), pl.program_id(1)

        @pl.when((a == 0) & (b == 0))
        def _():
            for cp in _push_copies(push, pins, pouts, sems):
                cp.start()

        body(*refs[:n_in], *refs[o0:o0 + n_out], *refs[s0:s0 + n_scr])

        @pl.when((a == grid[0] - 1) & (b == grid[1] - 1))
        def _():
            for cp in _push_copies(push, pins, pouts, sems):
                cp.wait()

    outs = pl.pallas_call(
        carrying, name=name + "_push", grid=grid, in_specs=list(in_specs) + push["in_specs"],
        out_specs=list(out_specs) + push["out_specs"], out_shape=list(out_shape) + push["outs"],
        scratch_shapes=list(scratch_shapes) + push["sems"], input_output_aliases=_push_aliases(push, n_in, n_out),
        compiler_params=_cparams(("arbitrary", "arbitrary")))(*args, *push["ins"])
    return outs[:n_out], outs[n_out:]


def _attn_fwd(q, kv, push=None):
    t = q.shape[0]
    bq = bk = _attn_block(t)
    nq = t // bq

    def body(q_ref, k_ref, v_ref, o_ref, lse_ref):
        i = pl.program_id(1)
        qv = q_ref[...]
        qpos = i * bq + lax.broadcasted_iota(jnp.int32, (bq, 1), 0)
        n = i + 1

        def block(j, carry, masked):
            m, l, acc = carry
            rows = pl.ds(pl.multiple_of(j * bk, bk), bk)
            s = _dot_nt(qv, k_ref[rows, :])
            if masked:
                kpos = j * bk + lax.broadcasted_iota(jnp.int32, (1, bk), 1)
                s = jnp.where(kpos <= qpos, s, NEG_BIG)
            m_new = jnp.maximum(m, jnp.max(s, axis=1, keepdims=True))
            p = jnp.exp2((s - m_new) * ATTN_C2)
            a = jnp.exp2((m - m_new) * ATTN_C2)
            return m_new, a * l + jnp.sum(p, axis=1, keepdims=True), a * acc + _dot(_bf(p), v_ref[rows, :])

        init = (jnp.full((bq, 1), NEG_BIG, F32), jnp.zeros((bq, 1), F32), jnp.zeros((bq, LANE), F32))
        carry = lax.fori_loop(0, n - 1, functools.partial(block, masked=False), init)
        m, l, acc = block(n - 1, carry, True)
        o_ref[...] = (acc / l).astype(o_ref.dtype)
        lse_ref[0] = _row_vector(m * ATTN_C2 + jnp.log(l) * LOG2E)

    return _call_carrying_push(
        body, push, name="attn_fwd", grid=(MLA_HEADS, nq),
        in_specs=[pl.BlockSpec((bq, LANE), lambda h, i: (i, h)),
                  pl.BlockSpec((t, LANE), lambda h, i: (0, 2 * h)),
                  pl.BlockSpec((t, LANE), lambda h, i: (0, 2 * h + 1))],
        out_specs=[pl.BlockSpec((bq, LANE), lambda h, i: (i, h)),
                   pl.BlockSpec((1, 8, bq), lambda h, i: (h, 0, i))],
        out_shape=[jax.ShapeDtypeStruct((t, MLA_HEADS * LANE), BF16), jax.ShapeDtypeStruct((MLA_HEADS, 8, t), F32)],
        scratch_shapes=[], args=(q, kv, kv))


def _attn_bwd(q, kv, o, do, lse, push=None):
    t = q.shape[0]
    bk = bw = _attn_block(t)
    nk, nw = t // bk, t // bw
    lse = lse[:, 0, :].reshape(MLA_HEADS, nw, 1, bw)

    def body(q_ref, o_ref, do_ref, k_ref, v_ref, lse_ref, dq_ref, dkv_ref, dl_ref):
        j = pl.program_id(1)

        @pl.when(j == 0)
        def _():
            dq_ref[...] = jnp.zeros_like(dq_ref)
            for i in range(nw):
                rows = slice(i * bw, (i + 1) * bw)
                d = jnp.sum(o_ref[rows, :].astype(F32) * do_ref[rows, :].astype(F32), axis=1, keepdims=True)
                dl_ref[i] = jnp.broadcast_to(d, (bw, LANE)).T[0:1, :]

        kb = k_ref[...]
        vb = v_ref[...]
        kpos = j * bk + lax.broadcasted_iota(jnp.int32, (bk, 1), 0)

        def block(i, carry, masked):
            dk, dv = carry
            rows = pl.ds(pl.multiple_of(i * bw, bw), bw)
            qb = q_ref[rows, :]
            dob = do_ref[rows, :]
            pt = jnp.exp2(_dot_nt(kb, qb) * ATTN_C2 - lse_ref[0, i])
            if masked:
                qpos = i * bw + lax.broadcasted_iota(jnp.int32, (1, bw), 1)
                pt = jnp.where(kpos <= qpos, pt, 0.0)
            dv = dv + _dot(_bf(pt), dob)
            dst = _bf(pt * (_dot_nt(vb, dob) - dl_ref[i]))
            dk = dk + _dot(dst, qb)
            dq_ref[rows, :] += _dot_tn(dst, kb)
            return dk, dv

        i0 = j
        carry = block(i0, (jnp.zeros((bk, LANE), F32), jnp.zeros((bk, LANE), F32)), True)
        dk, dv = lax.fori_loop(i0 + 1, nw, functools.partial(block, masked=False), carry)
        dkv_ref[...] = jnp.concatenate([dk * ATTN_SCALE, dv], axis=1).astype(dkv_ref.dtype)

    head_rows = pl.BlockSpec((t, LANE), lambda h, j: (0, h))
    return _call_carrying_push(
        body, push, name="attn_bwd", grid=(MLA_HEADS, nk),
        in_specs=[head_rows, head_rows, head_rows,
                  pl.BlockSpec((bk, LANE), lambda h, j: (j, 2 * h)), pl.BlockSpec((bk, LANE), lambda h, j: (j, 2 * h + 1)),
                  pl.BlockSpec((1, nw, 1, bw), lambda h, j: (h, 0, 0, 0))],
        out_specs=[head_rows, pl.BlockSpec((bk, 2 * LANE), lambda h, j: (j, h))],
        out_shape=[jax.ShapeDtypeStruct((t, MLA_HEADS * LANE), F32), jax.ShapeDtypeStruct((t, 2 * MLA_HEADS * LANE), BF16)],
        scratch_shapes=[pltpu.VMEM((nw, 1, bw), F32)], args=(q, o, do, kv, kv, lse))


def _hg_tables():
    c = HG_CHUNK
    tri = (jnp.arange(c)[:, None] >= jnp.arange(c)[None, :]).astype(F32)
    mats = [tri]
    for m in HG_LEVELS:
        ref = (jnp.arange(c) // (2 * m)) * (2 * m) + m - 1
        mats.append((ref[:, None] >= jnp.arange(c)[None, :]).astype(F32))
    call = jnp.concatenate(mats, axis=0)
    return call.astype(BF16), call.T.astype(BF16)


def _table_dot(table, x):
    hi = _bf(x)
    rest = x - hi.astype(F32)
    mid = _bf(rest)
    lo = _bf(rest - mid.astype(F32))
    out = _dot(table, jnp.concatenate([hi, mid, lo], axis=1))
    n = x.shape[1]
    return out[:, 0:n] + out[:, n:2 * n] + out[:, 2 * n:3 * n]


def _hg_gates(hq, hf, lb):
    sg = _sigmoid(hf)
    sn = _sigmoid(-hf)
    f = lb + (1.0 - lb) * sg
    q = hq * _sigmoid(hq)
    g = jnp.log(jnp.maximum(f, F_MIN))
    k = (1.0 - lb) * sn
    return q, k, g, f, sg, sn


def _hg_level_masks(m):
    c = HG_CHUNK
    row = lax.broadcasted_iota(jnp.int32, (c, 1), 0)
    col = lax.broadcasted_iota(jnp.int32, (1, c), 1)
    shift = (2 * m).bit_length() - 1
    up = (row & m) != 0
    same = lax.shift_right_logical(row, shift) == lax.shift_right_logical(col, shift)
    return up, same


def _hg_level_factors(b, bref, up):
    lo = jnp.logical_not(up)
    eq = jnp.where(up, jnp.exp(jnp.where(up, b - bref, 0.0)), 0.0)
    ek = jnp.where(lo, jnp.exp(jnp.where(lo, bref - b, 0.0)), 0.0)
    return eq, ek


def _hg_intra(q, k, ball):
    c = HG_CHUNK
    b = ball[0:c]
    row = lax.broadcasted_iota(jnp.int32, (c, 1), 0)
    col = lax.broadcasted_iota(jnp.int32, (1, c), 1)
    a = jnp.where(row == col, _dot_nt(_bf(q), _bf(k)), 0.0)
    parts = []
    for lv, m in enumerate(HG_LEVELS):
        up, same = _hg_level_masks(m)
        eq, ek = _hg_level_factors(b, ball[(lv + 1) * c:(lv + 2) * c], up)
        qt, kt = q * eq, k * ek
        a = a + jnp.where(same, _dot_nt(_bf(qt), _bf(kt)), 0.0)
        parts.append((eq, ek, qt, kt))
    return a, parts


def _hg_chunk_fwd(hq, hf, hi, hg, lb, nw, st, call):
    q, k, g, _, _, _ = _hg_gates(hq, hf, lb)
    ball = _table_dot(call, g)
    b = ball[0:HG_CHUNK]
    a, _ = _hg_intra(q, k, ball)
    v16 = _bf(hi)
    o = _dot(_bf(a * HG_SCALE), v16) + _dot_nt(_bf(q * jnp.exp(b) * HG_SCALE), _bf(st))
    bl = b[HG_CHUNK - 1:HG_CHUNK]
    ke = k * jnp.exp(bl - b)
    st_new = st * jnp.exp(bl) + _dot(_bf(hi.T), _bf(ke))
    r = lax.rsqrt(jnp.mean(o * o, axis=-1, keepdims=True) + EPS)
    y = o * r * nw * (hg * _sigmoid(hg))
    return y, st_new


def _hg_chunk_bwd(hq, hf, hi, hg, lb, nw, st, call, call_t, dy, dst_new):
    c = HG_CHUNK
    q, k, g, f, sg, sn = _hg_gates(hq, hf, lb)
    ball = _table_dot(call, g)
    b = ball[0:c]
    a, parts = _hg_intra(q, k, ball)
    v16 = _bf(hi)
    st16 = _bf(st)
    eb = jnp.exp(b)
    qe = q * eb * HG_SCALE
    a16 = _bf(a * HG_SCALE)
    o = _dot(a16, v16) + _dot_nt(_bf(qe), st16)
    bl = b[c - 1:c]
    el = jnp.exp(bl)
    x = jnp.exp(bl - b)
    ke = k * x
    r = lax.rsqrt(jnp.mean(o * o, axis=-1, keepdims=True) + EPS)
    shg = _sigmoid(hg)
    gate = hg * shg
    ohat = o * r
    don = dy * gate
    dhg = dy * ohat * nw * (shg * (1.0 + hg * (1.0 - shg)))
    dnw = jnp.sum(don * ohat, axis=0, keepdims=True)
    tt = don * nw
    do = r * (tt - ohat * jnp.mean(tt * ohat, axis=-1, keepdims=True))
    do16 = _bf(do)
    dst16 = _bf(dst_new)
    da = _dot_nt(do16, v16) * HG_SCALE
    dv = _dot(_bf(a16.astype(F32).T), do16) + _dot_nt(_bf(ke), dst16)
    dqe = _dot(do16, st16)
    dke = _dot(v16, dst16)
    dst = dst_new * el + _dot(_bf(do.T), _bf(qe))
    dbl = jnp.sum(dst_new * st, axis=0, keepdims=True) * el
    dk = dke * x
    dxa = dke * ke
    db = dqe * qe - dxa
    dbl = dbl + jnp.sum(dxa, axis=0, keepdims=True)
    dq = dqe * eb * HG_SCALE
    row = lax.broadcasted_iota(jnp.int32, (c, 1), 0)
    col = lax.broadcasted_iota(jnp.int32, (1, c), 1)
    ddiag = jnp.sum(jnp.where(row == col, da, 0.0), axis=1, keepdims=True)
    dq = dq + ddiag * k
    dk = dk + ddiag * q
    dball = []
    for (eq, ek, qt, kt), m in zip(parts, HG_LEVELS):
        _, same = _hg_level_masks(m)
        gm = jnp.where(same, da, 0.0)
        dqt = _dot(_bf(gm), _bf(kt))
        dkt = _dot(_bf(gm.T), _bf(qt))
        dq = dq + dqt * eq
        dk = dk + dkt * ek
        darg = dqt * qt - dkt * kt
        db = db + darg
        dball.append(-darg)
    db = db + jnp.where(row == c - 1, dbl, 0.0)
    dg = _table_dot(call_t, jnp.concatenate([db] + dball, axis=0))
    shq = _sigmoid(hq)
    dhq = dq * (shq * (1.0 + hq * (1.0 - shq)))
    df = jnp.where(f > F_MIN, dg / jnp.maximum(f, F_MIN), 0.0)
    dlb = jnp.sum(df * (1.0 - sg) - dk * sn, axis=0, keepdims=True)
    dhf = df * (1.0 - lb) * sg * (1.0 - sg) - dk * (1.0 - lb) * sn * (1.0 - sn)
    return dhq, dhf, dv, dhg, dlb, dnw, dst


def _hg_col(group, h):
    return group // LANE + h


def _hgrn_fwd(z, lb, nw, call):
    t = z.shape[0]
    c, cs = HG_CHUNK, HG_CHUNKS_PER_STEP
    rows = c * cs
    nsteps = t // rows

    def body(hq_ref, hf_ref, hi_ref, hg_ref, lb_ref, nw_ref, call_ref, y_ref, sv_ref, st_ref):
        @pl.when(pl.program_id(1) == 0)
        def _():
            st_ref[...] = jnp.zeros_like(st_ref)

        for u in range(cs):
            sl = slice(u * c, (u + 1) * c)
            st = st_ref[...]
            sv_ref[0, u] = st
            y, st_new = _hg_chunk_fwd(hq_ref[sl, :], hf_ref[sl, :], hi_ref[sl, :], hg_ref[sl, :], lb_ref[...], nw_ref[...],
                                      st, call_ref[...])
            y_ref[sl, :] = y.astype(y_ref.dtype)
            st_ref[...] = st_new

    def zcol(group):
        return pl.BlockSpec((rows, LANE), functools.partial(lambda h, i, g: (i, _hg_col(g, h)), g=group))

    ncall = call.shape[0]
    return pl.pallas_call(
        body, name="hgrn_fwd", grid=(HG_HEADS, nsteps),
        in_specs=[zcol(Z_HQ), zcol(Z_HF), zcol(Z_HI), zcol(Z_HG),
                  pl.BlockSpec((1, LANE), lambda h, i: (0, h)), pl.BlockSpec((1, LANE), lambda h, i: (0, 0)),
                  pl.BlockSpec((ncall, c), lambda h, i: (0, 0))],
        out_specs=[pl.BlockSpec((rows, LANE), lambda h, i: (i, h)),
                   pl.BlockSpec((1, cs, c, c), lambda h, i: (h, i, 0, 0))],
        out_shape=[jax.ShapeDtypeStruct((t, HG_HEADS * LANE), BF16), jax.ShapeDtypeStruct((HG_HEADS, t // c, c, c), F32)],
        scratch_shapes=[pltpu.VMEM((c, c), F32)],
        compiler_params=_cparams(("parallel", "arbitrary")),
    )(z, z, z, z, lb, nw, call)


def _hgrn_bwd(z, lb, nw, call, call_t, saved, dy):
    t = z.shape[0]
    c, cs = HG_CHUNK, HG_CHUNKS_PER_STEP
    rows = c * cs
    nsteps = t // rows

    def body(hq_ref, hf_ref, hi_ref, hg_ref, lb_ref, nw_ref, call_ref, callt_ref, sv_ref, dy_ref,
             dhq_ref, dhf_ref, dhi_ref, dhg_ref, dlb_ref, dnw_ref, dst_ref):
        h, i = pl.program_id(0), pl.program_id(1)

        @pl.when(i == 0)
        def _():
            dst_ref[...] = jnp.zeros_like(dst_ref)
            dlb_ref[...] = jnp.zeros_like(dlb_ref)

        @pl.when((i == 0) & (h == 0))
        def _():
            dnw_ref[...] = jnp.zeros_like(dnw_ref)

        for u in reversed(range(cs)):
            sl = slice(u * c, (u + 1) * c)
            dhq, dhf, dhi, dhg, dlb, dnw, dst = _hg_chunk_bwd(
                hq_ref[sl, :], hf_ref[sl, :], hi_ref[sl, :], hg_ref[sl, :], lb_ref[...], nw_ref[...], sv_ref[0, u],
                call_ref[...], callt_ref[...], dy_ref[sl, :].astype(F32), dst_ref[...])
            dhq_ref[sl, :] = dhq.astype(BF16)
            dhf_ref[sl, :] = dhf.astype(BF16)
            dhi_ref[sl, :] = dhi.astype(BF16)
            dhg_ref[sl, :] = dhg.astype(BF16)
            dlb_ref[...] += dlb
            dnw_ref[...] += dnw
            dst_ref[...] = dst

    def zcol(group):
        return pl.BlockSpec((rows, LANE), functools.partial(lambda h, i, g: (nsteps - 1 - i, _hg_col(g, h)), g=group))

    head_rows = pl.BlockSpec((rows, LANE), lambda h, i: (nsteps - 1 - i, h))
    ncall = call.shape[0]
    piece = jax.ShapeDtypeStruct((t, HG_HEADS * LANE), BF16)
    return pl.pallas_call(
        body, name="hgrn_bwd", grid=(HG_HEADS, nsteps),
        in_specs=[zcol(Z_HQ), zcol(Z_HF), zcol(Z_HI), zcol(Z_HG),
                  pl.BlockSpec((1, LANE), lambda h, i: (0, h)), pl.BlockSpec((1, LANE), lambda h, i: (0, 0)),
                  pl.BlockSpec((ncall, c), lambda h, i: (0, 0)), pl.BlockSpec((c, ncall), lambda h, i: (0, 0)),
                  pl.BlockSpec((1, cs, c, c), lambda h, i: (h, nsteps - 1 - i, 0, 0)), head_rows],
        out_specs=[head_rows, head_rows, head_rows, head_rows,
                   pl.BlockSpec((1, LANE), lambda h, i: (0, h)), pl.BlockSpec((1, LANE), lambda h, i: (0, 0))],
        out_shape=[piece, piece, piece, piece,
                   jax.ShapeDtypeStruct((1, HG_HEADS * LANE), F32), jax.ShapeDtypeStruct((1, LANE), F32)],
        scratch_shapes=[pltpu.VMEM((c, c), F32)],
        compiler_params=_cparams(("arbitrary", "arbitrary")),
    )(z, z, z, z, lb, nw, call, call_t, saved, dy)


def _lb_fwd(raw):
    nl = raw.shape[0]

    def body(r_ref, o_ref):
        x = r_ref[...]
        e = jnp.exp(x - jnp.max(x, axis=0, keepdims=True))
        p = e / jnp.sum(e, axis=0, keepdims=True)
        acc = jnp.zeros_like(p[0:1])
        for l in range(nl):
            if l > 0:
                acc = acc + p[l:l + 1]
            o_ref[l:l + 1, :] = acc

    return pl.pallas_call(body, name="lb_fwd", out_shape=jax.ShapeDtypeStruct(raw.shape, F32))(raw)


def _lb_bwd(raw, dlbs):
    nl = raw.shape[0]

    def body(r_ref, d_ref, o_ref):
        x = r_ref[...]
        e = jnp.exp(x - jnp.max(x, axis=0, keepdims=True))
        p = e / jnp.sum(e, axis=0, keepdims=True)
        d = d_ref[...]
        dps = [jnp.zeros_like(d[0:1])]
        for i in range(1, nl):
            acc = d[i:i + 1]
            for l in range(i + 1, nl):
                acc = acc + d[l:l + 1]
            dps.append(acc)
        dot = dps[0] * p[0:1]
        for i in range(1, nl):
            dot = dot + dps[i] * p[i:i + 1]
        for i in range(nl):
            o_ref[i:i + 1, :] = p[i:i + 1] * (dps[i] - dot)

    return pl.pallas_call(body, name="lb_bwd", out_shape=jax.ShapeDtypeStruct(raw.shape, F32))(raw, dlbs)


def _push_plan(srcs, gather, bufs=None, layer=None):
    nk = len(srcs)
    any_spec = pl.BlockSpec(memory_space=pl.ANY)
    if bufs is None:
        ins = list(srcs)
        outs = [jax.ShapeDtypeStruct(((N_DEV,) + s.shape) if gather else s.shape, s.dtype) for s in srcs]
    else:
        ins = list(srcs) + list(bufs)
        outs = [jax.ShapeDtypeStruct(b.shape, b.dtype) for b in bufs]
    sems = [pltpu.SemaphoreType.DMA((nk * N_DEV,)), pltpu.SemaphoreType.DMA((nk * N_DEV,)), pltpu.SemaphoreType.DMA((nk,))]
    return dict(nk=nk, gather=gather, layer=layer, ins=ins, in_specs=[any_spec] * len(ins), outs=outs,
                out_specs=[any_spec] * nk, sems=sems, alias_from=None if bufs is None else nk)


def _push_aliases(plan, first_in, first_out):
    if plan is None or plan["alias_from"] is None:
        return {}
    return {first_in + plan["alias_from"] + k: first_out + k for k in range(plan["nk"])}


def _push_copies(plan, in_refs, out_refs, sems):
    nk, gather, layer = plan["nk"], plan["gather"], plan["layer"]
    send_sems, recv_sems, local_sems = sems
    me = 4 * lax.axis_index("x") + 2 * lax.axis_index("y") + lax.axis_index("c")

    def landing(k):
        return out_refs[k].at[me] if layer is None else out_refs[k].at[me, layer]

    copies = [pltpu.make_async_copy(in_refs[k] if gather else in_refs[k].at[me], landing(k), local_sems.at[k])
              for k in range(nk)]
    for r in range(1, N_DEV):
        to = (me + r) % N_DEV
        for k in range(nk):
            copies.append(pltpu.make_async_remote_copy(
                src_ref=in_refs[k] if gather else in_refs[k].at[to], dst_ref=landing(k),
                send_sem=send_sems.at[k * N_DEV + r], recv_sem=recv_sems.at[k * N_DEV + r],
                device_id=(to // 4, (to // 2) % 2, to % 2), device_id_type=pl.DeviceIdType.MESH))
    return copies


def _exchange(name, srcs, gather, bufs=None, layer=None):
    plan = _push_plan(srcs, gather, bufs, layer)
    nin, nk = len(plan["ins"]), plan["nk"]

    def body(*refs):
        copies = _push_copies(plan, refs[:nin], refs[nin:nin + nk], refs[nin + nk:])
        for cp in copies:
            cp.start()
        for cp in copies:
            cp.wait()

    return pl.pallas_call(
        body, name=name, in_specs=plan["in_specs"], out_specs=plan["out_specs"], out_shape=plan["outs"],
        scratch_shapes=plan["sems"], input_output_aliases=_push_aliases(plan, 0, 0),
    )(*plan["ins"])


def _adam_math(g, w, m, v):
    m2 = ADAM_B1 * m + (1.0 - ADAM_B1) * g
    v2 = ADAM_B2 * v + (1.0 - ADAM_B2) * (g * g)
    m_hat = m2 / (1.0 - ADAM_B1 ** ADAM_STEP)
    v_hat = v2 / (1.0 - ADAM_B2 ** ADAM_STEP)
    return -ADAM_LR * (m_hat / (jnp.sqrt(v_hat) + ADAM_EPS) + ADAM_WD * w), m2, v2


def _sum_slots(ref):
    g = ref[0].astype(F32)
    for s in range(1, N_DEV):
        g = g + ref[s].astype(F32)
    return g


def _adam_sharded(name, slots, w, m, v):
    nl, a, b = w.shape
    ta = a
    for cand in range(8, 257, 8):
        if a % cand == 0:
            ta = cand

    def body(s_ref, w_ref, m_ref, v_ref, g_ref, d_ref, m2_ref, v2_ref):
        g = _sum_slots(s_ref)
        d, m2, v2 = _adam_math(g, w_ref[...], m_ref[...], v_ref[...])
        g_ref[...] = g
        d_ref[...] = d
        m2_ref[...] = m2
        v2_ref[...] = v2

    blk = pl.BlockSpec((1, ta, b), lambda l, i: (l, i, 0))
    sds = jax.ShapeDtypeStruct(w.shape, F32)
    return pl.pallas_call(
        body, name=name, grid=(nl, a // ta),
        in_specs=[pl.BlockSpec((N_DEV, 1, ta, b), lambda l, i: (0, l, i, 0)), blk, blk, blk],
        out_specs=[blk] * 4, out_shape=[sds] * 4,
        compiler_params=_cparams(("parallel", "parallel")),
    )(slots, w, m, v)


def _sum_replicated(slots):
    def body(s_ref, g_ref):
        g_ref[...] = _sum_slots(s_ref)

    return pl.pallas_call(body, name="sum_small", out_shape=jax.ShapeDtypeStruct(slots.shape[1:], F32))(slots)


def _adam_small(name, g, w, m, v):
    def body(g_ref, w_ref, m_ref, v_ref, d_ref, m2_ref, v2_ref):
        d, m2, v2 = _adam_math(g_ref[...], w_ref[...], m_ref[...], v_ref[...])
        d_ref[...] = d
        m2_ref[...] = m2
        v2_ref[...] = v2

    sds = jax.ShapeDtypeStruct(w.shape, F32)
    return pl.pallas_call(body, name=name, out_shape=[sds] * 3)(g, w, m, v)


def _cols_full(g):
    return jnp.transpose(g, (1, 0, 2)).reshape(g.shape[1], -1)


def _cols_shards(w):
    k = w.shape[0]
    return jnp.transpose(w.reshape(k, N_DEV, -1), (1, 0, 2))


def _swap_halves(x):
    half = x.shape[-1] // 2
    return jnp.concatenate([x[..., half:], x[..., :half]], axis=-1)


def _zeros_like_cols(x, n):
    return jnp.zeros(x.shape[:-1] + (n,), x.dtype)


def _w_in_internal(w):
    d = w.shape[0]
    kpe = w[:, 640:672]
    z64, z32 = jnp.zeros((d, 64), w.dtype), jnp.zeros((d, 32), w.dtype)
    return jnp.concatenate(
        [w[:, 0:640], z64, kpe, z32, z64, _swap_halves(kpe), z32, jnp.zeros((d, Z_HQ - Z_KPESW - LANE), w.dtype),
         w[:, 672:2720], w[:, 2720:4768]], axis=1)


def _w_in_grad(g):
    kpe = g[:, Z_KPE + 64:Z_KPE + 96] + _swap_halves(g[:, Z_KPESW + 64:Z_KPESW + 96])
    return jnp.concatenate([g[:, 0:640], kpe, g[:, Z_HQ:Z_W]], axis=1)


def _w_uq_internal(w):
    k = w.shape[0]
    w3 = w.reshape(k, MLA_HEADS, QK_NOPE + QK_ROPE)
    nope, rope = w3[..., :QK_NOPE], w3[..., QK_NOPE:]
    plain = jnp.concatenate([nope, rope, _zeros_like_cols(rope, 32)], axis=-1).reshape(k, -1)
    swapped = jnp.concatenate([_zeros_like_cols(nope, 64), _swap_halves(rope), _zeros_like_cols(rope, 32)], axis=-1).reshape(k, -1)
    return jnp.concatenate([plain, swapped], axis=1)


def _w_uq_grad(g):
    k = g.shape[0]
    half = MLA_HEADS * LANE
    g1, g2 = g[:, :half].reshape(k, MLA_HEADS, LANE), g[:, half:].reshape(k, MLA_HEADS, LANE)
    rope = g1[..., 64:96] + _swap_halves(g2[..., 64:96])
    return jnp.concatenate([g1[..., :64], rope], axis=-1).reshape(k, -1)


def _w_ukv_internal(w):
    k = w.shape[0]
    w3 = w.reshape(k, MLA_HEADS, QK_NOPE + V_HEAD)
    kn, vv = w3[..., :QK_NOPE], w3[..., QK_NOPE:]
    z = _zeros_like_cols(kn, 64)
    return jnp.concatenate([kn, z, vv, z], axis=-1).reshape(k, -1)


def _w_ukv_grad(g):
    k = g.shape[0]
    g3 = g.reshape(k, MLA_HEADS, 2 * LANE)
    return jnp.concatenate([g3[..., 0:64], g3[..., LANE:LANE + 64]], axis=-1).reshape(k, -1)


def _w_pa_internal(w):
    n = w.shape[1]
    w3 = w.reshape(MLA_HEADS, V_HEAD, n)
    return jnp.concatenate([w3, jnp.zeros_like(w3)], axis=1).reshape(-1, n)


def _w_pa_grad(g):
    n = g.shape[1]
    return g.reshape(MLA_HEADS, 2 * V_HEAD, n)[:, :V_HEAD].reshape(-1, n)


def _rope_tables(t):
    half = QK_ROPE // 2
    inv = ROPE_THETA ** (-jnp.arange(half, dtype=F32) / half)
    ang = jnp.arange(t, dtype=F32)[:, None] * inv[None, :]
    cos, sin = jnp.cos(ang), jnp.sin(ang)
    one, zero = jnp.ones((t, 64), F32), jnp.zeros((t, 64), F32)
    z32 = jnp.zeros((t, 32), F32)
    cq = jnp.concatenate([one, cos, cos, z32], axis=1)
    ck = jnp.concatenate([zero, cos, cos, z32], axis=1)
    sq = jnp.concatenate([zero, -sin, sin, z32], axis=1)
    return cq, ck, sq


def _ffn_fwd(tag, h, nw, w_gu, w_down):
    t, d = h.shape
    dff = w_down.shape[0]
    tm, tn = _row_tile(t), _tile(dff, 1408)
    xn = _rms_fwd(tag + "_norm", h, nw)

    def act_fn(accs, _):
        g, u = accs
        return g, u, g * _sigmoid(g) * u

    spec = _tile_spec(tm, tn)
    g, u, act = _mm(tag + "_gu", [xn], [(0, w_gu, 0), (0, w_gu, dff // tn)], [],
                    [((t, dff), BF16) + spec] * 3, act_fn, tm=tm, tn=tn, n=dff)
    tn2 = _tile(d, 1024)
    h2, = _mm(tag + "_down", [act], [(0, w_down, 0)], [(h,) + _tile_spec(tm, tn2)],
              [((t, d), F32) + _tile_spec(tm, tn2)], lambda accs, ex: (ex[0] + 0.5 * accs[0],), tm=tm, tn=tn2, n=d)
    return h2, (h, xn, g, u, act)


def _ffn_bwd(tag, dh2, saved, nw, w_gu_t, w_down_t):
    h, xn, g, u, act = saved
    t, d = h.shape
    dff = act.shape[1]
    tm, tn = _row_tile(t), _tile(dff, 1408)

    def dact_fn(accs, ex):
        gg, uu = ex[0].astype(F32), ex[1].astype(F32)
        da = 0.5 * accs[0]
        sg = _sigmoid(gg)
        return da * uu * (sg * (1.0 + gg * (1.0 - sg))), da * (gg * sg)

    spec = _tile_spec(tm, tn)
    dg, du = _mm(tag + "_dact", [dh2], [(0, w_down_t, 0)], [(g,) + spec, (u,) + spec],
                 [((t, dff), BF16) + spec] * 2, dact_fn, tm=tm, tn=tn, n=dff)
    dw_down = _mm_tn(tag + "_dwdown", act, dh2, alpha=0.5, out_dtype=BF16)
    tn2 = _tile(d, 512)
    dxn, = _mm(tag + "_dxn", [dg, du], [(0, w_gu_t[:dff], 0), (1, w_gu_t[dff:], 0)], [],
               [((t, d), F32) + _tile_spec(tm, tn2)], lambda accs, _: (accs[0] + accs[1],), tm=tm, tn=tn2, n=d)
    dw_gu = jnp.concatenate([_mm_tn(tag + "_dwg", xn, dg, alpha=1.0, out_dtype=BF16),
                             _mm_tn(tag + "_dwu", xn, du, alpha=1.0, out_dtype=BF16)], axis=1)
    dh, dnw = _rms_bwd(tag + "_dnorm", h, nw, dxn, dh2)
    return dh, dnw, dw_gu, dw_down


def _kv_pattern(kr):
    z = jnp.zeros_like(kr)
    return jnp.concatenate([kr, z] * MLA_HEADS, axis=1)


def _mix_fwd(h, p, tabs, lb, call, push):
    t, d = h.shape
    tm = _row_tile(t)
    cq_t, ck_t, sq_t = tabs
    u = _rms_fwd("mix_norm", h, p["mix_norm"])
    tnz = _tile(Z_W, 1024)
    z, = _mm("mix_in", [u], [(0, p["w_in"], 0)], [], [((t, Z_W), F32) + _tile_spec(tm, tnz)], lambda a, _: (a[0],),
             tm=tm, tn=tnz, n=Z_W)
    cqn, ckvn, krot = _mla_prep(z, p["q_norm"], p["kv_norm"], ck_t, sq_t)
    wq = MLA_HEADS * LANE
    lane_rows = lambda i, j: (i, 0)

    def q_fn(accs, ex):
        c8 = jnp.concatenate([ex[0]] * MLA_HEADS, axis=1)
        s8 = jnp.concatenate([ex[1]] * MLA_HEADS, axis=1)
        return (accs[0] * c8 + accs[1] * s8,)

    q, = _mm("mla_q", [cqn], [(0, p["w_uq"], 0), (0, p["w_uq"], 1)],
             [(cq_t, (tm, LANE), lane_rows), (sq_t, (tm, LANE), lane_rows)],
             [((t, wq), BF16) + _tile_spec(tm, wq)], q_fn, tm=tm, tn=wq, n=wq)
    kv, = _mm("mla_kv", [ckvn], [(0, p["w_ukv"], 0)], [(krot, (tm, LANE), lane_rows)],
              [((t, 2 * wq), BF16) + _tile_spec(tm, 2 * wq)], lambda a, ex: (a[0] + _kv_pattern(ex[0]),),
              tm=tm, tn=2 * wq, n=2 * wq)
    (o_a, lse), pushed = _attn_fwd(q, kv, push)
    o_b, st_saved = _hgrn_fwd(z, lb, p["hg_norm"], call)
    tn = _tile(d, 512)

    def merge_fn(accs, ex):
        ya, yb = accs
        return ya, yb, _sigmoid(ex[0]) * ya + _sigmoid(ex[1]) * yb

    spec = _tile_spec(tm, tn)
    ya, yb, merged = _mm("mix_merge", [o_a, o_b], [(0, p["w_pa"], 0), (1, p["w_pr"], 0)],
                         [(z,) + _tile_spec(tm, tn, Z_GA // tn), (z,) + _tile_spec(tm, tn, Z_GB // tn)],
                         [((t, d), BF16) + spec] * 3, merge_fn, tm=tm, tn=tn, n=d)
    tn2 = _tile(d, 1024)
    h2, = _mm("mix_out", [merged], [(0, p["w_out"], 0)], [(h,) + _tile_spec(tm, tn2)],
              [((t, d), F32) + _tile_spec(tm, tn2)], lambda a, ex: (ex[0] + a[0],), tm=tm, tn=tn2, n=d)
    return h2, (h, u, z, cqn, ckvn, q, kv, o_a, lse, o_b, st_saved, ya, yb, merged), pushed


def _mix_bwd(dh2, saved, p, tabs, lb, call, call_t, push):
    h, u, z, cqn, ckvn, q, kv, o_a, lse, o_b, st_saved, ya, yb, merged = saved
    t, d = h.shape
    tm = _row_tile(t)
    cq_t, ck_t, sq_t = tabs
    tn = _tile(d, 512)
    spec = _tile_spec(tm, tn)

    def dmerge_fn(accs, ex):
        dm = accs[0]
        yav, ybv = ex[0].astype(F32), ex[1].astype(F32)
        sa, sb = _sigmoid(ex[2]), _sigmoid(ex[3])
        return dm * sa, dm * sb, dm * yav * sa * (1.0 - sa), dm * ybv * sb * (1.0 - sb)

    dya, dyb, dga, dgb = _mm("mix_dmerge", [dh2], [(0, p["w_out_t"], 0)],
                             [(ya,) + spec, (yb,) + spec, (z,) + _tile_spec(tm, tn, Z_GA // tn),
                              (z,) + _tile_spec(tm, tn, Z_GB // tn)],
                             [((t, d), BF16) + spec] * 4, dmerge_fn, tm=tm, tn=tn, n=d)
    dw_out = _mm_tn("mix_dwout", merged, dh2, alpha=1.0, out_dtype=BF16)
    wq = MLA_HEADS * LANE
    do_a, = _mm("mix_doa", [dya], [(0, p["w_pa_t"], 0)], [], [((t, wq), BF16) + _tile_spec(tm, wq)], lambda a, _: (a[0],),
                tm=tm, tn=wq, n=wq)
    wr = HG_HEADS * LANE
    do_b, = _mm("mix_dob", [dyb], [(0, p["w_pr_t"], 0)], [], [((t, wr), BF16) + _tile_spec(tm, wr)], lambda a, _: (a[0],),
                tm=tm, tn=wr, n=wr)
    dw_pa = _mm_tn("mix_dwpa", o_a, dya, alpha=1.0, out_dtype=F32)
    dw_pr = _mm_tn("mix_dwpr", o_b, dyb, alpha=1.0, out_dtype=BF16)
    (dq, dkv), pushed = _attn_bwd(q, kv, o_a, do_a, lse, push)
    dqq = _rope_bwd_q(dq, cq_t, sq_t)
    dcq, = _mm("mla_dcq", [dqq], [(0, p["w_uq_t"], 0)], [], [((t, Q_LORA), F32) + _tile_spec(tm, Q_LORA)],
               lambda a, _: (a[0],), tm=tm, tn=Q_LORA, n=Q_LORA)
    dckv, = _mm("mla_dckv", [dkv], [(0, p["w_ukv_t"], 0)], [], [((t, KV_LORA), F32) + _tile_spec(tm, KV_LORA)],
                lambda a, _: (a[0],), tm=tm, tn=KV_LORA, n=KV_LORA)
    dw_uq = _mm_tn("mla_dwuq", cqn, dqq, alpha=1.0, out_dtype=F32)
    dw_ukv = _mm_tn("mla_dwukv", ckvn, dkv, alpha=1.0, out_dtype=F32)
    dz_mla, dqn, dkvn = _mla_prep_bwd(z, p["q_norm"], p["kv_norm"], ck_t, sq_t, dcq, dckv, dkv)
    dhq, dhf, dhi, dhg, dlb, dhgn = _hgrn_bwd(z, lb, p["hg_norm"], call, call_t, st_saved, do_b)
    dz = jnp.concatenate([dz_mla, dhq, dhf, dhi, dhg, dga, dgb], axis=1)
    du, = _mm("mix_du", [dz], [(0, p["w_in_t"], 0)], [], [((t, d), F32) + spec], lambda a, _: (a[0],), tm=tm, tn=tn, n=d)
    dw_in = _mm_tn("mix_dwin", u, dz, alpha=1.0, out_dtype=F32)
    dh, dmn = _rms_bwd("mix_dnorm", h, p["mix_norm"], du, dh2)
    grads = dict(mix_norm=dmn, q_norm=dqn, kv_norm=dkvn, hg_norm=dhgn, lb=dlb, w_in=_w_in_grad(dw_in), w_uq=_w_uq_grad(dw_uq),
                 w_ukv=_w_ukv_grad(dw_ukv), w_proj_attn=_w_pa_grad(dw_pa), w_proj_rec=dw_pr, w_out=dw_out)
    return dh, grads, pushed


SHARDED = ("ffn1_w_gu", "ffn1_w_down", "w_in", "w_uq", "w_ukv", "w_proj_attn", "w_proj_rec", "w_out", "ffn2_w_gu", "ffn2_w_down")
ROW_SHARDED = ("ffn1_w_down", "w_out", "ffn2_w_down")
SMALL = ("ffn1_norm", "mix_norm", "q_norm", "kv_norm", "hg_lb_raw", "hg_norm", "ffn2_norm", "final_norm")
WEIGHTS = ("meta_tokens", "ffn1_norm", "ffn1_w_gu", "ffn1_w_down", "mix_norm", "w_in", "q_norm", "kv_norm", "w_uq", "w_ukv",
           "hg_lb_raw", "hg_norm", "w_proj_attn", "w_proj_rec", "w_out", "ffn2_norm", "ffn2_w_gu", "ffn2_w_down", "final_norm")


def _pack_small(vals):
    flat = jnp.concatenate([vals[n].reshape(-1) for n in SMALL])
    return flat.reshape(-1, LANE)


def _unpack_small(packed, like):
    flat = packed.reshape(-1)
    out, off = {}, 0
    for n in SMALL:
        size = math.prod(like[n].shape)
        out[n] = flat[off:off + size].reshape(like[n].shape)
        off += size
    return out


def kernel(x, meta_tokens, ffn1_norm, ffn1_w_gu, ffn1_w_down, mix_norm, w_in, q_norm, kv_norm, w_uq, w_ukv, hg_lb_raw, hg_norm, w_proj_attn, w_proj_rec, w_out, ffn2_norm, ffn2_w_gu, ffn2_w_down, final_norm, loss_target, m_meta_tokens, m_ffn1_norm, m_ffn1_w_gu, m_ffn1_w_down, m_mix_norm, m_w_in, m_q_norm, m_kv_norm, m_w_uq, m_w_ukv, m_hg_lb_raw, m_hg_norm, m_w_proj_attn, m_w_proj_rec, m_w_out, m_ffn2_norm, m_ffn2_w_gu, m_ffn2_w_down, m_final_norm, v_meta_tokens, v_ffn1_norm, v_ffn1_w_gu, v_ffn1_w_down, v_mix_norm, v_w_in, v_q_norm, v_kv_norm, v_w_uq, v_w_ukv, v_hg_lb_raw, v_hg_norm, v_w_proj_attn, v_w_proj_rec, v_w_out, v_ffn2_norm, v_ffn2_w_gu, v_ffn2_w_down, v_final_norm):
    w = dict(meta_tokens=meta_tokens, ffn1_norm=ffn1_norm, ffn1_w_gu=ffn1_w_gu, ffn1_w_down=ffn1_w_down, mix_norm=mix_norm,
             w_in=w_in, q_norm=q_norm, kv_norm=kv_norm, w_uq=w_uq, w_ukv=w_ukv, hg_lb_raw=hg_lb_raw, hg_norm=hg_norm,
             w_proj_attn=w_proj_attn, w_proj_rec=w_proj_rec, w_out=w_out, ffn2_norm=ffn2_norm, ffn2_w_gu=ffn2_w_gu,
             ffn2_w_down=ffn2_w_down, final_norm=final_norm)
    mom = dict(meta_tokens=m_meta_tokens, ffn1_norm=m_ffn1_norm, ffn1_w_gu=m_ffn1_w_gu, ffn1_w_down=m_ffn1_w_down,
               mix_norm=m_mix_norm, w_in=m_w_in, q_norm=m_q_norm, kv_norm=m_kv_norm, w_uq=m_w_uq, w_ukv=m_w_ukv,
               hg_lb_raw=m_hg_lb_raw, hg_norm=m_hg_norm, w_proj_attn=m_w_proj_attn, w_proj_rec=m_w_proj_rec, w_out=m_w_out,
               ffn2_norm=m_ffn2_norm, ffn2_w_gu=m_ffn2_w_gu, ffn2_w_down=m_ffn2_w_down, final_norm=m_final_norm)
    var = dict(meta_tokens=v_meta_tokens, ffn1_norm=v_ffn1_norm, ffn1_w_gu=v_ffn1_w_gu, ffn1_w_down=v_ffn1_w_down,
               mix_norm=v_mix_norm, w_in=v_w_in, q_norm=v_q_norm, kv_norm=v_kv_norm, w_uq=v_w_uq, w_ukv=v_w_ukv,
               hg_lb_raw=v_hg_lb_raw, hg_norm=v_hg_norm, w_proj_attn=v_w_proj_attn, w_proj_rec=v_w_proj_rec, w_out=v_w_out,
               ffn2_norm=v_ffn2_norm, ffn2_w_gu=v_ffn2_w_gu, ffn2_w_down=v_ffn2_w_down, final_norm=v_final_norm)
    nl = ffn1_norm.shape[0]
    seq, d = x.shape[1], x.shape[2]
    n_real = N_META + seq
    t = -(-n_real // ROW_ALIGN) * ROW_ALIGN
    me = 4 * lax.axis_index("x") + 2 * lax.axis_index("y") + lax.axis_index("c")

    def own_shards(l):
        return [w[n][l].astype(BF16) for n in SHARDED]

    gathered = _exchange("gather_weights", own_shards(0) + [meta_tokens], gather=True)
    meta_full = _cols_full(gathered[-1])

    def layer_params(l, full):
        def mat(n):
            g = full[n]
            return g.reshape(-1, g.shape[-1]) if n in ROW_SHARDED else _cols_full(g)

        p = {}
        for tag in ("ffn1", "ffn2"):
            p[tag + "_w_gu"] = mat(tag + "_w_gu")
            p[tag + "_w_down"] = mat(tag + "_w_down")
            p[tag + "_w_gu_t"] = p[tag + "_w_gu"].T
            p[tag + "_w_down_t"] = p[tag + "_w_down"].T
            p[tag + "_norm"] = w[tag + "_norm"][l:l + 1]
        p["w_in"] = _w_in_internal(mat("w_in"))
        p["w_uq"] = _w_uq_internal(mat("w_uq"))
        p["w_ukv"] = _w_ukv_internal(mat("w_ukv"))
        p["w_pa"] = _w_pa_internal(mat("w_proj_attn"))
        p["w_pr"] = mat("w_proj_rec")
        p["w_out"] = mat("w_out")
        for n in ("w_in", "w_uq", "w_ukv", "w_pa", "w_pr", "w_out"):
            p[n + "_t"] = p[n].T
        for n in ("mix_norm", "q_norm", "kv_norm", "hg_norm"):
            p[n] = w[n][l:l + 1]
        return p

    tabs = _rope_tables(t)
    call, call_t = _hg_tables()
    lbs = _lb_fwd(hg_lb_raw)

    pad = jnp.zeros((t - n_real, d), F32)
    h = jnp.concatenate([meta_full, x[0], pad], axis=0)
    tgt = jnp.concatenate([jnp.zeros((N_META, d), F32), loss_target[0], pad], axis=0)
    saved, params = [], []
    full = dict(zip(SHARDED, gathered[:-1]))
    for l in range(nl):
        p = layer_params(l, full)
        params.append(p)
        h, s1 = _ffn_fwd("ffn1", h, p["ffn1_norm"], p["ffn1_w_gu"], p["ffn1_w_down"])
        push = _push_plan(own_shards(l + 1), gather=True) if l + 1 < nl else None
        h, s2, pushed = _mix_fwd(h, p, tabs, lbs[l:l + 1], call, push)
        if pushed is not None:
            full = dict(zip(SHARDED, pushed))
        h, s3 = _ffn_fwd("ffn2", h, p["ffn2_norm"], p["ffn2_w_gu"], p["ffn2_w_down"])
        saved.append((s1, s2, s3))
    dh, d_final, loss_part = _loss_head(h, final_norm.reshape(1, d), tgt, n_real)
    loss = lax.psum(loss_part[0, 0], ("x", "y", "c"))

    def shards(gm):
        out = []
        for n in SHARDED:
            g = gm[n].astype(BF16)
            out.append(g.reshape(N_DEV, -1, g.shape[-1]) if n in ROW_SHARDED else _cols_shards(g))
        return out

    per_layer = []
    slots = [jnp.zeros((N_DEV,) + w[n].shape, BF16) for n in SHARDED]
    ready = None
    for l in reversed(range(nl)):
        p = params[l]
        s1, s2, s3 = saved[l]
        dh, dn2, dgu2, ddown2 = _ffn_bwd("ffn2", dh, s3, p["ffn2_norm"], p["ffn2_w_gu_t"], p["ffn2_w_down_t"])
        push = None if ready is None else _push_plan(ready, gather=False, bufs=slots, layer=l + 1)
        dh, gm, pushed = _mix_bwd(dh, s2, p, tabs, lbs[l:l + 1], call, call_t, push)
        if pushed is not None:
            slots = list(pushed)
        dh, dn1, dgu1, ddown1 = _ffn_bwd("ffn1", dh, s1, p["ffn1_norm"], p["ffn1_w_gu_t"], p["ffn1_w_down_t"])
        gm.update(ffn1_norm=dn1, ffn2_norm=dn2, ffn1_w_gu=dgu1, ffn2_w_gu=dgu2, ffn1_w_down=ddown1, ffn2_w_down=ddown2)
        per_layer.append(gm)
        ready = shards(gm)
    per_layer.reverse()
    grad_x = dh[N_META:n_real][None]

    slots = _exchange("scatter_grads", ready, gather=False, bufs=slots, layer=0)
    grads, delta, new_m, new_v = {}, {}, {}, {}
    for n, s in zip(SHARDED, slots):
        grads[n], delta[n], new_m[n], new_v[n] = _adam_sharded("adam_" + n, s, w[n], mom[n], var[n])

    small = {n: jnp.concatenate([gm[n] for gm in per_layer], axis=0) for n in SMALL if n not in ("hg_lb_raw", "final_norm")}
    small["hg_lb_raw"] = _lb_bwd(hg_lb_raw, jnp.concatenate([gm["lb"] for gm in per_layer], axis=0))
    small["final_norm"] = d_final
    packed = jnp.concatenate([_pack_small(small), dh[:N_META].reshape(-1, LANE)], axis=0)
    summed = _sum_replicated(_exchange("gather_small", [packed], gather=True)[0])
    n_small = packed.shape[0] - N_META * d // LANE
    sd, sm, sv = _adam_small("adam_small", summed[:n_small], _pack_small(w), _pack_small(mom), _pack_small(var))
    grads.update(_unpack_small(summed[:n_small], w))
    delta.update(_unpack_small(sd, w))
    new_m.update(_unpack_small(sm, w))
    new_v.update(_unpack_small(sv, w))
    dmeta = lax.dynamic_slice_in_dim(summed[n_small:].reshape(N_META, d), me * (d // N_DEV), d // N_DEV, axis=1)
    grads["meta_tokens"] = dmeta
    delta["meta_tokens"], new_m["meta_tokens"], new_v["meta_tokens"] = _adam_small(
        "adam_meta", dmeta, meta_tokens, m_meta_tokens, v_meta_tokens)

    return (loss, grad_x, *[grads[n] for n in WEIGHTS], *[delta[n] for n in WEIGHTS], *[new_m[n] for n in WEIGHTS],
            *[new_v[n] for n in WEIGHTS])
```

```python
import functools
import math

import jax
import jax.numpy as jnp
from jax import lax
from jax.experimental import pallas as pl
from jax.experimental.pallas import tpu as pltpu

F32 = jnp.float32
BF16 = jnp.bfloat16

N_DEV = 8
N_META = 16
MLA_HEADS = 8
Q_LORA = 384
KV_LORA = 256
QK_NOPE = 64
QK_ROPE = 32
V_HEAD = 64
ROPE_THETA = 10000.0
HG_HEADS = 4
HG_DIM = 128
EPS = 1e-6
NEG_BIG = -1e30
F_MIN = 1e-20
ADAM_LR = 0.001
ADAM_B1 = 0.9
ADAM_B2 = 0.999
ADAM_EPS = 1e-08
ADAM_WD = 0.01
ADAM_STEP = 10

LANE = 128
ROW_ALIGN = 256
HG_CHUNK = 128
HG_LEVELS = (64, 32, 16, 8, 4, 2, 1)
VMEM_LIMIT = 48 * 1024 * 1024

Z_CQ, Z_CKV, Z_KPE, Z_KPESW, Z_HQ, Z_HF, Z_HI, Z_HG, Z_GA, Z_GB, Z_W = 0, 384, 640, 768, 1024, 1536, 2048, 2560, 3072, 4096, 5120
ATTN_SCALE = float((QK_NOPE + QK_ROPE) ** -0.5)
LOG2E = 1.4426950408889634
ATTN_C2 = ATTN_SCALE * LOG2E
HG_SCALE = float(HG_DIM ** -0.5)


def _cparams(sem):
    return pltpu.CompilerParams(dimension_semantics=sem, vmem_limit_bytes=VMEM_LIMIT)


def _tile(n, cap):
    if n <= cap:
        return n
    best = None
    for t in range(LANE, cap + 1, LANE):
        if n % t == 0:
            best = t
    assert best is not None, (n, cap)
    return best


def _row_tile(m):
    for t in (384, 256, 128):
        if m % t == 0:
            return t
    raise ValueError(m)


def _bf(x):
    return x.astype(BF16)


def _dot(a, b):
    return jnp.dot(a, b, preferred_element_type=F32)


def _dot_nt(a, b):
    return lax.dot_general(a, b, (((1,), (1,)), ((), ())), preferred_element_type=F32)


def _dot_tn(a, b):
    return lax.dot_general(a, b, (((0,), (0,)), ((), ())), preferred_element_type=F32)


def _sigmoid(x):
    return 1.0 / (1.0 + jnp.exp(-x))


def _mm(name, a_list, pairs, extras, outs, fn, *, tm, tn, n):
    m = a_list[0].shape[0]
    na, nb, ne, no = len(a_list), len(pairs), len(extras), len(outs)

    def body(*refs):
        a_refs = refs[:na]
        b_refs = refs[na:na + nb]
        e_refs = refs[na + nb:na + nb + ne]
        o_refs = refs[na + nb + ne:]
        a_vals = [_bf(r[...]) for r in a_refs]
        accs = [_dot(a_vals[ai], b_refs[k][...]) for k, (ai, _, _) in enumerate(pairs)]
        res = fn(accs, [r[...] for r in e_refs])
        for r, v in zip(o_refs, res):
            r[...] = v.astype(r.dtype)

    def spec(block_shape, index_map):
        return pl.BlockSpec(block_shape, functools.partial(lambda j, i, im: im(i, j), im=index_map))

    in_specs = [spec((tm, a.shape[1]), lambda i, j: (i, 0)) for a in a_list]
    for _, b, off in pairs:
        in_specs.append(spec((b.shape[0], tn), functools.partial(lambda i, j, off: (0, j + off), off=off)))
    in_specs += [spec(bs, im) for _, bs, im in extras]
    return pl.pallas_call(
        body, name=name, grid=(n // tn, m // tm),
        in_specs=in_specs,
        out_specs=[spec(bs, im) for _, _, bs, im in outs],
        out_shape=[jax.ShapeDtypeStruct(s, d) for s, d, _, _ in outs],
        compiler_params=_cparams(("parallel", "parallel")),
    )(*a_list, *[b for _, b, _ in pairs], *[e for e, _, _ in extras])


def _tile_spec(tm, tn, col_off=0):
    return (tm, tn), functools.partial(lambda i, j, off: (i, j + off), off=col_off)


def _mm_tn(name, a, b, *, alpha, out_dtype):
    t, k = a.shape
    n = b.shape[1]
    tk, tn = _tile(k, 1408), _tile(n, 1408)
    tt = next(c for c in (768, 512, 256) if t % c == 0)
    nt = t // tt

    def body(a_ref, b_ref, o_ref, acc_ref):
        s = pl.program_id(2)

        @pl.when(s == 0)
        def _():
            acc_ref[...] = jnp.zeros_like(acc_ref)

        acc_ref[...] += _dot_tn(_bf(a_ref[...]), _bf(b_ref[...]))

        @pl.when(s == nt - 1)
        def _():
            o_ref[...] = (alpha * acc_ref[...]).astype(o_ref.dtype)

    return pl.pallas_call(
        body, name=name, grid=(k // tk, n // tn, nt),
        in_specs=[pl.BlockSpec((tt, tk), lambda i, j, s: (s, i)), pl.BlockSpec((tt, tn), lambda i, j, s: (s, j))],
        out_specs=pl.BlockSpec((tk, tn), lambda i, j, s: (i, j)),
        out_shape=jax.ShapeDtypeStruct((k, n), out_dtype),
        scratch_shapes=[pltpu.VMEM((tk, tn), F32)],
        compiler_params=_cparams(("parallel", "parallel", "arbitrary")),
    )(a, b)


def _rms_parts(x):
    r = lax.rsqrt(jnp.mean(x * x, axis=-1, keepdims=True) + EPS)
    return r, x * r


def _rms_bwd_math(x, w, dxn):
    r, xhat = _rms_parts(x)
    t = dxn * w
    dx = r * (t - xhat * jnp.mean(t * xhat, axis=-1, keepdims=True))
    dw = jnp.sum(dxn * xhat, axis=0, keepdims=True)
    return dx, dw


def _rms_fwd(name, h, w):
    t, d = h.shape
    tm = _row_tile(t)

    def body(h_ref, w_ref, o_ref):
        _, xhat = _rms_parts(h_ref[...])
        o_ref[...] = (xhat * w_ref[...]).astype(o_ref.dtype)

    return pl.pallas_call(
        body, name=name, grid=(t // tm,),
        in_specs=[pl.BlockSpec((tm, d), lambda i: (i, 0)), pl.BlockSpec((1, d), lambda i: (0, 0))],
        out_specs=pl.BlockSpec((tm, d), lambda i: (i, 0)),
        out_shape=jax.ShapeDtypeStruct((t, d), BF16),
        compiler_params=_cparams(("parallel",)),
    )(h, w)


def _rms_bwd(name, h, w, dxn, dh_in):
    t, d = h.shape
    tm = _row_tile(t)

    def body(h_ref, w_ref, dxn_ref, dh_ref, o_ref, dw_ref):
        dx, dw = _rms_bwd_math(h_ref[...], w_ref[...], dxn_ref[...])
        o_ref[...] = dh_ref[...] + dx

        @pl.when(pl.program_id(0) == 0)
        def _():
            dw_ref[...] = jnp.zeros_like(dw_ref)

        dw_ref[...] += dw

    row = pl.BlockSpec((tm, d), lambda i: (i, 0))
    vec = pl.BlockSpec((1, d), lambda i: (0, 0))
    return pl.pallas_call(
        body, name=name, grid=(t // tm,),
        in_specs=[row, vec, row, row],
        out_specs=[row, vec],
        out_shape=[jax.ShapeDtypeStruct((t, d), F32), jax.ShapeDtypeStruct((1, d), F32)],
        compiler_params=_cparams(("arbitrary",)),
    )(h, w, dxn, dh_in)


def _loss_head(h, w, tgt, n_real):
    t, d = h.shape
    tm = _row_tile(t)

    def body(h_ref, w_ref, t_ref, dh_ref, dw_ref, loss_ref):
        i = pl.program_id(0)
        x = h_ref[...]
        wv = w_ref[...]
        _, xhat = _rms_parts(x)
        rows = i * tm + lax.broadcasted_iota(jnp.int32, (tm, 1), 0)
        valid = (rows >= N_META) & (rows < n_real)
        e = jnp.where(valid, xhat * wv - t_ref[...], 0.0)
        dx, dw = _rms_bwd_math(x, wv, e * (1.0 / d))
        dh_ref[...] = dx

        @pl.when(i == 0)
        def _():
            dw_ref[...] = jnp.zeros_like(dw_ref)
            loss_ref[...] = jnp.zeros_like(loss_ref)

        dw_ref[...] += dw
        loss_ref[...] += (0.5 / d) * jnp.sum(jnp.sum(e * e, axis=-1, keepdims=True), axis=0, keepdims=True)

    row = pl.BlockSpec((tm, d), lambda i: (i, 0))
    vec = pl.BlockSpec((1, d), lambda i: (0, 0))
    return pl.pallas_call(
        body, name="loss_head", grid=(t // tm,),
        in_specs=[row, vec, row],
        out_specs=[row, vec, pl.BlockSpec((1, 1), lambda i: (0, 0))],
        out_shape=[jax.ShapeDtypeStruct((t, d), F32), jax.ShapeDtypeStruct((1, d), F32), jax.ShapeDtypeStruct((1, 1), F32)],
        compiler_params=_cparams(("arbitrary",)),
    )(h, w, tgt)


def _heads(x):
    return jnp.concatenate([x] * MLA_HEADS, axis=1)


def _mla_front(z, qn_w, kvn_w, tabs, w_uq, w_ukv):
    t = z.shape[0]
    tm = _row_tile(t)
    wq = MLA_HEADS * LANE

    def body(z_ref, qw_ref, kw_ref, cq_ref, ck_ref, sq_ref, wuq_ref, wukv_ref, cqn_ref, ckvn_ref, q_ref, kv_ref):
        zz = z_ref[...]
        _, qhat = _rms_parts(zz[:, Z_CQ:Z_CKV])
        _, khat = _rms_parts(zz[:, Z_CKV:Z_KPE])
        cqn = (qhat * qw_ref[...]).astype(BF16)
        ckvn = (khat * kw_ref[...]).astype(BF16)
        cqn_ref[...] = cqn
        ckvn_ref[...] = ckvn
        krot = zz[:, Z_KPE:Z_KPESW] * ck_ref[...] + zz[:, Z_KPESW:Z_KPESW + LANE] * sq_ref[...]
        qq = _dot(cqn, wuq_ref[...])
        q_ref[...] = (qq[:, :wq] * _heads(cq_ref[...]) + qq[:, wq:] * _heads(sq_ref[...])).astype(BF16)
        kv_ref[...] = (_dot(ckvn, wukv_ref[...]) + _kv_pattern(krot)).astype(BF16)

    def rows(wd):
        return pl.BlockSpec((tm, wd), lambda i: (i, 0))

    def whole(a):
        return pl.BlockSpec(a.shape, lambda i: (0, 0))

    return pl.pallas_call(
        body, name="mla_front", grid=(t // tm,),
        in_specs=[rows(Z_HQ), whole(qn_w), whole(kvn_w), rows(LANE), rows(LANE), rows(LANE), whole(w_uq), whole(w_ukv)],
        out_specs=[rows(Q_LORA), rows(KV_LORA), rows(wq), rows(2 * wq)],
        out_shape=[jax.ShapeDtypeStruct((t, Q_LORA), BF16), jax.ShapeDtypeStruct((t, KV_LORA), BF16),
                   jax.ShapeDtypeStruct((t, wq), BF16), jax.ShapeDtypeStruct((t, 2 * wq), BF16)],
        compiler_params=_cparams(("parallel",)),
    )(z, qn_w, kvn_w, *tabs, w_uq, w_ukv)


def _mla_back(z, qn_w, kvn_w, tabs, cqn, ckvn, dq, dkv, w_uq_t, w_ukv_t):
    t = z.shape[0]
    tm = next(c for c in (768, 512, 256) if t % c == 0)
    wq = MLA_HEADS * LANE

    def body(z_ref, qw_ref, kw_ref, cq_ref, ck_ref, sq_ref, cqn_ref, ckvn_ref, dq_ref, dkv_ref, wuqt_ref, wukvt_ref,
             dz_ref, dwuq_ref, dwukv_ref, dqw_ref, dkw_ref):
        @pl.when(pl.program_id(0) == 0)
        def _():
            for r in (dwuq_ref, dwukv_ref, dqw_ref, dkw_ref):
                r[...] = jnp.zeros_like(r)

        zz = z_ref[...]
        d = dq_ref[...] * ATTN_SCALE
        dqq = jnp.concatenate([d * _heads(cq_ref[...]), d * _heads(sq_ref[...])], axis=1).astype(BF16)
        dkv_v = dkv_ref[...]
        dwuq_ref[...] += _dot_tn(cqn_ref[...], dqq)
        dwukv_ref[...] += _dot_tn(ckvn_ref[...], dkv_v)
        dcq, dqw = _rms_bwd_math(zz[:, Z_CQ:Z_CKV], qw_ref[...], _dot(dqq, wuqt_ref[...]))
        dckv, dkw = _rms_bwd_math(zz[:, Z_CKV:Z_KPE], kw_ref[...], _dot(dkv_v, wukvt_ref[...]))
        dkr = jnp.zeros((tm, LANE), F32)
        for hd in range(MLA_HEADS):
            dkr = dkr + dkv_v[:, 2 * LANE * hd:2 * LANE * hd + LANE].astype(F32)
        dz_ref[...] = jnp.concatenate(
            [dcq, dckv, dkr * ck_ref[...], dkr * sq_ref[...], jnp.zeros((tm, Z_HQ - Z_KPESW - LANE), F32)], axis=1
        ).astype(BF16)
        dqw_ref[...] += dqw
        dkw_ref[...] += dkw

    def rows(wd):
        return pl.BlockSpec((tm, wd), lambda i: (i, 0))

    def whole(shape):
        return pl.BlockSpec(shape, lambda i: (0, 0))

    return pl.pallas_call(
        body, name="mla_back", grid=(t // tm,),
        in_specs=[rows(Z_HQ), whole(qn_w.shape), whole(kvn_w.shape), rows(LANE), rows(LANE), rows(LANE), rows(Q_LORA),
                  rows(KV_LORA), rows(wq), rows(2 * wq), whole(w_uq_t.shape), whole(w_ukv_t.shape)],
        out_specs=[rows(Z_HQ), whole((Q_LORA, 2 * wq)), whole((KV_LORA, 2 * wq)), whole((1, Q_LORA)), whole((1, KV_LORA))],
        out_shape=[jax.ShapeDtypeStruct((t, Z_HQ), BF16), jax.ShapeDtypeStruct((Q_LORA, 2 * wq), F32),
                   jax.ShapeDtypeStruct((KV_LORA, 2 * wq), F32), jax.ShapeDtypeStruct((1, Q_LORA), F32),
                   jax.ShapeDtypeStruct((1, KV_LORA), F32)],
        compiler_params=_cparams(("arbitrary",)),
    )(z, qn_w, kvn_w, *tabs, cqn, ckvn, dq, dkv, w_uq_t, w_ukv_t)


def _attn_block(t):
    for b in (768, 512, 256):
        if t % b == 0:
            return b
    raise ValueError(t)


def _call_carrying_push(body, push, *, name, grid, in_specs, out_specs, out_shape, scratch_shapes, args):
    if push is None:
        outs = pl.pallas_call(body, name=name, grid=grid, in_specs=in_specs, out_specs=out_specs, out_shape=out_shape,
                              scratch_shapes=scratch_shapes, compiler_params=_cparams(("parallel", "arbitrary")))(*args)
        return outs, None
    n_in, n_out, n_scr, n_pin, nk = len(in_specs), len(out_specs), len(scratch_shapes), len(push["ins"]), push["nk"]

    def carrying(*refs):
        o0 = n_in + n_pin
        s0 = o0 + n_out + nk
        pins, pouts, sems = refs[n_in:o0], refs[o0 + n_out:s0], refs[s0 + n_scr:]
        a, b = pl.program_id(0), pl.program_id(1)

        @pl.when((a == 0) & (b == 0))
        def _():
            for cp in _push_copies(push, pins, pouts, sems):
                cp.start()

        body(*refs[:n_in], *refs[o0:o0 + n_out], *refs[s0:s0 + n_scr])

        @pl.when((a == grid[0] - 1) & (b == grid[1] - 1))
        def _():
            for cp in _push_copies(push, pins, pouts, sems):
                cp.wait()

    outs = pl.pallas_call(
        carrying, name=name + "_push", grid=grid, in_specs=list(in_specs) + push["in_specs"],
        out_specs=list(out_specs) + push["out_specs"], out_shape=list(out_shape) + push["outs"],
        scratch_shapes=list(scratch_shapes) + push["sems"], input_output_aliases=_push_aliases(push, n_in, n_out),
        compiler_params=_cparams(("arbitrary", "arbitrary")))(*args, *push["ins"])
    return outs[:n_out], outs[n_out:]


def _attn_fwd(q, kv, push=None):
    t = q.shape[0]
    bq = bk = _attn_block(t)
    nq = t // bq
    v_t = jnp.transpose(kv.reshape(nq, bk, MLA_HEADS, 2, LANE)[:, :, :, 1, :], (2, 0, 3, 1))

    def body(q_ref, k_ref, vt_ref, o_ref, lse_ref):
        i = pl.program_id(1)
        qv = q_ref[...]
        qpos = i * bq + lax.broadcasted_iota(jnp.int32, (1, bq), 1)

        def block(j, carry, masked):
            m, l, acc = carry
            s = _dot_nt(k_ref[pl.ds(pl.multiple_of(j * bk, bk), bk), :], qv)
            if masked:
                kpos = j * bk + lax.broadcasted_iota(jnp.int32, (bk, 1), 0)
                s = jnp.where(kpos <= qpos, s, NEG_BIG)
            m_new = jnp.maximum(m, jnp.max(s, axis=0, keepdims=True))
            p = jnp.exp2((s - m_new) * ATTN_C2)
            a = jnp.exp2((m - m_new) * ATTN_C2)
            return m_new, a * l + jnp.sum(p, axis=0, keepdims=True), a * acc + _dot(vt_ref[0, j], _bf(p))

        init = (jnp.full((1, bq), NEG_BIG, F32), jnp.zeros((1, bq), F32), jnp.zeros((LANE, bq), F32))
        carry = lax.fori_loop(0, i, functools.partial(block, masked=False), init)
        m, l, acc = block(i, carry, True)
        o_ref[...] = (acc / l).T.astype(o_ref.dtype)
        lse_ref[0, 0] = m * ATTN_C2 + jnp.log(l) * LOG2E

    return _call_carrying_push(
        body, push, name="attn_fwd", grid=(MLA_HEADS, nq),
        in_specs=[pl.BlockSpec((bq, LANE), lambda h, i: (i, h)),
                  pl.BlockSpec((t, LANE), lambda h, i: (0, 2 * h)),
                  pl.BlockSpec((1, nq, LANE, bk), lambda h, i: (h, 0, 0, 0))],
        out_specs=[pl.BlockSpec((bq, LANE), lambda h, i: (i, h)),
                   pl.BlockSpec((1, 1, 1, bq), lambda h, i: (h, i, 0, 0))],
        out_shape=[jax.ShapeDtypeStruct((t, MLA_HEADS * LANE), BF16), jax.ShapeDtypeStruct((MLA_HEADS, nq, 1, bq), F32)],
        scratch_shapes=[], args=(q, kv, v_t))


def _attn_bwd(q, kv, o, do, lse, push=None):
    t = q.shape[0]
    bk = bw = _attn_block(t)
    nk, nw = t // bk, t // bw

    def body(q_ref, o_ref, do_ref, k_ref, v_ref, lse_ref, dq_ref, dkv_ref, dl_ref):
        j = pl.program_id(1)

        @pl.when(j == 0)
        def _():
            dq_ref[...] = jnp.zeros_like(dq_ref)
            for i in range(nw):
                rows = slice(i * bw, (i + 1) * bw)
                d = jnp.sum(o_ref[rows, :].astype(F32) * do_ref[rows, :].astype(F32), axis=1, keepdims=True)
                dl_ref[i] = jnp.broadcast_to(d, (bw, LANE)).T[0:1, :]

        kb = k_ref[...]
        vb = v_ref[...]
        kpos = j * bk + lax.broadcasted_iota(jnp.int32, (bk, 1), 0)

        def block(i, carry, masked):
            dk, dv = carry
            rows = pl.ds(pl.multiple_of(i * bw, bw), bw)
            qb = q_ref[rows, :]
            dob = do_ref[rows, :]
            pt = jnp.exp2(_dot_nt(kb, qb) * ATTN_C2 - lse_ref[0, i])
            if masked:
                qpos = i * bw + lax.broadcasted_iota(jnp.int32, (1, bw), 1)
                pt = jnp.where(kpos <= qpos, pt, 0.0)
            dv = dv + _dot(_bf(pt), dob)
            dst = _bf(pt * (_dot_nt(vb, dob) - dl_ref[i]))
            dk = dk + _dot(dst, qb)
            dq_ref[rows, :] += _dot_tn(dst, kb)
            return dk, dv

        i0 = j
        carry = block(i0, (jnp.zeros((bk, LANE), F32), jnp.zeros((bk, LANE), F32)), True)
        dk, dv = lax.fori_loop(i0 + 1, nw, functools.partial(block, masked=False), carry)
        dkv_ref[...] = jnp.concatenate([dk * ATTN_SCALE, dv], axis=1).astype(dkv_ref.dtype)

    head_rows = pl.BlockSpec((t, LANE), lambda h, j: (0, h))
    return _call_carrying_push(
        body, push, name="attn_bwd", grid=(MLA_HEADS, nk),
        in_specs=[head_rows, head_rows, head_rows,
                  pl.BlockSpec((bk, LANE), lambda h, j: (j, 2 * h)), pl.BlockSpec((bk, LANE), lambda h, j: (j, 2 * h + 1)),
                  pl.BlockSpec((1, nw, 1, bw), lambda h, j: (h, 0, 0, 0))],
        out_specs=[head_rows, pl.BlockSpec((bk, 2 * LANE), lambda h, j: (j, h))],
        out_shape=[jax.ShapeDtypeStruct((t, MLA_HEADS * LANE), F32), jax.ShapeDtypeStruct((t, 2 * MLA_HEADS * LANE), BF16)],
        scratch_shapes=[pltpu.VMEM((nw, 1, bw), F32)], args=(q, o, do, kv, kv, lse))


def _hg_tables():
    c = HG_CHUNK
    tri = (jnp.arange(c)[:, None] >= jnp.arange(c)[None, :]).astype(F32)
    mats = [tri]
    for m in HG_LEVELS:
        ref = (jnp.arange(c) // (2 * m)) * (2 * m) + m - 1
        mats.append((ref[:, None] >= jnp.arange(c)[None, :]).astype(F32))
    call = jnp.concatenate(mats, axis=0)
    return call.astype(BF16), call.T.astype(BF16)


def _table_dot(table, x):
    hi = _bf(x)
    rest = x - hi.astype(F32)
    mid = _bf(rest)
    lo = _bf(rest - mid.astype(F32))
    out = _dot(table, jnp.concatenate([hi, mid, lo], axis=1))
    n = x.shape[1]
    return out[:, 0:n] + out[:, n:2 * n] + out[:, 2 * n:3 * n]


def _hg_gates(hq, hf, lb):
    sg = _sigmoid(hf)
    sn = _sigmoid(-hf)
    f = lb + (1.0 - lb) * sg
    q = hq * _sigmoid(hq)
    g = jnp.log(jnp.maximum(f, F_MIN))
    k = (1.0 - lb) * sn
    return q, k, g, f, sg, sn


def _hg_level_masks(m):
    c = HG_CHUNK
    row = lax.broadcasted_iota(jnp.int32, (c, 1), 0)
    col = lax.broadcasted_iota(jnp.int32, (1, c), 1)
    shift = (2 * m).bit_length() - 1
    up = (row & m) != 0
    same = lax.shift_right_logical(row, shift) == lax.shift_right_logical(col, shift)
    return up, same


def _hg_level_factors(b, bref, up):
    lo = jnp.logical_not(up)
    eq = jnp.where(up, jnp.exp(jnp.where(up, b - bref, 0.0)), 0.0)
    ek = jnp.where(lo, jnp.exp(jnp.where(lo, bref - b, 0.0)), 0.0)
    return eq, ek


def _hg_intra(q, k, ball):
    c = HG_CHUNK
    b = ball[0:c]
    row = lax.broadcasted_iota(jnp.int32, (c, 1), 0)
    col = lax.broadcasted_iota(jnp.int32, (1, c), 1)
    a = jnp.where(row == col, _dot_nt(_bf(q), _bf(k)), 0.0)
    parts = []
    for lv, m in enumerate(HG_LEVELS):
        up, same = _hg_level_masks(m)
        eq, ek = _hg_level_factors(b, ball[(lv + 1) * c:(lv + 2) * c], up)
        qt, kt = q * eq, k * ek
        a = a + jnp.where(same, _dot_nt(_bf(qt), _bf(kt)), 0.0)
        parts.append((eq, ek, qt, kt))
    return a, parts


def _hg_chunk_fwd(hq, hf, hi, hg, lb, nw, st, call):
    q, k, g, _, _, _ = _hg_gates(hq, hf, lb)
    ball = _table_dot(call, g)
    b = ball[0:HG_CHUNK]
    a, _ = _hg_intra(q, k, ball)
    v16 = _bf(hi)
    o = _dot(_bf(a * HG_SCALE), v16) + _dot_nt(_bf(q * jnp.exp(b) * HG_SCALE), _bf(st))
    bl = b[HG_CHUNK - 1:HG_CHUNK]
    ke = k * jnp.exp(bl - b)
    st_new = st * jnp.exp(bl) + _dot(_bf(hi.T), _bf(ke))
    r = lax.rsqrt(jnp.mean(o * o, axis=-1, keepdims=True) + EPS)
    y = o * r * nw * (hg * _sigmoid(hg))
    return y, st_new


def _hg_chunk_bwd(hq, hf, hi, hg, lb, nw, st, call, call_t, dy, dst_new):
    c = HG_CHUNK
    q, k, g, f, sg, sn = _hg_gates(hq, hf, lb)
    ball = _table_dot(call, g)
    b = ball[0:c]
    a, parts = _hg_intra(q, k, ball)
    v16 = _bf(hi)
    st16 = _bf(st)
    eb = jnp.exp(b)
    qe = q * eb * HG_SCALE
    a16 = _bf(a * HG_SCALE)
    o = _dot(a16, v16) + _dot_nt(_bf(qe), st16)
    bl = b[c - 1:c]
    el = jnp.exp(bl)
    x = jnp.exp(bl - b)
    ke = k * x
    r = lax.rsqrt(jnp.mean(o * o, axis=-1, keepdims=True) + EPS)
    shg = _sigmoid(hg)
    gate = hg * shg
    ohat = o * r
    don = dy * gate
    dhg = dy * ohat * nw * (shg * (1.0 + hg * (1.0 - shg)))
    dnw = jnp.sum(don * ohat, axis=0, keepdims=True)
    tt = don * nw
    do = r * (tt - ohat * jnp.mean(tt * ohat, axis=-1, keepdims=True))
    do16 = _bf(do)
    dst16 = _bf(dst_new)
    da = _dot_nt(do16, v16) * HG_SCALE
    dv = _dot(_bf(a16.astype(F32).T), do16) + _dot_nt(_bf(ke), dst16)
    dqe = _dot(do16, st16)
    dke = _dot(v16, dst16)
    dst = dst_new * el + _dot(_bf(do.T), _bf(qe))
    dbl = jnp.sum(dst_new * st, axis=0, keepdims=True) * el
    dk = dke * x
    dxa = dke * ke
    db = dqe * qe - dxa
    dbl = dbl + jnp.sum(dxa, axis=0, keepdims=True)
    dq = dqe * eb * HG_SCALE
    row = lax.broadcasted_iota(jnp.int32, (c, 1), 0)
    col = lax.broadcasted_iota(jnp.int32, (1, c), 1)
    ddiag = jnp.sum(jnp.where(row == col, da, 0.0), axis=1, keepdims=True)
    dq = dq + ddiag * k
    dk = dk + ddiag * q
    dball = []
    for (eq, ek, qt, kt), m in zip(parts, HG_LEVELS):
        _, same = _hg_level_masks(m)
        gm = jnp.where(same, da, 0.0)
        dqt = _dot(_bf(gm), _bf(kt))
        dkt = _dot(_bf(gm.T), _bf(qt))
        dq = dq + dqt * eq
        dk = dk + dkt * ek
        darg = dqt * qt - dkt * kt
        db = db + darg
        dball.append(-darg)
    db = db + jnp.where(row == c - 1, dbl, 0.0)
    dg = _table_dot(call_t, jnp.concatenate([db] + dball, axis=0))
    shq = _sigmoid(hq)
    dhq = dq * (shq * (1.0 + hq * (1.0 - shq)))
    df = jnp.where(f > F_MIN, dg / jnp.maximum(f, F_MIN), 0.0)
    dlb = jnp.sum(df * (1.0 - sg) - dk * sn, axis=0, keepdims=True)
    dhf = df * (1.0 - lb) * sg * (1.0 - sg) - dk * (1.0 - lb) * sn * (1.0 - sn)
    return dhq, dhf, dv, dhg, dlb, dnw, dst


def _hg_chunks_per_step(t):
    return 6 if t % (6 * HG_CHUNK) == 0 else 2


def _hg_col(group, h):
    return group // LANE + h


def _hgrn_fwd(z, lb, nw, call):
    t = z.shape[0]
    c, cs = HG_CHUNK, _hg_chunks_per_step(t)
    rows = c * cs
    nsteps = t // rows

    def body(hq_ref, hf_ref, hi_ref, hg_ref, lb_ref, nw_ref, call_ref, y_ref, sv_ref, st_ref):
        @pl.when(pl.program_id(1) == 0)
        def _():
            st_ref[...] = jnp.zeros_like(st_ref)

        for u in range(cs):
            sl = slice(u * c, (u + 1) * c)
            st = st_ref[...]
            sv_ref[0, u] = st
            y, st_new = _hg_chunk_fwd(hq_ref[sl, :], hf_ref[sl, :], hi_ref[sl, :], hg_ref[sl, :], lb_ref[...], nw_ref[...],
                                      st, call_ref[...])
            y_ref[sl, :] = y.astype(y_ref.dtype)
            st_ref[...] = st_new

    def zcol(group):
        return pl.BlockSpec((rows, LANE), functools.partial(lambda h, i, g: (i, _hg_col(g, h)), g=group))

    ncall = call.shape[0]
    return pl.pallas_call(
        body, name="hgrn_fwd", grid=(HG_HEADS, nsteps),
        in_specs=[zcol(Z_HQ), zcol(Z_HF), zcol(Z_HI), zcol(Z_HG),
                  pl.BlockSpec((1, LANE), lambda h, i: (0, h)), pl.BlockSpec((1, LANE), lambda h, i: (0, 0)),
                  pl.BlockSpec((ncall, c), lambda h, i: (0, 0))],
        out_specs=[pl.BlockSpec((rows, LANE), lambda h, i: (i, h)),
                   pl.BlockSpec((1, cs, c, c), lambda h, i: (h, i, 0, 0))],
        out_shape=[jax.ShapeDtypeStruct((t, HG_HEADS * LANE), BF16), jax.ShapeDtypeStruct((HG_HEADS, t // c, c, c), F32)],
        scratch_shapes=[pltpu.VMEM((c, c), F32)],
        compiler_params=_cparams(("parallel", "arbitrary")),
    )(z, z, z, z, lb, nw, call)


def _hgrn_bwd(z, lb, nw, call, call_t, saved, dy):
    t = z.shape[0]
    c, cs = HG_CHUNK, _hg_chunks_per_step(t)
    rows = c * cs
    nsteps = t // rows

    def body(hq_ref, hf_ref, hi_ref, hg_ref, lb_ref, nw_ref, call_ref, callt_ref, sv_ref, dy_ref,
             dhq_ref, dhf_ref, dhi_ref, dhg_ref, dlb_ref, dnw_ref, dst_ref):
        h, i = pl.program_id(0), pl.program_id(1)

        @pl.when(i == 0)
        def _():
            dst_ref[...] = jnp.zeros_like(dst_ref)
            dlb_ref[...] = jnp.zeros_like(dlb_ref)

        @pl.when((i == 0) & (h == 0))
        def _():
            dnw_ref[...] = jnp.zeros_like(dnw_ref)

        for u in reversed(range(cs)):
            sl = slice(u * c, (u + 1) * c)
            dhq, dhf, dhi, dhg, dlb, dnw, dst = _hg_chunk_bwd(
                hq_ref[sl, :], hf_ref[sl, :], hi_ref[sl, :], hg_ref[sl, :], lb_ref[...], nw_ref[...], sv_ref[0, u],
                call_ref[...], callt_ref[...], dy_ref[sl, :].astype(F32), dst_ref[...])
            dhq_ref[sl, :] = dhq.astype(BF16)
            dhf_ref[sl, :] = dhf.astype(BF16)
            dhi_ref[sl, :] = dhi.astype(BF16)
            dhg_ref[sl, :] = dhg.astype(BF16)
            dlb_ref[...] += dlb
            dnw_ref[...] += dnw
            dst_ref[...] = dst

    def zcol(group):
        return pl.BlockSpec((rows, LANE), functools.partial(lambda h, i, g: (nsteps - 1 - i, _hg_col(g, h)), g=group))

    head_rows = pl.BlockSpec((rows, LANE), lambda h, i: (nsteps - 1 - i, h))
    ncall = call.shape[0]
    piece = jax.ShapeDtypeStruct((t, HG_HEADS * LANE), BF16)
    return pl.pallas_call(
        body, name="hgrn_bwd", grid=(HG_HEADS, nsteps),
        in_specs=[zcol(Z_HQ), zcol(Z_HF), zcol(Z_HI), zcol(Z_HG),
                  pl.BlockSpec((1, LANE), lambda h, i: (0, h)), pl.BlockSpec((1, LANE), lambda h, i: (0, 0)),
                  pl.BlockSpec((ncall, c), lambda h, i: (0, 0)), pl.BlockSpec((c, ncall), lambda h, i: (0, 0)),
                  pl.BlockSpec((1, cs, c, c), lambda h, i: (h, nsteps - 1 - i, 0, 0)), head_rows],
        out_specs=[head_rows, head_rows, head_rows, head_rows,
                   pl.BlockSpec((1, LANE), lambda h, i: (0, h)), pl.BlockSpec((1, LANE), lambda h, i: (0, 0))],
        out_shape=[piece, piece, piece, piece,
                   jax.ShapeDtypeStruct((1, HG_HEADS * LANE), F32), jax.ShapeDtypeStruct((1, LANE), F32)],
        scratch_shapes=[pltpu.VMEM((c, c), F32)],
        compiler_params=_cparams(("arbitrary", "arbitrary")),
    )(z, z, z, z, lb, nw, call, call_t, saved, dy)


def _lb_fwd(raw):
    nl = raw.shape[0]

    def body(r_ref, o_ref):
        x = r_ref[...]
        e = jnp.exp(x - jnp.max(x, axis=0, keepdims=True))
        p = e / jnp.sum(e, axis=0, keepdims=True)
        acc = jnp.zeros_like(p[0:1])
        for l in range(nl):
            if l > 0:
                acc = acc + p[l:l + 1]
            o_ref[l:l + 1, :] = acc

    return pl.pallas_call(body, name="lb_fwd", out_shape=jax.ShapeDtypeStruct(raw.shape, F32))(raw)


def _lb_bwd(raw, dlbs):
    nl = raw.shape[0]

    def body(r_ref, d_ref, o_ref):
        x = r_ref[...]
        e = jnp.exp(x - jnp.max(x, axis=0, keepdims=True))
        p = e / jnp.sum(e, axis=0, keepdims=True)
        d = d_ref[...]
        dps = [jnp.zeros_like(d[0:1])]
        for i in range(1, nl):
            acc = d[i:i + 1]
            for l in range(i + 1, nl):
                acc = acc + d[l:l + 1]
            dps.append(acc)
        dot = dps[0] * p[0:1]
        for i in range(1, nl):
            dot = dot + dps[i] * p[i:i + 1]
        for i in range(nl):
            o_ref[i:i + 1, :] = p[i:i + 1] * (dps[i] - dot)

    return pl.pallas_call(body, name="lb_bwd", out_shape=jax.ShapeDtypeStruct(raw.shape, F32))(raw, dlbs)


def _push_plan(srcs, gather, bufs=None, layer=None):
    nk = len(srcs)
    any_spec = pl.BlockSpec(memory_space=pl.ANY)
    if bufs is None:
        ins = list(srcs)
        outs = [jax.ShapeDtypeStruct(((N_DEV,) + s.shape) if gather else s.shape, s.dtype) for s in srcs]
    else:
        ins = list(srcs) + list(bufs)
        outs = [jax.ShapeDtypeStruct(b.shape, b.dtype) for b in bufs]
    sems = [pltpu.SemaphoreType.DMA((nk * N_DEV,)), pltpu.SemaphoreType.DMA((nk * N_DEV,)), pltpu.SemaphoreType.DMA((nk,))]
    return dict(nk=nk, gather=gather, layer=layer, ins=ins, in_specs=[any_spec] * len(ins), outs=outs,
                out_specs=[any_spec] * nk, sems=sems, alias_from=None if bufs is None else nk)


def _push_aliases(plan, first_in, first_out):
    if plan is None or plan["alias_from"] is None:
        return {}
    return {first_in + plan["alias_from"] + k: first_out + k for k in range(plan["nk"])}


def _push_copies(plan, in_refs, out_refs, sems):
    nk, gather, layer = plan["nk"], plan["gather"], plan["layer"]
    send_sems, recv_sems, local_sems = sems
    me = 4 * lax.axis_index("x") + 2 * lax.axis_index("y") + lax.axis_index("c")

    def landing(k):
        return out_refs[k].at[me] if layer is None else out_refs[k].at[me, layer]

    copies = [pltpu.make_async_copy(in_refs[k] if gather else in_refs[k].at[me], landing(k), local_sems.at[k])
              for k in range(nk)]
    for r in range(1, N_DEV):
        to = (me + r) % N_DEV
        for k in range(nk):
            copies.append(pltpu.make_async_remote_copy(
                src_ref=in_refs[k] if gather else in_refs[k].at[to], dst_ref=landing(k),
                send_sem=send_sems.at[k * N_DEV + r], recv_sem=recv_sems.at[k * N_DEV + r],
                device_id=(to // 4, (to // 2) % 2, to % 2), device_id_type=pl.DeviceIdType.MESH))
    return copies


def _exchange(name, srcs, gather, bufs=None, layer=None):
    plan = _push_plan(srcs, gather, bufs, layer)
    nin, nk = len(plan["ins"]), plan["nk"]

    def body(*refs):
        copies = _push_copies(plan, refs[:nin], refs[nin:nin + nk], refs[nin + nk:])
        for cp in copies:
            cp.start()
        for cp in copies:
            cp.wait()

    return pl.pallas_call(
        body, name=name, in_specs=plan["in_specs"], out_specs=plan["out_specs"], out_shape=plan["outs"],
        scratch_shapes=plan["sems"], input_output_aliases=_push_aliases(plan, 0, 0),
    )(*plan["ins"])


def _adam_math(g, w, m, v):
    m2 = ADAM_B1 * m + (1.0 - ADAM_B1) * g
    v2 = ADAM_B2 * v + (1.0 - ADAM_B2) * (g * g)
    m_hat = m2 / (1.0 - ADAM_B1 ** ADAM_STEP)
    v_hat = v2 / (1.0 - ADAM_B2 ** ADAM_STEP)
    return -ADAM_LR * (m_hat / (jnp.sqrt(v_hat) + ADAM_EPS) + ADAM_WD * w), m2, v2


def _sum_slots(ref):
    g = ref[0].astype(F32)
    for s in range(1, N_DEV):
        g = g + ref[s].astype(F32)
    return g


def _adam_sharded(name, slots, w, m, v):
    nl, a, b = w.shape
    ta = a
    for cand in range(8, 257, 8):
        if a % cand == 0:
            ta = cand

    def body(s_ref, w_ref, m_ref, v_ref, g_ref, d_ref, m2_ref, v2_ref):
        g = _sum_slots(s_ref)
        d, m2, v2 = _adam_math(g, w_ref[...], m_ref[...], v_ref[...])
        g_ref[...] = g
        d_ref[...] = d
        m2_ref[...] = m2
        v2_ref[...] = v2

    blk = pl.BlockSpec((1, ta, b), lambda l, i: (l, i, 0))
    sds = jax.ShapeDtypeStruct(w.shape, F32)
    return pl.pallas_call(
        body, name=name, grid=(nl, a // ta),
        in_specs=[pl.BlockSpec((N_DEV, 1, ta, b), lambda l, i: (0, l, i, 0)), blk, blk, blk],
        out_specs=[blk] * 4, out_shape=[sds] * 4,
        compiler_params=_cparams(("parallel", "parallel")),
    )(slots, w, m, v)


def _sum_replicated(slots):
    def body(s_ref, g_ref):
        g_ref[...] = _sum_slots(s_ref)

    return pl.pallas_call(body, name="sum_small", out_shape=jax.ShapeDtypeStruct(slots.shape[1:], F32))(slots)


def _adam_small(name, g, w, m, v):
    def body(g_ref, w_ref, m_ref, v_ref, d_ref, m2_ref, v2_ref):
        d, m2, v2 = _adam_math(g_ref[...], w_ref[...], m_ref[...], v_ref[...])
        d_ref[...] = d
        m2_ref[...] = m2
        v2_ref[...] = v2

    sds = jax.ShapeDtypeStruct(w.shape, F32)
    return pl.pallas_call(body, name=name, out_shape=[sds] * 3)(g, w, m, v)


def _cols_full(g):
    return jnp.transpose(g, (1, 0, 2)).reshape(g.shape[1], -1)


def _cols_shards(w):
    k = w.shape[0]
    return jnp.transpose(w.reshape(k, N_DEV, -1), (1, 0, 2))


def _swap_halves(x):
    half = x.shape[-1] // 2
    return jnp.concatenate([x[..., half:], x[..., :half]], axis=-1)


def _zeros_like_cols(x, n):
    return jnp.zeros(x.shape[:-1] + (n,), x.dtype)


def _w_in_internal(w):
    d = w.shape[0]
    kpe = w[:, 640:672]
    z64, z32 = jnp.zeros((d, 64), w.dtype), jnp.zeros((d, 32), w.dtype)
    return jnp.concatenate(
        [w[:, 0:640], z64, kpe, z32, z64, _swap_halves(kpe), z32, jnp.zeros((d, Z_HQ - Z_KPESW - LANE), w.dtype),
         w[:, 672:2720], w[:, 2720:4768]], axis=1)


def _w_in_grad(g):
    kpe = g[:, Z_KPE + 64:Z_KPE + 96] + _swap_halves(g[:, Z_KPESW + 64:Z_KPESW + 96])
    return jnp.concatenate([g[:, 0:640], kpe, g[:, Z_HQ:Z_W]], axis=1)


def _w_uq_internal(w):
    k = w.shape[0]
    w3 = w.reshape(k, MLA_HEADS, QK_NOPE + QK_ROPE)
    nope, rope = w3[..., :QK_NOPE], w3[..., QK_NOPE:]
    plain = jnp.concatenate([nope, rope, _zeros_like_cols(rope, 32)], axis=-1).reshape(k, -1)
    swapped = jnp.concatenate([_zeros_like_cols(nope, 64), _swap_halves(rope), _zeros_like_cols(rope, 32)], axis=-1).reshape(k, -1)
    return jnp.concatenate([plain, swapped], axis=1)


def _w_uq_grad(g):
    k = g.shape[0]
    half = MLA_HEADS * LANE
    g1, g2 = g[:, :half].reshape(k, MLA_HEADS, LANE), g[:, half:].reshape(k, MLA_HEADS, LANE)
    rope = g1[..., 64:96] + _swap_halves(g2[..., 64:96])
    return jnp.concatenate([g1[..., :64], rope], axis=-1).reshape(k, -1)


def _w_ukv_internal(w):
    k = w.shape[0]
    w3 = w.reshape(k, MLA_HEADS, QK_NOPE + V_HEAD)
    kn, vv = w3[..., :QK_NOPE], w3[..., QK_NOPE:]
    z = _zeros_like_cols(kn, 64)
    return jnp.concatenate([kn, z, vv, z], axis=-1).reshape(k, -1)


def _w_ukv_grad(g):
    k = g.shape[0]
    g3 = g.reshape(k, MLA_HEADS, 2 * LANE)
    return jnp.concatenate([g3[..., 0:64], g3[..., LANE:LANE + 64]], axis=-1).reshape(k, -1)


def _w_pa_internal(w):
    n = w.shape[1]
    w3 = w.reshape(MLA_HEADS, V_HEAD, n)
    return jnp.concatenate([w3, jnp.zeros_like(w3)], axis=1).reshape(-1, n)


def _w_pa_grad(g):
    n = g.shape[1]
    return g.reshape(MLA_HEADS, 2 * V_HEAD, n)[:, :V_HEAD].reshape(-1, n)


def _rope_tables(t):
    half = QK_ROPE // 2
    inv = ROPE_THETA ** (-jnp.arange(half, dtype=F32) / half)
    ang = jnp.arange(t, dtype=F32)[:, None] * inv[None, :]
    cos, sin = jnp.cos(ang), jnp.sin(ang)
    one, zero = jnp.ones((t, 64), F32), jnp.zeros((t, 64), F32)
    z32 = jnp.zeros((t, 32), F32)
    cq = jnp.concatenate([one, cos, cos, z32], axis=1)
    ck = jnp.concatenate([zero, cos, cos, z32], axis=1)
    sq = jnp.concatenate([zero, -sin, sin, z32], axis=1)
    return cq, ck, sq


def _ffn_fwd(tag, h, nw, w_gu, w_down):
    t, d = h.shape
    dff = w_down.shape[0]
    tm, tn = _row_tile(t), _tile(dff, 1408)
    xn = _rms_fwd(tag + "_norm", h, nw)

    def act_fn(accs, _):
        g, u = accs
        return g, u, g * _sigmoid(g) * u

    spec = _tile_spec(tm, tn)
    g, u, act = _mm(tag + "_gu", [xn], [(0, w_gu, 0), (0, w_gu, dff // tn)], [],
                    [((t, dff), BF16) + spec] * 3, act_fn, tm=tm, tn=tn, n=dff)
    tn2 = _tile(d, 1024)
    h2, = _mm(tag + "_down", [act], [(0, w_down, 0)], [(h,) + _tile_spec(tm, tn2)],
              [((t, d), F32) + _tile_spec(tm, tn2)], lambda accs, ex: (ex[0] + 0.5 * accs[0],), tm=tm, tn=tn2, n=d)
    return h2, (h, xn, g, u, act)


def _ffn_bwd(tag, dh2, saved, nw, w_gu_t, w_down_t):
    h, xn, g, u, act = saved
    t, d = h.shape
    dff = act.shape[1]
    tm, tn = _row_tile(t), _tile(dff, 1408)

    def dact_fn(accs, ex):
        gg, uu = ex[0].astype(F32), ex[1].astype(F32)
        da = 0.5 * accs[0]
        sg = _sigmoid(gg)
        return da * uu * (sg * (1.0 + gg * (1.0 - sg))), da * (gg * sg)

    spec = _tile_spec(tm, tn)
    dg, du = _mm(tag + "_dact", [dh2], [(0, w_down_t, 0)], [(g,) + spec, (u,) + spec],
                 [((t, dff), BF16) + spec] * 2, dact_fn, tm=tm, tn=tn, n=dff)
    dw_down = _mm_tn(tag + "_dwdown", act, dh2, alpha=0.5, out_dtype=BF16)
    tn2 = _tile(d, 512)
    dxn, = _mm(tag + "_dxn", [dg, du], [(0, w_gu_t[:dff], 0), (1, w_gu_t[dff:], 0)], [],
               [((t, d), F32) + _tile_spec(tm, tn2)], lambda accs, _: (accs[0] + accs[1],), tm=tm, tn=tn2, n=d)
    dw_gu = jnp.concatenate([_mm_tn(tag + "_dwg", xn, dg, alpha=1.0, out_dtype=BF16),
                             _mm_tn(tag + "_dwu", xn, du, alpha=1.0, out_dtype=BF16)], axis=1)
    dh, dnw = _rms_bwd(tag + "_dnorm", h, nw, dxn, dh2)
    return dh, dnw, dw_gu, dw_down


def _kv_pattern(kr):
    z = jnp.zeros_like(kr)
    return jnp.concatenate([kr, z] * MLA_HEADS, axis=1)


def _mix_fwd(h, p, tabs, lb, call, push):
    t, d = h.shape
    tm = _row_tile(t)
    u = _rms_fwd("mix_norm", h, p["mix_norm"])
    tnz = _tile(Z_W, 1024)
    z, = _mm("mix_in", [u], [(0, p["w_in"], 0)], [], [((t, Z_W), F32) + _tile_spec(tm, tnz)], lambda a, _: (a[0],),
             tm=tm, tn=tnz, n=Z_W)
    cqn, ckvn, q, kv = _mla_front(z, p["q_norm"], p["kv_norm"], tabs, p["w_uq"], p["w_ukv"])
    (o_a, lse), pushed = _attn_fwd(q, kv, push)
    o_b, st_saved = _hgrn_fwd(z, lb, p["hg_norm"], call)
    tn = _tile(d, 512)

    def merge_fn(accs, ex):
        ya, yb = accs
        return ya, yb, _sigmoid(ex[0]) * ya + _sigmoid(ex[1]) * yb

    spec = _tile_spec(tm, tn)
    ya, yb, merged = _mm("mix_merge", [o_a, o_b], [(0, p["w_pa"], 0), (1, p["w_pr"], 0)],
                         [(z,) + _tile_spec(tm, tn, Z_GA // tn), (z,) + _tile_spec(tm, tn, Z_GB // tn)],
                         [((t, d), BF16) + spec] * 3, merge_fn, tm=tm, tn=tn, n=d)
    tn2 = _tile(d, 1024)
    h2, = _mm("mix_out", [merged], [(0, p["w_out"], 0)], [(h,) + _tile_spec(tm, tn2)],
              [((t, d), F32) + _tile_spec(tm, tn2)], lambda a, ex: (ex[0] + a[0],), tm=tm, tn=tn2, n=d)
    return h2, (h, u, z, cqn, ckvn, q, kv, o_a, lse, o_b, st_saved, ya, yb, merged), pushed


def _mix_bwd(dh2, saved, p, tabs, lb, call, call_t, push):
    h, u, z, cqn, ckvn, q, kv, o_a, lse, o_b, st_saved, ya, yb, merged = saved
    t, d = h.shape
    tm = _row_tile(t)
    tn = _tile(d, 512)
    spec = _tile_spec(tm, tn)

    def dmerge_fn(accs, ex):
        dm = accs[0]
        yav, ybv = ex[0].astype(F32), ex[1].astype(F32)
        sa, sb = _sigmoid(ex[2]), _sigmoid(ex[3])
        return dm * sa, dm * sb, dm * yav * sa * (1.0 - sa), dm * ybv * sb * (1.0 - sb)

    dya, dyb, dga, dgb = _mm("mix_dmerge", [dh2], [(0, p["w_out_t"], 0)],
                             [(ya,) + spec, (yb,) + spec, (z,) + _tile_spec(tm, tn, Z_GA // tn),
                              (z,) + _tile_spec(tm, tn, Z_GB // tn)],
                             [((t, d), BF16) + spec] * 4, dmerge_fn, tm=tm, tn=tn, n=d)
    dw_out = _mm_tn("mix_dwout", merged, dh2, alpha=1.0, out_dtype=BF16)
    wq = MLA_HEADS * LANE
    do_a, = _mm("mix_doa", [dya], [(0, p["w_pa_t"], 0)], [], [((t, wq), BF16) + _tile_spec(tm, wq)], lambda a, _: (a[0],),
                tm=tm, tn=wq, n=wq)
    wr = HG_HEADS * LANE
    do_b, = _mm("mix_dob", [dyb], [(0, p["w_pr_t"], 0)], [], [((t, wr), BF16) + _tile_spec(tm, wr)], lambda a, _: (a[0],),
                tm=tm, tn=wr, n=wr)
    dw_pa = _mm_tn("mix_dwpa", o_a, dya, alpha=1.0, out_dtype=F32)
    dw_pr = _mm_tn("mix_dwpr", o_b, dyb, alpha=1.0, out_dtype=BF16)
    (dq, dkv), pushed = _attn_bwd(q, kv, o_a, do_a, lse, push)
    dz_mla, dw_uq, dw_ukv, dqn, dkvn = _mla_back(z, p["q_norm"], p["kv_norm"], tabs, cqn, ckvn, dq, dkv,
                                                 p["w_uq_t"], p["w_ukv_t"])
    dhq, dhf, dhi, dhg, dlb, dhgn = _hgrn_bwd(z, lb, p["hg_norm"], call, call_t, st_saved, do_b)
    dz = jnp.concatenate([dz_mla, dhq, dhf, dhi, dhg, dga, dgb], axis=1)
    du, = _mm("mix_du", [dz], [(0, p["w_in_t"], 0)], [], [((t, d), F32) + spec], lambda a, _: (a[0],), tm=tm, tn=tn, n=d)
    dw_in = _mm_tn("mix_dwin", u, dz, alpha=1.0, out_dtype=F32)
    dh, dmn = _rms_bwd("mix_dnorm", h, p["mix_norm"], du, dh2)
    grads = dict(mix_norm=dmn, q_norm=dqn, kv_norm=dkvn, hg_norm=dhgn, lb=dlb, w_in=_w_in_grad(dw_in), w_uq=_w_uq_grad(dw_uq),
                 w_ukv=_w_ukv_grad(dw_ukv), w_proj_attn=_w_pa_grad(dw_pa), w_proj_rec=dw_pr, w_out=dw_out)
    return dh, grads, pushed


SHARDED = ("ffn1_w_gu", "ffn1_w_down", "w_in", "w_uq", "w_ukv", "w_proj_attn", "w_proj_rec", "w_out", "ffn2_w_gu", "ffn2_w_down")
ROW_SHARDED = ("ffn1_w_down", "w_out", "ffn2_w_down")
SMALL = ("ffn1_norm", "mix_norm", "q_norm", "kv_norm", "hg_lb_raw", "hg_norm", "ffn2_norm", "final_norm")
WEIGHTS = ("meta_tokens", "ffn1_norm", "ffn1_w_gu", "ffn1_w_down", "mix_norm", "w_in", "q_norm", "kv_norm", "w_uq", "w_ukv",
           "hg_lb_raw", "hg_norm", "w_proj_attn", "w_proj_rec", "w_out", "ffn2_norm", "ffn2_w_gu", "ffn2_w_down", "final_norm")


def _pack_small(vals):
    flat = jnp.concatenate([vals[n].reshape(-1) for n in SMALL])
    return flat.reshape(-1, LANE)


def _unpack_small(packed, like):
    flat = packed.reshape(-1)
    out, off = {}, 0
    for n in SMALL:
        size = math.prod(like[n].shape)
        out[n] = flat[off:off + size].reshape(like[n].shape)
        off += size
    return out


def kernel(x, meta_tokens, ffn1_norm, ffn1_w_gu, ffn1_w_down, mix_norm, w_in, q_norm, kv_norm, w_uq, w_ukv, hg_lb_raw, hg_norm, w_proj_attn, w_proj_rec, w_out, ffn2_norm, ffn2_w_gu, ffn2_w_down, final_norm, loss_target, m_meta_tokens, m_ffn1_norm, m_ffn1_w_gu, m_ffn1_w_down, m_mix_norm, m_w_in, m_q_norm, m_kv_norm, m_w_uq, m_w_ukv, m_hg_lb_raw, m_hg_norm, m_w_proj_attn, m_w_proj_rec, m_w_out, m_ffn2_norm, m_ffn2_w_gu, m_ffn2_w_down, m_final_norm, v_meta_tokens, v_ffn1_norm, v_ffn1_w_gu, v_ffn1_w_down, v_mix_norm, v_w_in, v_q_norm, v_kv_norm, v_w_uq, v_w_ukv, v_hg_lb_raw, v_hg_norm, v_w_proj_attn, v_w_proj_rec, v_w_out, v_ffn2_norm, v_ffn2_w_gu, v_ffn2_w_down, v_final_norm):
    w = dict(meta_tokens=meta_tokens, ffn1_norm=ffn1_norm, ffn1_w_gu=ffn1_w_gu, ffn1_w_down=ffn1_w_down, mix_norm=mix_norm,
             w_in=w_in, q_norm=q_norm, kv_norm=kv_norm, w_uq=w_uq, w_ukv=w_ukv, hg_lb_raw=hg_lb_raw, hg_norm=hg_norm,
             w_proj_attn=w_proj_attn, w_proj_rec=w_proj_rec, w_out=w_out, ffn2_norm=ffn2_norm, ffn2_w_gu=ffn2_w_gu,
             ffn2_w_down=ffn2_w_down, final_norm=final_norm)
    mom = dict(meta_tokens=m_meta_tokens, ffn1_norm=m_ffn1_norm, ffn1_w_gu=m_ffn1_w_gu, ffn1_w_down=m_ffn1_w_down,
               mix_norm=m_mix_norm, w_in=m_w_in, q_norm=m_q_norm, kv_norm=m_kv_norm, w_uq=m_w_uq, w_ukv=m_w_ukv,
               hg_lb_raw=m_hg_lb_raw, hg_norm=m_hg_norm, w_proj_attn=m_w_proj_attn, w_proj_rec=m_w_proj_rec, w_out=m_w_out,
               ffn2_norm=m_ffn2_norm, ffn2_w_gu=m_ffn2_w_gu, ffn2_w_down=m_ffn2_w_down, final_norm=m_final_norm)
    var = dict(meta_tokens=v_meta_tokens, ffn1_norm=v_ffn1_norm, ffn1_w_gu=v_ffn1_w_gu, ffn1_w_down=v_ffn1_w_down,
               mix_norm=v_mix_norm, w_in=v_w_in, q_norm=v_q_norm, kv_norm=v_kv_norm, w_uq=v_w_uq, w_ukv=v_w_ukv,
               hg_lb_raw=v_hg_lb_raw, hg_norm=v_hg_norm, w_proj_attn=v_w_proj_attn, w_proj_rec=v_w_proj_rec, w_out=v_w_out,
               ffn2_norm=v_ffn2_norm, ffn2_w_gu=v_ffn2_w_gu, ffn2_w_down=v_ffn2_w_down, final_norm=v_final_norm)
    nl = ffn1_norm.shape[0]
    seq, d = x.shape[1], x.shape[2]
    n_real = N_META + seq
    t = -(-n_real // ROW_ALIGN) * ROW_ALIGN
    me = 4 * lax.axis_index("x") + 2 * lax.axis_index("y") + lax.axis_index("c")

    def own_shards(l):
        return [w[n][l].astype(BF16) for n in SHARDED]

    gathered = _exchange("gather_weights", own_shards(0) + [meta_tokens], gather=True)
    meta_full = _cols_full(gathered[-1])

    def layer_params(l, full):
        def mat(n):
            g = full[n]
            return g.reshape(-1, g.shape[-1]) if n in ROW_SHARDED else _cols_full(g)

        p = {}
        for tag in ("ffn1", "ffn2"):
            p[tag + "_w_gu"] = mat(tag + "_w_gu")
            p[tag + "_w_down"] = mat(tag + "_w_down")
            p[tag + "_w_gu_t"] = p[tag + "_w_gu"].T
            p[tag + "_w_down_t"] = p[tag + "_w_down"].T
            p[tag + "_norm"] = w[tag + "_norm"][l:l + 1]
        p["w_in"] = _w_in_internal(mat("w_in"))
        p["w_uq"] = _w_uq_internal(mat("w_uq"))
        p["w_ukv"] = _w_ukv_internal(mat("w_ukv"))
        p["w_pa"] = _w_pa_internal(mat("w_proj_attn"))
        p["w_pr"] = mat("w_proj_rec")
        p["w_out"] = mat("w_out")
        for n in ("w_in", "w_uq", "w_ukv", "w_pa", "w_pr", "w_out"):
            p[n + "_t"] = p[n].T
        for n in ("mix_norm", "q_norm", "kv_norm", "hg_norm"):
            p[n] = w[n][l:l + 1]
        return p

    tabs = _rope_tables(t)
    call, call_t = _hg_tables()
    lbs = _lb_fwd(hg_lb_raw)

    pad = jnp.zeros((t - n_real, d), F32)
    h = jnp.concatenate([meta_full, x[0], pad], axis=0)
    tgt = jnp.concatenate([jnp.zeros((N_META, d), F32), loss_target[0], pad], axis=0)
    saved, params = [], []
    full = dict(zip(SHARDED, gathered[:-1]))
    for l in range(nl):
        p = layer_params(l, full)
        params.append(p)
        h, s1 = _ffn_fwd("ffn1", h, p["ffn1_norm"], p["ffn1_w_gu"], p["ffn1_w_down"])
        push = _push_plan(own_shards(l + 1), gather=True) if l + 1 < nl else None
        h, s2, pushed = _mix_fwd(h, p, tabs, lbs[l:l + 1], call, push)
        if pushed is not None:
            full = dict(zip(SHARDED, pushed))
        h, s3 = _ffn_fwd("ffn2", h, p["ffn2_norm"], p["ffn2_w_gu"], p["ffn2_w_down"])
        saved.append((s1, s2, s3))
    dh, d_final, loss_part = _loss_head(h, final_norm.reshape(1, d), tgt, n_real)
    loss = lax.psum(loss_part[0, 0], ("x", "y", "c"))

    def shards(gm):
        out = []
        for n in SHARDED:
            g = gm[n].astype(BF16)
            out.append(g.reshape(N_DEV, -1, g.shape[-1]) if n in ROW_SHARDED else _cols_shards(g))
        return out

    per_layer = []
    slots = [jnp.zeros((N_DEV,) + w[n].shape, BF16) for n in SHARDED]
    ready = None
    for l in reversed(range(nl)):
        p = params[l]
        s1, s2, s3 = saved[l]
        dh, dn2, dgu2, ddown2 = _ffn_bwd("ffn2", dh, s3, p["ffn2_norm"], p["ffn2_w_gu_t"], p["ffn2_w_down_t"])
        push = None if ready is None else _push_plan(ready, gather=False, bufs=slots, layer=l + 1)
        dh, gm, pushed = _mix_bwd(dh, s2, p, tabs, lbs[l:l + 1], call, call_t, push)
        if pushed is not None:
            slots = list(pushed)
        dh, dn1, dgu1, ddown1 = _ffn_bwd("ffn1", dh, s1, p["ffn1_norm"], p["ffn1_w_gu_t"], p["ffn1_w_down_t"])
        gm.update(ffn1_norm=dn1, ffn2_norm=dn2, ffn1_w_gu=dgu1, ffn2_w_gu=dgu2, ffn1_w_down=ddown1, ffn2_w_down=ddown2)
        per_layer.append(gm)
        ready = shards(gm)
    per_layer.reverse()
    grad_x = dh[N_META:n_real][None]

    slots = _exchange("scatter_grads", ready, gather=False, bufs=slots, layer=0)
    grads, delta, new_m, new_v = {}, {}, {}, {}
    for n, s in zip(SHARDED, slots):
        grads[n], delta[n], new_m[n], new_v[n] = _adam_sharded("adam_" + n, s, w[n], mom[n], var[n])

    small = {n: jnp.concatenate([gm[n] for gm in per_layer], axis=0) for n in SMALL if n not in ("hg_lb_raw", "final_norm")}
    small["hg_lb_raw"] = _lb_bwd(hg_lb_raw, jnp.concatenate([gm["lb"] for gm in per_layer], axis=0))
    small["final_norm"] = d_final
    packed = jnp.concatenate([_pack_small(small), dh[:N_META].reshape(-1, LANE)], axis=0)
    summed = _sum_replicated(_exchange("gather_small", [packed], gather=True)[0])
    n_small = packed.shape[0] - N_META * d // LANE
    sd, sm, sv = _adam_small("adam_small", summed[:n_small], _pack_small(w), _pack_small(mom), _pack_small(var))
    grads.update(_unpack_small(summed[:n_small], w))
    delta.update(_unpack_small(sd, w))
    new_m.update(_unpack_small(sm, w))
    new_v.update(_unpack_small(sv, w))
    dmeta = lax.dynamic_slice_in_dim(summed[n_small:].reshape(N_META, d), me * (d // N_DEV), d // N_DEV, axis=1)
    grads["meta_tokens"] = dmeta
    delta["meta_tokens"], new_m["meta_tokens"], new_v["meta_tokens"] = _adam_small(
        "adam_meta", dmeta, meta_tokens, m_meta_tokens, v_meta_tokens)

    return (loss, grad_x, *[grads[n] for n in WEIGHTS], *[delta[n] for n in WEIGHTS], *[new_m[n] for n in WEIGHTS],
            *[new_v[n] for n in WEIGHTS])
```

```python
import functools
import math

import jax
import jax.numpy as jnp
from jax import lax
from jax.experimental import pallas as pl
from jax.experimental.pallas import tpu as pltpu

F32 = jnp.float32
BF16 = jnp.bfloat16

N_DEV = 8
N_META = 16
MLA_HEADS = 8
Q_LORA = 384
KV_LORA = 256
QK_NOPE = 64
QK_ROPE = 32
V_HEAD = 64
ROPE_THETA = 10000.0
HG_HEADS = 4
HG_DIM = 128
EPS = 1e-6
NEG_BIG = -1e30
F_MIN = 1e-20
ADAM_LR = 0.001
ADAM_B1 = 0.9
ADAM_B2 = 0.999
ADAM_EPS = 1e-08
ADAM_WD = 0.01
ADAM_STEP = 10

LANE = 128
ROW_ALIGN = 256
HG_CHUNK = 128
HG_LEVELS = (64, 32, 16, 8, 4, 2, 1)
VMEM_LIMIT = 48 * 1024 * 1024

Z_CQ, Z_CKV, Z_KPE, Z_KPESW, Z_HQ, Z_HF, Z_HI, Z_HG, Z_GA, Z_GB, Z_W = 0, 384, 640, 768, 1024, 1536, 2048, 2560, 3072, 4096, 5120
ATTN_SCALE = float((QK_NOPE + QK_ROPE) ** -0.5)
LOG2E = 1.4426950408889634
ATTN_C2 = ATTN_SCALE * LOG2E
HG_SCALE = float(HG_DIM ** -0.5)


def _cparams(sem):
    return pltpu.CompilerParams(dimension_semantics=sem, vmem_limit_bytes=VMEM_LIMIT)


def _tile(n, cap):
    if n <= cap:
        return n
    best = None
    for t in range(LANE, cap + 1, LANE):
        if n % t == 0:
            best = t
    assert best is not None, (n, cap)
    return best


def _row_tile(m):
    for t in (384, 256, 128):
        if m % t == 0:
            return t
    raise ValueError(m)


def _bf(x):
    return x.astype(BF16)


def _dot(a, b):
    return jnp.dot(a, b, preferred_element_type=F32)


def _dot_nt(a, b):
    return lax.dot_general(a, b, (((1,), (1,)), ((), ())), preferred_element_type=F32)


def _dot_tn(a, b):
    return lax.dot_general(a, b, (((0,), (0,)), ((), ())), preferred_element_type=F32)


def _sigmoid(x):
    return 1.0 / (1.0 + jnp.exp(-x))


def _mm(name, a_list, pairs, extras, outs, fn, *, tm, tn, n):
    m = a_list[0].shape[0]
    na, nb, ne, no = len(a_list), len(pairs), len(extras), len(outs)

    def body(*refs):
        a_refs = refs[:na]
        b_refs = refs[na:na + nb]
        e_refs = refs[na + nb:na + nb + ne]
        o_refs = refs[na + nb + ne:]
        a_vals = [_bf(r[...]) for r in a_refs]
        accs = [_dot(a_vals[ai], b_refs[k][...]) for k, (ai, _, _) in enumerate(pairs)]
        res = fn(accs, [r[...] for r in e_refs])
        for r, v in zip(o_refs, res):
            r[...] = v.astype(r.dtype)

    def spec(block_shape, index_map):
        return pl.BlockSpec(block_shape, functools.partial(lambda j, i, im: im(i, j), im=index_map))

    in_specs = [spec((tm, a.shape[1]), lambda i, j: (i, 0)) for a in a_list]
    for _, b, off in pairs:
        in_specs.append(spec((b.shape[0], tn), functools.partial(lambda i, j, off: (0, j + off), off=off)))
    in_specs += [spec(bs, im) for _, bs, im in extras]
    return pl.pallas_call(
        body, name=name, grid=(n // tn, m // tm),
        in_specs=in_specs,
        out_specs=[spec(bs, im) for _, _, bs, im in outs],
        out_shape=[jax.ShapeDtypeStruct(s, d) for s, d, _, _ in outs],
        compiler_params=_cparams(("parallel", "parallel")),
    )(*a_list, *[b for _, b, _ in pairs], *[e for e, _, _ in extras])


def _tile_spec(tm, tn, col_off=0):
    return (tm, tn), functools.partial(lambda i, j, off: (i, j + off), off=col_off)


def _mm_tn(name, a, b, *, alpha, out_dtype):
    t, k = a.shape
    n = b.shape[1]
    tk, tn = _tile(k, 1408), _tile(n, 1408)
    tt = next(c for c in (768, 512, 256) if t % c == 0)
    nt = t // tt

    def body(a_ref, b_ref, o_ref, acc_ref):
        s = pl.program_id(2)

        @pl.when(s == 0)
        def _():
            acc_ref[...] = jnp.zeros_like(acc_ref)

        acc_ref[...] += _dot_tn(_bf(a_ref[...]), _bf(b_ref[...]))

        @pl.when(s == nt - 1)
        def _():
            o_ref[...] = (alpha * acc_ref[...]).astype(o_ref.dtype)

    return pl.pallas_call(
        body, name=name, grid=(k // tk, n // tn, nt),
        in_specs=[pl.BlockSpec((tt, tk), lambda i, j, s: (s, i)), pl.BlockSpec((tt, tn), lambda i, j, s: (s, j))],
        out_specs=pl.BlockSpec((tk, tn), lambda i, j, s: (i, j)),
        out_shape=jax.ShapeDtypeStruct((k, n), out_dtype),
        scratch_shapes=[pltpu.VMEM((tk, tn), F32)],
        compiler_params=_cparams(("parallel", "parallel", "arbitrary")),
    )(a, b)


def _rms_parts(x):
    r = lax.rsqrt(jnp.mean(x * x, axis=-1, keepdims=True) + EPS)
    return r, x * r


def _rms_bwd_math(x, w, dxn):
    r, xhat = _rms_parts(x)
    t = dxn * w
    dx = r * (t - xhat * jnp.mean(t * xhat, axis=-1, keepdims=True))
    dw = jnp.sum(dxn * xhat, axis=0, keepdims=True)
    return dx, dw


def _rms_fwd(name, h, w):
    t, d = h.shape
    tm = _row_tile(t)

    def body(h_ref, w_ref, o_ref):
        _, xhat = _rms_parts(h_ref[...])
        o_ref[...] = (xhat * w_ref[...]).astype(o_ref.dtype)

    return pl.pallas_call(
        body, name=name, grid=(t // tm,),
        in_specs=[pl.BlockSpec((tm, d), lambda i: (i, 0)), pl.BlockSpec((1, d), lambda i: (0, 0))],
        out_specs=pl.BlockSpec((tm, d), lambda i: (i, 0)),
        out_shape=jax.ShapeDtypeStruct((t, d), BF16),
        compiler_params=_cparams(("parallel",)),
    )(h, w)


def _rms_bwd(name, h, w, dxn, dh_in):
    t, d = h.shape
    tm = _row_tile(t)

    def body(h_ref, w_ref, dxn_ref, dh_ref, o_ref, dw_ref):
        dx, dw = _rms_bwd_math(h_ref[...], w_ref[...], dxn_ref[...])
        o_ref[...] = dh_ref[...] + dx

        @pl.when(pl.program_id(0) == 0)
        def _():
            dw_ref[...] = jnp.zeros_like(dw_ref)

        dw_ref[...] += dw

    row = pl.BlockSpec((tm, d), lambda i: (i, 0))
    vec = pl.BlockSpec((1, d), lambda i: (0, 0))
    return pl.pallas_call(
        body, name=name, grid=(t // tm,),
        in_specs=[row, vec, row, row],
        out_specs=[row, vec],
        out_shape=[jax.ShapeDtypeStruct((t, d), F32), jax.ShapeDtypeStruct((1, d), F32)],
        compiler_params=_cparams(("arbitrary",)),
    )(h, w, dxn, dh_in)


def _loss_head(h, w, tgt, n_real):
    t, d = h.shape
    tm = _row_tile(t)

    def body(h_ref, w_ref, t_ref, dh_ref, dw_ref, loss_ref):
        i = pl.program_id(0)
        x = h_ref[...]
        wv = w_ref[...]
        _, xhat = _rms_parts(x)
        rows = i * tm + lax.broadcasted_iota(jnp.int32, (tm, 1), 0)
        valid = (rows >= N_META) & (rows < n_real)
        e = jnp.where(valid, xhat * wv - t_ref[...], 0.0)
        dx, dw = _rms_bwd_math(x, wv, e * (1.0 / d))
        dh_ref[...] = dx

        @pl.when(i == 0)
        def _():
            dw_ref[...] = jnp.zeros_like(dw_ref)
            loss_ref[...] = jnp.zeros_like(loss_ref)

        dw_ref[...] += dw
        loss_ref[...] += (0.5 / d) * jnp.sum(jnp.sum(e * e, axis=-1, keepdims=True), axis=0, keepdims=True)

    row = pl.BlockSpec((tm, d), lambda i: (i, 0))
    vec = pl.BlockSpec((1, d), lambda i: (0, 0))
    return pl.pallas_call(
        body, name="loss_head", grid=(t // tm,),
        in_specs=[row, vec, row],
        out_specs=[row, vec, pl.BlockSpec((1, 1), lambda i: (0, 0))],
        out_shape=[jax.ShapeDtypeStruct((t, d), F32), jax.ShapeDtypeStruct((1, d), F32), jax.ShapeDtypeStruct((1, 1), F32)],
        compiler_params=_cparams(("arbitrary",)),
    )(h, w, tgt)


def _heads(x):
    return jnp.concatenate([x] * MLA_HEADS, axis=1)


def _mla_front(z, qn_w, kvn_w, tabs, w_uq, w_ukv):
    t = z.shape[0]
    tm = _attn_block(t)
    wq = MLA_HEADS * LANE

    def body(z_ref, qw_ref, kw_ref, cq_ref, ck_ref, sq_ref, wuq_ref, wukv_ref, cqn_ref, ckvn_ref, q_ref, kv_ref, vt_ref):
        zz = z_ref[...]
        _, qhat = _rms_parts(zz[:, Z_CQ:Z_CKV])
        _, khat = _rms_parts(zz[:, Z_CKV:Z_KPE])
        cqn = (qhat * qw_ref[...]).astype(BF16)
        ckvn = (khat * kw_ref[...]).astype(BF16)
        cqn_ref[...] = cqn
        ckvn_ref[...] = ckvn
        krot = zz[:, Z_KPE:Z_KPESW] * ck_ref[...] + zz[:, Z_KPESW:Z_KPESW + LANE] * sq_ref[...]
        qq = _dot(cqn, wuq_ref[...])
        q_ref[...] = (qq[:, :wq] * _heads(cq_ref[...]) + qq[:, wq:] * _heads(sq_ref[...])).astype(BF16)
        kvv = _dot(ckvn, wukv_ref[...]) + _kv_pattern(krot)
        kv_ref[...] = kvv.astype(BF16)
        for hd in range(MLA_HEADS):
            vt_ref[hd, 0] = kvv[:, 2 * LANE * hd + LANE:2 * LANE * (hd + 1)].T.astype(BF16)

    def rows(wd):
        return pl.BlockSpec((tm, wd), lambda i: (i, 0))

    def whole(a):
        return pl.BlockSpec(a.shape, lambda i: (0, 0))

    return pl.pallas_call(
        body, name="mla_front", grid=(t // tm,),
        in_specs=[rows(Z_HQ), whole(qn_w), whole(kvn_w), rows(LANE), rows(LANE), rows(LANE), whole(w_uq), whole(w_ukv)],
        out_specs=[rows(Q_LORA), rows(KV_LORA), rows(wq), rows(2 * wq),
                   pl.BlockSpec((MLA_HEADS, 1, LANE, tm), lambda i: (0, i, 0, 0))],
        out_shape=[jax.ShapeDtypeStruct((t, Q_LORA), BF16), jax.ShapeDtypeStruct((t, KV_LORA), BF16),
                   jax.ShapeDtypeStruct((t, wq), BF16), jax.ShapeDtypeStruct((t, 2 * wq), BF16),
                   jax.ShapeDtypeStruct((MLA_HEADS, t // tm, LANE, tm), BF16)],
        compiler_params=_cparams(("parallel",)),
    )(z, qn_w, kvn_w, *tabs, w_uq, w_ukv)


def _mla_back(z, qn_w, kvn_w, tabs, cqn, ckvn, dq, dkv, w_uq_t, w_ukv_t):
    t = z.shape[0]
    tm = next(c for c in (768, 512, 256) if t % c == 0)
    wq = MLA_HEADS * LANE

    def body(z_ref, qw_ref, kw_ref, cq_ref, ck_ref, sq_ref, cqn_ref, ckvn_ref, dq_ref, dkv_ref, wuqt_ref, wukvt_ref,
             dz_ref, dwuq_ref, dwukv_ref, dqw_ref, dkw_ref):
        @pl.when(pl.program_id(0) == 0)
        def _():
            for r in (dwuq_ref, dwukv_ref, dqw_ref, dkw_ref):
                r[...] = jnp.zeros_like(r)

        zz = z_ref[...]
        d = dq_ref[...] * ATTN_SCALE
        dqq = jnp.concatenate([d * _heads(cq_ref[...]), d * _heads(sq_ref[...])], axis=1).astype(BF16)
        dkv_v = dkv_ref[...]
        dwuq_ref[...] += _dot_tn(cqn_ref[...], dqq)
        dwukv_ref[...] += _dot_tn(ckvn_ref[...], dkv_v)
        dcq, dqw = _rms_bwd_math(zz[:, Z_CQ:Z_CKV], qw_ref[...], _dot(dqq, wuqt_ref[...]))
        dckv, dkw = _rms_bwd_math(zz[:, Z_CKV:Z_KPE], kw_ref[...], _dot(dkv_v, wukvt_ref[...]))
        dkr = jnp.zeros((tm, LANE), F32)
        for hd in range(MLA_HEADS):
            dkr = dkr + dkv_v[:, 2 * LANE * hd:2 * LANE * hd + LANE].astype(F32)
        dz_ref[...] = jnp.concatenate(
            [dcq, dckv, dkr * ck_ref[...], dkr * sq_ref[...], jnp.zeros((tm, Z_HQ - Z_KPESW - LANE), F32)], axis=1
        ).astype(BF16)
        dqw_ref[...] += dqw
        dkw_ref[...] += dkw

    def rows(wd):
        return pl.BlockSpec((tm, wd), lambda i: (i, 0))

    def whole(shape):
        return pl.BlockSpec(shape, lambda i: (0, 0))

    return pl.pallas_call(
        body, name="mla_back", grid=(t // tm,),
        in_specs=[rows(Z_HQ), whole(qn_w.shape), whole(kvn_w.shape), rows(LANE), rows(LANE), rows(LANE), rows(Q_LORA),
                  rows(KV_LORA), rows(wq), rows(2 * wq), whole(w_uq_t.shape), whole(w_ukv_t.shape)],
        out_specs=[rows(Z_HQ), whole((Q_LORA, 2 * wq)), whole((KV_LORA, 2 * wq)), whole((1, Q_LORA)), whole((1, KV_LORA))],
        out_shape=[jax.ShapeDtypeStruct((t, Z_HQ), BF16), jax.ShapeDtypeStruct((Q_LORA, 2 * wq), F32),
                   jax.ShapeDtypeStruct((KV_LORA, 2 * wq), F32), jax.ShapeDtypeStruct((1, Q_LORA), F32),
                   jax.ShapeDtypeStruct((1, KV_LORA), F32)],
        compiler_params=_cparams(("arbitrary",)),
    )(z, qn_w, kvn_w, *tabs, cqn, ckvn, dq, dkv, w_uq_t, w_ukv_t)


def _attn_block(t):
    for b in (768, 512, 256):
        if t % b == 0:
            return b
    raise ValueError(t)


def _call_carrying_push(body, push, *, name, grid, in_specs, out_specs, out_shape, scratch_shapes, args):
    if push is None:
        outs = pl.pallas_call(body, name=name, grid=grid, in_specs=in_specs, out_specs=out_specs, out_shape=out_shape,
                              scratch_shapes=scratch_shapes, compiler_params=_cparams(("parallel", "arbitrary")))(*args)
        return outs, None
    n_in, n_out, n_scr, n_pin, nk = len(in_specs), len(out_specs), len(scratch_shapes), len(push["ins"]), push["nk"]

    def carrying(*refs):
        o0 = n_in + n_pin
        s0 = o0 + n_out + nk
        pins, pouts, sems = refs[n_in:o0], refs[o0 + n_out:s0], refs[s0 + n_scr:]
        a, b = pl.program_id(0), pl.program_id(1)

        @pl.when((a == 0) & (b == 0))
        def _():
            for cp in _push_copies(push, pins, pouts, sems):
                cp.start()

        body(*refs[:n_in], *refs[o0:o0 + n_out], *refs[s0:s0 + n_scr])

        @pl.when((a == grid[0] - 1) & (b == grid[1] - 1))
        def _():
            for cp in _push_copies(push, pins, pouts, sems):
                cp.wait()

    outs = pl.pallas_call(
        carrying, name=name + "_push", grid=grid, in_specs=list(in_specs) + push["in_specs"],
        out_specs=list(out_specs) + push["out_specs"], out_shape=list(out_shape) + push["outs"],
        scratch_shapes=list(scratch_shapes) + push["sems"], input_output_aliases=_push_aliases(push, n_in, n_out),
        compiler_params=_cparams(("arbitrary", "arbitrary")))(*args, *push["ins"])
    return outs[:n_out], outs[n_out:]


def _attn_fwd(q, kv, v_t, push=None):
    t = q.shape[0]
    bq = bk = _attn_block(t)
    nq = t // bq

    def body(q_ref, k_ref, vt_ref, o_ref, lse_ref):
        i = pl.program_id(1)
        qv = q_ref[...]
        qpos = i * bq + lax.broadcasted_iota(jnp.int32, (1, bq), 1)

        def block(j, carry, masked):
            m, l, acc = carry
            s = _dot_nt(k_ref[pl.ds(pl.multiple_of(j * bk, bk), bk), :], qv)
            if masked:
                kpos = j * bk + lax.broadcasted_iota(jnp.int32, (bk, 1), 0)
                s = jnp.where(kpos <= qpos, s, NEG_BIG)
            m_new = jnp.maximum(m, jnp.max(s, axis=0, keepdims=True))
            p = jnp.exp2((s - m_new) * ATTN_C2)
            a = jnp.exp2((m - m_new) * ATTN_C2)
            return m_new, a * l + jnp.sum(p, axis=0, keepdims=True), a * acc + _dot(vt_ref[0, j], _bf(p))

        init = (jnp.full((1, bq), NEG_BIG, F32), jnp.zeros((1, bq), F32), jnp.zeros((LANE, bq), F32))
        carry = lax.fori_loop(0, i, functools.partial(block, masked=False), init)
        m, l, acc = block(i, carry, True)
        o_ref[...] = (acc / l).T.astype(o_ref.dtype)
        lse_ref[0, 0] = m * ATTN_C2 + jnp.log(l) * LOG2E

    return _call_carrying_push(
        body, push, name="attn_fwd", grid=(MLA_HEADS, nq),
        in_specs=[pl.BlockSpec((bq, LANE), lambda h, i: (i, h)),
                  pl.BlockSpec((t, LANE), lambda h, i: (0, 2 * h)),
                  pl.BlockSpec((1, nq, LANE, bk), lambda h, i: (h, 0, 0, 0))],
        out_specs=[pl.BlockSpec((bq, LANE), lambda h, i: (i, h)),
                   pl.BlockSpec((1, 1, 1, bq), lambda h, i: (h, i, 0, 0))],
        out_shape=[jax.ShapeDtypeStruct((t, MLA_HEADS * LANE), BF16), jax.ShapeDtypeStruct((MLA_HEADS, nq, 1, bq), F32)],
        scratch_shapes=[], args=(q, kv, v_t))


def _attn_bwd(q, kv, o, do, lse, push=None):
    t = q.shape[0]
    bk = bw = _attn_block(t)
    nk, nw = t // bk, t // bw

    def body(q_ref, o_ref, do_ref, k_ref, v_ref, lse_ref, dq_ref, dkv_ref, dl_ref):
        j = pl.program_id(1)

        @pl.when(j == 0)
        def _():
            dq_ref[...] = jnp.zeros_like(dq_ref)
            for i in range(nw):
                rows = slice(i * bw, (i + 1) * bw)
                d = jnp.sum(o_ref[rows, :].astype(F32) * do_ref[rows, :].astype(F32), axis=1, keepdims=True)
                dl_ref[i] = jnp.broadcast_to(d, (bw, LANE)).T[0:1, :]

        kb = k_ref[...]
        vb = v_ref[...]
        kpos = j * bk + lax.broadcasted_iota(jnp.int32, (bk, 1), 0)

        def block(i, carry, masked):
            dk, dv = carry
            rows = pl.ds(pl.multiple_of(i * bw, bw), bw)
            qb = q_ref[rows, :]
            dob = do_ref[rows, :]
            pt = jnp.exp2(_dot_nt(kb, qb) * ATTN_C2 - lse_ref[0, i])
            if masked:
                qpos = i * bw + lax.broadcasted_iota(jnp.int32, (1, bw), 1)
                pt = jnp.where(kpos <= qpos, pt, 0.0)
            dv = dv + _dot(_bf(pt), dob)
            dst = _bf(pt * (_dot_nt(vb, dob) - dl_ref[i]))
            dk = dk + _dot(dst, qb)
            dq_ref[rows, :] += _dot_tn(dst, kb)
            return dk, dv

        i0 = j
        carry = block(i0, (jnp.zeros((bk, LANE), F32), jnp.zeros((bk, LANE), F32)), True)
        dk, dv = lax.fori_loop(i0 + 1, nw, functools.partial(block, masked=False), carry)
        dkv_ref[...] = jnp.concatenate([dk * ATTN_SCALE, dv], axis=1).astype(dkv_ref.dtype)

    head_rows = pl.BlockSpec((t, LANE), lambda h, j: (0, h))
    return _call_carrying_push(
        body, push, name="attn_bwd", grid=(MLA_HEADS, nk),
        in_specs=[head_rows, head_rows, head_rows,
                  pl.BlockSpec((bk, LANE), lambda h, j: (j, 2 * h)), pl.BlockSpec((bk, LANE), lambda h, j: (j, 2 * h + 1)),
                  pl.BlockSpec((1, nw, 1, bw), lambda h, j: (h, 0, 0, 0))],
        out_specs=[head_rows, pl.BlockSpec((bk, 2 * LANE), lambda h, j: (j, h))],
        out_shape=[jax.ShapeDtypeStruct((t, MLA_HEADS * LANE), F32), jax.ShapeDtypeStruct((t, 2 * MLA_HEADS * LANE), BF16)],
        scratch_shapes=[pltpu.VMEM((nw, 1, bw), F32)], args=(q, o, do, kv, kv, lse))


def _hg_tables():
    c = HG_CHUNK
    tri = (jnp.arange(c)[:, None] >= jnp.arange(c)[None, :]).astype(BF16)
    return tri, tri.T


def _hg_level_ref(b, m):
    c = HG_CHUNK
    if 2 * m >= 8:
        x = b.reshape(c // (2 * m), 2 * m, c)
        return jnp.broadcast_to(x[:, m - 1:m, :], x.shape).reshape(c, c)
    row = lax.broadcasted_iota(jnp.int32, (c, 1), 0)
    if m == 2:
        pos = row & 3
        return jnp.where(pos == 0, pltpu.roll(b, c - 1, 0),
                         jnp.where(pos == 1, b, jnp.where(pos == 2, pltpu.roll(b, 1, 0), pltpu.roll(b, 2, 0))))
    return jnp.where((row & 1) == 0, b, pltpu.roll(b, 1, 0))


def _hg_level_ref_t(d, m):
    c = HG_CHUNK
    row = lax.broadcasted_iota(jnp.int32, (c, 1), 0)
    if 2 * m >= 8:
        x = d.reshape(c // (2 * m), 2 * m, c)
        s = jnp.broadcast_to(jnp.sum(x, axis=1, keepdims=True), x.shape).reshape(c, c)
        return jnp.where((row & (2 * m - 1)) == m - 1, s, 0.0)
    if m == 2:
        s = pltpu.roll(d, 1, 0) + d + pltpu.roll(d, c - 1, 0) + pltpu.roll(d, c - 2, 0)
        return jnp.where((row & 3) == 1, s, 0.0)
    return jnp.where((row & 1) == 0, d + pltpu.roll(d, c - 1, 0), 0.0)


def _table_dot(table, x):
    hi = _bf(x)
    rest = x - hi.astype(F32)
    mid = _bf(rest)
    lo = _bf(rest - mid.astype(F32))
    out = _dot(table, jnp.concatenate([hi, mid, lo], axis=1))
    n = x.shape[1]
    return out[:, 0:n] + out[:, n:2 * n] + out[:, 2 * n:3 * n]


def _hg_gates(hq, hf, lb):
    sg = _sigmoid(hf)
    sn = _sigmoid(-hf)
    f = lb + (1.0 - lb) * sg
    q = hq * _sigmoid(hq)
    g = jnp.log(jnp.maximum(f, F_MIN))
    k = (1.0 - lb) * sn
    return q, k, g, f, sg, sn


def _hg_level_masks(m):
    c = HG_CHUNK
    row = lax.broadcasted_iota(jnp.int32, (c, 1), 0)
    col = lax.broadcasted_iota(jnp.int32, (1, c), 1)
    shift = (2 * m).bit_length() - 1
    up = (row & m) != 0
    same = lax.shift_right_logical(row, shift) == lax.shift_right_logical(col, shift)
    return up, same


def _hg_level_factors(b, bref, up):
    lo = jnp.logical_not(up)
    eq = jnp.where(up, jnp.exp(jnp.where(up, b - bref, 0.0)), 0.0)
    ek = jnp.where(lo, jnp.exp(jnp.where(lo, bref - b, 0.0)), 0.0)
    return eq, ek


def _hg_intra(q, k, b):
    c = HG_CHUNK
    row = lax.broadcasted_iota(jnp.int32, (c, 1), 0)
    col = lax.broadcasted_iota(jnp.int32, (1, c), 1)
    a = jnp.where(row == col, _dot_nt(_bf(q), _bf(k)), 0.0)
    parts = []
    for m in HG_LEVELS:
        up, same = _hg_level_masks(m)
        eq, ek = _hg_level_factors(b, _hg_level_ref(b, m), up)
        qt, kt = q * eq, k * ek
        a = a + jnp.where(same, _dot_nt(_bf(qt), _bf(kt)), 0.0)
        parts.append((eq, ek, qt, kt))
    return a, parts


def _hg_chunk_fwd(hq, hf, hi, hg, lb, nw, st, call):
    q, k, g, _, _, _ = _hg_gates(hq, hf, lb)
    b = _table_dot(call, g)
    a, _ = _hg_intra(q, k, b)
    v16 = _bf(hi)
    o = _dot(_bf(a * HG_SCALE), v16) + _dot_nt(_bf(q * jnp.exp(b) * HG_SCALE), _bf(st))
    bl = b[HG_CHUNK - 1:HG_CHUNK]
    ke = k * jnp.exp(bl - b)
    st_new = st * jnp.exp(bl) + _dot(_bf(hi.T), _bf(ke))
    r = lax.rsqrt(jnp.mean(o * o, axis=-1, keepdims=True) + EPS)
    y = o * r * nw * (hg * _sigmoid(hg))
    return y, st_new


def _hg_chunk_bwd(hq, hf, hi, hg, lb, nw, st, call, call_t, dy, dst_new):
    c = HG_CHUNK
    q, k, g, f, sg, sn = _hg_gates(hq, hf, lb)
    b = _table_dot(call, g)
    a, parts = _hg_intra(q, k, b)
    v16 = _bf(hi)
    st16 = _bf(st)
    eb = jnp.exp(b)
    qe = q * eb * HG_SCALE
    a16 = _bf(a * HG_SCALE)
    o = _dot(a16, v16) + _dot_nt(_bf(qe), st16)
    bl = b[c - 1:c]
    el = jnp.exp(bl)
    x = jnp.exp(bl - b)
    ke = k * x
    r = lax.rsqrt(jnp.mean(o * o, axis=-1, keepdims=True) + EPS)
    shg = _sigmoid(hg)
    gate = hg * shg
    ohat = o * r
    don = dy * gate
    dhg = dy * ohat * nw * (shg * (1.0 + hg * (1.0 - shg)))
    dnw = jnp.sum(don * ohat, axis=0, keepdims=True)
    tt = don * nw
    do = r * (tt - ohat * jnp.mean(tt * ohat, axis=-1, keepdims=True))
    do16 = _bf(do)
    dst16 = _bf(dst_new)
    da = _dot_nt(do16, v16) * HG_SCALE
    dv = _dot(_bf(a16.astype(F32).T), do16) + _dot_nt(_bf(ke), dst16)
    dqe = _dot(do16, st16)
    dke = _dot(v16, dst16)
    dst = dst_new * el + _dot(_bf(do.T), _bf(qe))
    dbl = jnp.sum(dst_new * st, axis=0, keepdims=True) * el
    dk = dke * x
    dxa = dke * ke
    db = dqe * qe - dxa
    dbl = dbl + jnp.sum(dxa, axis=0, keepdims=True)
    dq = dqe * eb * HG_SCALE
    row = lax.broadcasted_iota(jnp.int32, (c, 1), 0)
    col = lax.broadcasted_iota(jnp.int32, (1, c), 1)
    ddiag = jnp.sum(jnp.where(row == col, da, 0.0), axis=1, keepdims=True)
    dq = dq + ddiag * k
    dk = dk + ddiag * q
    for (eq, ek, qt, kt), m in zip(parts, HG_LEVELS):
        _, same = _hg_level_masks(m)
        gm = jnp.where(same, da, 0.0)
        dqt = _dot(_bf(gm), _bf(kt))
        dkt = _dot(_bf(gm.T), _bf(qt))
        dq = dq + dqt * eq
        dk = dk + dkt * ek
        darg = dqt * qt - dkt * kt
        db = db + darg - _hg_level_ref_t(darg, m)
    db = db + jnp.where(row == c - 1, dbl, 0.0)
    dg = _table_dot(call_t, db)
    shq = _sigmoid(hq)
    dhq = dq * (shq * (1.0 + hq * (1.0 - shq)))
    df = jnp.where(f > F_MIN, dg / jnp.maximum(f, F_MIN), 0.0)
    dlb = jnp.sum(df * (1.0 - sg) - dk * sn, axis=0, keepdims=True)
    dhf = df * (1.0 - lb) * sg * (1.0 - sg) - dk * (1.0 - lb) * sn * (1.0 - sn)
    return dhq, dhf, dv, dhg, dlb, dnw, dst


def _hg_chunks_per_step(t):
    return 6 if t % (6 * HG_CHUNK) == 0 else 2


def _hg_col(group, h):
    return group // LANE + h


def _hgrn_fwd(z, lb, nw, call):
    t = z.shape[0]
    c, cs = HG_CHUNK, _hg_chunks_per_step(t)
    rows = c * cs
    nsteps = t // rows

    def body(hq_ref, hf_ref, hi_ref, hg_ref, lb_ref, nw_ref, call_ref, y_ref, sv_ref, st_ref):
        @pl.when(pl.program_id(1) == 0)
        def _():
            st_ref[...] = jnp.zeros_like(st_ref)

        for u in range(cs):
            sl = slice(u * c, (u + 1) * c)
            st = st_ref[...]
            sv_ref[0, u] = st
            y, st_new = _hg_chunk_fwd(hq_ref[sl, :], hf_ref[sl, :], hi_ref[sl, :], hg_ref[sl, :], lb_ref[...], nw_ref[...],
                                      st, call_ref[...])
            y_ref[sl, :] = y.astype(y_ref.dtype)
            st_ref[...] = st_new

    def zcol(group):
        return pl.BlockSpec((rows, LANE), functools.partial(lambda h, i, g: (i, _hg_col(g, h)), g=group))

    ncall = call.shape[0]
    return pl.pallas_call(
        body, name="hgrn_fwd", grid=(HG_HEADS, nsteps),
        in_specs=[zcol(Z_HQ), zcol(Z_HF), zcol(Z_HI), zcol(Z_HG),
                  pl.BlockSpec((1, LANE), lambda h, i: (0, h)), pl.BlockSpec((1, LANE), lambda h, i: (0, 0)),
                  pl.BlockSpec((ncall, c), lambda h, i: (0, 0))],
        out_specs=[pl.BlockSpec((rows, LANE), lambda h, i: (i, h)),
                   pl.BlockSpec((1, cs, c, c), lambda h, i: (h, i, 0, 0))],
        out_shape=[jax.ShapeDtypeStruct((t, HG_HEADS * LANE), BF16), jax.ShapeDtypeStruct((HG_HEADS, t // c, c, c), F32)],
        scratch_shapes=[pltpu.VMEM((c, c), F32)],
        compiler_params=_cparams(("parallel", "arbitrary")),
    )(z, z, z, z, lb, nw, call)


def _hgrn_bwd(z, lb, nw, call, call_t, saved, dy):
    t = z.shape[0]
    c, cs = HG_CHUNK, _hg_chunks_per_step(t)
    rows = c * cs
    nsteps = t // rows

    def body(hq_ref, hf_ref, hi_ref, hg_ref, lb_ref, nw_ref, call_ref, callt_ref, sv_ref, dy_ref,
             dhq_ref, dhf_ref, dhi_ref, dhg_ref, dlb_ref, dnw_ref, dst_ref):
        h, i = pl.program_id(0), pl.program_id(1)

        @pl.when(i == 0)
        def _():
            dst_ref[...] = jnp.zeros_like(dst_ref)
            dlb_ref[...] = jnp.zeros_like(dlb_ref)

        @pl.when((i == 0) & (h == 0))
        def _():
            dnw_ref[...] = jnp.zeros_like(dnw_ref)

        for u in reversed(range(cs)):
            sl = slice(u * c, (u + 1) * c)
            dhq, dhf, dhi, dhg, dlb, dnw, dst = _hg_chunk_bwd(
                hq_ref[sl, :], hf_ref[sl, :], hi_ref[sl, :], hg_ref[sl, :], lb_ref[...], nw_ref[...], sv_ref[0, u],
                call_ref[...], callt_ref[...], dy_ref[sl, :].astype(F32), dst_ref[...])
            dhq_ref[sl, :] = dhq.astype(BF16)
            dhf_ref[sl, :] = dhf.astype(BF16)
            dhi_ref[sl, :] = dhi.astype(BF16)
            dhg_ref[sl, :] = dhg.astype(BF16)
            dlb_ref[...] += dlb
            dnw_ref[...] += dnw
            dst_ref[...] = dst

    def zcol(group):
        return pl.BlockSpec((rows, LANE), functools.partial(lambda h, i, g: (nsteps - 1 - i, _hg_col(g, h)), g=group))

    head_rows = pl.BlockSpec((rows, LANE), lambda h, i: (nsteps - 1 - i, h))
    ncall = call.shape[0]
    piece = jax.ShapeDtypeStruct((t, HG_HEADS * LANE), BF16)
    return pl.pallas_call(
        body, name="hgrn_bwd", grid=(HG_HEADS, nsteps),
        in_specs=[zcol(Z_HQ), zcol(Z_HF), zcol(Z_HI), zcol(Z_HG),
                  pl.BlockSpec((1, LANE), lambda h, i: (0, h)), pl.BlockSpec((1, LANE), lambda h, i: (0, 0)),
                  pl.BlockSpec((ncall, c), lambda h, i: (0, 0)), pl.BlockSpec((c, ncall), lambda h, i: (0, 0)),
                  pl.BlockSpec((1, cs, c, c), lambda h, i: (h, nsteps - 1 - i, 0, 0)), head_rows],
        out_specs=[head_rows, head_rows, head_rows, head_rows,
                   pl.BlockSpec((1, LANE), lambda h, i: (0, h)), pl.BlockSpec((1, LANE), lambda h, i: (0, 0))],
        out_shape=[piece, piece, piece, piece,
                   jax.ShapeDtypeStruct((1, HG_HEADS * LANE), F32), jax.ShapeDtypeStruct((1, LANE), F32)],
        scratch_shapes=[pltpu.VMEM((c, c), F32)],
        compiler_params=_cparams(("arbitrary", "arbitrary")),
    )(z, z, z, z, lb, nw, call, call_t, saved, dy)


def _lb_fwd(raw):
    nl = raw.shape[0]

    def body(r_ref, o_ref):
        x = r_ref[...]
        e = jnp.exp(x - jnp.max(x, axis=0, keepdims=True))
        p = e / jnp.sum(e, axis=0, keepdims=True)
        acc = jnp.zeros_like(p[0:1])
        for l in range(nl):
            if l > 0:
                acc = acc + p[l:l + 1]
            o_ref[l:l + 1, :] = acc

    return pl.pallas_call(body, name="lb_fwd", out_shape=jax.ShapeDtypeStruct(raw.shape, F32))(raw)


def _lb_bwd(raw, dlbs):
    nl = raw.shape[0]

    def body(r_ref, d_ref, o_ref):
        x = r_ref[...]
        e = jnp.exp(x - jnp.max(x, axis=0, keepdims=True))
        p = e / jnp.sum(e, axis=0, keepdims=True)
        d = d_ref[...]
        dps = [jnp.zeros_like(d[0:1])]
        for i in range(1, nl):
            acc = d[i:i + 1]
            for l in range(i + 1, nl):
                acc = acc + d[l:l + 1]
            dps.append(acc)
        dot = dps[0] * p[0:1]
        for i in range(1, nl):
            dot = dot + dps[i] * p[i:i + 1]
        for i in range(nl):
            o_ref[i:i + 1, :] = p[i:i + 1] * (dps[i] - dot)

    return pl.pallas_call(body, name="lb_bwd", out_shape=jax.ShapeDtypeStruct(raw.shape, F32))(raw, dlbs)


def _push_plan(srcs, gather, bufs=None, layer=None):
    nk = len(srcs)
    any_spec = pl.BlockSpec(memory_space=pl.ANY)
    if bufs is None:
        ins = list(srcs)
        outs = [jax.ShapeDtypeStruct(((N_DEV,) + s.shape) if gather else s.shape, s.dtype) for s in srcs]
    else:
        ins = list(srcs) + list(bufs)
        outs = [jax.ShapeDtypeStruct(b.shape, b.dtype) for b in bufs]
    sems = [pltpu.SemaphoreType.DMA((nk * N_DEV,)), pltpu.SemaphoreType.DMA((nk * N_DEV,)), pltpu.SemaphoreType.DMA((nk,))]
    return dict(nk=nk, gather=gather, layer=layer, ins=ins, in_specs=[any_spec] * len(ins), outs=outs,
                out_specs=[any_spec] * nk, sems=sems, alias_from=None if bufs is None else nk)


def _push_aliases(plan, first_in, first_out):
    if plan is None or plan["alias_from"] is None:
        return {}
    return {first_in + plan["alias_from"] + k: first_out + k for k in range(plan["nk"])}


def _push_copies(plan, in_refs, out_refs, sems):
    nk, gather, layer = plan["nk"], plan["gather"], plan["layer"]
    send_sems, recv_sems, local_sems = sems
    me = 4 * lax.axis_index("x") + 2 * lax.axis_index("y") + lax.axis_index("c")

    def landing(k):
        return out_refs[k].at[me] if layer is None else out_refs[k].at[me, layer]

    copies = [pltpu.make_async_copy(in_refs[k] if gather else in_refs[k].at[me], landing(k), local_sems.at[k])
              for k in range(nk)]
    for r in range(1, N_DEV):
        to = (me + r) % N_DEV
        for k in range(nk):
            copies.append(pltpu.make_async_remote_copy(
                src_ref=in_refs[k] if gather else in_refs[k].at[to], dst_ref=landing(k),
                send_sem=send_sems.at[k * N_DEV + r], recv_sem=recv_sems.at[k * N_DEV + r],
                device_id=(to // 4, (to // 2) % 2, to % 2), device_id_type=pl.DeviceIdType.MESH))
    return copies


def _exchange(name, srcs, gather, bufs=None, layer=None):
    plan = _push_plan(srcs, gather, bufs, layer)
    nin, nk = len(plan["ins"]), plan["nk"]

    def body(*refs):
        copies = _push_copies(plan, refs[:nin], refs[nin:nin + nk], refs[nin + nk:])
        for cp in copies:
            cp.start()
        for cp in copies:
            cp.wait()

    return pl.pallas_call(
        body, name=name, in_specs=plan["in_specs"], out_specs=plan["out_specs"], out_shape=plan["outs"],
        scratch_shapes=plan["sems"], input_output_aliases=_push_aliases(plan, 0, 0),
    )(*plan["ins"])


def _adam_math(g, w, m, v):
    m2 = ADAM_B1 * m + (1.0 - ADAM_B1) * g
    v2 = ADAM_B2 * v + (1.0 - ADAM_B2) * (g * g)
    m_hat = m2 / (1.0 - ADAM_B1 ** ADAM_STEP)
    v_hat = v2 / (1.0 - ADAM_B2 ** ADAM_STEP)
    return -ADAM_LR * (m_hat / (jnp.sqrt(v_hat) + ADAM_EPS) + ADAM_WD * w), m2, v2


def _sum_slots(ref):
    g = ref[0].astype(F32)
    for s in range(1, N_DEV):
        g = g + ref[s].astype(F32)
    return g


def _adam_sharded(name, slots, w, m, v):
    nl, a, b = w.shape
    ta = a
    for cand in range(8, 257, 8):
        if a % cand == 0:
            ta = cand

    def body(s_ref, w_ref, m_ref, v_ref, g_ref, d_ref, m2_ref, v2_ref):
        g = _sum_slots(s_ref)
        d, m2, v2 = _adam_math(g, w_ref[...], m_ref[...], v_ref[...])
        g_ref[...] = g
        d_ref[...] = d
        m2_ref[...] = m2
        v2_ref[...] = v2

    blk = pl.BlockSpec((1, ta, b), lambda l, i: (l, i, 0))
    sds = jax.ShapeDtypeStruct(w.shape, F32)
    return pl.pallas_call(
        body, name=name, grid=(nl, a // ta),
        in_specs=[pl.BlockSpec((N_DEV, 1, ta, b), lambda l, i: (0, l, i, 0)), blk, blk, blk],
        out_specs=[blk] * 4, out_shape=[sds] * 4,
        compiler_params=_cparams(("parallel", "parallel")),
    )(slots, w, m, v)


def _sum_replicated(slots):
    def body(s_ref, g_ref):
        g_ref[...] = _sum_slots(s_ref)

    return pl.pallas_call(body, name="sum_small", out_shape=jax.ShapeDtypeStruct(slots.shape[1:], F32))(slots)


def _adam_small(name, g, w, m, v):
    def body(g_ref, w_ref, m_ref, v_ref, d_ref, m2_ref, v2_ref):
        d, m2, v2 = _adam_math(g_ref[...], w_ref[...], m_ref[...], v_ref[...])
        d_ref[...] = d
        m2_ref[...] = m2
        v2_ref[...] = v2

    sds = jax.ShapeDtypeStruct(w.shape, F32)
    return pl.pallas_call(body, name=name, out_shape=[sds] * 3)(g, w, m, v)


def _cols_full(g):
    return jnp.transpose(g, (1, 0, 2)).reshape(g.shape[1], -1)


def _cols_shards(w):
    k = w.shape[0]
    return jnp.transpose(w.reshape(k, N_DEV, -1), (1, 0, 2))


def _swap_halves(x):
    half = x.shape[-1] // 2
    return jnp.concatenate([x[..., half:], x[..., :half]], axis=-1)


def _zeros_like_cols(x, n):
    return jnp.zeros(x.shape[:-1] + (n,), x.dtype)


def _w_in_internal(w):
    d = w.shape[0]
    kpe = w[:, 640:672]
    z64, z32 = jnp.zeros((d, 64), w.dtype), jnp.zeros((d, 32), w.dtype)
    return jnp.concatenate(
        [w[:, 0:640], z64, kpe, z32, z64, _swap_halves(kpe), z32, jnp.zeros((d, Z_HQ - Z_KPESW - LANE), w.dtype),
         w[:, 672:2720], w[:, 2720:4768]], axis=1)


def _w_in_grad(g):
    kpe = g[:, Z_KPE + 64:Z_KPE + 96] + _swap_halves(g[:, Z_KPESW + 64:Z_KPESW + 96])
    return jnp.concatenate([g[:, 0:640], kpe, g[:, Z_HQ:Z_W]], axis=1)


def _w_uq_internal(w):
    k = w.shape[0]
    w3 = w.reshape(k, MLA_HEADS, QK_NOPE + QK_ROPE)
    nope, rope = w3[..., :QK_NOPE], w3[..., QK_NOPE:]
    plain = jnp.concatenate([nope, rope, _zeros_like_cols(rope, 32)], axis=-1).reshape(k, -1)
    swapped = jnp.concatenate([_zeros_like_cols(nope, 64), _swap_halves(rope), _zeros_like_cols(rope, 32)], axis=-1).reshape(k, -1)
    return jnp.concatenate([plain, swapped], axis=1)


def _w_uq_grad(g):
    k = g.shape[0]
    half = MLA_HEADS * LANE
    g1, g2 = g[:, :half].reshape(k, MLA_HEADS, LANE), g[:, half:].reshape(k, MLA_HEADS, LANE)
    rope = g1[..., 64:96] + _swap_halves(g2[..., 64:96])
    return jnp.concatenate([g1[..., :64], rope], axis=-1).reshape(k, -1)


def _w_ukv_internal(w):
    k = w.shape[0]
    w3 = w.reshape(k, MLA_HEADS, QK_NOPE + V_HEAD)
    kn, vv = w3[..., :QK_NOPE], w3[..., QK_NOPE:]
    z = _zeros_like_cols(kn, 64)
    return jnp.concatenate([kn, z, vv, z], axis=-1).reshape(k, -1)


def _w_ukv_grad(g):
    k = g.shape[0]
    g3 = g.reshape(k, MLA_HEADS, 2 * LANE)
    return jnp.concatenate([g3[..., 0:64], g3[..., LANE:LANE + 64]], axis=-1).reshape(k, -1)


def _w_pa_internal(w):
    n = w.shape[1]
    w3 = w.reshape(MLA_HEADS, V_HEAD, n)
    return jnp.concatenate([w3, jnp.zeros_like(w3)], axis=1).reshape(-1, n)


def _w_pa_grad(g):
    n = g.shape[1]
    return g.reshape(MLA_HEADS, 2 * V_HEAD, n)[:, :V_HEAD].reshape(-1, n)


def _rope_tables(t):
    half = QK_ROPE // 2
    inv = ROPE_THETA ** (-jnp.arange(half, dtype=F32) / half)
    ang = jnp.arange(t, dtype=F32)[:, None] * inv[None, :]
    cos, sin = jnp.cos(ang), jnp.sin(ang)
    one, zero = jnp.ones((t, 64), F32), jnp.zeros((t, 64), F32)
    z32 = jnp.zeros((t, 32), F32)
    cq = jnp.concatenate([one, cos, cos, z32], axis=1)
    ck = jnp.concatenate([zero, cos, cos, z32], axis=1)
    sq = jnp.concatenate([zero, -sin, sin, z32], axis=1)
    return cq, ck, sq


def _residual_mm(name, a, w, h, alpha, next_norm_w):
    t, d = h.shape
    tm = _row_tile(t)
    rows = _tile_spec(tm, d)
    if next_norm_w is None:
        h2, = _mm(name, [a], [(0, w, 0)], [(h,) + rows], [((t, d), F32) + rows],
                  lambda accs, ex: (ex[0] + alpha * accs[0],), tm=tm, tn=d, n=d)
        return h2, None

    def fn(accs, ex):
        h2 = ex[0] + alpha * accs[0]
        return h2, _rms_parts(h2)[1] * ex[1]

    return _mm(name, [a], [(0, w, 0)], [(h,) + rows, (next_norm_w, (1, d), lambda i, j: (0, 0))],
               [((t, d), F32) + rows, ((t, d), BF16) + rows], fn, tm=tm, tn=d, n=d)


def _ffn_fwd(tag, h, xn, w_gu, w_down, next_norm_w):
    t, d = h.shape
    dff = w_down.shape[0]
    tm, tn = _row_tile(t), _tile(dff, 1408)

    def act_fn(accs, _):
        g, u = accs
        return g, u, g * _sigmoid(g) * u

    spec = _tile_spec(tm, tn)
    g, u, act = _mm(tag + "_gu", [xn], [(0, w_gu, 0), (0, w_gu, dff // tn)], [],
                    [((t, dff), BF16) + spec] * 3, act_fn, tm=tm, tn=tn, n=dff)
    h2, xn2 = _residual_mm(tag + "_down", act, w_down, h, 0.5, next_norm_w)
    return h2, xn2, (h, xn, g, u, act)


def _ffn_bwd(tag, dh2, saved, nw, w_gu_t, w_down_t):
    h, xn, g, u, act = saved
    t, d = h.shape
    dff = act.shape[1]
    tm, tn = _row_tile(t), _tile(dff, 1408)

    def dact_fn(accs, ex):
        gg, uu = ex[0].astype(F32), ex[1].astype(F32)
        da = 0.5 * accs[0]
        sg = _sigmoid(gg)
        return da * uu * (sg * (1.0 + gg * (1.0 - sg))), da * (gg * sg)

    spec = _tile_spec(tm, tn)
    dg, du = _mm(tag + "_dact", [dh2], [(0, w_down_t, 0)], [(g,) + spec, (u,) + spec],
                 [((t, dff), BF16) + spec] * 2, dact_fn, tm=tm, tn=tn, n=dff)
    dw_down = _mm_tn(tag + "_dwdown", act, dh2, alpha=0.5, out_dtype=BF16)
    tn2 = _tile(d, 512)
    dxn, = _mm(tag + "_dxn", [dg, du], [(0, w_gu_t[:dff], 0), (1, w_gu_t[dff:], 0)], [],
               [((t, d), F32) + _tile_spec(tm, tn2)], lambda accs, _: (accs[0] + accs[1],), tm=tm, tn=tn2, n=d)
    dw_gu = jnp.concatenate([_mm_tn(tag + "_dwg", xn, dg, alpha=1.0, out_dtype=BF16),
                             _mm_tn(tag + "_dwu", xn, du, alpha=1.0, out_dtype=BF16)], axis=1)
    dh, dnw = _rms_bwd(tag + "_dnorm", h, nw, dxn, dh2)
    return dh, dnw, dw_gu, dw_down


def _kv_pattern(kr):
    z = jnp.zeros_like(kr)
    return jnp.concatenate([kr, z] * MLA_HEADS, axis=1)


def _mix_fwd(h, u, p, tabs, lb, call, push, next_norm_w):
    t, d = h.shape
    tm = _row_tile(t)
    tnz = _tile(Z_W, 1024)
    z, = _mm("mix_in", [u], [(0, p["w_in"], 0)], [], [((t, Z_W), F32) + _tile_spec(tm, tnz)], lambda a, _: (a[0],),
             tm=tm, tn=tnz, n=Z_W)
    cqn, ckvn, q, kv, v_t = _mla_front(z, p["q_norm"], p["kv_norm"], tabs, p["w_uq"], p["w_ukv"])
    (o_a, lse), pushed = _attn_fwd(q, kv, v_t, push)
    o_b, st_saved = _hgrn_fwd(z, lb, p["hg_norm"], call)
    tn = _tile(d, 512)

    def merge_fn(accs, ex):
        ya, yb = accs
        return ya, yb, _sigmoid(ex[0]) * ya + _sigmoid(ex[1]) * yb

    spec = _tile_spec(tm, tn)
    ya, yb, merged = _mm("mix_merge", [o_a, o_b], [(0, p["w_pa"], 0), (1, p["w_pr"], 0)],
                         [(z,) + _tile_spec(tm, tn, Z_GA // tn), (z,) + _tile_spec(tm, tn, Z_GB // tn)],
                         [((t, d), BF16) + spec] * 3, merge_fn, tm=tm, tn=tn, n=d)
    h2, xn2 = _residual_mm("mix_out", merged, p["w_out"], h, 1.0, next_norm_w)
    return h2, xn2, (h, u, z, cqn, ckvn, q, kv, o_a, lse, o_b, st_saved, ya, yb, merged), pushed


def _mix_bwd(dh2, saved, p, tabs, lb, call, call_t, push):
    h, u, z, cqn, ckvn, q, kv, o_a, lse, o_b, st_saved, ya, yb, merged = saved
    t, d = h.shape
    tm = _row_tile(t)
    tn = _tile(d, 512)
    spec = _tile_spec(tm, tn)

    def dmerge_fn(accs, ex):
        dm = accs[0]
        yav, ybv = ex[0].astype(F32), ex[1].astype(F32)
        sa, sb = _sigmoid(ex[2]), _sigmoid(ex[3])
        return dm * sa, dm * sb, dm * yav * sa * (1.0 - sa), dm * ybv * sb * (1.0 - sb)

    dya, dyb, dga, dgb = _mm("mix_dmerge", [dh2], [(0, p["w_out_t"], 0)],
                             [(ya,) + spec, (yb,) + spec, (z,) + _tile_spec(tm, tn, Z_GA // tn),
                              (z,) + _tile_spec(tm, tn, Z_GB // tn)],
                             [((t, d), BF16) + spec] * 4, dmerge_fn, tm=tm, tn=tn, n=d)
    dw_out = _mm_tn("mix_dwout", merged, dh2, alpha=1.0, out_dtype=BF16)
    wq = MLA_HEADS * LANE
    do_a, = _mm("mix_doa", [dya], [(0, p["w_pa_t"], 0)], [], [((t, wq), BF16) + _tile_spec(tm, wq)], lambda a, _: (a[0],),
                tm=tm, tn=wq, n=wq)
    wr = HG_HEADS * LANE
    do_b, = _mm("mix_dob", [dyb], [(0, p["w_pr_t"], 0)], [], [((t, wr), BF16) + _tile_spec(tm, wr)], lambda a, _: (a[0],),
                tm=tm, tn=wr, n=wr)
    dw_pa = _mm_tn("mix_dwpa", o_a, dya, alpha=1.0, out_dtype=F32)
    dw_pr = _mm_tn("mix_dwpr", o_b, dyb, alpha=1.0, out_dtype=BF16)
    (dq, dkv), pushed = _attn_bwd(q, kv, o_a, do_a, lse, push)
    dz_mla, dw_uq, dw_ukv, dqn, dkvn = _mla_back(z, p["q_norm"], p["kv_norm"], tabs, cqn, ckvn, dq, dkv,
                                                 p["w_uq_t"], p["w_ukv_t"])
    dhq, dhf, dhi, dhg, dlb, dhgn = _hgrn_bwd(z, lb, p["hg_norm"], call, call_t, st_saved, do_b)
    dz = jnp.concatenate([dz_mla, dhq, dhf, dhi, dhg, dga, dgb], axis=1)
    du, = _mm("mix_du", [dz], [(0, p["w_in_t"], 0)], [], [((t, d), F32) + spec], lambda a, _: (a[0],), tm=tm, tn=tn, n=d)
    dw_in = _mm_tn("mix_dwin", u, dz, alpha=1.0, out_dtype=F32)
    dh, dmn = _rms_bwd("mix_dnorm", h, p["mix_norm"], du, dh2)
    grads = dict(mix_norm=dmn, q_norm=dqn, kv_norm=dkvn, hg_norm=dhgn, lb=dlb, w_in=_w_in_grad(dw_in), w_uq=_w_uq_grad(dw_uq),
                 w_ukv=_w_ukv_grad(dw_ukv), w_proj_attn=_w_pa_grad(dw_pa), w_proj_rec=dw_pr, w_out=dw_out)
    return dh, grads, pushed


SHARDED = ("ffn1_w_gu", "ffn1_w_down", "w_in", "w_uq", "w_ukv", "w_proj_attn", "w_proj_rec", "w_out", "ffn2_w_gu", "ffn2_w_down")
ROW_SHARDED = ("ffn1_w_down", "w_out", "ffn2_w_down")
SMALL = ("ffn1_norm", "mix_norm", "q_norm", "kv_norm", "hg_lb_raw", "hg_norm", "ffn2_norm", "final_norm")
WEIGHTS = ("meta_tokens", "ffn1_norm", "ffn1_w_gu", "ffn1_w_down", "mix_norm", "w_in", "q_norm", "kv_norm", "w_uq", "w_ukv",
           "hg_lb_raw", "hg_norm", "w_proj_attn", "w_proj_rec", "w_out", "ffn2_norm", "ffn2_w_gu", "ffn2_w_down", "final_norm")


def _pack_small(vals):
    flat = jnp.concatenate([vals[n].reshape(-1) for n in SMALL])
    return flat.reshape(-1, LANE)


def _unpack_small(packed, like):
    flat = packed.reshape(-1)
    out, off = {}, 0
    for n in SMALL:
        size = math.prod(like[n].shape)
        out[n] = flat[off:off + size].reshape(like[n].shape)
        off += size
    return out


def kernel(x, meta_tokens, ffn1_norm, ffn1_w_gu, ffn1_w_down, mix_norm, w_in, q_norm, kv_norm, w_uq, w_ukv, hg_lb_raw, hg_norm, w_proj_attn, w_proj_rec, w_out, ffn2_norm, ffn2_w_gu, ffn2_w_down, final_norm, loss_target, m_meta_tokens, m_ffn1_norm, m_ffn1_w_gu, m_ffn1_w_down, m_mix_norm, m_w_in, m_q_norm, m_kv_norm, m_w_uq, m_w_ukv, m_hg_lb_raw, m_hg_norm, m_w_proj_attn, m_w_proj_rec, m_w_out, m_ffn2_norm, m_ffn2_w_gu, m_ffn2_w_down, m_final_norm, v_meta_tokens, v_ffn1_norm, v_ffn1_w_gu, v_ffn1_w_down, v_mix_norm, v_w_in, v_q_norm, v_kv_norm, v_w_uq, v_w_ukv, v_hg_lb_raw, v_hg_norm, v_w_proj_attn, v_w_proj_rec, v_w_out, v_ffn2_norm, v_ffn2_w_gu, v_ffn2_w_down, v_final_norm):
    w = dict(meta_tokens=meta_tokens, ffn1_norm=ffn1_norm, ffn1_w_gu=ffn1_w_gu, ffn1_w_down=ffn1_w_down, mix_norm=mix_norm,
             w_in=w_in, q_norm=q_norm, kv_norm=kv_norm, w_uq=w_uq, w_ukv=w_ukv, hg_lb_raw=hg_lb_raw, hg_norm=hg_norm,
             w_proj_attn=w_proj_attn, w_proj_rec=w_proj_rec, w_out=w_out, ffn2_norm=ffn2_norm, ffn2_w_gu=ffn2_w_gu,
             ffn2_w_down=ffn2_w_down, final_norm=final_norm)
    mom = dict(meta_tokens=m_meta_tokens, ffn1_norm=m_ffn1_norm, ffn1_w_gu=m_ffn1_w_gu, ffn1_w_down=m_ffn1_w_down,
               mix_norm=m_mix_norm, w_in=m_w_in, q_norm=m_q_norm, kv_norm=m_kv_norm, w_uq=m_w_uq, w_ukv=m_w_ukv,
               hg_lb_raw=m_hg_lb_raw, hg_norm=m_hg_norm, w_proj_attn=m_w_proj_attn, w_proj_rec=m_w_proj_rec, w_out=m_w_out,
               ffn2_norm=m_ffn2_norm, ffn2_w_gu=m_ffn2_w_gu, ffn2_w_down=m_ffn2_w_down, final_norm=m_final_norm)
    var = dict(meta_tokens=v_meta_tokens, ffn1_norm=v_ffn1_norm, ffn1_w_gu=v_ffn1_w_gu, ffn1_w_down=v_ffn1_w_down,
               mix_norm=v_mix_norm, w_in=v_w_in, q_norm=v_q_norm, kv_norm=v_kv_norm, w_uq=v_w_uq, w_ukv=v_w_ukv,
               hg_lb_raw=v_hg_lb_raw, hg_norm=v_hg_norm, w_proj_attn=v_w_proj_attn, w_proj_rec=v_w_proj_rec, w_out=v_w_out,
               ffn2_norm=v_ffn2_norm, ffn2_w_gu=v_ffn2_w_gu, ffn2_w_down=v_ffn2_w_down, final_norm=v_final_norm)
    nl = ffn1_norm.shape[0]
    seq, d = x.shape[1], x.shape[2]
    n_real = N_META + seq
    t = -(-n_real // ROW_ALIGN) * ROW_ALIGN
    me = 4 * lax.axis_index("x") + 2 * lax.axis_index("y") + lax.axis_index("c")

    def own_shards(l):
        return [w[n][l].astype(BF16) for n in SHARDED]

    gathered = _exchange("gather_weights", own_shards(0) + [meta_tokens], gather=True)
    meta_full = _cols_full(gathered[-1])

    def layer_params(l, full):
        def mat(n):
            g = full[n]
            return g.reshape(-1, g.shape[-1]) if n in ROW_SHARDED else _cols_full(g)

        p = {}
        for tag in ("ffn1", "ffn2"):
            p[tag + "_w_gu"] = mat(tag + "_w_gu")
            p[tag + "_w_down"] = mat(tag + "_w_down")
            p[tag + "_w_gu_t"] = p[tag + "_w_gu"].T
            p[tag + "_w_down_t"] = p[tag + "_w_down"].T
            p[tag + "_norm"] = w[tag + "_norm"][l:l + 1]
        p["w_in"] = _w_in_internal(mat("w_in"))
        p["w_uq"] = _w_uq_internal(mat("w_uq"))
        p["w_ukv"] = _w_ukv_internal(mat("w_ukv"))
        p["w_pa"] = _w_pa_internal(mat("w_proj_attn"))
        p["w_pr"] = mat("w_proj_rec")
        p["w_out"] = mat("w_out")
        for n in ("w_in", "w_uq", "w_ukv", "w_pa", "w_pr", "w_out"):
            p[n + "_t"] = p[n].T
        for n in ("mix_norm", "q_norm", "kv_norm", "hg_norm"):
            p[n] = w[n][l:l + 1]
        return p

    tabs = _rope_tables(t)
    call, call_t = _hg_tables()
    lbs = _lb_fwd(hg_lb_raw)

    pad = jnp.zeros((t - n_real, d), F32)
    h = jnp.concatenate([meta_full, x[0], pad], axis=0)
    tgt = jnp.concatenate([jnp.zeros((N_META, d), F32), loss_target[0], pad], axis=0)
    saved, params = [], []
    full = dict(zip(SHARDED, gathered[:-1]))
    xn = _rms_fwd("first_norm", h, ffn1_norm[0:1])
    for l in range(nl):
        p = layer_params(l, full)
        params.append(p)
        h, xn, s1 = _ffn_fwd("ffn1", h, xn, p["ffn1_w_gu"], p["ffn1_w_down"], p["mix_norm"])
        push = _push_plan(own_shards(l + 1), gather=True) if l + 1 < nl else None
        h, xn, s2, pushed = _mix_fwd(h, xn, p, tabs, lbs[l:l + 1], call, push, p["ffn2_norm"])
        if pushed is not None:
            full = dict(zip(SHARDED, pushed))
        h, xn, s3 = _ffn_fwd("ffn2", h, xn, p["ffn2_w_gu"], p["ffn2_w_down"],
                             ffn1_norm[l + 1:l + 2] if l + 1 < nl else None)
        saved.append((s1, s2, s3))
    dh, d_final, loss_part = _loss_head(h, final_norm.reshape(1, d), tgt, n_real)
    loss = lax.psum(loss_part[0, 0], ("x", "y", "c"))

    def shards(gm):
        out = []
        for n in SHARDED:
            g = gm[n].astype(BF16)
            out.append(g.reshape(N_DEV, -1, g.shape[-1]) if n in ROW_SHARDED else _cols_shards(g))
        return out

    per_layer = []
    slots = [jnp.zeros((N_DEV,) + w[n].shape, BF16) for n in SHARDED]
    ready = None
    for l in reversed(range(nl)):
        p = params[l]
        s1, s2, s3 = saved[l]
        dh, dn2, dgu2, ddown2 = _ffn_bwd("ffn2", dh, s3, p["ffn2_norm"], p["ffn2_w_gu_t"], p["ffn2_w_down_t"])
        push = None if ready is None else _push_plan(ready, gather=False, bufs=slots, layer=l + 1)
        dh, gm, pushed = _mix_bwd(dh, s2, p, tabs, lbs[l:l + 1], call, call_t, push)
        if pushed is not None:
            slots = list(pushed)
        dh, dn1, dgu1, ddown1 = _ffn_bwd("ffn1", dh, s1, p["ffn1_norm"], p["ffn1_w_gu_t"], p["ffn1_w_down_t"])
        gm.update(ffn1_norm=dn1, ffn2_norm=dn2, ffn1_w_gu=dgu1, ffn2_w_gu=dgu2, ffn1_w_down=ddown1, ffn2_w_down=ddown2)
        per_layer.append(gm)
        ready = shards(gm)
    per_layer.reverse()
    grad_x = dh[N_META:n_real][None]

    slots = _exchange("scatter_grads", ready, gather=False, bufs=slots, layer=0)
    grads, delta, new_m, new_v = {}, {}, {}, {}
    for n, s in zip(SHARDED, slots):
        grads[n], delta[n], new_m[n], new_v[n] = _adam_sharded("adam_" + n, s, w[n], mom[n], var[n])

    small = {n: jnp.concatenate([gm[n] for gm in per_layer], axis=0) for n in SMALL if n not in ("hg_lb_raw", "final_norm")}
    small["hg_lb_raw"] = _lb_bwd(hg_lb_raw, jnp.concatenate([gm["lb"] for gm in per_layer], axis=0))
    small["final_norm"] = d_final
    packed = jnp.concatenate([_pack_small(small), dh[:N_META].reshape(-1, LANE)], axis=0)
    summed = _sum_replicated(_exchange("gather_small", [packed], gather=True)[0])
    n_small = packed.shape[0] - N_META * d // LANE
    sd, sm, sv = _adam_small("adam_small", summed[:n_small], _pack_small(w), _pack_small(mom), _pack_small(var))
    grads.update(_unpack_small(summed[:n_small], w))
    delta.update(_unpack_small(sd, w))
    new_m.update(_unpack_small(sm, w))
    new_v.update(_unpack_small(sv, w))
    dmeta = lax.dynamic_slice_in_dim(summed[n_small:].reshape(N_META, d), me * (d // N_DEV), d // N_DEV, axis=1)
    grads["meta_tokens"] = dmeta
    delta["meta_tokens"], new_m["meta_tokens"], new_v["meta_tokens"] = _adam_small(
        "adam_meta", dmeta, meta_tokens, m_meta_tokens, v_meta_tokens)

    return (loss, grad_x, *[grads[n] for n in WEIGHTS], *[delta[n] for n in WEIGHTS], *[new_m[n] for n in WEIGHTS],
            *[new_v[n] for n in WEIGHTS])
```

```python
import functools
import math

import jax
import jax.numpy as jnp
from jax import lax
from jax.experimental import pallas as pl
from jax.experimental.pallas import tpu as pltpu

F32 = jnp.float32
BF16 = jnp.bfloat16

N_DEV = 8
N_META = 16
MLA_HEADS = 8
Q_LORA = 384
KV_LORA = 256
QK_NOPE = 64
QK_ROPE = 32
V_HEAD = 64
ROPE_THETA = 10000.0
HG_HEADS = 4
HG_DIM = 128
EPS = 1e-6
NEG_BIG = -1e30
F_MIN = 1e-20
ADAM_LR = 0.001
ADAM_B1 = 0.9
ADAM_B2 = 0.999
ADAM_EPS = 1e-08
ADAM_WD = 0.01
ADAM_STEP = 10

LANE = 128
ROW_ALIGN = 256
HG_CHUNK = 128
HG_LEVELS = (64, 32, 16, 8, 4, 2, 1)
VMEM_LIMIT = 48 * 1024 * 1024

Z_CQ, Z_CKV, Z_KPE, Z_KPESW, Z_HQ, Z_HF, Z_HI, Z_HG, Z_GA, Z_GB, Z_W = 0, 384, 640, 768, 1024, 1536, 2048, 2560, 3072, 4096, 5120
ATTN_SCALE = float((QK_NOPE + QK_ROPE) ** -0.5)
LOG2E = 1.4426950408889634
ATTN_C2 = ATTN_SCALE * LOG2E
HG_SCALE = float(HG_DIM ** -0.5)


def _cparams(sem):
    return pltpu.CompilerParams(dimension_semantics=sem, vmem_limit_bytes=VMEM_LIMIT)


def _tile(n, cap):
    if n <= cap:
        return n
    best = None
    for t in range(LANE, cap + 1, LANE):
        if n % t == 0:
            best = t
    assert best is not None, (n, cap)
    return best


def _row_tile(m):
    for t in (384, 256, 128):
        if m % t == 0:
            return t
    raise ValueError(m)


def _bf(x):
    return x.astype(BF16)


def _dot(a, b):
    return jnp.dot(a, b, preferred_element_type=F32)


def _dot_nt(a, b):
    return lax.dot_general(a, b, (((1,), (1,)), ((), ())), preferred_element_type=F32)


def _dot_tn(a, b):
    return lax.dot_general(a, b, (((0,), (0,)), ((), ())), preferred_element_type=F32)


def _sigmoid(x):
    return 1.0 / (1.0 + jnp.exp(-x))


def _mm(name, a_list, pairs, extras, outs, fn, *, tm, tn, n, push=None):
    m = a_list[0].shape[0]
    na, nb, ne, no = len(a_list), len(pairs), len(extras), len(outs)

    def body(*refs):
        a_refs = refs[:na]
        b_refs = refs[na:na + nb]
        e_refs = refs[na + nb:na + nb + ne]
        o_refs = refs[na + nb + ne:]
        a_vals = [_bf(r[...]) for r in a_refs]
        accs = [_dot(a_vals[ai], b_refs[k][...]) for k, (ai, _, _) in enumerate(pairs)]
        res = fn(accs, [r[...] for r in e_refs])
        for r, v in zip(o_refs, res):
            r[...] = v.astype(r.dtype)

    def spec(block_shape, index_map):
        return pl.BlockSpec(block_shape, functools.partial(lambda j, i, im: im(i, j), im=index_map))

    in_specs = [spec((tm, a.shape[1]), lambda i, j: (i, 0)) for a in a_list]
    for _, b, off in pairs:
        in_specs.append(spec((b.shape[0], tn), functools.partial(lambda i, j, off: (0, j + off), off=off)))
    in_specs += [spec(bs, im) for _, bs, im in extras]
    out_specs = [spec(bs, im) for _, _, bs, im in outs]
    out_shape = [jax.ShapeDtypeStruct(s, d) for s, d, _, _ in outs]
    args = (*a_list, *[b for _, b, _ in pairs], *[e for e, _, _ in extras])
    if push is not None:
        return _call_carrying_push(body, push, name=name, grid=(n // tn, m // tm), in_specs=in_specs, out_specs=out_specs,
                                   out_shape=out_shape, scratch_shapes=[], args=args)
    return pl.pallas_call(
        body, name=name, grid=(n // tn, m // tm), in_specs=in_specs, out_specs=out_specs, out_shape=out_shape,
        compiler_params=_cparams(("parallel", "parallel")),
    )(*args)


def _tile_spec(tm, tn, col_off=0):
    return (tm, tn), functools.partial(lambda i, j, off: (i, j + off), off=col_off)


def _mm_tn(name, a, b, *, alpha, out_dtype):
    t, k = a.shape
    n = b.shape[1]
    tk, tn = _tile(k, 1408), _tile(n, 1408)
    tt = next(c for c in (768, 512, 256) if t % c == 0)
    nt = t // tt

    def body(a_ref, b_ref, o_ref, acc_ref):
        s = pl.program_id(2)

        @pl.when(s == 0)
        def _():
            acc_ref[...] = jnp.zeros_like(acc_ref)

        acc_ref[...] += _dot_tn(_bf(a_ref[...]), _bf(b_ref[...]))

        @pl.when(s == nt - 1)
        def _():
            o_ref[...] = (alpha * acc_ref[...]).astype(o_ref.dtype)

    return pl.pallas_call(
        body, name=name, grid=(k // tk, n // tn, nt),
        in_specs=[pl.BlockSpec((tt, tk), lambda i, j, s: (s, i)), pl.BlockSpec((tt, tn), lambda i, j, s: (s, j))],
        out_specs=pl.BlockSpec((tk, tn), lambda i, j, s: (i, j)),
        out_shape=jax.ShapeDtypeStruct((k, n), out_dtype),
        scratch_shapes=[pltpu.VMEM((tk, tn), F32)],
        compiler_params=_cparams(("parallel", "parallel", "arbitrary")),
    )(a, b)


def _rms_parts(x):
    r = lax.rsqrt(jnp.mean(x * x, axis=-1, keepdims=True) + EPS)
    return r, x * r


def _rms_bwd_math(x, w, dxn):
    r, xhat = _rms_parts(x)
    t = dxn * w
    dx = r * (t - xhat * jnp.mean(t * xhat, axis=-1, keepdims=True))
    dw = jnp.sum(dxn * xhat, axis=0, keepdims=True)
    return dx, dw


def _rms_fwd(name, h, w):
    t, d = h.shape
    tm = _row_tile(t)

    def body(h_ref, w_ref, o_ref):
        _, xhat = _rms_parts(h_ref[...])
        o_ref[...] = (xhat * w_ref[...]).astype(o_ref.dtype)

    return pl.pallas_call(
        body, name=name, grid=(t // tm,),
        in_specs=[pl.BlockSpec((tm, d), lambda i: (i, 0)), pl.BlockSpec((1, d), lambda i: (0, 0))],
        out_specs=pl.BlockSpec((tm, d), lambda i: (i, 0)),
        out_shape=jax.ShapeDtypeStruct((t, d), BF16),
        compiler_params=_cparams(("parallel",)),
    )(h, w)


def _rms_bwd(name, h, w, dxn, dh_in):
    t, d = h.shape
    tm = _row_tile(t)

    def body(h_ref, w_ref, dxn_ref, dh_ref, o_ref, dw_ref):
        dx, dw = _rms_bwd_math(h_ref[...], w_ref[...], dxn_ref[...])
        o_ref[...] = dh_ref[...] + dx

        @pl.when(pl.program_id(0) == 0)
        def _():
            dw_ref[...] = jnp.zeros_like(dw_ref)

        dw_ref[...] += dw

    row = pl.BlockSpec((tm, d), lambda i: (i, 0))
    vec = pl.BlockSpec((1, d), lambda i: (0, 0))
    return pl.pallas_call(
        body, name=name, grid=(t // tm,),
        in_specs=[row, vec, row, row],
        out_specs=[row, vec],
        out_shape=[jax.ShapeDtypeStruct((t, d), F32), jax.ShapeDtypeStruct((1, d), F32)],
        compiler_params=_cparams(("arbitrary",)),
    )(h, w, dxn, dh_in)


def _loss_head(h, w, tgt, n_real):
    t, d = h.shape
    tm = _row_tile(t)

    def body(h_ref, w_ref, t_ref, dh_ref, dw_ref, loss_ref):
        i = pl.program_id(0)
        x = h_ref[...]
        wv = w_ref[...]
        _, xhat = _rms_parts(x)
        rows = i * tm + lax.broadcasted_iota(jnp.int32, (tm, 1), 0)
        valid = (rows >= N_META) & (rows < n_real)
        e = jnp.where(valid, xhat * wv - t_ref[...], 0.0)
        dx, dw = _rms_bwd_math(x, wv, e * (1.0 / d))
        dh_ref[...] = dx

        @pl.when(i == 0)
        def _():
            dw_ref[...] = jnp.zeros_like(dw_ref)
            loss_ref[...] = jnp.zeros_like(loss_ref)

        dw_ref[...] += dw
        loss_ref[...] += (0.5 / d) * jnp.sum(jnp.sum(e * e, axis=-1, keepdims=True), axis=0, keepdims=True)

    row = pl.BlockSpec((tm, d), lambda i: (i, 0))
    vec = pl.BlockSpec((1, d), lambda i: (0, 0))
    return pl.pallas_call(
        body, name="loss_head", grid=(t // tm,),
        in_specs=[row, vec, row],
        out_specs=[row, vec, pl.BlockSpec((1, 1), lambda i: (0, 0))],
        out_shape=[jax.ShapeDtypeStruct((t, d), F32), jax.ShapeDtypeStruct((1, d), F32), jax.ShapeDtypeStruct((1, 1), F32)],
        compiler_params=_cparams(("arbitrary",)),
    )(h, w, tgt)


def _heads(x):
    return jnp.concatenate([x] * MLA_HEADS, axis=1)


def _mla_front(z, qn_w, kvn_w, tabs, w_uq, w_ukv):
    t = z.shape[0]
    tm = _attn_block(t)
    wq = MLA_HEADS * LANE

    def body(z_ref, qw_ref, kw_ref, cq_ref, ck_ref, sq_ref, wuq_ref, wukv_ref, cqn_ref, ckvn_ref, q_ref, kv_ref, vt_ref):
        zz = z_ref[...]
        _, qhat = _rms_parts(zz[:, Z_CQ:Z_CKV])
        _, khat = _rms_parts(zz[:, Z_CKV:Z_KPE])
        cqn = (qhat * qw_ref[...]).astype(BF16)
        ckvn = (khat * kw_ref[...]).astype(BF16)
        cqn_ref[...] = cqn
        ckvn_ref[...] = ckvn
        krot = zz[:, Z_KPE:Z_KPESW] * ck_ref[...] + zz[:, Z_KPESW:Z_KPESW + LANE] * sq_ref[...]
        qq = _dot(cqn, wuq_ref[...])
        q_ref[...] = (qq[:, :wq] * _heads(cq_ref[...]) + qq[:, wq:] * _heads(sq_ref[...])).astype(BF16)
        kvv = _dot(ckvn, wukv_ref[...]) + _kv_pattern(krot)
        kv_ref[...] = kvv.astype(BF16)
        for hd in range(MLA_HEADS):
            vt_ref[hd, 0] = kvv[:, 2 * LANE * hd + LANE:2 * LANE * (hd + 1)].T.astype(BF16)

    def rows(wd):
        return pl.BlockSpec((tm, wd), lambda i: (i, 0))

    def whole(a):
        return pl.BlockSpec(a.shape, lambda i: (0, 0))

    return pl.pallas_call(
        body, name="mla_front", grid=(t // tm,),
        in_specs=[rows(Z_HQ), whole(qn_w), whole(kvn_w), rows(LANE), rows(LANE), rows(LANE), whole(w_uq), whole(w_ukv)],
        out_specs=[rows(Q_LORA), rows(KV_LORA), rows(wq), rows(2 * wq),
                   pl.BlockSpec((MLA_HEADS, 1, LANE, tm), lambda i: (0, i, 0, 0))],
        out_shape=[jax.ShapeDtypeStruct((t, Q_LORA), BF16), jax.ShapeDtypeStruct((t, KV_LORA), BF16),
                   jax.ShapeDtypeStruct((t, wq), BF16), jax.ShapeDtypeStruct((t, 2 * wq), BF16),
                   jax.ShapeDtypeStruct((MLA_HEADS, t // tm, LANE, tm), BF16)],
        compiler_params=_cparams(("parallel",)),
    )(z, qn_w, kvn_w, *tabs, w_uq, w_ukv)


def _mla_back(z, qn_w, kvn_w, tabs, cqn, ckvn, dq, dkv, w_uq_t, w_ukv_t):
    t = z.shape[0]
    tm = next(c for c in (768, 512, 256) if t % c == 0)
    wq = MLA_HEADS * LANE

    def body(z_ref, qw_ref, kw_ref, cq_ref, ck_ref, sq_ref, cqn_ref, ckvn_ref, dq_ref, dkv_ref, wuqt_ref, wukvt_ref,
             dz_ref, dwuq_ref, dwukv_ref, dqw_ref, dkw_ref):
        @pl.when(pl.program_id(0) == 0)
        def _():
            for r in (dwuq_ref, dwukv_ref, dqw_ref, dkw_ref):
                r[...] = jnp.zeros_like(r)

        zz = z_ref[...]
        d = dq_ref[...] * ATTN_SCALE
        dqq = jnp.concatenate([d * _heads(cq_ref[...]), d * _heads(sq_ref[...])], axis=1).astype(BF16)
        dkv_v = dkv_ref[...]
        dwuq_ref[...] += _dot_tn(cqn_ref[...], dqq)
        dwukv_ref[...] += _dot_tn(ckvn_ref[...], dkv_v)
        dcq, dqw = _rms_bwd_math(zz[:, Z_CQ:Z_CKV], qw_ref[...], _dot(dqq, wuqt_ref[...]))
        dckv, dkw = _rms_bwd_math(zz[:, Z_CKV:Z_KPE], kw_ref[...], _dot(dkv_v, wukvt_ref[...]))
        dkr = jnp.zeros((tm, LANE), F32)
        for hd in range(MLA_HEADS):
            dkr = dkr + dkv_v[:, 2 * LANE * hd:2 * LANE * hd + LANE].astype(F32)
        dz_ref[...] = jnp.concatenate(
            [dcq, dckv, dkr * ck_ref[...], dkr * sq_ref[...], jnp.zeros((tm, Z_HQ - Z_KPESW - LANE), F32)], axis=1
        ).astype(BF16)
        dqw_ref[...] += dqw
        dkw_ref[...] += dkw

    def rows(wd):
        return pl.BlockSpec((tm, wd), lambda i: (i, 0))

    def whole(shape):
        return pl.BlockSpec(shape, lambda i: (0, 0))

    return pl.pallas_call(
        body, name="mla_back", grid=(t // tm,),
        in_specs=[rows(Z_HQ), whole(qn_w.shape), whole(kvn_w.shape), rows(LANE), rows(LANE), rows(LANE), rows(Q_LORA),
                  rows(KV_LORA), rows(wq), rows(2 * wq), whole(w_uq_t.shape), whole(w_ukv_t.shape)],
        out_specs=[rows(Z_HQ), whole((Q_LORA, 2 * wq)), whole((KV_LORA, 2 * wq)), whole((1, Q_LORA)), whole((1, KV_LORA))],
        out_shape=[jax.ShapeDtypeStruct((t, Z_HQ), BF16), jax.ShapeDtypeStruct((Q_LORA, 2 * wq), F32),
                   jax.ShapeDtypeStruct((KV_LORA, 2 * wq), F32), jax.ShapeDtypeStruct((1, Q_LORA), F32),
                   jax.ShapeDtypeStruct((1, KV_LORA), F32)],
        compiler_params=_cparams(("arbitrary",)),
    )(z, qn_w, kvn_w, *tabs, cqn, ckvn, dq, dkv, w_uq_t, w_ukv_t)


def _attn_block(t):
    for b in (768, 512, 256):
        if t % b == 0:
            return b
    raise ValueError(t)


def _call_carrying_push(body, push, *, name, grid, in_specs, out_specs, out_shape, scratch_shapes, args):
    if push is None:
        outs = pl.pallas_call(body, name=name, grid=grid, in_specs=in_specs, out_specs=out_specs, out_shape=out_shape,
                              scratch_shapes=scratch_shapes, compiler_params=_cparams(("parallel", "arbitrary")))(*args)
        return outs, None
    n_in, n_out, n_scr, n_pin, nk = len(in_specs), len(out_specs), len(scratch_shapes), len(push["ins"]), push["nk"]

    def carrying(*refs):
        o0 = n_in + n_pin
        s0 = o0 + n_out + nk
        pins, pouts, sems = refs[n_in:o0], refs[o0 + n_out:s0], refs[s0 + n_scr:]
        a, b = pl.program_id(0), pl.program_id(1)

        @pl.when((a == 0) & (b == 0))
        def _():
            for cp in _push_copies(push, pins, pouts, sems):
                cp.start()

        body(*refs[:n_in], *refs[o0:o0 + n_out], *refs[s0:s0 + n_scr])

        @pl.when((a == grid[0] - 1) & (b == grid[1] - 1))
        def _():
            for cp in _push_copies(push, pins, pouts, sems):
                cp.wait()

    outs = pl.pallas_call(
        carrying, name=name + "_push", grid=grid, in_specs=list(in_specs) + push["in_specs"],
        out_specs=list(out_specs) + push["out_specs"], out_shape=list(out_shape) + push["outs"],
        scratch_shapes=list(scratch_shapes) + push["sems"], input_output_aliases=_push_aliases(push, n_in, n_out),
        compiler_params=_cparams(("arbitrary", "arbitrary")))(*args, *push["ins"])
    return outs[:n_out], outs[n_out:]


def _attn_fwd(q, kv, v_t, push=None):
    t = q.shape[0]
    bq = bk = _attn_block(t)
    nq = t // bq

    def body(q_ref, k_ref, vt_ref, o_ref, lse_ref):
        i = pl.program_id(1)
        qv = q_ref[...]
        qpos = i * bq + lax.broadcasted_iota(jnp.int32, (1, bq), 1)

        def block(j, carry, masked):
            m, l, acc = carry
            s = _dot_nt(k_ref[pl.ds(pl.multiple_of(j * bk, bk), bk), :], qv)
            if masked:
                kpos = j * bk + lax.broadcasted_iota(jnp.int32, (bk, 1), 0)
                s = jnp.where(kpos <= qpos, s, NEG_BIG)
            m_new = jnp.maximum(m, jnp.max(s, axis=0, keepdims=True))
            p = jnp.exp2((s - m_new) * ATTN_C2)
            a = jnp.exp2((m - m_new) * ATTN_C2)
            return m_new, a * l + jnp.sum(p, axis=0, keepdims=True), a * acc + _dot(vt_ref[0, j], _bf(p))

        init = (jnp.full((1, bq), NEG_BIG, F32), jnp.zeros((1, bq), F32), jnp.zeros((LANE, bq), F32))
        carry = lax.fori_loop(0, i, functools.partial(block, masked=False), init)
        m, l, acc = block(i, carry, True)
        o_ref[...] = (acc / l).T.astype(o_ref.dtype)
        lse_ref[0, 0] = m * ATTN_C2 + jnp.log(l) * LOG2E

    return _call_carrying_push(
        body, push, name="attn_fwd", grid=(MLA_HEADS, nq),
        in_specs=[pl.BlockSpec((bq, LANE), lambda h, i: (i, h)),
                  pl.BlockSpec((t, LANE), lambda h, i: (0, 2 * h)),
                  pl.BlockSpec((1, nq, LANE, bk), lambda h, i: (h, 0, 0, 0))],
        out_specs=[pl.BlockSpec((bq, LANE), lambda h, i: (i, h)),
                   pl.BlockSpec((1, 1, 1, bq), lambda h, i: (h, i, 0, 0))],
        out_shape=[jax.ShapeDtypeStruct((t, MLA_HEADS * LANE), BF16), jax.ShapeDtypeStruct((MLA_HEADS, nq, 1, bq), F32)],
        scratch_shapes=[], args=(q, kv, v_t))


def _attn_bwd(q, kv, o, do, lse, push=None):
    t = q.shape[0]
    bk = bw = _attn_block(t)
    nk, nw = t // bk, t // bw

    def body(q_ref, o_ref, do_ref, k_ref, v_ref, lse_ref, dq_ref, dkv_ref, dl_ref):
        j = pl.program_id(1)

        @pl.when(j == 0)
        def _():
            dq_ref[...] = jnp.zeros_like(dq_ref)
            for i in range(nw):
                rows = slice(i * bw, (i + 1) * bw)
                d = jnp.sum(o_ref[rows, :].astype(F32) * do_ref[rows, :].astype(F32), axis=1, keepdims=True)
                dl_ref[i] = jnp.broadcast_to(d, (bw, LANE)).T[0:1, :]

        kb = k_ref[...]
        vb = v_ref[...]
        kpos = j * bk + lax.broadcasted_iota(jnp.int32, (bk, 1), 0)

        def block(i, carry, masked):
            dk, dv = carry
            rows = pl.ds(pl.multiple_of(i * bw, bw), bw)
            qb = q_ref[rows, :]
            dob = do_ref[rows, :]
            pt = jnp.exp2(_dot_nt(kb, qb) * ATTN_C2 - lse_ref[0, i])
            if masked:
                qpos = i * bw + lax.broadcasted_iota(jnp.int32, (1, bw), 1)
                pt = jnp.where(kpos <= qpos, pt, 0.0)
            dv = dv + _dot(_bf(pt), dob)
            dst = _bf(pt * (_dot_nt(vb, dob) - dl_ref[i]))
            dk = dk + _dot(dst, qb)
            dq_ref[rows, :] += _dot_tn(dst, kb)
            return dk, dv

        i0 = j
        carry = block(i0, (jnp.zeros((bk, LANE), F32), jnp.zeros((bk, LANE), F32)), True)
        dk, dv = lax.fori_loop(i0 + 1, nw, functools.partial(block, masked=False), carry)
        dkv_ref[...] = jnp.concatenate([dk * ATTN_SCALE, dv], axis=1).astype(dkv_ref.dtype)

    head_rows = pl.BlockSpec((t, LANE), lambda h, j: (0, h))
    return _call_carrying_push(
        body, push, name="attn_bwd", grid=(MLA_HEADS, nk),
        in_specs=[head_rows, head_rows, head_rows,
                  pl.BlockSpec((bk, LANE), lambda h, j: (j, 2 * h)), pl.BlockSpec((bk, LANE), lambda h, j: (j, 2 * h + 1)),
                  pl.BlockSpec((1, nw, 1, bw), lambda h, j: (h, 0, 0, 0))],
        out_specs=[head_rows, pl.BlockSpec((bk, 2 * LANE), lambda h, j: (j, h))],
        out_shape=[jax.ShapeDtypeStruct((t, MLA_HEADS * LANE), F32), jax.ShapeDtypeStruct((t, 2 * MLA_HEADS * LANE), BF16)],
        scratch_shapes=[pltpu.VMEM((nw, 1, bw), F32)], args=(q, o, do, kv, kv, lse))


def _hg_tables():
    c = HG_CHUNK
    tri = (jnp.arange(c)[:, None] >= jnp.arange(c)[None, :]).astype(BF16)
    return tri, tri.T


def _hg_level_ref(b, m):
    c = HG_CHUNK
    if 2 * m >= 8:
        x = b.reshape(c // (2 * m), 2 * m, c)
        return jnp.broadcast_to(x[:, m - 1:m, :], x.shape).reshape(c, c)
    row = lax.broadcasted_iota(jnp.int32, (c, 1), 0)
    if m == 2:
        pos = row & 3
        return jnp.where(pos == 0, pltpu.roll(b, c - 1, 0),
                         jnp.where(pos == 1, b, jnp.where(pos == 2, pltpu.roll(b, 1, 0), pltpu.roll(b, 2, 0))))
    return jnp.where((row & 1) == 0, b, pltpu.roll(b, 1, 0))


def _hg_level_ref_t(d, m):
    c = HG_CHUNK
    row = lax.broadcasted_iota(jnp.int32, (c, 1), 0)
    if 2 * m >= 8:
        x = d.reshape(c // (2 * m), 2 * m, c)
        s = jnp.broadcast_to(jnp.sum(x, axis=1, keepdims=True), x.shape).reshape(c, c)
        return jnp.where((row & (2 * m - 1)) == m - 1, s, 0.0)
    if m == 2:
        s = pltpu.roll(d, 1, 0) + d + pltpu.roll(d, c - 1, 0) + pltpu.roll(d, c - 2, 0)
        return jnp.where((row & 3) == 1, s, 0.0)
    return jnp.where((row & 1) == 0, d + pltpu.roll(d, c - 1, 0), 0.0)


def _table_dot(table, x):
    hi = _bf(x)
    rest = x - hi.astype(F32)
    mid = _bf(rest)
    lo = _bf(rest - mid.astype(F32))
    out = _dot(table, jnp.concatenate([hi, mid, lo], axis=1))
    n = x.shape[1]
    return out[:, 0:n] + out[:, n:2 * n] + out[:, 2 * n:3 * n]


def _hg_gates(hq, hf, lb):
    sg = _sigmoid(hf)
    sn = _sigmoid(-hf)
    f = lb + (1.0 - lb) * sg
    q = hq * _sigmoid(hq)
    g = jnp.log(jnp.maximum(f, F_MIN))
    k = (1.0 - lb) * sn
    return q, k, g, f, sg, sn


def _hg_level_masks(m):
    c = HG_CHUNK
    row = lax.broadcasted_iota(jnp.int32, (c, 1), 0)
    col = lax.broadcasted_iota(jnp.int32, (1, c), 1)
    shift = (2 * m).bit_length() - 1
    up = (row & m) != 0
    same = lax.shift_right_logical(row, shift) == lax.shift_right_logical(col, shift)
    return up, same


def _hg_level_factors(b, bref, up):
    arg = b - bref
    e = jnp.exp(jnp.where(up, arg, -arg))
    return jnp.where(up, e, 0.0), jnp.where(up, 0.0, e)


def _hg_intra(q, k, b):
    c = HG_CHUNK
    row = lax.broadcasted_iota(jnp.int32, (c, 1), 0)
    col = lax.broadcasted_iota(jnp.int32, (1, c), 1)
    a = jnp.where(row == col, _dot_nt(_bf(q), _bf(k)), 0.0)
    parts = []
    for m in HG_LEVELS:
        up, same = _hg_level_masks(m)
        eq, ek = _hg_level_factors(b, _hg_level_ref(b, m), up)
        qt, kt = q * eq, k * ek
        a = a + jnp.where(same, _dot_nt(_bf(qt), _bf(kt)), 0.0)
        parts.append((eq, ek, qt, kt))
    return a, parts


def _hg_chunk_fwd(hq, hf, hi, hg, lb, nw, st, call):
    q, k, g, _, _, _ = _hg_gates(hq, hf, lb)
    b = _table_dot(call, g)
    a, _ = _hg_intra(q, k, b)
    v16 = _bf(hi)
    o = _dot(_bf(a * HG_SCALE), v16) + _dot_nt(_bf(q * jnp.exp(b) * HG_SCALE), _bf(st))
    bl = b[HG_CHUNK - 1:HG_CHUNK]
    ke = k * jnp.exp(bl - b)
    st_new = st * jnp.exp(bl) + _dot(_bf(hi.T), _bf(ke))
    r = lax.rsqrt(jnp.mean(o * o, axis=-1, keepdims=True) + EPS)
    y = o * r * nw * (hg * _sigmoid(hg))
    return y, st_new


def _hg_chunk_bwd(hq, hf, hi, hg, lb, nw, st, call, call_t, dy, dst_new):
    c = HG_CHUNK
    q, k, g, f, sg, sn = _hg_gates(hq, hf, lb)
    b = _table_dot(call, g)
    a, parts = _hg_intra(q, k, b)
    v16 = _bf(hi)
    st16 = _bf(st)
    eb = jnp.exp(b)
    qe = q * eb * HG_SCALE
    a16 = _bf(a * HG_SCALE)
    o = _dot(a16, v16) + _dot_nt(_bf(qe), st16)
    bl = b[c - 1:c]
    el = jnp.exp(bl)
    x = jnp.exp(bl - b)
    ke = k * x
    r = lax.rsqrt(jnp.mean(o * o, axis=-1, keepdims=True) + EPS)
    shg = _sigmoid(hg)
    gate = hg * shg
    ohat = o * r
    don = dy * gate
    dhg = dy * ohat * nw * (shg * (1.0 + hg * (1.0 - shg)))
    dnw = jnp.sum(don * ohat, axis=0, keepdims=True)
    tt = don * nw
    do = r * (tt - ohat * jnp.mean(tt * ohat, axis=-1, keepdims=True))
    do16 = _bf(do)
    dst16 = _bf(dst_new)
    da = _dot_nt(do16, v16) * HG_SCALE
    dv = _dot(_bf(a16.astype(F32).T), do16) + _dot_nt(_bf(ke), dst16)
    dqe = _dot(do16, st16)
    dke = _dot(v16, dst16)
    dst = dst_new * el + _dot(_bf(do.T), _bf(qe))
    dbl = jnp.sum(dst_new * st, axis=0, keepdims=True) * el
    dk = dke * x
    dxa = dke * ke
    db = dqe * qe - dxa
    dbl = dbl + jnp.sum(dxa, axis=0, keepdims=True)
    dq = dqe * eb * HG_SCALE
    row = lax.broadcasted_iota(jnp.int32, (c, 1), 0)
    col = lax.broadcasted_iota(jnp.int32, (1, c), 1)
    ddiag = jnp.sum(jnp.where(row == col, da, 0.0), axis=1, keepdims=True)
    dq = dq + ddiag * k
    dk = dk + ddiag * q
    for (eq, ek, qt, kt), m in zip(parts, HG_LEVELS):
        _, same = _hg_level_masks(m)
        gm = jnp.where(same, da, 0.0)
        dqt = _dot(_bf(gm), _bf(kt))
        dkt = _dot(_bf(gm.T), _bf(qt))
        dq = dq + dqt * eq
        dk = dk + dkt * ek
        darg = dqt * qt - dkt * kt
        db = db + darg - _hg_level_ref_t(darg, m)
    db = db + jnp.where(row == c - 1, dbl, 0.0)
    dg = _table_dot(call_t, db)
    shq = _sigmoid(hq)
    dhq = dq * (shq * (1.0 + hq * (1.0 - shq)))
    df = jnp.where(f > F_MIN, dg / jnp.maximum(f, F_MIN), 0.0)
    dlb = jnp.sum(df * (1.0 - sg) - dk * sn, axis=0, keepdims=True)
    dhf = df * (1.0 - lb) * sg * (1.0 - sg) - dk * (1.0 - lb) * sn * (1.0 - sn)
    return dhq, dhf, dv, dhg, dlb, dnw, dst


def _hg_chunks_per_step(t):
    return 6 if t % (6 * HG_CHUNK) == 0 else 2


def _hg_col(group, h):
    return group // LANE + h


def _hgrn_fwd(z, lb, nw, call, push=None):
    t = z.shape[0]
    c, cs = HG_CHUNK, _hg_chunks_per_step(t)
    rows = c * cs
    nsteps = t // rows

    def body(hq_ref, hf_ref, hi_ref, hg_ref, lb_ref, nw_ref, call_ref, y_ref, sv_ref, st_ref):
        @pl.when(pl.program_id(1) == 0)
        def _():
            st_ref[...] = jnp.zeros_like(st_ref)

        for u in range(cs):
            sl = slice(u * c, (u + 1) * c)
            st = st_ref[...]
            sv_ref[0, u] = st
            y, st_new = _hg_chunk_fwd(hq_ref[sl, :], hf_ref[sl, :], hi_ref[sl, :], hg_ref[sl, :], lb_ref[...], nw_ref[...],
                                      st, call_ref[...])
            y_ref[sl, :] = y.astype(y_ref.dtype)
            st_ref[...] = st_new

    def zcol(group):
        return pl.BlockSpec((rows, LANE), functools.partial(lambda h, i, g: (i, _hg_col(g, h)), g=group))

    ncall = call.shape[0]
    return _call_carrying_push(
        body, push, name="hgrn_fwd", grid=(HG_HEADS, nsteps),
        in_specs=[zcol(Z_HQ), zcol(Z_HF), zcol(Z_HI), zcol(Z_HG),
                  pl.BlockSpec((1, LANE), lambda h, i: (0, h)), pl.BlockSpec((1, LANE), lambda h, i: (0, 0)),
                  pl.BlockSpec((ncall, c), lambda h, i: (0, 0))],
        out_specs=[pl.BlockSpec((rows, LANE), lambda h, i: (i, h)),
                   pl.BlockSpec((1, cs, c, c), lambda h, i: (h, i, 0, 0))],
        out_shape=[jax.ShapeDtypeStruct((t, HG_HEADS * LANE), BF16), jax.ShapeDtypeStruct((HG_HEADS, t // c, c, c), F32)],
        scratch_shapes=[pltpu.VMEM((c, c), F32)], args=(z, z, z, z, lb, nw, call))


def _hgrn_bwd(z, lb, nw, call, call_t, saved, dy):
    t = z.shape[0]
    c, cs = HG_CHUNK, _hg_chunks_per_step(t)
    rows = c * cs
    nsteps = t // rows

    def body(hq_ref, hf_ref, hi_ref, hg_ref, lb_ref, nw_ref, call_ref, callt_ref, sv_ref, dy_ref,
             dhq_ref, dhf_ref, dhi_ref, dhg_ref, dlb_ref, dnw_ref, dst_ref):
        h, i = pl.program_id(0), pl.program_id(1)

        @pl.when(i == 0)
        def _():
            dst_ref[...] = jnp.zeros_like(dst_ref)
            dlb_ref[...] = jnp.zeros_like(dlb_ref)

        @pl.when((i == 0) & (h == 0))
        def _():
            dnw_ref[...] = jnp.zeros_like(dnw_ref)

        for u in reversed(range(cs)):
            sl = slice(u * c, (u + 1) * c)
            dhq, dhf, dhi, dhg, dlb, dnw, dst = _hg_chunk_bwd(
                hq_ref[sl, :], hf_ref[sl, :], hi_ref[sl, :], hg_ref[sl, :], lb_ref[...], nw_ref[...], sv_ref[0, u],
                call_ref[...], callt_ref[...], dy_ref[sl, :].astype(F32), dst_ref[...])
            dhq_ref[sl, :] = dhq.astype(BF16)
            dhf_ref[sl, :] = dhf.astype(BF16)
            dhi_ref[sl, :] = dhi.astype(BF16)
            dhg_ref[sl, :] = dhg.astype(BF16)
            dlb_ref[...] += dlb
            dnw_ref[...] += dnw
            dst_ref[...] = dst

    def zcol(group):
        return pl.BlockSpec((rows, LANE), functools.partial(lambda h, i, g: (nsteps - 1 - i, _hg_col(g, h)), g=group))

    head_rows = pl.BlockSpec((rows, LANE), lambda h, i: (nsteps - 1 - i, h))
    ncall = call.shape[0]
    piece = jax.ShapeDtypeStruct((t, HG_HEADS * LANE), BF16)
    return pl.pallas_call(
        body, name="hgrn_bwd", grid=(HG_HEADS, nsteps),
        in_specs=[zcol(Z_HQ), zcol(Z_HF), zcol(Z_HI), zcol(Z_HG),
                  pl.BlockSpec((1, LANE), lambda h, i: (0, h)), pl.BlockSpec((1, LANE), lambda h, i: (0, 0)),
                  pl.BlockSpec((ncall, c), lambda h, i: (0, 0)), pl.BlockSpec((c, ncall), lambda h, i: (0, 0)),
                  pl.BlockSpec((1, cs, c, c), lambda h, i: (h, nsteps - 1 - i, 0, 0)), head_rows],
        out_specs=[head_rows, head_rows, head_rows, head_rows,
                   pl.BlockSpec((1, LANE), lambda h, i: (0, h)), pl.BlockSpec((1, LANE), lambda h, i: (0, 0))],
        out_shape=[piece, piece, piece, piece,
                   jax.ShapeDtypeStruct((1, HG_HEADS * LANE), F32), jax.ShapeDtypeStruct((1, LANE), F32)],
        scratch_shapes=[pltpu.VMEM((c, c), F32)],
        compiler_params=_cparams(("arbitrary", "arbitrary")),
    )(z, z, z, z, lb, nw, call, call_t, saved, dy)


def _lb_fwd(raw):
    nl = raw.shape[0]

    def body(r_ref, o_ref):
        x = r_ref[...]
        e = jnp.exp(x - jnp.max(x, axis=0, keepdims=True))
        p = e / jnp.sum(e, axis=0, keepdims=True)
        acc = jnp.zeros_like(p[0:1])
        for l in range(nl):
            if l > 0:
                acc = acc + p[l:l + 1]
            o_ref[l:l + 1, :] = acc

    return pl.pallas_call(body, name="lb_fwd", out_shape=jax.ShapeDtypeStruct(raw.shape, F32))(raw)


def _lb_bwd(raw, dlbs):
    nl = raw.shape[0]

    def body(r_ref, d_ref, o_ref):
        x = r_ref[...]
        e = jnp.exp(x - jnp.max(x, axis=0, keepdims=True))
        p = e / jnp.sum(e, axis=0, keepdims=True)
        d = d_ref[...]
        dps = [jnp.zeros_like(d[0:1])]
        for i in range(1, nl):
            acc = d[i:i + 1]
            for l in range(i + 1, nl):
                acc = acc + d[l:l + 1]
            dps.append(acc)
        dot = dps[0] * p[0:1]
        for i in range(1, nl):
            dot = dot + dps[i] * p[i:i + 1]
        for i in range(nl):
            o_ref[i:i + 1, :] = p[i:i + 1] * (dps[i] - dot)

    return pl.pallas_call(body, name="lb_bwd", out_shape=jax.ShapeDtypeStruct(raw.shape, F32))(raw, dlbs)


def _push_plan(srcs, gather, bufs=None, layer=None):
    nk = len(srcs)
    any_spec = pl.BlockSpec(memory_space=pl.ANY)
    if bufs is None:
        ins = list(srcs)
        outs = [jax.ShapeDtypeStruct(((N_DEV,) + s.shape) if gather else s.shape, s.dtype) for s in srcs]
    else:
        ins = list(srcs) + list(bufs)
        outs = [jax.ShapeDtypeStruct(b.shape, b.dtype) for b in bufs]
    sems = [pltpu.SemaphoreType.DMA((nk * N_DEV,)), pltpu.SemaphoreType.DMA((nk * N_DEV,)), pltpu.SemaphoreType.DMA((nk,))]
    return dict(nk=nk, gather=gather, layer=layer, ins=ins, in_specs=[any_spec] * len(ins), outs=outs,
                out_specs=[any_spec] * nk, sems=sems, alias_from=None if bufs is None else nk)


def _push_aliases(plan, first_in, first_out):
    if plan is None or plan["alias_from"] is None:
        return {}
    return {first_in + plan["alias_from"] + k: first_out + k for k in range(plan["nk"])}


def _push_copies(plan, in_refs, out_refs, sems):
    nk, gather, layer = plan["nk"], plan["gather"], plan["layer"]
    send_sems, recv_sems, local_sems = sems
    me = 4 * lax.axis_index("x") + 2 * lax.axis_index("y") + lax.axis_index("c")

    def landing(k):
        return out_refs[k].at[me] if layer is None else out_refs[k].at[me, layer]

    copies = [pltpu.make_async_copy(in_refs[k] if gather else in_refs[k].at[me], landing(k), local_sems.at[k])
              for k in range(nk)]
    for r in range(1, N_DEV):
        to = (me + r) % N_DEV
        for k in range(nk):
            copies.append(pltpu.make_async_remote_copy(
                src_ref=in_refs[k] if gather else in_refs[k].at[to], dst_ref=landing(k),
                send_sem=send_sems.at[k * N_DEV + r], recv_sem=recv_sems.at[k * N_DEV + r],
                device_id=(to // 4, (to // 2) % 2, to % 2), device_id_type=pl.DeviceIdType.MESH))
    return copies


def _exchange(name, srcs, gather, bufs=None, layer=None):
    plan = _push_plan(srcs, gather, bufs, layer)
    nin, nk = len(plan["ins"]), plan["nk"]

    def body(*refs):
        copies = _push_copies(plan, refs[:nin], refs[nin:nin + nk], refs[nin + nk:])
        for cp in copies:
            cp.start()
        for cp in copies:
            cp.wait()

    return pl.pallas_call(
        body, name=name, in_specs=plan["in_specs"], out_specs=plan["out_specs"], out_shape=plan["outs"],
        scratch_shapes=plan["sems"], input_output_aliases=_push_aliases(plan, 0, 0),
    )(*plan["ins"])


def _adam_math(g, w, m, v):
    m2 = ADAM_B1 * m + (1.0 - ADAM_B1) * g
    v2 = ADAM_B2 * v + (1.0 - ADAM_B2) * (g * g)
    m_hat = m2 / (1.0 - ADAM_B1 ** ADAM_STEP)
    v_hat = v2 / (1.0 - ADAM_B2 ** ADAM_STEP)
    return -ADAM_LR * (m_hat / (jnp.sqrt(v_hat) + ADAM_EPS) + ADAM_WD * w), m2, v2


def _sum_slots(ref):
    g = ref[0].astype(F32)
    for s in range(1, N_DEV):
        g = g + ref[s].astype(F32)
    return g


def _adam_sharded(name, slots, w, m, v):
    nl, a, b = w.shape
    ta = a
    for cand in range(8, 257, 8):
        if a % cand == 0:
            ta = cand

    def body(s_ref, w_ref, m_ref, v_ref, g_ref, d_ref, m2_ref, v2_ref):
        g = _sum_slots(s_ref)
        d, m2, v2 = _adam_math(g, w_ref[...], m_ref[...], v_ref[...])
        g_ref[...] = g
        d_ref[...] = d
        m2_ref[...] = m2
        v2_ref[...] = v2

    blk = pl.BlockSpec((1, ta, b), lambda l, i: (l, i, 0))
    sds = jax.ShapeDtypeStruct(w.shape, F32)
    return pl.pallas_call(
        body, name=name, grid=(nl, a // ta),
        in_specs=[pl.BlockSpec((N_DEV, 1, ta, b), lambda l, i: (0, l, i, 0)), blk, blk, blk],
        out_specs=[blk] * 4, out_shape=[sds] * 4,
        compiler_params=_cparams(("parallel", "parallel")),
    )(slots, w, m, v)


def _sum_replicated(slots):
    def body(s_ref, g_ref):
        g_ref[...] = _sum_slots(s_ref)

    return pl.pallas_call(body, name="sum_small", out_shape=jax.ShapeDtypeStruct(slots.shape[1:], F32))(slots)


def _adam_small(name, g, w, m, v):
    def body(g_ref, w_ref, m_ref, v_ref, d_ref, m2_ref, v2_ref):
        d, m2, v2 = _adam_math(g_ref[...], w_ref[...], m_ref[...], v_ref[...])
        d_ref[...] = d
        m2_ref[...] = m2
        v2_ref[...] = v2

    sds = jax.ShapeDtypeStruct(w.shape, F32)
    return pl.pallas_call(body, name=name, out_shape=[sds] * 3)(g, w, m, v)


def _cols_full(g):
    return jnp.transpose(g, (1, 0, 2)).reshape(g.shape[1], -1)


def _cols_shards(w):
    k = w.shape[0]
    return jnp.transpose(w.reshape(k, N_DEV, -1), (1, 0, 2))


def _swap_halves(x):
    half = x.shape[-1] // 2
    return jnp.concatenate([x[..., half:], x[..., :half]], axis=-1)


def _zeros_like_cols(x, n):
    return jnp.zeros(x.shape[:-1] + (n,), x.dtype)


def _w_in_internal(w):
    d = w.shape[0]
    kpe = w[:, 640:672]
    z64, z32 = jnp.zeros((d, 64), w.dtype), jnp.zeros((d, 32), w.dtype)
    return jnp.concatenate(
        [w[:, 0:640], z64, kpe, z32, z64, _swap_halves(kpe), z32, jnp.zeros((d, Z_HQ - Z_KPESW - LANE), w.dtype),
         w[:, 672:2720], w[:, 2720:4768]], axis=1)


def _w_in_grad(g):
    kpe = g[:, Z_KPE + 64:Z_KPE + 96] + _swap_halves(g[:, Z_KPESW + 64:Z_KPESW + 96])
    return jnp.concatenate([g[:, 0:640], kpe, g[:, Z_HQ:Z_W]], axis=1)


def _w_uq_internal(w):
    k = w.shape[0]
    w3 = w.reshape(k, MLA_HEADS, QK_NOPE + QK_ROPE)
    nope, rope = w3[..., :QK_NOPE], w3[..., QK_NOPE:]
    plain = jnp.concatenate([nope, rope, _zeros_like_cols(rope, 32)], axis=-1).reshape(k, -1)
    swapped = jnp.concatenate([_zeros_like_cols(nope, 64), _swap_halves(rope), _zeros_like_cols(rope, 32)], axis=-1).reshape(k, -1)
    return jnp.concatenate([plain, swapped], axis=1)


def _w_uq_grad(g):
    k = g.shape[0]
    half = MLA_HEADS * LANE
    g1, g2 = g[:, :half].reshape(k, MLA_HEADS, LANE), g[:, half:].reshape(k, MLA_HEADS, LANE)
    rope = g1[..., 64:96] + _swap_halves(g2[..., 64:96])
    return jnp.concatenate([g1[..., :64], rope], axis=-1).reshape(k, -1)


def _w_ukv_internal(w):
    k = w.shape[0]
    w3 = w.reshape(k, MLA_HEADS, QK_NOPE + V_HEAD)
    kn, vv = w3[..., :QK_NOPE], w3[..., QK_NOPE:]
    z = _zeros_like_cols(kn, 64)
    return jnp.concatenate([kn, z, vv, z], axis=-1).reshape(k, -1)


def _w_ukv_grad(g):
    k = g.shape[0]
    g3 = g.reshape(k, MLA_HEADS, 2 * LANE)
    return jnp.concatenate([g3[..., 0:64], g3[..., LANE:LANE + 64]], axis=-1).reshape(k, -1)


def _w_pa_internal(w):
    n = w.shape[1]
    w3 = w.reshape(MLA_HEADS, V_HEAD, n)
    return jnp.concatenate([w3, jnp.zeros_like(w3)], axis=1).reshape(-1, n)


def _w_pa_grad(g):
    n = g.shape[1]
    return g.reshape(MLA_HEADS, 2 * V_HEAD, n)[:, :V_HEAD].reshape(-1, n)


def _rope_tables(t):
    half = QK_ROPE // 2
    inv = ROPE_THETA ** (-jnp.arange(half, dtype=F32) / half)
    ang = jnp.arange(t, dtype=F32)[:, None] * inv[None, :]
    cos, sin = jnp.cos(ang), jnp.sin(ang)
    one, zero = jnp.ones((t, 64), F32), jnp.zeros((t, 64), F32)
    z32 = jnp.zeros((t, 32), F32)
    cq = jnp.concatenate([one, cos, cos, z32], axis=1)
    ck = jnp.concatenate([zero, cos, cos, z32], axis=1)
    sq = jnp.concatenate([zero, -sin, sin, z32], axis=1)
    return cq, ck, sq


def _residual_mm(name, a, w, h, alpha, next_norm_w):
    t, d = h.shape
    tm = _row_tile(t)
    rows = _tile_spec(tm, d)
    if next_norm_w is None:
        h2, = _mm(name, [a], [(0, w, 0)], [(h,) + rows], [((t, d), F32) + rows],
                  lambda accs, ex: (ex[0] + alpha * accs[0],), tm=tm, tn=d, n=d)
        return h2, None

    def fn(accs, ex):
        h2 = ex[0] + alpha * accs[0]
        return h2, _rms_parts(h2)[1] * ex[1]

    return _mm(name, [a], [(0, w, 0)], [(h,) + rows, (next_norm_w, (1, d), lambda i, j: (0, 0))],
               [((t, d), F32) + rows, ((t, d), BF16) + rows], fn, tm=tm, tn=d, n=d)


def _ffn_fwd(tag, h, xn, w_gu, w_down, next_norm_w, push=None):
    t, d = h.shape
    dff = w_down.shape[0]
    tm, tn = _row_tile(t), _tile(dff, 1408)

    def act_fn(accs, _):
        g, u = accs
        return g, u, g * _sigmoid(g) * u

    spec = _tile_spec(tm, tn)
    res = _mm(tag + "_gu", [xn], [(0, w_gu, 0), (0, w_gu, dff // tn)], [],
              [((t, dff), BF16) + spec] * 3, act_fn, tm=tm, tn=tn, n=dff, push=push)
    (g, u, act), pushed = res if push is not None else (res, None)
    h2, xn2 = _residual_mm(tag + "_down", act, w_down, h, 0.5, next_norm_w)
    return h2, xn2, (h, xn, g, u, act), pushed


def _ffn_bwd(tag, dh2, saved, nw, w_gu_t, w_down_t, pushes=(None, None)):
    h, xn, g, u, act = saved
    t, d = h.shape
    dff = act.shape[1]
    tm, tn = _row_tile(t), _tile(dff, 1408)

    def dact_fn(accs, ex):
        gg, uu = ex[0].astype(F32), ex[1].astype(F32)
        da = 0.5 * accs[0]
        sg = _sigmoid(gg)
        return da * uu * (sg * (1.0 + gg * (1.0 - sg))), da * (gg * sg)

    spec = _tile_spec(tm, tn)
    res = _mm(tag + "_dact", [dh2], [(0, w_down_t, 0)], [(g,) + spec, (u,) + spec],
              [((t, dff), BF16) + spec] * 2, dact_fn, tm=tm, tn=tn, n=dff, push=pushes[0])
    (dg, du), pushed0 = res if pushes[0] is not None else (res, None)
    dw_down = _mm_tn(tag + "_dwdown", act, dh2, alpha=0.5, out_dtype=BF16)
    tn2 = _tile(d, 512)
    res = _mm(tag + "_dxn", [dg, du], [(0, w_gu_t[:dff], 0), (1, w_gu_t[dff:], 0)], [],
              [((t, d), F32) + _tile_spec(tm, tn2)], lambda accs, _: (accs[0] + accs[1],), tm=tm, tn=tn2, n=d,
              push=pushes[1])
    (dxn,), pushed1 = res if pushes[1] is not None else (res, None)
    dw_gu = jnp.concatenate([_mm_tn(tag + "_dwg", xn, dg, alpha=1.0, out_dtype=BF16),
                             _mm_tn(tag + "_dwu", xn, du, alpha=1.0, out_dtype=BF16)], axis=1)
    dh, dnw = _rms_bwd(tag + "_dnorm", h, nw, dxn, dh2)
    return dh, dnw, dw_gu, dw_down, (pushed0, pushed1)


def _kv_pattern(kr):
    z = jnp.zeros_like(kr)
    return jnp.concatenate([kr, z] * MLA_HEADS, axis=1)


def _mix_fwd(h, u, p, tabs, lb, call, push, hg_push, next_norm_w):
    t, d = h.shape
    tm = _row_tile(t)
    tnz = _tile(Z_W, 1024)
    z, = _mm("mix_in", [u], [(0, p["w_in"], 0)], [], [((t, Z_W), F32) + _tile_spec(tm, tnz)], lambda a, _: (a[0],),
             tm=tm, tn=tnz, n=Z_W)
    cqn, ckvn, q, kv, v_t = _mla_front(z, p["q_norm"], p["kv_norm"], tabs, p["w_uq"], p["w_ukv"])
    (o_a, lse), pushed = _attn_fwd(q, kv, v_t, push)
    (o_b, st_saved), hg_pushed = _hgrn_fwd(z, lb, p["hg_norm"], call, hg_push)
    tn = _tile(d, 512)

    def merge_fn(accs, ex):
        ya, yb = accs
        return ya, yb, _sigmoid(ex[0]) * ya + _sigmoid(ex[1]) * yb

    spec = _tile_spec(tm, tn)
    ya, yb, merged = _mm("mix_merge", [o_a, o_b], [(0, p["w_pa"], 0), (1, p["w_pr"], 0)],
                         [(z,) + _tile_spec(tm, tn, Z_GA // tn), (z,) + _tile_spec(tm, tn, Z_GB // tn)],
                         [((t, d), BF16) + spec] * 3, merge_fn, tm=tm, tn=tn, n=d)
    h2, xn2 = _residual_mm("mix_out", merged, p["w_out"], h, 1.0, next_norm_w)
    return h2, xn2, (h, u, z, cqn, ckvn, q, kv, o_a, lse, o_b, st_saved, ya, yb, merged), pushed, hg_pushed


def _mix_bwd(dh2, saved, p, tabs, lb, call, call_t, push):
    h, u, z, cqn, ckvn, q, kv, o_a, lse, o_b, st_saved, ya, yb, merged = saved
    t, d = h.shape
    tm = _row_tile(t)
    tn = _tile(d, 512)
    spec = _tile_spec(tm, tn)

    def dmerge_fn(accs, ex):
        dm = accs[0]
        yav, ybv = ex[0].astype(F32), ex[1].astype(F32)
        sa, sb = _sigmoid(ex[2]), _sigmoid(ex[3])
        return dm * sa, dm * sb, dm * yav * sa * (1.0 - sa), dm * ybv * sb * (1.0 - sb)

    dya, dyb, dga, dgb = _mm("mix_dmerge", [dh2], [(0, p["w_out_t"], 0)],
                             [(ya,) + spec, (yb,) + spec, (z,) + _tile_spec(tm, tn, Z_GA // tn),
                              (z,) + _tile_spec(tm, tn, Z_GB // tn)],
                             [((t, d), BF16) + spec] * 4, dmerge_fn, tm=tm, tn=tn, n=d)
    dw_out = _mm_tn("mix_dwout", merged, dh2, alpha=1.0, out_dtype=BF16)
    wq = MLA_HEADS * LANE
    do_a, = _mm("mix_doa", [dya], [(0, p["w_pa_t"], 0)], [], [((t, wq), BF16) + _tile_spec(tm, wq)], lambda a, _: (a[0],),
                tm=tm, tn=wq, n=wq)
    wr = HG_HEADS * LANE
    do_b, = _mm("mix_dob", [dyb], [(0, p["w_pr_t"], 0)], [], [((t, wr), BF16) + _tile_spec(tm, wr)], lambda a, _: (a[0],),
                tm=tm, tn=wr, n=wr)
    dw_pa = _mm_tn("mix_dwpa", o_a, dya, alpha=1.0, out_dtype=F32)
    dw_pr = _mm_tn("mix_dwpr", o_b, dyb, alpha=1.0, out_dtype=BF16)
    (dq, dkv), pushed = _attn_bwd(q, kv, o_a, do_a, lse, push)
    dz_mla, dw_uq, dw_ukv, dqn, dkvn = _mla_back(z, p["q_norm"], p["kv_norm"], tabs, cqn, ckvn, dq, dkv,
                                                 p["w_uq_t"], p["w_ukv_t"])
    dhq, dhf, dhi, dhg, dlb, dhgn = _hgrn_bwd(z, lb, p["hg_norm"], call, call_t, st_saved, do_b)
    dz = jnp.concatenate([dz_mla, dhq, dhf, dhi, dhg, dga, dgb], axis=1)
    du, = _mm("mix_du", [dz], [(0, p["w_in_t"], 0)], [], [((t, d), F32) + spec], lambda a, _: (a[0],), tm=tm, tn=tn, n=d)
    dw_in = _mm_tn("mix_dwin", u, dz, alpha=1.0, out_dtype=F32)
    dh, dmn = _rms_bwd("mix_dnorm", h, p["mix_norm"], du, dh2)
    grads = dict(mix_norm=dmn, q_norm=dqn, kv_norm=dkvn, hg_norm=dhgn, lb=dlb, w_in=_w_in_grad(dw_in), w_uq=_w_uq_grad(dw_uq),
                 w_ukv=_w_ukv_grad(dw_ukv), w_proj_attn=_w_pa_grad(dw_pa), w_proj_rec=dw_pr, w_out=dw_out)
    return dh, grads, pushed


SHARDED = ("ffn1_w_gu", "ffn1_w_down", "w_in", "w_uq", "w_ukv", "w_proj_attn", "w_proj_rec", "w_out", "ffn2_w_gu", "ffn2_w_down")
ROW_SHARDED = ("ffn1_w_down", "w_out", "ffn2_w_down")
FFN1_W = ("ffn1_w_gu", "ffn1_w_down")
MIX_W = ("w_in", "w_uq", "w_ukv", "w_proj_attn", "w_proj_rec", "w_out")
FFN2_W = ("ffn2_w_gu", "ffn2_w_down")
EARLY_A = ("ffn2_w_gu",)
EARLY_B = ("ffn2_w_down", "w_in")
SMALL = ("ffn1_norm", "mix_norm", "q_norm", "kv_norm", "hg_lb_raw", "hg_norm", "ffn2_norm", "final_norm")
WEIGHTS = ("meta_tokens", "ffn1_norm", "ffn1_w_gu", "ffn1_w_down", "mix_norm", "w_in", "q_norm", "kv_norm", "w_uq", "w_ukv",
           "hg_lb_raw", "hg_norm", "w_proj_attn", "w_proj_rec", "w_out", "ffn2_norm", "ffn2_w_gu", "ffn2_w_down", "final_norm")


def _pack_small(vals):
    flat = jnp.concatenate([vals[n].reshape(-1) for n in SMALL])
    return flat.reshape(-1, LANE)


def _unpack_small(packed, like):
    flat = packed.reshape(-1)
    out, off = {}, 0
    for n in SMALL:
        size = math.prod(like[n].shape)
        out[n] = flat[off:off + size].reshape(like[n].shape)
        off += size
    return out


def kernel(x, meta_tokens, ffn1_norm, ffn1_w_gu, ffn1_w_down, mix_norm, w_in, q_norm, kv_norm, w_uq, w_ukv, hg_lb_raw, hg_norm, w_proj_attn, w_proj_rec, w_out, ffn2_norm, ffn2_w_gu, ffn2_w_down, final_norm, loss_target, m_meta_tokens, m_ffn1_norm, m_ffn1_w_gu, m_ffn1_w_down, m_mix_norm, m_w_in, m_q_norm, m_kv_norm, m_w_uq, m_w_ukv, m_hg_lb_raw, m_hg_norm, m_w_proj_attn, m_w_proj_rec, m_w_out, m_ffn2_norm, m_ffn2_w_gu, m_ffn2_w_down, m_final_norm, v_meta_tokens, v_ffn1_norm, v_ffn1_w_gu, v_ffn1_w_down, v_mix_norm, v_w_in, v_q_norm, v_kv_norm, v_w_uq, v_w_ukv, v_hg_lb_raw, v_hg_norm, v_w_proj_attn, v_w_proj_rec, v_w_out, v_ffn2_norm, v_ffn2_w_gu, v_ffn2_w_down, v_final_norm):
    w = dict(meta_tokens=meta_tokens, ffn1_norm=ffn1_norm, ffn1_w_gu=ffn1_w_gu, ffn1_w_down=ffn1_w_down, mix_norm=mix_norm,
             w_in=w_in, q_norm=q_norm, kv_norm=kv_norm, w_uq=w_uq, w_ukv=w_ukv, hg_lb_raw=hg_lb_raw, hg_norm=hg_norm,
             w_proj_attn=w_proj_attn, w_proj_rec=w_proj_rec, w_out=w_out, ffn2_norm=ffn2_norm, ffn2_w_gu=ffn2_w_gu,
             ffn2_w_down=ffn2_w_down, final_norm=final_norm)
    mom = dict(meta_tokens=m_meta_tokens, ffn1_norm=m_ffn1_norm, ffn1_w_gu=m_ffn1_w_gu, ffn1_w_down=m_ffn1_w_down,
               mix_norm=m_mix_norm, w_in=m_w_in, q_norm=m_q_norm, kv_norm=m_kv_norm, w_uq=m_w_uq, w_ukv=m_w_ukv,
               hg_lb_raw=m_hg_lb_raw, hg_norm=m_hg_norm, w_proj_attn=m_w_proj_attn, w_proj_rec=m_w_proj_rec, w_out=m_w_out,
               ffn2_norm=m_ffn2_norm, ffn2_w_gu=m_ffn2_w_gu, ffn2_w_down=m_ffn2_w_down, final_norm=m_final_norm)
    var = dict(meta_tokens=v_meta_tokens, ffn1_norm=v_ffn1_norm, ffn1_w_gu=v_ffn1_w_gu, ffn1_w_down=v_ffn1_w_down,
               mix_norm=v_mix_norm, w_in=v_w_in, q_norm=v_q_norm, kv_norm=v_kv_norm, w_uq=v_w_uq, w_ukv=v_w_ukv,
               hg_lb_raw=v_hg_lb_raw, hg_norm=v_hg_norm, w_proj_attn=v_w_proj_attn, w_proj_rec=v_w_proj_rec, w_out=v_w_out,
               ffn2_norm=v_ffn2_norm, ffn2_w_gu=v_ffn2_w_gu, ffn2_w_down=v_ffn2_w_down, final_norm=v_final_norm)
    nl = ffn1_norm.shape[0]
    seq, d = x.shape[1], x.shape[2]
    n_real = N_META + seq
    t = -(-n_real // ROW_ALIGN) * ROW_ALIGN
    me = 4 * lax.axis_index("x") + 2 * lax.axis_index("y") + lax.axis_index("c")

    def own_shards(l, names):
        return [w[n][l].astype(BF16) for n in names]

    gathered = _exchange("gather_weights", own_shards(0, FFN1_W) + [meta_tokens], gather=True)
    meta_full = _cols_full(gathered[-1])

    def mat(full, n):
        g = full[n]
        return g.reshape(-1, g.shape[-1]) if n in ROW_SHARDED else _cols_full(g)

    def ffn_params(p, tag, l, full):
        p[tag + "_w_gu"] = mat(full, tag + "_w_gu")
        p[tag + "_w_down"] = mat(full, tag + "_w_down")
        p[tag + "_w_gu_t"] = p[tag + "_w_gu"].T
        p[tag + "_w_down_t"] = p[tag + "_w_down"].T
        p[tag + "_norm"] = w[tag + "_norm"][l:l + 1]

    def mix_params(p, l, full):
        p["w_in"] = _w_in_internal(mat(full, "w_in"))
        p["w_uq"] = _w_uq_internal(mat(full, "w_uq"))
        p["w_ukv"] = _w_ukv_internal(mat(full, "w_ukv"))
        p["w_pa"] = _w_pa_internal(mat(full, "w_proj_attn"))
        p["w_pr"] = mat(full, "w_proj_rec")
        p["w_out"] = mat(full, "w_out")
        for n in ("w_in", "w_uq", "w_ukv", "w_pa", "w_pr", "w_out"):
            p[n + "_t"] = p[n].T
        for n in ("mix_norm", "q_norm", "kv_norm", "hg_norm"):
            p[n] = w[n][l:l + 1]

    tabs = _rope_tables(t)
    call, call_t = _hg_tables()
    lbs = _lb_fwd(hg_lb_raw)

    pad = jnp.zeros((t - n_real, d), F32)
    h = jnp.concatenate([meta_full, x[0], pad], axis=0)
    tgt = jnp.concatenate([jnp.zeros((N_META, d), F32), loss_target[0], pad], axis=0)
    saved, params = [], []
    full = dict(zip(FFN1_W, gathered[:-1]))
    xn = _rms_fwd("first_norm", h, ffn1_norm[0:1])
    for l in range(nl):
        p = {}
        params.append(p)
        ffn_params(p, "ffn1", l, full)
        push = _push_plan(own_shards(0, MIX_W), gather=True) if l == 0 else None
        h, xn, s1, pushed = _ffn_fwd("ffn1", h, xn, p["ffn1_w_gu"], p["ffn1_w_down"], mix_norm[l:l + 1], push)
        if pushed is not None:
            full.update(zip(MIX_W, pushed))
        mix_params(p, l, full)
        push = _push_plan(own_shards(l + 1, SHARDED), gather=True) if l + 1 < nl else None
        hg_push = _push_plan(own_shards(0, FFN2_W), gather=True) if l == 0 else None
        h, xn, s2, pushed, hg_pushed = _mix_fwd(h, xn, p, tabs, lbs[l:l + 1], call, push, hg_push, ffn2_norm[l:l + 1])
        if hg_pushed is not None:
            full.update(zip(FFN2_W, hg_pushed))
        ffn_params(p, "ffn2", l, full)
        h, xn, s3, _ = _ffn_fwd("ffn2", h, xn, p["ffn2_w_gu"], p["ffn2_w_down"],
                                ffn1_norm[l + 1:l + 2] if l + 1 < nl else None)
        if pushed is not None:
            full = dict(zip(SHARDED, pushed))
        saved.append((s1, s2, s3))
    dh, d_final, loss_part = _loss_head(h, final_norm.reshape(1, d), tgt, n_real)
    loss = lax.psum(loss_part[0, 0], ("x", "y", "c"))

    def shard(n, g):
        g = g.astype(BF16)
        return g.reshape(N_DEV, -1, g.shape[-1]) if n in ROW_SHARDED else _cols_shards(g)

    def scatter_plan(names, gm, layer):
        idx = [SHARDED.index(n) for n in names]
        return idx, _push_plan([shard(n, gm[n]) for n in names], gather=False, bufs=[slots[i] for i in idx], layer=layer)

    def landed(idx, pushed):
        for i, s in zip(idx, pushed):
            slots[i] = s

    per_layer = []
    slots = [jnp.zeros((N_DEV,) + w[n].shape, BF16) for n in SHARDED]
    above = None
    for l in reversed(range(nl)):
        p = params[l]
        s1, s2, s3 = saved[l]
        dh, dn2, dgu2, ddown2, _ = _ffn_bwd("ffn2", dh, s3, p["ffn2_norm"], p["ffn2_w_gu_t"], p["ffn2_w_down_t"])
        idx, push = (None, None) if above is None else scatter_plan(SHARDED, above, l + 1)
        dh, gm, pushed = _mix_bwd(dh, s2, p, tabs, lbs[l:l + 1], call, call_t, push)
        if pushed is not None:
            landed(idx, pushed)
        gm.update(ffn2_norm=dn2, ffn2_w_gu=dgu2, ffn2_w_down=ddown2)
        pushes, idxs = (None, None), (None, None)
        if l == 0:
            (ia, pa), (ib, pb) = scatter_plan(EARLY_A, gm, 0), scatter_plan(EARLY_B, gm, 0)
            pushes, idxs = (pa, pb), (ia, ib)
        dh, dn1, dgu1, ddown1, pushed2 = _ffn_bwd("ffn1", dh, s1, p["ffn1_norm"], p["ffn1_w_gu_t"], p["ffn1_w_down_t"],
                                                 pushes)
        for i, got in zip(idxs, pushed2):
            if got is not None:
                landed(i, got)
        gm.update(ffn1_norm=dn1, ffn1_w_gu=dgu1, ffn1_w_down=ddown1)
        per_layer.append(gm)
        above = gm
    per_layer.reverse()
    grad_x = dh[N_META:n_real][None]

    last = [n for n in SHARDED if n not in EARLY_A + EARLY_B]
    idx = [SHARDED.index(n) for n in last]
    landed(idx, _exchange("scatter_grads", [shard(n, above[n]) for n in last], gather=False,
                          bufs=[slots[i] for i in idx], layer=0))
    grads, delta, new_m, new_v = {}, {}, {}, {}
    for n, s in zip(SHARDED, slots):
        grads[n], delta[n], new_m[n], new_v[n] = _adam_sharded("adam_" + n, s, w[n], mom[n], var[n])

    small = {n: jnp.concatenate([gm[n] for gm in per_layer], axis=0) for n in SMALL if n not in ("hg_lb_raw", "final_norm")}
    small["hg_lb_raw"] = _lb_bwd(hg_lb_raw, jnp.concatenate([gm["lb"] for gm in per_layer], axis=0))
    small["final_norm"] = d_final
    packed = jnp.concatenate([_pack_small(small), dh[:N_META].reshape(-1, LANE)], axis=0)
    summed = _sum_replicated(_exchange("gather_small", [packed], gather=True)[0])
    n_small = packed.shape[0] - N_META * d // LANE
    sd, sm, sv = _adam_small("adam_small", summed[:n_small], _pack_small(w), _pack_small(mom), _pack_small(var))
    grads.update(_unpack_small(summed[:n_small], w))
    delta.update(_unpack_small(sd, w))
    new_m.update(_unpack_small(sm, w))
    new_v.update(_unpack_small(sv, w))
    dmeta = lax.dynamic_slice_in_dim(summed[n_small:].reshape(N_META, d), me * (d // N_DEV), d // N_DEV, axis=1)
    grads["meta_tokens"] = dmeta
    delta["meta_tokens"], new_m["meta_tokens"], new_v["meta_tokens"] = _adam_small(
        "adam_meta", dmeta, meta_tokens, m_meta_tokens, v_meta_tokens)

    return (loss, grad_x, *[grads[n] for n in WEIGHTS], *[delta[n] for n in WEIGHTS], *[new_m[n] for n in WEIGHTS],
            *[new_v[n] for n in WEIGHTS])
```

```python
import functools
import math

import jax
import jax.numpy as jnp
from jax import lax
from jax.experimental import pallas as pl
from jax.experimental.pallas import tpu as pltpu

F32 = jnp.float32
BF16 = jnp.bfloat16

N_DEV = 8
N_META = 16
MLA_HEADS = 8
Q_LORA = 384
KV_LORA = 256
QK_NOPE = 64
QK_ROPE = 32
V_HEAD = 64
ROPE_THETA = 10000.0
HG_HEADS = 4
HG_DIM = 128
EPS = 1e-6
NEG_BIG = -1e30
F_MIN = 1e-20
ADAM_LR = 0.001
ADAM_B1 = 0.9
ADAM_B2 = 0.999
ADAM_EPS = 1e-08
ADAM_WD = 0.01
ADAM_STEP = 10

LANE = 128
ROW_ALIGN = 256
HG_CHUNK = 128
HG_LEVELS = (64, 32, 16, 8, 4, 2, 1)
VMEM_LIMIT = 48 * 1024 * 1024

Z_CQ, Z_CKV, Z_KPE, Z_KPESW, Z_HQ, Z_HF, Z_HI, Z_HG, Z_GA, Z_GB, Z_W = 0, 384, 640, 768, 1024, 1536, 2048, 2560, 3072, 4096, 5120
ATTN_SCALE = float((QK_NOPE + QK_ROPE) ** -0.5)
LOG2E = 1.4426950408889634
ATTN_C2 = ATTN_SCALE * LOG2E
HG_SCALE = float(HG_DIM ** -0.5)


def _cparams(sem):
    return pltpu.CompilerParams(dimension_semantics=sem, vmem_limit_bytes=VMEM_LIMIT)


def _tile(n, cap):
    if n <= cap:
        return n
    best = None
    for t in range(LANE, cap + 1, LANE):
        if n % t == 0:
            best = t
    assert best is not None, (n, cap)
    return best


def _row_tile(m):
    for t in (384, 256, 128):
        if m % t == 0:
            return t
    raise ValueError(m)


def _bf(x):
    return x.astype(BF16)


def _dot(a, b):
    return jnp.dot(a, b, preferred_element_type=F32)


def _dot_nt(a, b):
    return lax.dot_general(a, b, (((1,), (1,)), ((), ())), preferred_element_type=F32)


def _dot_tn(a, b):
    return lax.dot_general(a, b, (((0,), (0,)), ((), ())), preferred_element_type=F32)


def _sigmoid(x):
    return 1.0 / (1.0 + jnp.exp(-x))


def _mm(name, a_list, pairs, extras, outs, fn, *, tm, tn, n, push=None):
    m = a_list[0].shape[0]
    na, nb, ne, no = len(a_list), len(pairs), len(extras), len(outs)

    def body(*refs):
        a_refs = refs[:na]
        b_refs = refs[na:na + nb]
        e_refs = refs[na + nb:na + nb + ne]
        o_refs = refs[na + nb + ne:]
        a_vals = [_bf(r[...]) for r in a_refs]
        accs = [_dot(a_vals[ai], b_refs[k][...]) for k, (ai, _, _) in enumerate(pairs)]
        res = fn(accs, [r[...] for r in e_refs])
        for r, v in zip(o_refs, res):
            r[...] = v.astype(r.dtype)

    def spec(block_shape, index_map):
        return pl.BlockSpec(block_shape, functools.partial(lambda j, i, im: im(i, j), im=index_map))

    in_specs = [spec((tm, a.shape[1]), lambda i, j: (i, 0)) for a in a_list]
    for _, b, off in pairs:
        in_specs.append(spec((b.shape[0], tn), functools.partial(lambda i, j, off: (0, j + off), off=off)))
    in_specs += [spec(bs, im) for _, bs, im in extras]
    out_specs = [spec(bs, im) for _, _, bs, im in outs]
    out_shape = [jax.ShapeDtypeStruct(s, d) for s, d, _, _ in outs]
    args = (*a_list, *[b for _, b, _ in pairs], *[e for e, _, _ in extras])
    if push is not None:
        return _call_carrying_push(body, push, name=name, grid=(n // tn, m // tm), in_specs=in_specs, out_specs=out_specs,
                                   out_shape=out_shape, scratch_shapes=[], args=args)
    return pl.pallas_call(
        body, name=name, grid=(n // tn, m // tm), in_specs=in_specs, out_specs=out_specs, out_shape=out_shape,
        compiler_params=_cparams(("parallel", "parallel")),
    )(*args)


def _tile_spec(tm, tn, col_off=0):
    return (tm, tn), functools.partial(lambda i, j, off: (i, j + off), off=col_off)


def _mm_tn(name, a, b, *, alpha, out_dtype):
    t, k = a.shape
    n = b.shape[1]
    tk, tn = _tile(k, 1408), _tile(n, 1408)
    tt = next(c for c in (768, 512, 256) if t % c == 0)
    nt = t // tt

    def body(a_ref, b_ref, o_ref, acc_ref):
        s = pl.program_id(2)

        @pl.when(s == 0)
        def _():
            acc_ref[...] = jnp.zeros_like(acc_ref)

        acc_ref[...] += _dot_tn(_bf(a_ref[...]), _bf(b_ref[...]))

        @pl.when(s == nt - 1)
        def _():
            o_ref[...] = (alpha * acc_ref[...]).astype(o_ref.dtype)

    return pl.pallas_call(
        body, name=name, grid=(k // tk, n // tn, nt),
        in_specs=[pl.BlockSpec((tt, tk), lambda i, j, s: (s, i)), pl.BlockSpec((tt, tn), lambda i, j, s: (s, j))],
        out_specs=pl.BlockSpec((tk, tn), lambda i, j, s: (i, j)),
        out_shape=jax.ShapeDtypeStruct((k, n), out_dtype),
        scratch_shapes=[pltpu.VMEM((tk, tn), F32)],
        compiler_params=_cparams(("parallel", "parallel", "arbitrary")),
    )(a, b)


def _rms_parts(x):
    r = lax.rsqrt(jnp.mean(x * x, axis=-1, keepdims=True) + EPS)
    return r, x * r


def _rms_bwd_math(x, w, dxn):
    r, xhat = _rms_parts(x)
    t = dxn * w
    dx = r * (t - xhat * jnp.mean(t * xhat, axis=-1, keepdims=True))
    dw = jnp.sum(dxn * xhat, axis=0, keepdims=True)
    return dx, dw


def _rms_fwd(name, h, w):
    t, d = h.shape
    tm = _row_tile(t)

    def body(h_ref, w_ref, o_ref):
        _, xhat = _rms_parts(h_ref[...])
        o_ref[...] = (xhat * w_ref[...]).astype(o_ref.dtype)

    return pl.pallas_call(
        body, name=name, grid=(t // tm,),
        in_specs=[pl.BlockSpec((tm, d), lambda i: (i, 0)), pl.BlockSpec((1, d), lambda i: (0, 0))],
        out_specs=pl.BlockSpec((tm, d), lambda i: (i, 0)),
        out_shape=jax.ShapeDtypeStruct((t, d), BF16),
        compiler_params=_cparams(("parallel",)),
    )(h, w)


def _rms_bwd(name, h, w, dxn, dh_in):
    t, d = h.shape
    tm = _row_tile(t)

    def body(h_ref, w_ref, dxn_ref, dh_ref, o_ref, dw_ref):
        dx, dw = _rms_bwd_math(h_ref[...], w_ref[...], dxn_ref[...])
        o_ref[...] = dh_ref[...] + dx

        @pl.when(pl.program_id(0) == 0)
        def _():
            dw_ref[...] = jnp.zeros_like(dw_ref)

        dw_ref[...] += dw

    row = pl.BlockSpec((tm, d), lambda i: (i, 0))
    vec = pl.BlockSpec((1, d), lambda i: (0, 0))
    return pl.pallas_call(
        body, name=name, grid=(t // tm,),
        in_specs=[row, vec, row, row],
        out_specs=[row, vec],
        out_shape=[jax.ShapeDtypeStruct((t, d), F32), jax.ShapeDtypeStruct((1, d), F32)],
        compiler_params=_cparams(("arbitrary",)),
    )(h, w, dxn, dh_in)


def _loss_head(h, w, tgt, n_real):
    t, d = h.shape
    tm = _row_tile(t)

    def body(h_ref, w_ref, t_ref, dh_ref, dw_ref, loss_ref):
        i = pl.program_id(0)
        x = h_ref[...]
        wv = w_ref[...]
        _, xhat = _rms_parts(x)
        rows = i * tm + lax.broadcasted_iota(jnp.int32, (tm, 1), 0)
        valid = (rows >= N_META) & (rows < n_real)
        e = jnp.where(valid, xhat * wv - t_ref[...], 0.0)
        dx, dw = _rms_bwd_math(x, wv, e * (1.0 / d))
        dh_ref[...] = dx

        @pl.when(i == 0)
        def _():
            dw_ref[...] = jnp.zeros_like(dw_ref)
            loss_ref[...] = jnp.zeros_like(loss_ref)

        dw_ref[...] += dw
        loss_ref[...] += (0.5 / d) * jnp.sum(jnp.sum(e * e, axis=-1, keepdims=True), axis=0, keepdims=True)

    row = pl.BlockSpec((tm, d), lambda i: (i, 0))
    vec = pl.BlockSpec((1, d), lambda i: (0, 0))
    return pl.pallas_call(
        body, name="loss_head", grid=(t // tm,),
        in_specs=[row, vec, row],
        out_specs=[row, vec, pl.BlockSpec((1, 1), lambda i: (0, 0))],
        out_shape=[jax.ShapeDtypeStruct((t, d), F32), jax.ShapeDtypeStruct((1, d), F32), jax.ShapeDtypeStruct((1, 1), F32)],
        compiler_params=_cparams(("arbitrary",)),
    )(h, w, tgt)


def _heads(x):
    return jnp.concatenate([x] * MLA_HEADS, axis=1)


def _mla_front(z, qn_w, kvn_w, tabs, w_uq, w_ukv):
    t = z.shape[0]
    tm = _attn_block(t)
    wq = MLA_HEADS * LANE

    def body(z_ref, qw_ref, kw_ref, cq_ref, ck_ref, sq_ref, wuq_ref, wukv_ref, cqn_ref, ckvn_ref, q_ref, kv_ref, vt_ref):
        zz = z_ref[...]
        _, qhat = _rms_parts(zz[:, Z_CQ:Z_CKV])
        _, khat = _rms_parts(zz[:, Z_CKV:Z_KPE])
        cqn = (qhat * qw_ref[...]).astype(BF16)
        ckvn = (khat * kw_ref[...]).astype(BF16)
        cqn_ref[...] = cqn
        ckvn_ref[...] = ckvn
        krot = zz[:, Z_KPE:Z_KPESW] * ck_ref[...] + zz[:, Z_KPESW:Z_KPESW + LANE] * sq_ref[...]
        qq = _dot(cqn, wuq_ref[...])
        q_ref[...] = (qq[:, :wq] * _heads(cq_ref[...]) + qq[:, wq:] * _heads(sq_ref[...])).astype(BF16)
        kvv = _dot(ckvn, wukv_ref[...]) + _kv_pattern(krot)
        kv_ref[...] = kvv.astype(BF16)
        for hd in range(MLA_HEADS):
            vt_ref[hd, 0] = kvv[:, 2 * LANE * hd + LANE:2 * LANE * (hd + 1)].T.astype(BF16)

    def rows(wd):
        return pl.BlockSpec((tm, wd), lambda i: (i, 0))

    def whole(a):
        return pl.BlockSpec(a.shape, lambda i: (0, 0))

    return pl.pallas_call(
        body, name="mla_front", grid=(t // tm,),
        in_specs=[rows(Z_HQ), whole(qn_w), whole(kvn_w), rows(LANE), rows(LANE), rows(LANE), whole(w_uq), whole(w_ukv)],
        out_specs=[rows(Q_LORA), rows(KV_LORA), rows(wq), rows(2 * wq),
                   pl.BlockSpec((MLA_HEADS, 1, LANE, tm), lambda i: (0, i, 0, 0))],
        out_shape=[jax.ShapeDtypeStruct((t, Q_LORA), BF16), jax.ShapeDtypeStruct((t, KV_LORA), BF16),
                   jax.ShapeDtypeStruct((t, wq), BF16), jax.ShapeDtypeStruct((t, 2 * wq), BF16),
                   jax.ShapeDtypeStruct((MLA_HEADS, t // tm, LANE, tm), BF16)],
        compiler_params=_cparams(("parallel",)),
    )(z, qn_w, kvn_w, *tabs, w_uq, w_ukv)


def _mla_back(z, qn_w, kvn_w, tabs, cqn, ckvn, dq, dkv, w_uq_t, w_ukv_t):
    t = z.shape[0]
    tm = next(c for c in (768, 512, 256) if t % c == 0)
    wq = MLA_HEADS * LANE

    def body(z_ref, qw_ref, kw_ref, cq_ref, ck_ref, sq_ref, cqn_ref, ckvn_ref, dq_ref, dkv_ref, wuqt_ref, wukvt_ref,
             dz_ref, dwuq_ref, dwukv_ref, dqw_ref, dkw_ref):
        @pl.when(pl.program_id(0) == 0)
        def _():
            for r in (dwuq_ref, dwukv_ref, dqw_ref, dkw_ref):
                r[...] = jnp.zeros_like(r)

        zz = z_ref[...]
        d = dq_ref[...] * ATTN_SCALE
        dqq = jnp.concatenate([d * _heads(cq_ref[...]), d * _heads(sq_ref[...])], axis=1).astype(BF16)
        dkv_v = dkv_ref[...]
        dwuq_ref[...] += _dot_tn(cqn_ref[...], dqq)
        dwukv_ref[...] += _dot_tn(ckvn_ref[...], dkv_v)
        dcq, dqw = _rms_bwd_math(zz[:, Z_CQ:Z_CKV], qw_ref[...], _dot(dqq, wuqt_ref[...]))
        dckv, dkw = _rms_bwd_math(zz[:, Z_CKV:Z_KPE], kw_ref[...], _dot(dkv_v, wukvt_ref[...]))
        dkr = jnp.zeros((tm, LANE), F32)
        for hd in range(MLA_HEADS):
            dkr = dkr + dkv_v[:, 2 * LANE * hd:2 * LANE * hd + LANE].astype(F32)
        dz_ref[...] = jnp.concatenate(
            [dcq, dckv, dkr * ck_ref[...], dkr * sq_ref[...], jnp.zeros((tm, Z_HQ - Z_KPESW - LANE), F32)], axis=1
        ).astype(BF16)
        dqw_ref[...] += dqw
        dkw_ref[...] += dkw

    def rows(wd):
        return pl.BlockSpec((tm, wd), lambda i: (i, 0))

    def whole(shape):
        return pl.BlockSpec(shape, lambda i: (0, 0))

    return pl.pallas_call(
        body, name="mla_back", grid=(t // tm,),
        in_specs=[rows(Z_HQ), whole(qn_w.shape), whole(kvn_w.shape), rows(LANE), rows(LANE), rows(LANE), rows(Q_LORA),
                  rows(KV_LORA), rows(wq), rows(2 * wq), whole(w_uq_t.shape), whole(w_ukv_t.shape)],
        out_specs=[rows(Z_HQ), whole((Q_LORA, 2 * wq)), whole((KV_LORA, 2 * wq)), whole((1, Q_LORA)), whole((1, KV_LORA))],
        out_shape=[jax.ShapeDtypeStruct((t, Z_HQ), BF16), jax.ShapeDtypeStruct((Q_LORA, 2 * wq), F32),
                   jax.ShapeDtypeStruct((KV_LORA, 2 * wq), F32), jax.ShapeDtypeStruct((1, Q_LORA), F32),
                   jax.ShapeDtypeStruct((1, KV_LORA), F32)],
        compiler_params=_cparams(("arbitrary",)),
    )(z, qn_w, kvn_w, *tabs, cqn, ckvn, dq, dkv, w_uq_t, w_ukv_t)


def _attn_block(t):
    for b in (768, 512, 256):
        if t % b == 0:
            return b
    raise ValueError(t)


def _call_carrying_push(body, push, *, name, grid, in_specs, out_specs, out_shape, scratch_shapes, args):
    if push is None:
        outs = pl.pallas_call(body, name=name, grid=grid, in_specs=in_specs, out_specs=out_specs, out_shape=out_shape,
                              scratch_shapes=scratch_shapes, compiler_params=_cparams(("parallel", "arbitrary")))(*args)
        return outs, None
    n_in, n_out, n_scr, n_pin, nk = len(in_specs), len(out_specs), len(scratch_shapes), len(push["ins"]), push["nk"]

    def carrying(*refs):
        o0 = n_in + n_pin
        s0 = o0 + n_out + nk
        pins, pouts, sems = refs[n_in:o0], refs[o0 + n_out:s0], refs[s0 + n_scr:]
        a, b = pl.program_id(0), pl.program_id(1)

        @pl.when((a == 0) & (b == 0))
        def _():
            for cp in _push_copies(push, pins, pouts, sems):
                cp.start()

        body(*refs[:n_in], *refs[o0:o0 + n_out], *refs[s0:s0 + n_scr])

        @pl.when((a == grid[0] - 1) & (b == grid[1] - 1))
        def _():
            for cp in _push_copies(push, pins, pouts, sems):
                cp.wait()

    outs = pl.pallas_call(
        carrying, name=name + "_push", grid=grid, in_specs=list(in_specs) + push["in_specs"],
        out_specs=list(out_specs) + push["out_specs"], out_shape=list(out_shape) + push["outs"],
        scratch_shapes=list(scratch_shapes) + push["sems"], input_output_aliases=_push_aliases(push, n_in, n_out),
        compiler_params=_cparams(("arbitrary", "arbitrary")))(*args, *push["ins"])
    return outs[:n_out], outs[n_out:]


def _attn_fwd(q, kv, v_t, push=None):
    t = q.shape[0]
    bq = bk = _attn_block(t)
    nq = t // bq

    def body(q_ref, k_ref, vt_ref, o_ref, lse_ref):
        i = pl.program_id(1)
        qv = q_ref[...]
        qpos = i * bq + lax.broadcasted_iota(jnp.int32, (1, bq), 1)

        def block(j, carry, masked):
            m, l, acc = carry
            s = _dot_nt(k_ref[pl.ds(pl.multiple_of(j * bk, bk), bk), :], qv)
            if masked:
                kpos = j * bk + lax.broadcasted_iota(jnp.int32, (bk, 1), 0)
                s = jnp.where(kpos <= qpos, s, NEG_BIG)
            m_new = jnp.maximum(m, jnp.max(s, axis=0, keepdims=True))
            p = jnp.exp2((s - m_new) * ATTN_C2)
            a = jnp.exp2((m - m_new) * ATTN_C2)
            return m_new, a * l + jnp.sum(p, axis=0, keepdims=True), a * acc + _dot(vt_ref[0, j], _bf(p))

        init = (jnp.full((1, bq), NEG_BIG, F32), jnp.zeros((1, bq), F32), jnp.zeros((LANE, bq), F32))
        carry = lax.fori_loop(0, i, functools.partial(block, masked=False), init)
        m, l, acc = block(i, carry, True)
        o_ref[...] = (acc / l).T.astype(o_ref.dtype)
        lse_ref[0, 0] = m * ATTN_C2 + jnp.log(l) * LOG2E

    return _call_carrying_push(
        body, push, name="attn_fwd", grid=(MLA_HEADS, nq),
        in_specs=[pl.BlockSpec((bq, LANE), lambda h, i: (i, h)),
                  pl.BlockSpec((t, LANE), lambda h, i: (0, 2 * h)),
                  pl.BlockSpec((1, nq, LANE, bk), lambda h, i: (h, 0, 0, 0))],
        out_specs=[pl.BlockSpec((bq, LANE), lambda h, i: (i, h)),
                   pl.BlockSpec((1, 1, 1, bq), lambda h, i: (h, i, 0, 0))],
        out_shape=[jax.ShapeDtypeStruct((t, MLA_HEADS * LANE), BF16), jax.ShapeDtypeStruct((MLA_HEADS, nq, 1, bq), F32)],
        scratch_shapes=[], args=(q, kv, v_t))


def _attn_bwd(q, kv, o, do, lse, push=None):
    t = q.shape[0]
    bk = bw = _attn_block(t)
    nk, nw = t // bk, t // bw

    def body(q_ref, o_ref, do_ref, k_ref, v_ref, lse_ref, dq_ref, dkv_ref, dl_ref):
        j = pl.program_id(1)

        @pl.when(j == 0)
        def _():
            dq_ref[...] = jnp.zeros_like(dq_ref)
            for i in range(nw):
                rows = slice(i * bw, (i + 1) * bw)
                d = jnp.sum(o_ref[rows, :].astype(F32) * do_ref[rows, :].astype(F32), axis=1, keepdims=True)
                dl_ref[i] = jnp.broadcast_to(d, (bw, LANE)).T[0:1, :]

        kb = k_ref[...]
        vb = v_ref[...]
        kpos = j * bk + lax.broadcasted_iota(jnp.int32, (bk, 1), 0)

        def block(i, carry, masked):
            dk, dv = carry
            rows = pl.ds(pl.multiple_of(i * bw, bw), bw)
            qb = q_ref[rows, :]
            dob = do_ref[rows, :]
            pt = jnp.exp2(_dot_nt(kb, qb) * ATTN_C2 - lse_ref[0, i])
            if masked:
                qpos = i * bw + lax.broadcasted_iota(jnp.int32, (1, bw), 1)
                pt = jnp.where(kpos <= qpos, pt, 0.0)
            dv = dv + _dot(_bf(pt), dob)
            dst = _bf(pt * (_dot_nt(vb, dob) - dl_ref[i]))
            dk = dk + _dot(dst, qb)
            dq_ref[rows, :] += _dot_tn(dst, kb)
            return dk, dv

        i0 = j
        carry = block(i0, (jnp.zeros((bk, LANE), F32), jnp.zeros((bk, LANE), F32)), True)
        dk, dv = lax.fori_loop(i0 + 1, nw, functools.partial(block, masked=False), carry)
        dkv_ref[...] = jnp.concatenate([dk * ATTN_SCALE, dv], axis=1).astype(dkv_ref.dtype)

    head_rows = pl.BlockSpec((t, LANE), lambda h, j: (0, h))
    return _call_carrying_push(
        body, push, name="attn_bwd", grid=(MLA_HEADS, nk),
        in_specs=[head_rows, head_rows, head_rows,
                  pl.BlockSpec((bk, LANE), lambda h, j: (j, 2 * h)), pl.BlockSpec((bk, LANE), lambda h, j: (j, 2 * h + 1)),
                  pl.BlockSpec((1, nw, 1, bw), lambda h, j: (h, 0, 0, 0))],
        out_specs=[head_rows, pl.BlockSpec((bk, 2 * LANE), lambda h, j: (j, h))],
        out_shape=[jax.ShapeDtypeStruct((t, MLA_HEADS * LANE), F32), jax.ShapeDtypeStruct((t, 2 * MLA_HEADS * LANE), BF16)],
        scratch_shapes=[pltpu.VMEM((nw, 1, bw), F32)], args=(q, o, do, kv, kv, lse))


def _hg_tables():
    c = HG_CHUNK
    tri = (jnp.arange(c)[:, None] >= jnp.arange(c)[None, :]).astype(BF16)
    return tri, tri.T


def _hg_level_ref(b, m):
    c = HG_CHUNK
    if 2 * m >= 8:
        x = b.reshape(c // (2 * m), 2 * m, c)
        return jnp.broadcast_to(x[:, m - 1:m, :], x.shape).reshape(c, c)
    row = lax.broadcasted_iota(jnp.int32, (c, 1), 0)
    if m == 2:
        pos = row & 3
        return jnp.where(pos == 0, pltpu.roll(b, c - 1, 0),
                         jnp.where(pos == 1, b, jnp.where(pos == 2, pltpu.roll(b, 1, 0), pltpu.roll(b, 2, 0))))
    return jnp.where((row & 1) == 0, b, pltpu.roll(b, 1, 0))


def _hg_level_ref_t(d, m):
    c = HG_CHUNK
    row = lax.broadcasted_iota(jnp.int32, (c, 1), 0)
    if 2 * m >= 8:
        x = d.reshape(c // (2 * m), 2 * m, c)
        s = jnp.broadcast_to(jnp.sum(x, axis=1, keepdims=True), x.shape).reshape(c, c)
        return jnp.where((row & (2 * m - 1)) == m - 1, s, 0.0)
    if m == 2:
        s = pltpu.roll(d, 1, 0) + d + pltpu.roll(d, c - 1, 0) + pltpu.roll(d, c - 2, 0)
        return jnp.where((row & 3) == 1, s, 0.0)
    return jnp.where((row & 1) == 0, d + pltpu.roll(d, c - 1, 0), 0.0)


def _table_dot(table, x):
    hi = _bf(x)
    rest = x - hi.astype(F32)
    mid = _bf(rest)
    lo = _bf(rest - mid.astype(F32))
    out = _dot(table, jnp.concatenate([hi, mid, lo], axis=1))
    n = x.shape[1]
    return out[:, 0:n] + out[:, n:2 * n] + out[:, 2 * n:3 * n]


def _hg_gates(hq, hf, lb):
    sg = _sigmoid(hf)
    sn = _sigmoid(-hf)
    f = lb + (1.0 - lb) * sg
    q = hq * _sigmoid(hq)
    g = jnp.log(jnp.maximum(f, F_MIN))
    k = (1.0 - lb) * sn
    return q, k, g, f, sg, sn


def _hg_level_masks(m):
    c = HG_CHUNK
    row = lax.broadcasted_iota(jnp.int32, (c, 1), 0)
    col = lax.broadcasted_iota(jnp.int32, (1, c), 1)
    shift = (2 * m).bit_length() - 1
    up = (row & m) != 0
    same = lax.shift_right_logical(row, shift) == lax.shift_right_logical(col, shift)
    return up, same


def _hg_level_factors(b, bref, up):
    arg = b - bref
    e = jnp.exp(jnp.where(up, arg, -arg))
    return jnp.where(up, e, 0.0), jnp.where(up, 0.0, e)


def _hg_intra(q, k, b, masks):
    c = HG_CHUNK
    row = lax.broadcasted_iota(jnp.int32, (c, 1), 0)
    col = lax.broadcasted_iota(jnp.int32, (1, c), 1)
    a = jnp.where(row == col, _dot_nt(_bf(q), _bf(k)), 0.0)
    parts = []
    for m, (up, same) in zip(HG_LEVELS, masks):
        eq, ek = _hg_level_factors(b, _hg_level_ref(b, m), up)
        qt, kt = q * eq, k * ek
        a = a + jnp.where(same, _dot_nt(_bf(qt), _bf(kt)), 0.0)
        parts.append((eq, ek, qt, kt))
    return a, parts


def _hg_chunk_fwd(hq, hf, hi, hg, lb, nw, st, call, masks):
    q, k, g, _, _, _ = _hg_gates(hq, hf, lb)
    b = _table_dot(call, g)
    a, _ = _hg_intra(q, k, b, masks)
    v16 = _bf(hi)
    o = _dot(_bf(a * HG_SCALE), v16) + _dot_nt(_bf(q * jnp.exp(b) * HG_SCALE), _bf(st))
    bl = b[HG_CHUNK - 1:HG_CHUNK]
    ke = k * jnp.exp(bl - b)
    st_new = st * jnp.exp(bl) + _dot(_bf(hi.T), _bf(ke))
    r = lax.rsqrt(jnp.mean(o * o, axis=-1, keepdims=True) + EPS)
    y = o * r * nw * (hg * _sigmoid(hg))
    return y, st_new


def _hg_chunk_bwd(hq, hf, hi, hg, lb, nw, st, call, call_t, dy, dst_new, masks):
    c = HG_CHUNK
    q, k, g, f, sg, sn = _hg_gates(hq, hf, lb)
    b = _table_dot(call, g)
    a, parts = _hg_intra(q, k, b, masks)
    v16 = _bf(hi)
    st16 = _bf(st)
    eb = jnp.exp(b)
    qe = q * eb * HG_SCALE
    a16 = _bf(a * HG_SCALE)
    o = _dot(a16, v16) + _dot_nt(_bf(qe), st16)
    bl = b[c - 1:c]
    el = jnp.exp(bl)
    x = jnp.exp(bl - b)
    ke = k * x
    r = lax.rsqrt(jnp.mean(o * o, axis=-1, keepdims=True) + EPS)
    shg = _sigmoid(hg)
    gate = hg * shg
    ohat = o * r
    don = dy * gate
    dhg = dy * ohat * nw * (shg * (1.0 + hg * (1.0 - shg)))
    dnw = jnp.sum(don * ohat, axis=0, keepdims=True)
    tt = don * nw
    do = r * (tt - ohat * jnp.mean(tt * ohat, axis=-1, keepdims=True))
    do16 = _bf(do)
    dst16 = _bf(dst_new)
    da = _dot_nt(do16, v16) * HG_SCALE
    dv = _dot(_bf(a16.astype(F32).T), do16) + _dot_nt(_bf(ke), dst16)
    dqe = _dot(do16, st16)
    dke = _dot(v16, dst16)
    dst = dst_new * el + _dot(_bf(do.T), _bf(qe))
    dbl = jnp.sum(dst_new * st, axis=0, keepdims=True) * el
    dk = dke * x
    dxa = dke * ke
    db = dqe * qe - dxa
    dbl = dbl + jnp.sum(dxa, axis=0, keepdims=True)
    dq = dqe * eb * HG_SCALE
    row = lax.broadcasted_iota(jnp.int32, (c, 1), 0)
    col = lax.broadcasted_iota(jnp.int32, (1, c), 1)
    ddiag = jnp.sum(jnp.where(row == col, da, 0.0), axis=1, keepdims=True)
    dq = dq + ddiag * k
    dk = dk + ddiag * q
    for (eq, ek, qt, kt), m, (_, same) in zip(parts, HG_LEVELS, masks):
        gm = jnp.where(same, da, 0.0)
        dqt = _dot(_bf(gm), _bf(kt))
        dkt = _dot(_bf(gm.T), _bf(qt))
        dq = dq + dqt * eq
        dk = dk + dkt * ek
        darg = dqt * qt - dkt * kt
        db = db + darg - _hg_level_ref_t(darg, m)
    db = db + jnp.where(row == c - 1, dbl, 0.0)
    dg = _table_dot(call_t, db)
    shq = _sigmoid(hq)
    dhq = dq * (shq * (1.0 + hq * (1.0 - shq)))
    df = jnp.where(f > F_MIN, dg / jnp.maximum(f, F_MIN), 0.0)
    dlb = jnp.sum(df * (1.0 - sg) - dk * sn, axis=0, keepdims=True)
    dhf = df * (1.0 - lb) * sg * (1.0 - sg) - dk * (1.0 - lb) * sn * (1.0 - sn)
    return dhq, dhf, dv, dhg, dlb, dnw, dst


def _hg_chunks_per_step(t):
    return 6 if t % (6 * HG_CHUNK) == 0 else 2


def _hg_col(group, h):
    return group // LANE + h


def _hgrn_fwd(z, lb, nw, call, push=None):
    t = z.shape[0]
    c, cs = HG_CHUNK, _hg_chunks_per_step(t)
    rows = c * cs
    nsteps = t // rows

    def body(hq_ref, hf_ref, hi_ref, hg_ref, lb_ref, nw_ref, call_ref, y_ref, sv_ref, st_ref):
        @pl.when(pl.program_id(1) == 0)
        def _():
            st_ref[...] = jnp.zeros_like(st_ref)

        masks = [_hg_level_masks(m) for m in HG_LEVELS]
        for u in range(cs):
            sl = slice(u * c, (u + 1) * c)
            st = st_ref[...]
            sv_ref[0, u] = st
            y, st_new = _hg_chunk_fwd(hq_ref[sl, :], hf_ref[sl, :], hi_ref[sl, :], hg_ref[sl, :], lb_ref[...], nw_ref[...],
                                      st, call_ref[...], masks)
            y_ref[sl, :] = y.astype(y_ref.dtype)
            st_ref[...] = st_new

    def zcol(group):
        return pl.BlockSpec((rows, LANE), functools.partial(lambda h, i, g: (i, _hg_col(g, h)), g=group))

    ncall = call.shape[0]
    return _call_carrying_push(
        body, push, name="hgrn_fwd", grid=(HG_HEADS, nsteps),
        in_specs=[zcol(Z_HQ), zcol(Z_HF), zcol(Z_HI), zcol(Z_HG),
                  pl.BlockSpec((1, LANE), lambda h, i: (0, h)), pl.BlockSpec((1, LANE), lambda h, i: (0, 0)),
                  pl.BlockSpec((ncall, c), lambda h, i: (0, 0))],
        out_specs=[pl.BlockSpec((rows, LANE), lambda h, i: (i, h)),
                   pl.BlockSpec((1, cs, c, c), lambda h, i: (h, i, 0, 0))],
        out_shape=[jax.ShapeDtypeStruct((t, HG_HEADS * LANE), BF16), jax.ShapeDtypeStruct((HG_HEADS, t // c, c, c), F32)],
        scratch_shapes=[pltpu.VMEM((c, c), F32)], args=(z, z, z, z, lb, nw, call))


def _hgrn_bwd(z, lb, nw, call, call_t, saved, dy):
    t = z.shape[0]
    c, cs = HG_CHUNK, _hg_chunks_per_step(t)
    rows = c * cs
    nsteps = t // rows

    def body(hq_ref, hf_ref, hi_ref, hg_ref, lb_ref, nw_ref, call_ref, callt_ref, sv_ref, dy_ref,
             dhq_ref, dhf_ref, dhi_ref, dhg_ref, dlb_ref, dnw_ref, dst_ref):
        h, i = pl.program_id(0), pl.program_id(1)

        @pl.when(i == 0)
        def _():
            dst_ref[...] = jnp.zeros_like(dst_ref)
            dlb_ref[...] = jnp.zeros_like(dlb_ref)

        @pl.when((i == 0) & (h == 0))
        def _():
            dnw_ref[...] = jnp.zeros_like(dnw_ref)

        masks = [_hg_level_masks(m) for m in HG_LEVELS]
        for u in reversed(range(cs)):
            sl = slice(u * c, (u + 1) * c)
            dhq, dhf, dhi, dhg, dlb, dnw, dst = _hg_chunk_bwd(
                hq_ref[sl, :], hf_ref[sl, :], hi_ref[sl, :], hg_ref[sl, :], lb_ref[...], nw_ref[...], sv_ref[0, u],
                call_ref[...], callt_ref[...], dy_ref[sl, :].astype(F32), dst_ref[...], masks)
            dhq_ref[sl, :] = dhq.astype(BF16)
            dhf_ref[sl, :] = dhf.astype(BF16)
            dhi_ref[sl, :] = dhi.astype(BF16)
            dhg_ref[sl, :] = dhg.astype(BF16)
            dlb_ref[...] += dlb
            dnw_ref[...] += dnw
            dst_ref[...] = dst

    def zcol(group):
        return pl.BlockSpec((rows, LANE), functools.partial(lambda h, i, g: (nsteps - 1 - i, _hg_col(g, h)), g=group))

    head_rows = pl.BlockSpec((rows, LANE), lambda h, i: (nsteps - 1 - i, h))
    ncall = call.shape[0]
    piece = jax.ShapeDtypeStruct((t, HG_HEADS * LANE), BF16)
    return pl.pallas_call(
        body, name="hgrn_bwd", grid=(HG_HEADS, nsteps),
        in_specs=[zcol(Z_HQ), zcol(Z_HF), zcol(Z_HI), zcol(Z_HG),
                  pl.BlockSpec((1, LANE), lambda h, i: (0, h)), pl.BlockSpec((1, LANE), lambda h, i: (0, 0)),
                  pl.BlockSpec((ncall, c), lambda h, i: (0, 0)), pl.BlockSpec((c, ncall), lambda h, i: (0, 0)),
                  pl.BlockSpec((1, cs, c, c), lambda h, i: (h, nsteps - 1 - i, 0, 0)), head_rows],
        out_specs=[head_rows, head_rows, head_rows, head_rows,
                   pl.BlockSpec((1, LANE), lambda h, i: (0, h)), pl.BlockSpec((1, LANE), lambda h, i: (0, 0))],
        out_shape=[piece, piece, piece, piece,
                   jax.ShapeDtypeStruct((1, HG_HEADS * LANE), F32), jax.ShapeDtypeStruct((1, LANE), F32)],
        scratch_shapes=[pltpu.VMEM((c, c), F32)],
        compiler_params=_cparams(("arbitrary", "arbitrary")),
    )(z, z, z, z, lb, nw, call, call_t, saved, dy)


def _lb_fwd(raw):
    nl = raw.shape[0]

    def body(r_ref, o_ref):
        x = r_ref[...]
        e = jnp.exp(x - jnp.max(x, axis=0, keepdims=True))
        p = e / jnp.sum(e, axis=0, keepdims=True)
        acc = jnp.zeros_like(p[0:1])
        for l in range(nl):
            if l > 0:
                acc = acc + p[l:l + 1]
            o_ref[l:l + 1, :] = acc

    return pl.pallas_call(body, name="lb_fwd", out_shape=jax.ShapeDtypeStruct(raw.shape, F32))(raw)


def _lb_bwd(raw, dlbs):
    nl = raw.shape[0]

    def body(r_ref, d_ref, o_ref):
        x = r_ref[...]
        e = jnp.exp(x - jnp.max(x, axis=0, keepdims=True))
        p = e / jnp.sum(e, axis=0, keepdims=True)
        d = d_ref[...]
        dps = [jnp.zeros_like(d[0:1])]
        for i in range(1, nl):
            acc = d[i:i + 1]
            for l in range(i + 1, nl):
                acc = acc + d[l:l + 1]
            dps.append(acc)
        dot = dps[0] * p[0:1]
        for i in range(1, nl):
            dot = dot + dps[i] * p[i:i + 1]
        for i in range(nl):
            o_ref[i:i + 1, :] = p[i:i + 1] * (dps[i] - dot)

    return pl.pallas_call(body, name="lb_bwd", out_shape=jax.ShapeDtypeStruct(raw.shape, F32))(raw, dlbs)


def _push_plan(srcs, gather, bufs=None, layer=None):
    nk = len(srcs)
    any_spec = pl.BlockSpec(memory_space=pl.ANY)
    if bufs is None:
        ins = list(srcs)
        outs = [jax.ShapeDtypeStruct(((N_DEV,) + s.shape) if gather else s.shape, s.dtype) for s in srcs]
    else:
        ins = list(srcs) + list(bufs)
        outs = [jax.ShapeDtypeStruct(b.shape, b.dtype) for b in bufs]
    sems = [pltpu.SemaphoreType.DMA((nk * N_DEV,)), pltpu.SemaphoreType.DMA((nk * N_DEV,)), pltpu.SemaphoreType.DMA((nk,))]
    return dict(nk=nk, gather=gather, layer=layer, ins=ins, in_specs=[any_spec] * len(ins), outs=outs,
                out_specs=[any_spec] * nk, sems=sems, alias_from=None if bufs is None else nk)


def _push_aliases(plan, first_in, first_out):
    if plan is None or plan["alias_from"] is None:
        return {}
    return {first_in + plan["alias_from"] + k: first_out + k for k in range(plan["nk"])}


def _push_copies(plan, in_refs, out_refs, sems):
    nk, gather, layer = plan["nk"], plan["gather"], plan["layer"]
    send_sems, recv_sems, local_sems = sems
    me = 4 * lax.axis_index("x") + 2 * lax.axis_index("y") + lax.axis_index("c")

    def landing(k):
        return out_refs[k].at[me] if layer is None else out_refs[k].at[me, layer]

    copies = [pltpu.make_async_copy(in_refs[k] if gather else in_refs[k].at[me], landing(k), local_sems.at[k])
              for k in range(nk)]
    for r in range(1, N_DEV):
        to = (me + r) % N_DEV
        for k in range(nk):
            copies.append(pltpu.make_async_remote_copy(
                src_ref=in_refs[k] if gather else in_refs[k].at[to], dst_ref=landing(k),
                send_sem=send_sems.at[k * N_DEV + r], recv_sem=recv_sems.at[k * N_DEV + r],
                device_id=(to // 4, (to // 2) % 2, to % 2), device_id_type=pl.DeviceIdType.MESH))
    return copies


def _exchange(name, srcs, gather, bufs=None, layer=None):
    plan = _push_plan(srcs, gather, bufs, layer)
    nin, nk = len(plan["ins"]), plan["nk"]

    def body(*refs):
        copies = _push_copies(plan, refs[:nin], refs[nin:nin + nk], refs[nin + nk:])
        for cp in copies:
            cp.start()
        for cp in copies:
            cp.wait()

    return pl.pallas_call(
        body, name=name, in_specs=plan["in_specs"], out_specs=plan["out_specs"], out_shape=plan["outs"],
        scratch_shapes=plan["sems"], input_output_aliases=_push_aliases(plan, 0, 0),
    )(*plan["ins"])


def _adam_math(g, w, m, v):
    m2 = ADAM_B1 * m + (1.0 - ADAM_B1) * g
    v2 = ADAM_B2 * v + (1.0 - ADAM_B2) * (g * g)
    m_hat = m2 / (1.0 - ADAM_B1 ** ADAM_STEP)
    v_hat = v2 / (1.0 - ADAM_B2 ** ADAM_STEP)
    return -ADAM_LR * (m_hat / (jnp.sqrt(v_hat) + ADAM_EPS) + ADAM_WD * w), m2, v2


def _sum_slots(ref):
    g = ref[0].astype(F32)
    for s in range(1, N_DEV):
        g = g + ref[s].astype(F32)
    return g


def _adam_sharded(name, slots, w, m, v):
    nl, a, b = w.shape
    ta = a
    for cand in range(8, 257, 8):
        if a % cand == 0:
            ta = cand

    def body(s_ref, w_ref, m_ref, v_ref, g_ref, d_ref, m2_ref, v2_ref):
        g = _sum_slots(s_ref)
        d, m2, v2 = _adam_math(g, w_ref[...], m_ref[...], v_ref[...])
        g_ref[...] = g
        d_ref[...] = d
        m2_ref[...] = m2
        v2_ref[...] = v2

    blk = pl.BlockSpec((1, ta, b), lambda l, i: (l, i, 0))
    sds = jax.ShapeDtypeStruct(w.shape, F32)
    return pl.pallas_call(
        body, name=name, grid=(nl, a // ta),
        in_specs=[pl.BlockSpec((N_DEV, 1, ta, b), lambda l, i: (0, l, i, 0)), blk, blk, blk],
        out_specs=[blk] * 4, out_shape=[sds] * 4,
        compiler_params=_cparams(("parallel", "parallel")),
    )(slots, w, m, v)


def _sum_replicated(slots):
    def body(s_ref, g_ref):
        g_ref[...] = _sum_slots(s_ref)

    return pl.pallas_call(body, name="sum_small", out_shape=jax.ShapeDtypeStruct(slots.shape[1:], F32))(slots)


def _adam_small(name, g, w, m, v):
    def body(g_ref, w_ref, m_ref, v_ref, d_ref, m2_ref, v2_ref):
        d, m2, v2 = _adam_math(g_ref[...], w_ref[...], m_ref[...], v_ref[...])
        d_ref[...] = d
        m2_ref[...] = m2
        v2_ref[...] = v2

    sds = jax.ShapeDtypeStruct(w.shape, F32)
    return pl.pallas_call(body, name=name, out_shape=[sds] * 3)(g, w, m, v)


def _cols_full(g):
    return jnp.transpose(g, (1, 0, 2)).reshape(g.shape[1], -1)


def _cols_shards(w):
    k = w.shape[0]
    return jnp.transpose(w.reshape(k, N_DEV, -1), (1, 0, 2))


def _swap_halves(x):
    half = x.shape[-1] // 2
    return jnp.concatenate([x[..., half:], x[..., :half]], axis=-1)


def _zeros_like_cols(x, n):
    return jnp.zeros(x.shape[:-1] + (n,), x.dtype)


def _w_in_internal(w):
    d = w.shape[0]
    kpe = w[:, 640:672]
    z64, z32 = jnp.zeros((d, 64), w.dtype), jnp.zeros((d, 32), w.dtype)
    return jnp.concatenate(
        [w[:, 0:640], z64, kpe, z32, z64, _swap_halves(kpe), z32, jnp.zeros((d, Z_HQ - Z_KPESW - LANE), w.dtype),
         w[:, 672:2720], w[:, 2720:4768]], axis=1)


def _w_in_grad(g):
    kpe = g[:, Z_KPE + 64:Z_KPE + 96] + _swap_halves(g[:, Z_KPESW + 64:Z_KPESW + 96])
    return jnp.concatenate([g[:, 0:640], kpe, g[:, Z_HQ:Z_W]], axis=1)


def _w_uq_internal(w):
    k = w.shape[0]
    w3 = w.reshape(k, MLA_HEADS, QK_NOPE + QK_ROPE)
    nope, rope = w3[..., :QK_NOPE], w3[..., QK_NOPE:]
    plain = jnp.concatenate([nope, rope, _zeros_like_cols(rope, 32)], axis=-1).reshape(k, -1)
    swapped = jnp.concatenate([_zeros_like_cols(nope, 64), _swap_halves(rope), _zeros_like_cols(rope, 32)], axis=-1).reshape(k, -1)
    return jnp.concatenate([plain, swapped], axis=1)


def _w_uq_grad(g):
    k = g.shape[0]
    half = MLA_HEADS * LANE
    g1, g2 = g[:, :half].reshape(k, MLA_HEADS, LANE), g[:, half:].reshape(k, MLA_HEADS, LANE)
    rope = g1[..., 64:96] + _swap_halves(g2[..., 64:96])
    return jnp.concatenate([g1[..., :64], rope], axis=-1).reshape(k, -1)


def _w_ukv_internal(w):
    k = w.shape[0]
    w3 = w.reshape(k, MLA_HEADS, QK_NOPE + V_HEAD)
    kn, vv = w3[..., :QK_NOPE], w3[..., QK_NOPE:]
    z = _zeros_like_cols(kn, 64)
    return jnp.concatenate([kn, z, vv, z], axis=-1).reshape(k, -1)


def _w_ukv_grad(g):
    k = g.shape[0]
    g3 = g.reshape(k, MLA_HEADS, 2 * LANE)
    return jnp.concatenate([g3[..., 0:64], g3[..., LANE:LANE + 64]], axis=-1).reshape(k, -1)


def _w_pa_internal(w):
    n = w.shape[1]
    w3 = w.reshape(MLA_HEADS, V_HEAD, n)
    return jnp.concatenate([w3, jnp.zeros_like(w3)], axis=1).reshape(-1, n)


def _w_pa_grad(g):
    n = g.shape[1]
    return g.reshape(MLA_HEADS, 2 * V_HEAD, n)[:, :V_HEAD].reshape(-1, n)


def _rope_tables(t):
    half = QK_ROPE // 2
    inv = ROPE_THETA ** (-jnp.arange(half, dtype=F32) / half)
    ang = jnp.arange(t, dtype=F32)[:, None] * inv[None, :]
    cos, sin = jnp.cos(ang), jnp.sin(ang)
    one, zero = jnp.ones((t, 64), F32), jnp.zeros((t, 64), F32)
    z32 = jnp.zeros((t, 32), F32)
    cq = jnp.concatenate([one, cos, cos, z32], axis=1)
    ck = jnp.concatenate([zero, cos, cos, z32], axis=1)
    sq = jnp.concatenate([zero, -sin, sin, z32], axis=1)
    return cq, ck, sq


def _mm_pushing(push, *args, **kwargs):
    res = _mm(*args, push=push, **kwargs)
    return res if push is not None else (res, None)


def _residual_mm(name, a, w, h, alpha, next_norm_w, push=None):
    t, d = h.shape
    tm = _row_tile(t)
    rows = _tile_spec(tm, d)
    if next_norm_w is None:
        (h2,), pushed = _mm_pushing(push, name, [a], [(0, w, 0)], [(h,) + rows], [((t, d), F32) + rows],
                                    lambda accs, ex: (ex[0] + alpha * accs[0],), tm=tm, tn=d, n=d)
        return h2, None, pushed

    def fn(accs, ex):
        h2 = ex[0] + alpha * accs[0]
        return h2, _rms_parts(h2)[1] * ex[1]

    (h2, xn2), pushed = _mm_pushing(push, name, [a], [(0, w, 0)],
                                    [(h,) + rows, (next_norm_w, (1, d), lambda i, j: (0, 0))],
                                    [((t, d), F32) + rows, ((t, d), BF16) + rows], fn, tm=tm, tn=d, n=d)
    return h2, xn2, pushed


def _ffn_fwd(tag, h, xn, w_gu, w_down, next_norm_w, pushes=(None, None)):
    t, d = h.shape
    dff = w_down.shape[0]
    tm, tn = _row_tile(t), _tile(dff, 1408)

    def act_fn(accs, _):
        g, u = accs
        return g, u, g * _sigmoid(g) * u

    spec = _tile_spec(tm, tn)
    (g, u, act), pushed0 = _mm_pushing(pushes[0], tag + "_gu", [xn], [(0, w_gu, 0), (0, w_gu, dff // tn)], [],
                                       [((t, dff), BF16) + spec] * 3, act_fn, tm=tm, tn=tn, n=dff)
    h2, xn2, pushed1 = _residual_mm(tag + "_down", act, w_down, h, 0.5, next_norm_w, pushes[1])
    return h2, xn2, (h, xn, g, u, act), (pushed0, pushed1)


def _ffn_bwd(tag, dh2, saved, nw, w_gu_t, w_down_t, pushes=(None, None)):
    h, xn, g, u, act = saved
    t, d = h.shape
    dff = act.shape[1]
    tm, tn = _row_tile(t), _tile(dff, 1408)

    def dact_fn(accs, ex):
        gg, uu = ex[0].astype(F32), ex[1].astype(F32)
        da = 0.5 * accs[0]
        sg = _sigmoid(gg)
        return da * uu * (sg * (1.0 + gg * (1.0 - sg))), da * (gg * sg)

    spec = _tile_spec(tm, tn)
    (dg, du), pushed0 = _mm_pushing(pushes[0], tag + "_dact", [dh2], [(0, w_down_t, 0)], [(g,) + spec, (u,) + spec],
                                    [((t, dff), BF16) + spec] * 2, dact_fn, tm=tm, tn=tn, n=dff)
    dw_down = _mm_tn(tag + "_dwdown", act, dh2, alpha=0.5, out_dtype=BF16)
    tn2 = _tile(d, 512)
    (dxn,), pushed1 = _mm_pushing(pushes[1], tag + "_dxn", [dg, du], [(0, w_gu_t[:dff], 0), (1, w_gu_t[dff:], 0)], [],
                                  [((t, d), F32) + _tile_spec(tm, tn2)], lambda accs, _: (accs[0] + accs[1],),
                                  tm=tm, tn=tn2, n=d)
    dw_gu = jnp.concatenate([_mm_tn(tag + "_dwg", xn, dg, alpha=1.0, out_dtype=BF16),
                             _mm_tn(tag + "_dwu", xn, du, alpha=1.0, out_dtype=BF16)], axis=1)
    dh, dnw = _rms_bwd(tag + "_dnorm", h, nw, dxn, dh2)
    return dh, dnw, dw_gu, dw_down, (pushed0, pushed1)


def _kv_pattern(kr):
    z = jnp.zeros_like(kr)
    return jnp.concatenate([kr, z] * MLA_HEADS, axis=1)


def _mix_fwd(h, u, p, tabs, lb, call, pushes, next_norm_w):
    push, hg_push, merge_push = pushes
    t, d = h.shape
    tm = _row_tile(t)
    tnz = _tile(Z_W, 1024)
    z, = _mm("mix_in", [u], [(0, p["w_in"], 0)], [], [((t, Z_W), F32) + _tile_spec(tm, tnz)], lambda a, _: (a[0],),
             tm=tm, tn=tnz, n=Z_W)
    cqn, ckvn, q, kv, v_t = _mla_front(z, p["q_norm"], p["kv_norm"], tabs, p["w_uq"], p["w_ukv"])
    (o_a, lse), pushed = _attn_fwd(q, kv, v_t, push)
    (o_b, st_saved), hg_pushed = _hgrn_fwd(z, lb, p["hg_norm"], call, hg_push)
    tn = _tile(d, 512)

    def merge_fn(accs, ex):
        ya, yb = accs
        return ya, yb, _sigmoid(ex[0]) * ya + _sigmoid(ex[1]) * yb

    spec = _tile_spec(tm, tn)
    (ya, yb, merged), merge_pushed = _mm_pushing(
        merge_push, "mix_merge", [o_a, o_b], [(0, p["w_pa"], 0), (1, p["w_pr"], 0)],
        [(z,) + _tile_spec(tm, tn, Z_GA // tn), (z,) + _tile_spec(tm, tn, Z_GB // tn)],
        [((t, d), BF16) + spec] * 3, merge_fn, tm=tm, tn=tn, n=d)
    h2, xn2, _ = _residual_mm("mix_out", merged, p["w_out"], h, 1.0, next_norm_w)
    return (h2, xn2, (h, u, z, cqn, ckvn, q, kv, o_a, lse, o_b, st_saved, ya, yb, merged),
            (pushed, hg_pushed, merge_pushed))


def _mix_bwd(dh2, saved, p, tabs, lb, call, call_t, pushes):
    push, dmerge_push, du_push = pushes
    h, u, z, cqn, ckvn, q, kv, o_a, lse, o_b, st_saved, ya, yb, merged = saved
    t, d = h.shape
    tm = _row_tile(t)
    tn = _tile(d, 512)
    spec = _tile_spec(tm, tn)

    def dmerge_fn(accs, ex):
        dm = accs[0]
        yav, ybv = ex[0].astype(F32), ex[1].astype(F32)
        sa, sb = _sigmoid(ex[2]), _sigmoid(ex[3])
        return dm * sa, dm * sb, dm * yav * sa * (1.0 - sa), dm * ybv * sb * (1.0 - sb)

    (dya, dyb, dga, dgb), dmerge_pushed = _mm_pushing(
        dmerge_push, "mix_dmerge", [dh2], [(0, p["w_out_t"], 0)],
        [(ya,) + spec, (yb,) + spec, (z,) + _tile_spec(tm, tn, Z_GA // tn), (z,) + _tile_spec(tm, tn, Z_GB // tn)],
        [((t, d), BF16) + spec] * 4, dmerge_fn, tm=tm, tn=tn, n=d)
    dw_out = _mm_tn("mix_dwout", merged, dh2, alpha=1.0, out_dtype=BF16)
    wq = MLA_HEADS * LANE
    do_a, = _mm("mix_doa", [dya], [(0, p["w_pa_t"], 0)], [], [((t, wq), BF16) + _tile_spec(tm, wq)], lambda a, _: (a[0],),
                tm=tm, tn=wq, n=wq)
    wr = HG_HEADS * LANE
    do_b, = _mm("mix_dob", [dyb], [(0, p["w_pr_t"], 0)], [], [((t, wr), BF16) + _tile_spec(tm, wr)], lambda a, _: (a[0],),
                tm=tm, tn=wr, n=wr)
    dw_pa = _mm_tn("mix_dwpa", o_a, dya, alpha=1.0, out_dtype=F32)
    dw_pr = _mm_tn("mix_dwpr", o_b, dyb, alpha=1.0, out_dtype=BF16)
    (dq, dkv), pushed = _attn_bwd(q, kv, o_a, do_a, lse, push)
    dz_mla, dw_uq, dw_ukv, dqn, dkvn = _mla_back(z, p["q_norm"], p["kv_norm"], tabs, cqn, ckvn, dq, dkv,
                                                 p["w_uq_t"], p["w_ukv_t"])
    dhq, dhf, dhi, dhg, dlb, dhgn = _hgrn_bwd(z, lb, p["hg_norm"], call, call_t, st_saved, do_b)
    dz = jnp.concatenate([dz_mla, dhq, dhf, dhi, dhg, dga, dgb], axis=1)
    (du,), du_pushed = _mm_pushing(du_push, "mix_du", [dz], [(0, p["w_in_t"], 0)], [], [((t, d), F32) + spec],
                                   lambda a, _: (a[0],), tm=tm, tn=tn, n=d)
    dw_in = _mm_tn("mix_dwin", u, dz, alpha=1.0, out_dtype=F32)
    dh, dmn = _rms_bwd("mix_dnorm", h, p["mix_norm"], du, dh2)
    grads = dict(mix_norm=dmn, q_norm=dqn, kv_norm=dkvn, hg_norm=dhgn, lb=dlb, w_in=_w_in_grad(dw_in), w_uq=_w_uq_grad(dw_uq),
                 w_ukv=_w_ukv_grad(dw_ukv), w_proj_attn=_w_pa_grad(dw_pa), w_proj_rec=dw_pr, w_out=dw_out)
    return dh, grads, (pushed, dmerge_pushed, du_pushed)


SHARDED = ("ffn1_w_gu", "ffn1_w_down", "w_in", "w_uq", "w_ukv", "w_proj_attn", "w_proj_rec", "w_out", "ffn2_w_gu", "ffn2_w_down")
ROW_SHARDED = ("ffn1_w_down", "w_out", "ffn2_w_down")
FFN1_W = ("ffn1_w_gu", "ffn1_w_down")
GATHER_BEHIND = (("w_in", "w_uq", "w_ukv"), ("w_proj_attn", "w_proj_rec", "w_out"), ("ffn2_w_gu",), ("ffn2_w_down",))
SCATTER_BEHIND = (("ffn2_w_gu",), ("ffn2_w_down", "w_in"))
SMALL = ("ffn1_norm", "mix_norm", "q_norm", "kv_norm", "hg_lb_raw", "hg_norm", "ffn2_norm", "final_norm")
WEIGHTS = ("meta_tokens", "ffn1_norm", "ffn1_w_gu", "ffn1_w_down", "mix_norm", "w_in", "q_norm", "kv_norm", "w_uq", "w_ukv",
           "hg_lb_raw", "hg_norm", "w_proj_attn", "w_proj_rec", "w_out", "ffn2_norm", "ffn2_w_gu", "ffn2_w_down", "final_norm")


def _pack_small(vals):
    flat = jnp.concatenate([vals[n].reshape(-1) for n in SMALL])
    return flat.reshape(-1, LANE)


def _unpack_small(packed, like):
    flat = packed.reshape(-1)
    out, off = {}, 0
    for n in SMALL:
        size = math.prod(like[n].shape)
        out[n] = flat[off:off + size].reshape(like[n].shape)
        off += size
    return out


def kernel(x, meta_tokens, ffn1_norm, ffn1_w_gu, ffn1_w_down, mix_norm, w_in, q_norm, kv_norm, w_uq, w_ukv, hg_lb_raw, hg_norm, w_proj_attn, w_proj_rec, w_out, ffn2_norm, ffn2_w_gu, ffn2_w_down, final_norm, loss_target, m_meta_tokens, m_ffn1_norm, m_ffn1_w_gu, m_ffn1_w_down, m_mix_norm, m_w_in, m_q_norm, m_kv_norm, m_w_uq, m_w_ukv, m_hg_lb_raw, m_hg_norm, m_w_proj_attn, m_w_proj_rec, m_w_out, m_ffn2_norm, m_ffn2_w_gu, m_ffn2_w_down, m_final_norm, v_meta_tokens, v_ffn1_norm, v_ffn1_w_gu, v_ffn1_w_down, v_mix_norm, v_w_in, v_q_norm, v_kv_norm, v_w_uq, v_w_ukv, v_hg_lb_raw, v_hg_norm, v_w_proj_attn, v_w_proj_rec, v_w_out, v_ffn2_norm, v_ffn2_w_gu, v_ffn2_w_down, v_final_norm):
    w = dict(meta_tokens=meta_tokens, ffn1_norm=ffn1_norm, ffn1_w_gu=ffn1_w_gu, ffn1_w_down=ffn1_w_down, mix_norm=mix_norm,
             w_in=w_in, q_norm=q_norm, kv_norm=kv_norm, w_uq=w_uq, w_ukv=w_ukv, hg_lb_raw=hg_lb_raw, hg_norm=hg_norm,
             w_proj_attn=w_proj_attn, w_proj_rec=w_proj_rec, w_out=w_out, ffn2_norm=ffn2_norm, ffn2_w_gu=ffn2_w_gu,
             ffn2_w_down=ffn2_w_down, final_norm=final_norm)
    mom = dict(meta_tokens=m_meta_tokens, ffn1_norm=m_ffn1_norm, ffn1_w_gu=m_ffn1_w_gu, ffn1_w_down=m_ffn1_w_down,
               mix_norm=m_mix_norm, w_in=m_w_in, q_norm=m_q_norm, kv_norm=m_kv_norm, w_uq=m_w_uq, w_ukv=m_w_ukv,
               hg_lb_raw=m_hg_lb_raw, hg_norm=m_hg_norm, w_proj_attn=m_w_proj_attn, w_proj_rec=m_w_proj_rec, w_out=m_w_out,
               ffn2_norm=m_ffn2_norm, ffn2_w_gu=m_ffn2_w_gu, ffn2_w_down=m_ffn2_w_down, final_norm=m_final_norm)
    var = dict(meta_tokens=v_meta_tokens, ffn1_norm=v_ffn1_norm, ffn1_w_gu=v_ffn1_w_gu, ffn1_w_down=v_ffn1_w_down,
               mix_norm=v_mix_norm, w_in=v_w_in, q_norm=v_q_norm, kv_norm=v_kv_norm, w_uq=v_w_uq, w_ukv=v_w_ukv,
               hg_lb_raw=v_hg_lb_raw, hg_norm=v_hg_norm, w_proj_attn=v_w_proj_attn, w_proj_rec=v_w_proj_rec, w_out=v_w_out,
               ffn2_norm=v_ffn2_norm, ffn2_w_gu=v_ffn2_w_gu, ffn2_w_down=v_ffn2_w_down, final_norm=v_final_norm)
    nl = ffn1_norm.shape[0]
    seq, d = x.shape[1], x.shape[2]
    n_real = N_META + seq
    t = -(-n_real // ROW_ALIGN) * ROW_ALIGN
    me = 4 * lax.axis_index("x") + 2 * lax.axis_index("y") + lax.axis_index("c")

    def own_shards(l, names):
        return [w[n][l].astype(BF16) for n in names]

    gathered = _exchange("gather_weights", own_shards(0, FFN1_W) + [meta_tokens], gather=True)
    meta_full = _cols_full(gathered[-1])

    def mat(full, n):
        g = full[n]
        return g.reshape(-1, g.shape[-1]) if n in ROW_SHARDED else _cols_full(g)

    def ffn_params(p, tag, l, full):
        p[tag + "_w_gu"] = mat(full, tag + "_w_gu")
        p[tag + "_w_down"] = mat(full, tag + "_w_down")
        p[tag + "_w_gu_t"] = p[tag + "_w_gu"].T
        p[tag + "_w_down_t"] = p[tag + "_w_down"].T
        p[tag + "_norm"] = w[tag + "_norm"][l:l + 1]

    def mix_params(p, l, full):
        p["w_in"] = _w_in_internal(mat(full, "w_in"))
        p["w_uq"] = _w_uq_internal(mat(full, "w_uq"))
        p["w_ukv"] = _w_ukv_internal(mat(full, "w_ukv"))
        p["w_pa"] = _w_pa_internal(mat(full, "w_proj_attn"))
        p["w_pr"] = mat(full, "w_proj_rec")
        p["w_out"] = mat(full, "w_out")
        for n in ("w_in", "w_uq", "w_ukv", "w_pa", "w_pr", "w_out"):
            p[n + "_t"] = p[n].T
        for n in ("mix_norm", "q_norm", "kv_norm", "hg_norm"):
            p[n] = w[n][l:l + 1]

    tabs = _rope_tables(t)
    call, call_t = _hg_tables()
    lbs = _lb_fwd(hg_lb_raw)

    pad = jnp.zeros((t - n_real, d), F32)
    h = jnp.concatenate([meta_full, x[0], pad], axis=0)
    tgt = jnp.concatenate([jnp.zeros((N_META, d), F32), loss_target[0], pad], axis=0)
    saved, params = [], []
    full = dict(zip(FFN1_W, gathered[:-1]))
    xn = _rms_fwd("first_norm", h, ffn1_norm[0:1])
    for l in range(nl):
        p = {}
        params.append(p)
        ffn_params(p, "ffn1", l, full)
        behind = [_push_plan(own_shards(0, names), gather=True) if l == 0 else None for names in GATHER_BEHIND]
        h, xn, s1, got = _ffn_fwd("ffn1", h, xn, p["ffn1_w_gu"], p["ffn1_w_down"], mix_norm[l:l + 1], behind[0:2])
        if l == 0:
            full.update(zip(GATHER_BEHIND[0], got[0]))
            full.update(zip(GATHER_BEHIND[1], got[1]))
        mix_params(p, l, full)
        push = _push_plan(own_shards(l + 1, SHARDED), gather=True) if l + 1 < nl else None
        h, xn, s2, got = _mix_fwd(h, xn, p, tabs, lbs[l:l + 1], call, (push, behind[2], behind[3]), ffn2_norm[l:l + 1])
        if l == 0:
            full.update(zip(GATHER_BEHIND[2], got[1]))
            full.update(zip(GATHER_BEHIND[3], got[2]))
        ffn_params(p, "ffn2", l, full)
        h, xn, s3, _ = _ffn_fwd("ffn2", h, xn, p["ffn2_w_gu"], p["ffn2_w_down"],
                                ffn1_norm[l + 1:l + 2] if l + 1 < nl else None)
        if push is not None:
            full = dict(zip(SHARDED, got[0]))
        saved.append((s1, s2, s3))
    dh, d_final, loss_part = _loss_head(h, final_norm.reshape(1, d), tgt, n_real)
    loss = lax.psum(loss_part[0, 0], ("x", "y", "c"))

    def shard(n, g):
        g = g.astype(BF16)
        return g.reshape(N_DEV, -1, g.shape[-1]) if n in ROW_SHARDED else _cols_shards(g)

    def scatter_plan(names, gm, layer):
        idx = [SHARDED.index(n) for n in names]
        return idx, _push_plan([shard(n, gm[n]) for n in names], gather=False, bufs=[slots[i] for i in idx], layer=layer)

    def landed(idx, pushed):
        for i, s in zip(idx, pushed):
            slots[i] = s

    per_layer = []
    slots = [jnp.zeros((N_DEV,) + w[n].shape, BF16) for n in SHARDED]
    above = None
    for l in reversed(range(nl)):
        p = params[l]
        s1, s2, s3 = saved[l]
        dh, dn2, dgu2, ddown2, _ = _ffn_bwd("ffn2", dh, s3, p["ffn2_norm"], p["ffn2_w_gu_t"], p["ffn2_w_down_t"])
        gm = dict(ffn2_norm=dn2, ffn2_w_gu=dgu2, ffn2_w_down=ddown2)
        idx, push = (None, None) if above is None else scatter_plan(SHARDED, above, l + 1)
        dh, gmix, got = _mix_bwd(dh, s2, p, tabs, lbs[l:l + 1], call, call_t, (push, None, None))
        if push is not None:
            landed(idx, got[0])
        gm.update(gmix)
        behind = [scatter_plan(names, gm, 0) if l == 0 else (None, None) for names in SCATTER_BEHIND]
        dh, dn1, dgu1, ddown1, got = _ffn_bwd("ffn1", dh, s1, p["ffn1_norm"], p["ffn1_w_gu_t"], p["ffn1_w_down_t"],
                                              [b[1] for b in behind])
        if l == 0:
            landed(behind[0][0], got[0])
            landed(behind[1][0], got[1])
        gm.update(ffn1_norm=dn1, ffn1_w_gu=dgu1, ffn1_w_down=ddown1)
        per_layer.append(gm)
        above = gm
    per_layer.reverse()
    grad_x = dh[N_META:n_real][None]

    last = [n for n in SHARDED if n not in sum(SCATTER_BEHIND, ())]
    idx = [SHARDED.index(n) for n in last]
    landed(idx, _exchange("scatter_grads", [shard(n, above[n]) for n in last], gather=False,
                          bufs=[slots[i] for i in idx], layer=0))
    grads, delta, new_m, new_v = {}, {}, {}, {}
    for n, s in zip(SHARDED, slots):
        grads[n], delta[n], new_m[n], new_v[n] = _adam_sharded("adam_" + n, s, w[n], mom[n], var[n])

    small = {n: jnp.concatenate([gm[n] for gm in per_layer], axis=0) for n in SMALL if n not in ("hg_lb_raw", "final_norm")}
    small["hg_lb_raw"] = _lb_bwd(hg_lb_raw, jnp.concatenate([gm["lb"] for gm in per_layer], axis=0))
    small["final_norm"] = d_final
    packed = jnp.concatenate([_pack_small(small), dh[:N_META].reshape(-1, LANE)], axis=0)
    summed = _sum_replicated(_exchange("gather_small", [packed], gather=True)[0])
    n_small = packed.shape[0] - N_META * d // LANE
    sd, sm, sv = _adam_small("adam_small", summed[:n_small], _pack_small(w), _pack_small(mom), _pack_small(var))
    grads.update(_unpack_small(summed[:n_small], w))
    delta.update(_unpack_small(sd, w))
    new_m.update(_unpack_small(sm, w))
    new_v.update(_unpack_small(sv, w))
    dmeta = lax.dynamic_slice_in_dim(summed[n_small:].reshape(N_META, d), me * (d // N_DEV), d // N_DEV, axis=1)
    grads["meta_tokens"] = dmeta
    delta["meta_tokens"], new_m["meta_tokens"], new_v["meta_tokens"] = _adam_small(
        "adam_meta", dmeta, meta_tokens, m_meta_tokens, v_meta_tokens)

    return (loss, grad_x, *[grads[n] for n in WEIGHTS], *[delta[n] for n in WEIGHTS], *[new_m[n] for n in WEIGHTS],
            *[new_v[n] for n in WEIGHTS])
```

```python
import functools
import math

import jax
import jax.numpy as jnp
from jax import lax
from jax.experimental import pallas as pl
from jax.experimental.pallas import tpu as pltpu

F32 = jnp.float32
BF16 = jnp.bfloat16

N_DEV = 8
N_META = 16
MLA_HEADS = 8
Q_LORA = 384
KV_LORA = 256
QK_NOPE = 64
QK_ROPE = 32
V_HEAD = 64
ROPE_THETA = 10000.0
HG_HEADS = 4
HG_DIM = 128
EPS = 1e-6
NEG_BIG = -1e30
F_MIN = 1e-20
ADAM_LR = 0.001
ADAM_B1 = 0.9
ADAM_B2 = 0.999
ADAM_EPS = 1e-08
ADAM_WD = 0.01
ADAM_STEP = 10

LANE = 128
ROW_ALIGN = 256
HG_CHUNK = 128
HG_LEVELS = (64, 32, 16, 8, 4, 2, 1)
VMEM_LIMIT = 48 * 1024 * 1024

Z_CQ, Z_CKV, Z_KPE, Z_KPESW, Z_HQ, Z_HF, Z_HI, Z_HG, Z_GA, Z_GB, Z_W = 0, 384, 640, 768, 1024, 1536, 2048, 2560, 3072, 4096, 5120
ATTN_SCALE = float((QK_NOPE + QK_ROPE) ** -0.5)
LOG2E = 1.4426950408889634
ATTN_C2 = ATTN_SCALE * LOG2E
HG_SCALE = float(HG_DIM ** -0.5)


def _cparams(sem):
    return pltpu.CompilerParams(dimension_semantics=sem, vmem_limit_bytes=VMEM_LIMIT)


def _tile(n, cap):
    if n <= cap:
        return n
    best = None
    for t in range(LANE, cap + 1, LANE):
        if n % t == 0:
            best = t
    assert best is not None, (n, cap)
    return best


def _row_tile(m):
    for t in (768, 384, 256, 128):
        if m % t == 0:
            return t
    raise ValueError(m)


def _bf(x):
    return x.astype(BF16)


def _dot(a, b):
    return jnp.dot(a, b, preferred_element_type=F32)


def _dot_nt(a, b):
    return lax.dot_general(a, b, (((1,), (1,)), ((), ())), preferred_element_type=F32)


def _dot_tn(a, b):
    return lax.dot_general(a, b, (((0,), (0,)), ((), ())), preferred_element_type=F32)


def _sigmoid(x):
    return 1.0 / (1.0 + jnp.exp(-x))


def _mm(name, a_list, pairs, extras, outs, fn, *, tm, tn, n, push=None):
    m = a_list[0].shape[0]
    na, nb, ne, no = len(a_list), len(pairs), len(extras), len(outs)

    def body(*refs):
        a_refs = refs[:na]
        b_refs = refs[na:na + nb]
        e_refs = refs[na + nb:na + nb + ne]
        o_refs = refs[na + nb + ne:]
        a_vals = [_bf(r[...]) for r in a_refs]
        accs = [_dot(a_vals[ai], b_refs[k][...]) for k, (ai, _, _) in enumerate(pairs)]
        res = fn(accs, [r[...] for r in e_refs])
        for r, v in zip(o_refs, res):
            r[...] = v.astype(r.dtype)

    def spec(block_shape, index_map):
        return pl.BlockSpec(block_shape, functools.partial(lambda j, i, im: im(i, j), im=index_map))

    in_specs = [spec((tm, a.shape[1]), lambda i, j: (i, 0)) for a in a_list]
    for _, b, off in pairs:
        in_specs.append(spec((b.shape[0], tn), functools.partial(lambda i, j, off: (0, j + off), off=off)))
    in_specs += [spec(bs, im) for _, bs, im in extras]
    out_specs = [spec(bs, im) for _, _, bs, im in outs]
    out_shape = [jax.ShapeDtypeStruct(s, d) for s, d, _, _ in outs]
    args = (*a_list, *[b for _, b, _ in pairs], *[e for e, _, _ in extras])
    if push is not None:
        return _call_carrying_push(body, push, name=name, grid=(n // tn, m // tm), in_specs=in_specs, out_specs=out_specs,
                                   out_shape=out_shape, scratch_shapes=[], args=args)
    return pl.pallas_call(
        body, name=name, grid=(n // tn, m // tm), in_specs=in_specs, out_specs=out_specs, out_shape=out_shape,
        compiler_params=_cparams(("parallel", "parallel")),
    )(*args)


def _tile_spec(tm, tn, col_off=0):
    return (tm, tn), functools.partial(lambda i, j, off: (i, j + off), off=col_off)


def _mm_tn(name, a, b, *, alpha, out_dtype):
    t, k = a.shape
    n = b.shape[1]
    tk, tn = _tile(k, 1408), _tile(n, 1408)
    tt = next(c for c in (768, 512, 256) if t % c == 0)
    nt = t // tt

    def body(a_ref, b_ref, o_ref, acc_ref):
        s = pl.program_id(2)

        @pl.when(s == 0)
        def _():
            acc_ref[...] = jnp.zeros_like(acc_ref)

        acc_ref[...] += _dot_tn(_bf(a_ref[...]), _bf(b_ref[...]))

        @pl.when(s == nt - 1)
        def _():
            o_ref[...] = (alpha * acc_ref[...]).astype(o_ref.dtype)

    return pl.pallas_call(
        body, name=name, grid=(k // tk, n // tn, nt),
        in_specs=[pl.BlockSpec((tt, tk), lambda i, j, s: (s, i)), pl.BlockSpec((tt, tn), lambda i, j, s: (s, j))],
        out_specs=pl.BlockSpec((tk, tn), lambda i, j, s: (i, j)),
        out_shape=jax.ShapeDtypeStruct((k, n), out_dtype),
        scratch_shapes=[pltpu.VMEM((tk, tn), F32)],
        compiler_params=_cparams(("parallel", "parallel", "arbitrary")),
    )(a, b)


def _rms_parts(x):
    r = lax.rsqrt(jnp.mean(x * x, axis=-1, keepdims=True) + EPS)
    return r, x * r


def _rms_bwd_math(x, w, dxn):
    r, xhat = _rms_parts(x)
    t = dxn * w
    dx = r * (t - xhat * jnp.mean(t * xhat, axis=-1, keepdims=True))
    dw = jnp.sum(dxn * xhat, axis=0, keepdims=True)
    return dx, dw


def _rms_fwd(name, h, w):
    t, d = h.shape
    tm = _row_tile(t)

    def body(h_ref, w_ref, o_ref):
        _, xhat = _rms_parts(h_ref[...])
        o_ref[...] = (xhat * w_ref[...]).astype(o_ref.dtype)

    return pl.pallas_call(
        body, name=name, grid=(t // tm,),
        in_specs=[pl.BlockSpec((tm, d), lambda i: (i, 0)), pl.BlockSpec((1, d), lambda i: (0, 0))],
        out_specs=pl.BlockSpec((tm, d), lambda i: (i, 0)),
        out_shape=jax.ShapeDtypeStruct((t, d), BF16),
        compiler_params=_cparams(("parallel",)),
    )(h, w)


def _rms_bwd(name, h, w, dxn, dh_in):
    t, d = h.shape
    tm = _row_tile(t)

    def body(h_ref, w_ref, dxn_ref, dh_ref, o_ref, dw_ref):
        dx, dw = _rms_bwd_math(h_ref[...], w_ref[...], dxn_ref[...])
        o_ref[...] = dh_ref[...] + dx

        @pl.when(pl.program_id(0) == 0)
        def _():
            dw_ref[...] = jnp.zeros_like(dw_ref)

        dw_ref[...] += dw

    row = pl.BlockSpec((tm, d), lambda i: (i, 0))
    vec = pl.BlockSpec((1, d), lambda i: (0, 0))
    return pl.pallas_call(
        body, name=name, grid=(t // tm,),
        in_specs=[row, vec, row, row],
        out_specs=[row, vec],
        out_shape=[jax.ShapeDtypeStruct((t, d), F32), jax.ShapeDtypeStruct((1, d), F32)],
        compiler_params=_cparams(("arbitrary",)),
    )(h, w, dxn, dh_in)


def _loss_head(h, w, tgt, n_real):
    t, d = h.shape
    tm = _row_tile(t)

    def body(h_ref, w_ref, t_ref, dh_ref, dw_ref, loss_ref):
        i = pl.program_id(0)
        x = h_ref[...]
        wv = w_ref[...]
        _, xhat = _rms_parts(x)
        rows = i * tm + lax.broadcasted_iota(jnp.int32, (tm, 1), 0)
        valid = (rows >= N_META) & (rows < n_real)
        e = jnp.where(valid, xhat * wv - t_ref[...], 0.0)
        dx, dw = _rms_bwd_math(x, wv, e * (1.0 / d))
        dh_ref[...] = dx

        @pl.when(i == 0)
        def _():
            dw_ref[...] = jnp.zeros_like(dw_ref)
            loss_ref[...] = jnp.zeros_like(loss_ref)

        dw_ref[...] += dw
        loss_ref[...] += (0.5 / d) * jnp.sum(jnp.sum(e * e, axis=-1, keepdims=True), axis=0, keepdims=True)

    row = pl.BlockSpec((tm, d), lambda i: (i, 0))
    vec = pl.BlockSpec((1, d), lambda i: (0, 0))
    return pl.pallas_call(
        body, name="loss_head", grid=(t // tm,),
        in_specs=[row, vec, row],
        out_specs=[row, vec, pl.BlockSpec((1, 1), lambda i: (0, 0))],
        out_shape=[jax.ShapeDtypeStruct((t, d), F32), jax.ShapeDtypeStruct((1, d), F32), jax.ShapeDtypeStruct((1, 1), F32)],
        compiler_params=_cparams(("arbitrary",)),
    )(h, w, tgt)


def _heads(x):
    return jnp.concatenate([x] * MLA_HEADS, axis=1)


def _mla_front(z, qn_w, kvn_w, tabs, w_uq, w_ukv):
    t = z.shape[0]
    tm = _attn_block(t)
    wq = MLA_HEADS * LANE

    def body(z_ref, qw_ref, kw_ref, cq_ref, ck_ref, sq_ref, wuq_ref, wukv_ref, cqn_ref, ckvn_ref, q_ref, kv_ref, vt_ref):
        zz = z_ref[...]
        _, qhat = _rms_parts(zz[:, Z_CQ:Z_CKV])
        _, khat = _rms_parts(zz[:, Z_CKV:Z_KPE])
        cqn = (qhat * qw_ref[...]).astype(BF16)
        ckvn = (khat * kw_ref[...]).astype(BF16)
        cqn_ref[...] = cqn
        ckvn_ref[...] = ckvn
        krot = zz[:, Z_KPE:Z_KPESW] * ck_ref[...] + zz[:, Z_KPESW:Z_KPESW + LANE] * sq_ref[...]
        qq = _dot(cqn, wuq_ref[...])
        q_ref[...] = (qq[:, :wq] * _heads(cq_ref[...]) + qq[:, wq:] * _heads(sq_ref[...])).astype(BF16)
        kvv = _dot(ckvn, wukv_ref[...]) + _kv_pattern(krot)
        kv_ref[...] = kvv.astype(BF16)
        for hd in range(MLA_HEADS):
            vt_ref[hd, 0] = kvv[:, 2 * LANE * hd + LANE:2 * LANE * (hd + 1)].T.astype(BF16)

    def rows(wd):
        return pl.BlockSpec((tm, wd), lambda i: (i, 0))

    def whole(a):
        return pl.BlockSpec(a.shape, lambda i: (0, 0))

    return pl.pallas_call(
        body, name="mla_front", grid=(t // tm,),
        in_specs=[rows(Z_HQ), whole(qn_w), whole(kvn_w), rows(LANE), rows(LANE), rows(LANE), whole(w_uq), whole(w_ukv)],
        out_specs=[rows(Q_LORA), rows(KV_LORA), rows(wq), rows(2 * wq),
                   pl.BlockSpec((MLA_HEADS, 1, LANE, tm), lambda i: (0, i, 0, 0))],
        out_shape=[jax.ShapeDtypeStruct((t, Q_LORA), BF16), jax.ShapeDtypeStruct((t, KV_LORA), BF16),
                   jax.ShapeDtypeStruct((t, wq), BF16), jax.ShapeDtypeStruct((t, 2 * wq), BF16),
                   jax.ShapeDtypeStruct((MLA_HEADS, t // tm, LANE, tm), BF16)],
        compiler_params=_cparams(("parallel",)),
    )(z, qn_w, kvn_w, *tabs, w_uq, w_ukv)


def _mla_back(z, qn_w, kvn_w, tabs, cqn, ckvn, dq, dkv, w_uq_t, w_ukv_t):
    t = z.shape[0]
    tm = next(c for c in (768, 512, 256) if t % c == 0)
    wq = MLA_HEADS * LANE

    def body(z_ref, qw_ref, kw_ref, cq_ref, ck_ref, sq_ref, cqn_ref, ckvn_ref, dq_ref, dkv_ref, wuqt_ref, wukvt_ref,
             dz_ref, dwuq_ref, dwukv_ref, dqw_ref, dkw_ref):
        @pl.when(pl.program_id(0) == 0)
        def _():
            for r in (dwuq_ref, dwukv_ref, dqw_ref, dkw_ref):
                r[...] = jnp.zeros_like(r)

        zz = z_ref[...]
        d = dq_ref[...] * ATTN_SCALE
        dqq = jnp.concatenate([d * _heads(cq_ref[...]), d * _heads(sq_ref[...])], axis=1).astype(BF16)
        dkv_v = dkv_ref[...]
        dwuq_ref[...] += _dot_tn(cqn_ref[...], dqq)
        dwukv_ref[...] += _dot_tn(ckvn_ref[...], dkv_v)
        dcq, dqw = _rms_bwd_math(zz[:, Z_CQ:Z_CKV], qw_ref[...], _dot(dqq, wuqt_ref[...]))
        dckv, dkw = _rms_bwd_math(zz[:, Z_CKV:Z_KPE], kw_ref[...], _dot(dkv_v, wukvt_ref[...]))
        dkr = jnp.zeros((tm, LANE), F32)
        for hd in range(MLA_HEADS):
            dkr = dkr + dkv_v[:, 2 * LANE * hd:2 * LANE * hd + LANE].astype(F32)
        dz_ref[...] = jnp.concatenate(
            [dcq, dckv, dkr * ck_ref[...], dkr * sq_ref[...], jnp.zeros((tm, Z_HQ - Z_KPESW - LANE), F32)], axis=1
        ).astype(BF16)
        dqw_ref[...] += dqw
        dkw_ref[...] += dkw

    def rows(wd):
        return pl.BlockSpec((tm, wd), lambda i: (i, 0))

    def whole(shape):
        return pl.BlockSpec(shape, lambda i: (0, 0))

    return pl.pallas_call(
        body, name="mla_back", grid=(t // tm,),
        in_specs=[rows(Z_HQ), whole(qn_w.shape), whole(kvn_w.shape), rows(LANE), rows(LANE), rows(LANE), rows(Q_LORA),
                  rows(KV_LORA), rows(wq), rows(2 * wq), whole(w_uq_t.shape), whole(w_ukv_t.shape)],
        out_specs=[rows(Z_HQ), whole((Q_LORA, 2 * wq)), whole((KV_LORA, 2 * wq)), whole((1, Q_LORA)), whole((1, KV_LORA))],
        out_shape=[jax.ShapeDtypeStruct((t, Z_HQ), BF16), jax.ShapeDtypeStruct((Q_LORA, 2 * wq), F32),
                   jax.ShapeDtypeStruct((KV_LORA, 2 * wq), F32), jax.ShapeDtypeStruct((1, Q_LORA), F32),
                   jax.ShapeDtypeStruct((1, KV_LORA), F32)],
        compiler_params=_cparams(("arbitrary",)),
    )(z, qn_w, kvn_w, *tabs, cqn, ckvn, dq, dkv, w_uq_t, w_ukv_t)


def _attn_block(t):
    for b in (768, 512, 256):
        if t % b == 0:
            return b
    raise ValueError(t)


def _call_carrying_push(body, push, *, name, grid, in_specs, out_specs, out_shape, scratch_shapes, args):
    if push is None:
        outs = pl.pallas_call(body, name=name, grid=grid, in_specs=in_specs, out_specs=out_specs, out_shape=out_shape,
                              scratch_shapes=scratch_shapes, compiler_params=_cparams(("parallel", "arbitrary")))(*args)
        return outs, None
    n_in, n_out, n_scr, n_pin, nk = len(in_specs), len(out_specs), len(scratch_shapes), len(push["ins"]), push["nk"]

    def carrying(*refs):
        o0 = n_in + n_pin
        s0 = o0 + n_out + nk
        pins, pouts, sems = refs[n_in:o0], refs[o0 + n_out:s0], refs[s0 + n_scr:]
        a, b = pl.program_id(0), pl.program_id(1)

        @pl.when((a == 0) & (b == 0))
        def _():
            for cp in _push_copies(push, pins, pouts, sems):
                cp.start()

        body(*refs[:n_in], *refs[o0:o0 + n_out], *refs[s0:s0 + n_scr])

        @pl.when((a == grid[0] - 1) & (b == grid[1] - 1))
        def _():
            for cp in _push_copies(push, pins, pouts, sems):
                cp.wait()

    outs = pl.pallas_call(
        carrying, name=name + "_push", grid=grid, in_specs=list(in_specs) + push["in_specs"],
        out_specs=list(out_specs) + push["out_specs"], out_shape=list(out_shape) + push["outs"],
        scratch_shapes=list(scratch_shapes) + push["sems"], input_output_aliases=_push_aliases(push, n_in, n_out),
        compiler_params=_cparams(("arbitrary", "arbitrary")))(*args, *push["ins"])
    return outs[:n_out], outs[n_out:]


def _attn_fwd(q, kv, v_t, push=None):
    t = q.shape[0]
    bq = bk = _attn_block(t)
    nq = t // bq

    def body(q_ref, k_ref, vt_ref, o_ref, lse_ref):
        i = pl.program_id(1)
        qv = q_ref[...]
        qpos = i * bq + lax.broadcasted_iota(jnp.int32, (1, bq), 1)

        def block(j, carry, masked):
            m, l, acc = carry
            s = _dot_nt(k_ref[pl.ds(pl.multiple_of(j * bk, bk), bk), :], qv)
            if masked:
                kpos = j * bk + lax.broadcasted_iota(jnp.int32, (bk, 1), 0)
                s = jnp.where(kpos <= qpos, s, NEG_BIG)
            m_new = jnp.maximum(m, jnp.max(s, axis=0, keepdims=True))
            p = jnp.exp2((s - m_new) * ATTN_C2)
            a = jnp.exp2((m - m_new) * ATTN_C2)
            return m_new, a * l + jnp.sum(p, axis=0, keepdims=True), a * acc + _dot(vt_ref[0, j], _bf(p))

        init = (jnp.full((1, bq), NEG_BIG, F32), jnp.zeros((1, bq), F32), jnp.zeros((LANE, bq), F32))
        carry = lax.fori_loop(0, i, functools.partial(block, masked=False), init)
        m, l, acc = block(i, carry, True)
        o_ref[...] = (acc / l).T.astype(o_ref.dtype)
        lse_ref[0, 0] = m * ATTN_C2 + jnp.log(l) * LOG2E

    return _call_carrying_push(
        body, push, name="attn_fwd", grid=(MLA_HEADS, nq),
        in_specs=[pl.BlockSpec((bq, LANE), lambda h, i: (i, h)),
                  pl.BlockSpec((t, LANE), lambda h, i: (0, 2 * h)),
                  pl.BlockSpec((1, nq, LANE, bk), lambda h, i: (h, 0, 0, 0))],
        out_specs=[pl.BlockSpec((bq, LANE), lambda h, i: (i, h)),
                   pl.BlockSpec((1, 1, 1, bq), lambda h, i: (h, i, 0, 0))],
        out_shape=[jax.ShapeDtypeStruct((t, MLA_HEADS * LANE), BF16), jax.ShapeDtypeStruct((MLA_HEADS, nq, 1, bq), F32)],
        scratch_shapes=[], args=(q, kv, v_t))


def _attn_bwd(q, kv, o, do, lse, push=None):
    t = q.shape[0]
    bk = bw = _attn_block(t)
    nk, nw = t // bk, t // bw

    def body(q_ref, o_ref, do_ref, k_ref, v_ref, lse_ref, dq_ref, dkv_ref, dl_ref):
        j = pl.program_id(1)

        @pl.when(j == 0)
        def _():
            dq_ref[...] = jnp.zeros_like(dq_ref)
            for i in range(nw):
                rows = slice(i * bw, (i + 1) * bw)
                d = jnp.sum(o_ref[rows, :].astype(F32) * do_ref[rows, :].astype(F32), axis=1, keepdims=True)
                dl_ref[i] = jnp.broadcast_to(d, (bw, LANE)).T[0:1, :]

        kb = k_ref[...]
        vb = v_ref[...]
        kpos = j * bk + lax.broadcasted_iota(jnp.int32, (bk, 1), 0)

        def block(i, carry, masked):
            dk, dv = carry
            rows = pl.ds(pl.multiple_of(i * bw, bw), bw)
            qb = q_ref[rows, :]
            dob = do_ref[rows, :]
            pt = jnp.exp2(_dot_nt(kb, qb) * ATTN_C2 - lse_ref[0, i])
            if masked:
                qpos = i * bw + lax.broadcasted_iota(jnp.int32, (1, bw), 1)
                pt = jnp.where(kpos <= qpos, pt, 0.0)
            dv = dv + _dot(_bf(pt), dob)
            dst = _bf(pt * (_dot_nt(vb, dob) - dl_ref[i]))
            dk = dk + _dot(dst, qb)
            dq_ref[rows, :] += _dot_tn(dst, kb)
            return dk, dv

        i0 = j
        carry = block(i0, (jnp.zeros((bk, LANE), F32), jnp.zeros((bk, LANE), F32)), True)
        dk, dv = lax.fori_loop(i0 + 1, nw, functools.partial(block, masked=False), carry)
        dkv_ref[...] = jnp.concatenate([dk * ATTN_SCALE, dv], axis=1).astype(dkv_ref.dtype)

    head_rows = pl.BlockSpec((t, LANE), lambda h, j: (0, h))
    return _call_carrying_push(
        body, push, name="attn_bwd", grid=(MLA_HEADS, nk),
        in_specs=[head_rows, head_rows, head_rows,
                  pl.BlockSpec((bk, LANE), lambda h, j: (j, 2 * h)), pl.BlockSpec((bk, LANE), lambda h, j: (j, 2 * h + 1)),
                  pl.BlockSpec((1, nw, 1, bw), lambda h, j: (h, 0, 0, 0))],
        out_specs=[head_rows, pl.BlockSpec((bk, 2 * LANE), lambda h, j: (j, h))],
        out_shape=[jax.ShapeDtypeStruct((t, MLA_HEADS * LANE), F32), jax.ShapeDtypeStruct((t, 2 * MLA_HEADS * LANE), BF16)],
        scratch_shapes=[pltpu.VMEM((nw, 1, bw), F32)], args=(q, o, do, kv, kv, lse))


def _hg_tables():
    c = HG_CHUNK
    tri = (jnp.arange(c)[:, None] >= jnp.arange(c)[None, :]).astype(BF16)
    return tri, tri.T


def _hg_level_ref(b, m):
    c = HG_CHUNK
    if 2 * m >= 8:
        x = b.reshape(c // (2 * m), 2 * m, c)
        return jnp.broadcast_to(x[:, m - 1:m, :], x.shape).reshape(c, c)
    row = lax.broadcasted_iota(jnp.int32, (c, 1), 0)
    if m == 2:
        pos = row & 3
        return jnp.where(pos == 0, pltpu.roll(b, c - 1, 0),
                         jnp.where(pos == 1, b, jnp.where(pos == 2, pltpu.roll(b, 1, 0), pltpu.roll(b, 2, 0))))
    return jnp.where((row & 1) == 0, b, pltpu.roll(b, 1, 0))


def _hg_level_ref_t(d, m):
    c = HG_CHUNK
    row = lax.broadcasted_iota(jnp.int32, (c, 1), 0)
    if 2 * m >= 8:
        x = d.reshape(c // (2 * m), 2 * m, c)
        s = jnp.broadcast_to(jnp.sum(x, axis=1, keepdims=True), x.shape).reshape(c, c)
        return jnp.where((row & (2 * m - 1)) == m - 1, s, 0.0)
    if m == 2:
        s = pltpu.roll(d, 1, 0) + d + pltpu.roll(d, c - 1, 0) + pltpu.roll(d, c - 2, 0)
        return jnp.where((row & 3) == 1, s, 0.0)
    return jnp.where((row & 1) == 0, d + pltpu.roll(d, c - 1, 0), 0.0)


def _table_dot(table, x):
    hi = _bf(x)
    rest = x - hi.astype(F32)
    mid = _bf(rest)
    lo = _bf(rest - mid.astype(F32))
    out = _dot(table, jnp.concatenate([hi, mid, lo], axis=1))
    n = x.shape[1]
    return out[:, 0:n] + out[:, n:2 * n] + out[:, 2 * n:3 * n]


def _hg_gates(hq, hf, lb):
    sg = _sigmoid(hf)
    sn = _sigmoid(-hf)
    f = lb + (1.0 - lb) * sg
    q = hq * _sigmoid(hq)
    g = jnp.log(jnp.maximum(f, F_MIN))
    k = (1.0 - lb) * sn
    return q, k, g, f, sg, sn


def _hg_level_masks(m):
    c = HG_CHUNK
    row = lax.broadcasted_iota(jnp.int32, (c, 1), 0)
    col = lax.broadcasted_iota(jnp.int32, (1, c), 1)
    shift = (2 * m).bit_length() - 1
    up = (row & m) != 0
    same = lax.shift_right_logical(row, shift) == lax.shift_right_logical(col, shift)
    return up, same


def _hg_level_factors(b, bref, up):
    arg = b - bref
    e = jnp.exp(jnp.where(up, arg, -arg))
    return jnp.where(up, e, 0.0), jnp.where(up, 0.0, e)


def _hg_intra(q, k, b, masks):
    c = HG_CHUNK
    row = lax.broadcasted_iota(jnp.int32, (c, 1), 0)
    col = lax.broadcasted_iota(jnp.int32, (1, c), 1)
    a = jnp.where(row == col, _dot_nt(_bf(q), _bf(k)), 0.0)
    parts = []
    for m, (up, same) in zip(HG_LEVELS, masks):
        eq, ek = _hg_level_factors(b, _hg_level_ref(b, m), up)
        qt, kt = q * eq, k * ek
        a = a + jnp.where(same, _dot_nt(_bf(qt), _bf(kt)), 0.0)
        parts.append((eq, ek, qt, kt))
    return a, parts


def _hg_chunk_fwd(hq, hf, hi, hg, lb, nw, st, call, masks):
    q, k, g, _, _, _ = _hg_gates(hq, hf, lb)
    b = _table_dot(call, g)
    a, _ = _hg_intra(q, k, b, masks)
    v16 = _bf(hi)
    o = _dot(_bf(a * HG_SCALE), v16) + _dot_nt(_bf(q * jnp.exp(b) * HG_SCALE), _bf(st))
    bl = b[HG_CHUNK - 1:HG_CHUNK]
    ke = k * jnp.exp(bl - b)
    st_new = st * jnp.exp(bl) + _dot(_bf(hi.T), _bf(ke))
    r = lax.rsqrt(jnp.mean(o * o, axis=-1, keepdims=True) + EPS)
    y = o * r * nw * (hg * _sigmoid(hg))
    return y, st_new


def _hg_chunk_bwd(hq, hf, hi, hg, lb, nw, st, call, call_t, dy, dst_new, masks):
    c = HG_CHUNK
    q, k, g, f, sg, sn = _hg_gates(hq, hf, lb)
    b = _table_dot(call, g)
    a, parts = _hg_intra(q, k, b, masks)
    v16 = _bf(hi)
    st16 = _bf(st)
    eb = jnp.exp(b)
    qe = q * eb * HG_SCALE
    a16 = _bf(a * HG_SCALE)
    o = _dot(a16, v16) + _dot_nt(_bf(qe), st16)
    bl = b[c - 1:c]
    el = jnp.exp(bl)
    x = jnp.exp(bl - b)
    ke = k * x
    r = lax.rsqrt(jnp.mean(o * o, axis=-1, keepdims=True) + EPS)
    shg = _sigmoid(hg)
    gate = hg * shg
    ohat = o * r
    don = dy * gate
    dhg = dy * ohat * nw * (shg * (1.0 + hg * (1.0 - shg)))
    dnw = jnp.sum(don * ohat, axis=0, keepdims=True)
    tt = don * nw
    do = r * (tt - ohat * jnp.mean(tt * ohat, axis=-1, keepdims=True))
    do16 = _bf(do)
    dst16 = _bf(dst_new)
    da = _dot_nt(do16, v16) * HG_SCALE
    dv = _dot(_bf(a16.astype(F32).T), do16) + _dot_nt(_bf(ke), dst16)
    dqe = _dot(do16, st16)
    dke = _dot(v16, dst16)
    dst = dst_new * el + _dot(_bf(do.T), _bf(qe))
    dbl = jnp.sum(dst_new * st, axis=0, keepdims=True) * el
    dk = dke * x
    dxa = dke * ke
    db = dqe * qe - dxa
    dbl = dbl + jnp.sum(dxa, axis=0, keepdims=True)
    dq = dqe * eb * HG_SCALE
    row = lax.broadcasted_iota(jnp.int32, (c, 1), 0)
    col = lax.broadcasted_iota(jnp.int32, (1, c), 1)
    ddiag = jnp.sum(jnp.where(row == col, da, 0.0), axis=1, keepdims=True)
    dq = dq + ddiag * k
    dk = dk + ddiag * q
    for (eq, ek, qt, kt), m, (_, same) in zip(parts, HG_LEVELS, masks):
        gm = jnp.where(same, da, 0.0)
        dqt = _dot(_bf(gm), _bf(kt))
        dkt = _dot(_bf(gm.T), _bf(qt))
        dq = dq + dqt * eq
        dk = dk + dkt * ek
        darg = dqt * qt - dkt * kt
        db = db + darg - _hg_level_ref_t(darg, m)
    db = db + jnp.where(row == c - 1, dbl, 0.0)
    dg = _table_dot(call_t, db)
    shq = _sigmoid(hq)
    dhq = dq * (shq * (1.0 + hq * (1.0 - shq)))
    df = jnp.where(f > F_MIN, dg / jnp.maximum(f, F_MIN), 0.0)
    dlb = jnp.sum(df * (1.0 - sg) - dk * sn, axis=0, keepdims=True)
    dhf = df * (1.0 - lb) * sg * (1.0 - sg) - dk * (1.0 - lb) * sn * (1.0 - sn)
    return dhq, dhf, dv, dhg, dlb, dnw, dst


def _hg_chunks_per_step(t):
    return 6 if t % (6 * HG_CHUNK) == 0 else 2


def _hg_col(group, h):
    return group // LANE + h


def _hgrn_fwd(z, lb, nw, call, push=None):
    t = z.shape[0]
    c, cs = HG_CHUNK, _hg_chunks_per_step(t)
    rows = c * cs
    nsteps = t // rows

    def body(hq_ref, hf_ref, hi_ref, hg_ref, lb_ref, nw_ref, call_ref, y_ref, sv_ref, st_ref):
        @pl.when(pl.program_id(1) == 0)
        def _():
            st_ref[...] = jnp.zeros_like(st_ref)

        masks = [_hg_level_masks(m) for m in HG_LEVELS]
        for u in range(cs):
            sl = slice(u * c, (u + 1) * c)
            st = st_ref[...]
            sv_ref[0, u] = st
            y, st_new = _hg_chunk_fwd(hq_ref[sl, :], hf_ref[sl, :], hi_ref[sl, :], hg_ref[sl, :], lb_ref[...], nw_ref[...],
                                      st, call_ref[...], masks)
            y_ref[sl, :] = y.astype(y_ref.dtype)
            st_ref[...] = st_new

    def zcol(group):
        return pl.BlockSpec((rows, LANE), functools.partial(lambda h, i, g: (i, _hg_col(g, h)), g=group))

    ncall = call.shape[0]
    return _call_carrying_push(
        body, push, name="hgrn_fwd", grid=(HG_HEADS, nsteps),
        in_specs=[zcol(Z_HQ), zcol(Z_HF), zcol(Z_HI), zcol(Z_HG),
                  pl.BlockSpec((1, LANE), lambda h, i: (0, h)), pl.BlockSpec((1, LANE), lambda h, i: (0, 0)),
                  pl.BlockSpec((ncall, c), lambda h, i: (0, 0))],
        out_specs=[pl.BlockSpec((rows, LANE), lambda h, i: (i, h)),
                   pl.BlockSpec((1, cs, c, c), lambda h, i: (h, i, 0, 0))],
        out_shape=[jax.ShapeDtypeStruct((t, HG_HEADS * LANE), BF16), jax.ShapeDtypeStruct((HG_HEADS, t // c, c, c), F32)],
        scratch_shapes=[pltpu.VMEM((c, c), F32)], args=(z, z, z, z, lb, nw, call))


def _hgrn_bwd(z, lb, nw, call, call_t, saved, dy):
    t = z.shape[0]
    c, cs = HG_CHUNK, _hg_chunks_per_step(t)
    rows = c * cs
    nsteps = t // rows

    def body(hq_ref, hf_ref, hi_ref, hg_ref, lb_ref, nw_ref, call_ref, callt_ref, sv_ref, dy_ref,
             dhq_ref, dhf_ref, dhi_ref, dhg_ref, dlb_ref, dnw_ref, dst_ref):
        h, i = pl.program_id(0), pl.program_id(1)

        @pl.when(i == 0)
        def _():
            dst_ref[...] = jnp.zeros_like(dst_ref)
            dlb_ref[...] = jnp.zeros_like(dlb_ref)

        @pl.when((i == 0) & (h == 0))
        def _():
            dnw_ref[...] = jnp.zeros_like(dnw_ref)

        masks = [_hg_level_masks(m) for m in HG_LEVELS]
        for u in reversed(range(cs)):
            sl = slice(u * c, (u + 1) * c)
            dhq, dhf, dhi, dhg, dlb, dnw, dst = _hg_chunk_bwd(
                hq_ref[sl, :], hf_ref[sl, :], hi_ref[sl, :], hg_ref[sl, :], lb_ref[...], nw_ref[...], sv_ref[0, u],
                call_ref[...], callt_ref[...], dy_ref[sl, :].astype(F32), dst_ref[...], masks)
            dhq_ref[sl, :] = dhq.astype(BF16)
            dhf_ref[sl, :] = dhf.astype(BF16)
            dhi_ref[sl, :] = dhi.astype(BF16)
            dhg_ref[sl, :] = dhg.astype(BF16)
            dlb_ref[...] += dlb
            dnw_ref[...] += dnw
            dst_ref[...] = dst

    def zcol(group):
        return pl.BlockSpec((rows, LANE), functools.partial(lambda h, i, g: (nsteps - 1 - i, _hg_col(g, h)), g=group))

    head_rows = pl.BlockSpec((rows, LANE), lambda h, i: (nsteps - 1 - i, h))
    ncall = call.shape[0]
    piece = jax.ShapeDtypeStruct((t, HG_HEADS * LANE), BF16)
    return pl.pallas_call(
        body, name="hgrn_bwd", grid=(HG_HEADS, nsteps),
        in_specs=[zcol(Z_HQ), zcol(Z_HF), zcol(Z_HI), zcol(Z_HG),
                  pl.BlockSpec((1, LANE), lambda h, i: (0, h)), pl.BlockSpec((1, LANE), lambda h, i: (0, 0)),
                  pl.BlockSpec((ncall, c), lambda h, i: (0, 0)), pl.BlockSpec((c, ncall), lambda h, i: (0, 0)),
                  pl.BlockSpec((1, cs, c, c), lambda h, i: (h, nsteps - 1 - i, 0, 0)), head_rows],
        out_specs=[head_rows, head_rows, head_rows, head_rows,
                   pl.BlockSpec((1, LANE), lambda h, i: (0, h)), pl.BlockSpec((1, LANE), lambda h, i: (0, 0))],
        out_shape=[piece, piece, piece, piece,
                   jax.ShapeDtypeStruct((1, HG_HEADS * LANE), F32), jax.ShapeDtypeStruct((1, LANE), F32)],
        scratch_shapes=[pltpu.VMEM((c, c), F32)],
        compiler_params=_cparams(("arbitrary", "arbitrary")),
    )(z, z, z, z, lb, nw, call, call_t, saved, dy)


def _lb_fwd(raw):
    nl = raw.shape[0]

    def body(r_ref, o_ref):
        x = r_ref[...]
        e = jnp.exp(x - jnp.max(x, axis=0, keepdims=True))
        p = e / jnp.sum(e, axis=0, keepdims=True)
        acc = jnp.zeros_like(p[0:1])
        for l in range(nl):
            if l > 0:
                acc = acc + p[l:l + 1]
            o_ref[l:l + 1, :] = acc

    return pl.pallas_call(body, name="lb_fwd", out_shape=jax.ShapeDtypeStruct(raw.shape, F32))(raw)


def _lb_bwd(raw, dlbs):
    nl = raw.shape[0]

    def body(r_ref, d_ref, o_ref):
        x = r_ref[...]
        e = jnp.exp(x - jnp.max(x, axis=0, keepdims=True))
        p = e / jnp.sum(e, axis=0, keepdims=True)
        d = d_ref[...]
        dps = [jnp.zeros_like(d[0:1])]
        for i in range(1, nl):
            acc = d[i:i + 1]
            for l in range(i + 1, nl):
                acc = acc + d[l:l + 1]
            dps.append(acc)
        dot = dps[0] * p[0:1]
        for i in range(1, nl):
            dot = dot + dps[i] * p[i:i + 1]
        for i in range(nl):
            o_ref[i:i + 1, :] = p[i:i + 1] * (dps[i] - dot)

    return pl.pallas_call(body, name="lb_bwd", out_shape=jax.ShapeDtypeStruct(raw.shape, F32))(raw, dlbs)


def _push_plan(srcs, gather, bufs=None, layer=None):
    nk = len(srcs)
    any_spec = pl.BlockSpec(memory_space=pl.ANY)
    if bufs is None:
        ins = list(srcs)
        outs = [jax.ShapeDtypeStruct(((N_DEV,) + s.shape) if gather else s.shape, s.dtype) for s in srcs]
    else:
        ins = list(srcs) + list(bufs)
        outs = [jax.ShapeDtypeStruct(b.shape, b.dtype) for b in bufs]
    sems = [pltpu.SemaphoreType.DMA((nk * N_DEV,)), pltpu.SemaphoreType.DMA((nk * N_DEV,)), pltpu.SemaphoreType.DMA((nk,))]
    return dict(nk=nk, gather=gather, layer=layer, ins=ins, in_specs=[any_spec] * len(ins), outs=outs,
                out_specs=[any_spec] * nk, sems=sems, alias_from=None if bufs is None else nk)


def _push_aliases(plan, first_in, first_out):
    if plan is None or plan["alias_from"] is None:
        return {}
    return {first_in + plan["alias_from"] + k: first_out + k for k in range(plan["nk"])}


def _push_copies(plan, in_refs, out_refs, sems):
    nk, gather, layer = plan["nk"], plan["gather"], plan["layer"]
    send_sems, recv_sems, local_sems = sems
    me = 4 * lax.axis_index("x") + 2 * lax.axis_index("y") + lax.axis_index("c")

    def landing(k):
        return out_refs[k].at[me] if layer is None else out_refs[k].at[me, layer]

    copies = [pltpu.make_async_copy(in_refs[k] if gather else in_refs[k].at[me], landing(k), local_sems.at[k])
              for k in range(nk)]
    for r in range(1, N_DEV):
        to = (me + r) % N_DEV
        for k in range(nk):
            copies.append(pltpu.make_async_remote_copy(
                src_ref=in_refs[k] if gather else in_refs[k].at[to], dst_ref=landing(k),
                send_sem=send_sems.at[k * N_DEV + r], recv_sem=recv_sems.at[k * N_DEV + r],
                device_id=(to // 4, (to // 2) % 2, to % 2), device_id_type=pl.DeviceIdType.MESH))
    return copies


def _exchange(name, srcs, gather, bufs=None, layer=None):
    plan = _push_plan(srcs, gather, bufs, layer)
    nin, nk = len(plan["ins"]), plan["nk"]

    def body(*refs):
        copies = _push_copies(plan, refs[:nin], refs[nin:nin + nk], refs[nin + nk:])
        for cp in copies:
            cp.start()
        for cp in copies:
            cp.wait()

    return pl.pallas_call(
        body, name=name, in_specs=plan["in_specs"], out_specs=plan["out_specs"], out_shape=plan["outs"],
        scratch_shapes=plan["sems"], input_output_aliases=_push_aliases(plan, 0, 0),
    )(*plan["ins"])


def _adam_math(g, w, m, v):
    m2 = ADAM_B1 * m + (1.0 - ADAM_B1) * g
    v2 = ADAM_B2 * v + (1.0 - ADAM_B2) * (g * g)
    m_hat = m2 / (1.0 - ADAM_B1 ** ADAM_STEP)
    v_hat = v2 / (1.0 - ADAM_B2 ** ADAM_STEP)
    return -ADAM_LR * (m_hat / (jnp.sqrt(v_hat) + ADAM_EPS) + ADAM_WD * w), m2, v2


def _sum_slots(ref):
    g = ref[0].astype(F32)
    for s in range(1, N_DEV):
        g = g + ref[s].astype(F32)
    return g


def _adam_sharded(name, slots, w, m, v):
    nl, a, b = w.shape
    ta = a
    for cand in range(8, 257, 8):
        if a % cand == 0:
            ta = cand

    def body(s_ref, w_ref, m_ref, v_ref, g_ref, d_ref, m2_ref, v2_ref):
        g = _sum_slots(s_ref)
        d, m2, v2 = _adam_math(g, w_ref[...], m_ref[...], v_ref[...])
        g_ref[...] = g
        d_ref[...] = d
        m2_ref[...] = m2
        v2_ref[...] = v2

    blk = pl.BlockSpec((1, ta, b), lambda l, i: (l, i, 0))
    sds = jax.ShapeDtypeStruct(w.shape, F32)
    return pl.pallas_call(
        body, name=name, grid=(nl, a // ta),
        in_specs=[pl.BlockSpec((N_DEV, 1, ta, b), lambda l, i: (0, l, i, 0)), blk, blk, blk],
        out_specs=[blk] * 4, out_shape=[sds] * 4,
        compiler_params=_cparams(("parallel", "parallel")),
    )(slots, w, m, v)


def _sum_replicated(slots):
    def body(s_ref, g_ref):
        g_ref[...] = _sum_slots(s_ref)

    return pl.pallas_call(body, name="sum_small", out_shape=jax.ShapeDtypeStruct(slots.shape[1:], F32))(slots)


def _adam_small(name, g, w, m, v):
    def body(g_ref, w_ref, m_ref, v_ref, d_ref, m2_ref, v2_ref):
        d, m2, v2 = _adam_math(g_ref[...], w_ref[...], m_ref[...], v_ref[...])
        d_ref[...] = d
        m2_ref[...] = m2
        v2_ref[...] = v2

    sds = jax.ShapeDtypeStruct(w.shape, F32)
    return pl.pallas_call(body, name=name, out_shape=[sds] * 3)(g, w, m, v)


def _cols_full(g):
    return jnp.transpose(g, (1, 0, 2)).reshape(g.shape[1], -1)


def _cols_shards(w):
    k = w.shape[0]
    return jnp.transpose(w.reshape(k, N_DEV, -1), (1, 0, 2))


def _swap_halves(x):
    half = x.shape[-1] // 2
    return jnp.concatenate([x[..., half:], x[..., :half]], axis=-1)


def _zeros_like_cols(x, n):
    return jnp.zeros(x.shape[:-1] + (n,), x.dtype)


def _w_in_internal(w):
    d = w.shape[0]
    kpe = w[:, 640:672]
    z64, z32 = jnp.zeros((d, 64), w.dtype), jnp.zeros((d, 32), w.dtype)
    return jnp.concatenate(
        [w[:, 0:640], z64, kpe, z32, z64, _swap_halves(kpe), z32, jnp.zeros((d, Z_HQ - Z_KPESW - LANE), w.dtype),
         w[:, 672:2720], w[:, 2720:4768]], axis=1)


def _w_in_grad(g):
    kpe = g[:, Z_KPE + 64:Z_KPE + 96] + _swap_halves(g[:, Z_KPESW + 64:Z_KPESW + 96])
    return jnp.concatenate([g[:, 0:640], kpe, g[:, Z_HQ:Z_W]], axis=1)


def _w_uq_internal(w):
    k = w.shape[0]
    w3 = w.reshape(k, MLA_HEADS, QK_NOPE + QK_ROPE)
    nope, rope = w3[..., :QK_NOPE], w3[..., QK_NOPE:]
    plain = jnp.concatenate([nope, rope, _zeros_like_cols(rope, 32)], axis=-1).reshape(k, -1)
    swapped = jnp.concatenate([_zeros_like_cols(nope, 64), _swap_halves(rope), _zeros_like_cols(rope, 32)], axis=-1).reshape(k, -1)
    return jnp.concatenate([plain, swapped], axis=1)


def _w_uq_grad(g):
    k = g.shape[0]
    half = MLA_HEADS * LANE
    g1, g2 = g[:, :half].reshape(k, MLA_HEADS, LANE), g[:, half:].reshape(k, MLA_HEADS, LANE)
    rope = g1[..., 64:96] + _swap_halves(g2[..., 64:96])
    return jnp.concatenate([g1[..., :64], rope], axis=-1).reshape(k, -1)


def _w_ukv_internal(w):
    k = w.shape[0]
    w3 = w.reshape(k, MLA_HEADS, QK_NOPE + V_HEAD)
    kn, vv = w3[..., :QK_NOPE], w3[..., QK_NOPE:]
    z = _zeros_like_cols(kn, 64)
    return jnp.concatenate([kn, z, vv, z], axis=-1).reshape(k, -1)


def _w_ukv_grad(g):
    k = g.shape[0]
    g3 = g.reshape(k, MLA_HEADS, 2 * LANE)
    return jnp.concatenate([g3[..., 0:64], g3[..., LANE:LANE + 64]], axis=-1).reshape(k, -1)


def _w_pa_internal(w):
    n = w.shape[1]
    w3 = w.reshape(MLA_HEADS, V_HEAD, n)
    return jnp.concatenate([w3, jnp.zeros_like(w3)], axis=1).reshape(-1, n)


def _w_pa_grad(g):
    n = g.shape[1]
    return g.reshape(MLA_HEADS, 2 * V_HEAD, n)[:, :V_HEAD].reshape(-1, n)


def _rope_tables(t):
    half = QK_ROPE // 2
    inv = ROPE_THETA ** (-jnp.arange(half, dtype=F32) / half)
    ang = jnp.arange(t, dtype=F32)[:, None] * inv[None, :]
    cos, sin = jnp.cos(ang), jnp.sin(ang)
    one, zero = jnp.ones((t, 64), F32), jnp.zeros((t, 64), F32)
    z32 = jnp.zeros((t, 32), F32)
    cq = jnp.concatenate([one, cos, cos, z32], axis=1)
    ck = jnp.concatenate([zero, cos, cos, z32], axis=1)
    sq = jnp.concatenate([zero, -sin, sin, z32], axis=1)
    return cq, ck, sq


def _mm_pushing(push, *args, **kwargs):
    res = _mm(*args, push=push, **kwargs)
    return res if push is not None else (res, None)


def _residual_mm(name, a, w, h, alpha, next_norm_w, push=None):
    t, d = h.shape
    tm = _row_tile(t)
    rows = _tile_spec(tm, d)
    if next_norm_w is None:
        (h2,), pushed = _mm_pushing(push, name, [a], [(0, w, 0)], [(h,) + rows], [((t, d), F32) + rows],
                                    lambda accs, ex: (ex[0] + alpha * accs[0],), tm=tm, tn=d, n=d)
        return h2, None, pushed

    def fn(accs, ex):
        h2 = ex[0] + alpha * accs[0]
        return h2, _rms_parts(h2)[1] * ex[1]

    (h2, xn2), pushed = _mm_pushing(push, name, [a], [(0, w, 0)],
                                    [(h,) + rows, (next_norm_w, (1, d), lambda i, j: (0, 0))],
                                    [((t, d), F32) + rows, ((t, d), BF16) + rows], fn, tm=tm, tn=d, n=d)
    return h2, xn2, pushed


def _ffn_fwd(tag, h, xn, w_gu, w_down, next_norm_w, pushes=(None, None)):
    t, d = h.shape
    dff = w_down.shape[0]
    tm, tn = _row_tile(t), _tile(dff, 1408)

    def act_fn(accs, _):
        g, u = accs
        return g, u, g * _sigmoid(g) * u

    spec = _tile_spec(tm, tn)
    (g, u, act), pushed0 = _mm_pushing(pushes[0], tag + "_gu", [xn], [(0, w_gu, 0), (0, w_gu, dff // tn)], [],
                                       [((t, dff), BF16) + spec] * 3, act_fn, tm=tm, tn=tn, n=dff)
    h2, xn2, pushed1 = _residual_mm(tag + "_down", act, w_down, h, 0.5, next_norm_w, pushes[1])
    return h2, xn2, (h, xn, g, u, act), (pushed0, pushed1)


def _ffn_bwd(tag, dh2, saved, nw, w_gu_t, w_down_t, pushes=(None, None)):
    h, xn, g, u, act = saved
    t, d = h.shape
    dff = act.shape[1]
    tm, tn = _row_tile(t), _tile(dff, 1408)

    def dact_fn(accs, ex):
        gg, uu = ex[0].astype(F32), ex[1].astype(F32)
        da = 0.5 * accs[0]
        sg = _sigmoid(gg)
        return da * uu * (sg * (1.0 + gg * (1.0 - sg))), da * (gg * sg)

    spec = _tile_spec(tm, tn)
    (dg, du), pushed0 = _mm_pushing(pushes[0], tag + "_dact", [dh2], [(0, w_down_t, 0)], [(g,) + spec, (u,) + spec],
                                    [((t, dff), BF16) + spec] * 2, dact_fn, tm=tm, tn=tn, n=dff)
    dw_down = _mm_tn(tag + "_dwdown", act, dh2, alpha=0.5, out_dtype=BF16)
    tn2 = _tile(d, 512)
    (dxn,), pushed1 = _mm_pushing(pushes[1], tag + "_dxn", [dg, du], [(0, w_gu_t[:dff], 0), (1, w_gu_t[dff:], 0)], [],
                                  [((t, d), F32) + _tile_spec(tm, tn2)], lambda accs, _: (accs[0] + accs[1],),
                                  tm=tm, tn=tn2, n=d)
    dw_gu = jnp.concatenate([_mm_tn(tag + "_dwg", xn, dg, alpha=1.0, out_dtype=BF16),
                             _mm_tn(tag + "_dwu", xn, du, alpha=1.0, out_dtype=BF16)], axis=1)
    dh, dnw = _rms_bwd(tag + "_dnorm", h, nw, dxn, dh2)
    return dh, dnw, dw_gu, dw_down, (pushed0, pushed1)


def _kv_pattern(kr):
    z = jnp.zeros_like(kr)
    return jnp.concatenate([kr, z] * MLA_HEADS, axis=1)


def _mix_fwd(h, u, p, tabs, lb, call, pushes, next_norm_w):
    push, hg_push, merge_push = pushes
    t, d = h.shape
    tm = _row_tile(t)
    tnz = _tile(Z_W, 1024)
    z, = _mm("mix_in", [u], [(0, p["w_in"], 0)], [], [((t, Z_W), F32) + _tile_spec(tm, tnz)], lambda a, _: (a[0],),
             tm=tm, tn=tnz, n=Z_W)
    cqn, ckvn, q, kv, v_t = _mla_front(z, p["q_norm"], p["kv_norm"], tabs, p["w_uq"], p["w_ukv"])
    (o_a, lse), pushed = _attn_fwd(q, kv, v_t, push)
    (o_b, st_saved), hg_pushed = _hgrn_fwd(z, lb, p["hg_norm"], call, hg_push)
    tn = _tile(d, 512)

    def merge_fn(accs, ex):
        ya, yb = accs
        return ya, yb, _sigmoid(ex[0]) * ya + _sigmoid(ex[1]) * yb

    spec = _tile_spec(tm, tn)
    (ya, yb, merged), merge_pushed = _mm_pushing(
        merge_push, "mix_merge", [o_a, o_b], [(0, p["w_pa"], 0), (1, p["w_pr"], 0)],
        [(z,) + _tile_spec(tm, tn, Z_GA // tn), (z,) + _tile_spec(tm, tn, Z_GB // tn)],
        [((t, d), BF16) + spec] * 3, merge_fn, tm=tm, tn=tn, n=d)
    h2, xn2, _ = _residual_mm("mix_out", merged, p["w_out"], h, 1.0, next_norm_w)
    return (h2, xn2, (h, u, z, cqn, ckvn, q, kv, o_a, lse, o_b, st_saved, ya, yb, merged),
            (pushed, hg_pushed, merge_pushed))


def _mix_bwd(dh2, saved, p, tabs, lb, call, call_t, pushes):
    push, dmerge_push, du_push = pushes
    h, u, z, cqn, ckvn, q, kv, o_a, lse, o_b, st_saved, ya, yb, merged = saved
    t, d = h.shape
    tm = _row_tile(t)
    tn = _tile(d, 512)
    spec = _tile_spec(tm, tn)

    def dmerge_fn(accs, ex):
        dm = accs[0]
        yav, ybv = ex[0].astype(F32), ex[1].astype(F32)
        sa, sb = _sigmoid(ex[2]), _sigmoid(ex[3])
        return dm * sa, dm * sb, dm * yav * sa * (1.0 - sa), dm * ybv * sb * (1.0 - sb)

    (dya, dyb, dga, dgb), dmerge_pushed = _mm_pushing(
        dmerge_push, "mix_dmerge", [dh2], [(0, p["w_out_t"], 0)],
        [(ya,) + spec, (yb,) + spec, (z,) + _tile_spec(tm, tn, Z_GA // tn), (z,) + _tile_spec(tm, tn, Z_GB // tn)],
        [((t, d), BF16) + spec] * 4, dmerge_fn, tm=tm, tn=tn, n=d)
    dw_out = _mm_tn("mix_dwout", merged, dh2, alpha=1.0, out_dtype=BF16)
    wq = MLA_HEADS * LANE
    do_a, = _mm("mix_doa", [dya], [(0, p["w_pa_t"], 0)], [], [((t, wq), BF16) + _tile_spec(tm, wq)], lambda a, _: (a[0],),
                tm=tm, tn=wq, n=wq)
    wr = HG_HEADS * LANE
    do_b, = _mm("mix_dob", [dyb], [(0, p["w_pr_t"], 0)], [], [((t, wr), BF16) + _tile_spec(tm, wr)], lambda a, _: (a[0],),
                tm=tm, tn=wr, n=wr)
    dw_pa = _mm_tn("mix_dwpa", o_a, dya, alpha=1.0, out_dtype=F32)
    dw_pr = _mm_tn("mix_dwpr", o_b, dyb, alpha=1.0, out_dtype=BF16)
    (dq, dkv), pushed = _attn_bwd(q, kv, o_a, do_a, lse, push)
    dz_mla, dw_uq, dw_ukv, dqn, dkvn = _mla_back(z, p["q_norm"], p["kv_norm"], tabs, cqn, ckvn, dq, dkv,
                                                 p["w_uq_t"], p["w_ukv_t"])
    dhq, dhf, dhi, dhg, dlb, dhgn = _hgrn_bwd(z, lb, p["hg_norm"], call, call_t, st_saved, do_b)
    dz = jnp.concatenate([dz_mla, dhq, dhf, dhi, dhg, dga, dgb], axis=1)
    (du,), du_pushed = _mm_pushing(du_push, "mix_du", [dz], [(0, p["w_in_t"], 0)], [], [((t, d), F32) + spec],
                                   lambda a, _: (a[0],), tm=tm, tn=tn, n=d)
    dw_in = _mm_tn("mix_dwin", u, dz, alpha=1.0, out_dtype=F32)
    dh, dmn = _rms_bwd("mix_dnorm", h, p["mix_norm"], du, dh2)
    grads = dict(mix_norm=dmn, q_norm=dqn, kv_norm=dkvn, hg_norm=dhgn, lb=dlb, w_in=_w_in_grad(dw_in), w_uq=_w_uq_grad(dw_uq),
                 w_ukv=_w_ukv_grad(dw_ukv), w_proj_attn=_w_pa_grad(dw_pa), w_proj_rec=dw_pr, w_out=dw_out)
    return dh, grads, (pushed, dmerge_pushed, du_pushed)


SHARDED = ("ffn1_w_gu", "ffn1_w_down", "w_in", "w_uq", "w_ukv", "w_proj_attn", "w_proj_rec", "w_out", "ffn2_w_gu", "ffn2_w_down")
ROW_SHARDED = ("ffn1_w_down", "w_out", "ffn2_w_down")
FFN1_W = ("ffn1_w_gu", "ffn1_w_down")
GATHER_BEHIND = (("w_in", "w_uq", "w_ukv"), ("w_proj_attn", "w_proj_rec", "w_out"), ("ffn2_w_gu",), ("ffn2_w_down",))
SCATTER_BEHIND = (("ffn2_w_gu",), ("ffn2_w_down", "w_in"))
SMALL = ("ffn1_norm", "mix_norm", "q_norm", "kv_norm", "hg_lb_raw", "hg_norm", "ffn2_norm", "final_norm")
WEIGHTS = ("meta_tokens", "ffn1_norm", "ffn1_w_gu", "ffn1_w_down", "mix_norm", "w_in", "q_norm", "kv_norm", "w_uq", "w_ukv",
           "hg_lb_raw", "hg_norm", "w_proj_attn", "w_proj_rec", "w_out", "ffn2_norm", "ffn2_w_gu", "ffn2_w_down", "final_norm")


def _pack_small(vals):
    flat = jnp.concatenate([vals[n].reshape(-1) for n in SMALL])
    return flat.reshape(-1, LANE)


def _unpack_small(packed, like):
    flat = packed.reshape(-1)
    out, off = {}, 0
    for n in SMALL:
        size = math.prod(like[n].shape)
        out[n] = flat[off:off + size].reshape(like[n].shape)
        off += size
    return out


def kernel(x, meta_tokens, ffn1_norm, ffn1_w_gu, ffn1_w_down, mix_norm, w_in, q_norm, kv_norm, w_uq, w_ukv, hg_lb_raw, hg_norm, w_proj_attn, w_proj_rec, w_out, ffn2_norm, ffn2_w_gu, ffn2_w_down, final_norm, loss_target, m_meta_tokens, m_ffn1_norm, m_ffn1_w_gu, m_ffn1_w_down, m_mix_norm, m_w_in, m_q_norm, m_kv_norm, m_w_uq, m_w_ukv, m_hg_lb_raw, m_hg_norm, m_w_proj_attn, m_w_proj_rec, m_w_out, m_ffn2_norm, m_ffn2_w_gu, m_ffn2_w_down, m_final_norm, v_meta_tokens, v_ffn1_norm, v_ffn1_w_gu, v_ffn1_w_down, v_mix_norm, v_w_in, v_q_norm, v_kv_norm, v_w_uq, v_w_ukv, v_hg_lb_raw, v_hg_norm, v_w_proj_attn, v_w_proj_rec, v_w_out, v_ffn2_norm, v_ffn2_w_gu, v_ffn2_w_down, v_final_norm):
    w = dict(meta_tokens=meta_tokens, ffn1_norm=ffn1_norm, ffn1_w_gu=ffn1_w_gu, ffn1_w_down=ffn1_w_down, mix_norm=mix_norm,
             w_in=w_in, q_norm=q_norm, kv_norm=kv_norm, w_uq=w_uq, w_ukv=w_ukv, hg_lb_raw=hg_lb_raw, hg_norm=hg_norm,
             w_proj_attn=w_proj_attn, w_proj_rec=w_proj_rec, w_out=w_out, ffn2_norm=ffn2_norm, ffn2_w_gu=ffn2_w_gu,
             ffn2_w_down=ffn2_w_down, final_norm=final_norm)
    mom = dict(meta_tokens=m_meta_tokens, ffn1_norm=m_ffn1_norm, ffn1_w_gu=m_ffn1_w_gu, ffn1_w_down=m_ffn1_w_down,
               mix_norm=m_mix_norm, w_in=m_w_in, q_norm=m_q_norm, kv_norm=m_kv_norm, w_uq=m_w_uq, w_ukv=m_w_ukv,
               hg_lb_raw=m_hg_lb_raw, hg_norm=m_hg_norm, w_proj_attn=m_w_proj_attn, w_proj_rec=m_w_proj_rec, w_out=m_w_out,
               ffn2_norm=m_ffn2_norm, ffn2_w_gu=m_ffn2_w_gu, ffn2_w_down=m_ffn2_w_down, final_norm=m_final_norm)
    var = dict(meta_tokens=v_meta_tokens, ffn1_norm=v_ffn1_norm, ffn1_w_gu=v_ffn1_w_gu, ffn1_w_down=v_ffn1_w_down,
               mix_norm=v_mix_norm, w_in=v_w_in, q_norm=v_q_norm, kv_norm=v_kv_norm, w_uq=v_w_uq, w_ukv=v_w_ukv,
               hg_lb_raw=v_hg_lb_raw, hg_norm=v_hg_norm, w_proj_attn=v_w_proj_attn, w_proj_rec=v_w_proj_rec, w_out=v_w_out,
               ffn2_norm=v_ffn2_norm, ffn2_w_gu=v_ffn2_w_gu, ffn2_w_down=v_ffn2_w_down, final_norm=v_final_norm)
    nl = ffn1_norm.shape[0]
    seq, d = x.shape[1], x.shape[2]
    n_real = N_META + seq
    t = -(-n_real // ROW_ALIGN) * ROW_ALIGN
    me = 4 * lax.axis_index("x") + 2 * lax.axis_index("y") + lax.axis_index("c")

    def own_shards(l, names):
        return [w[n][l].astype(BF16) for n in names]

    gathered = _exchange("gather_weights", own_shards(0, FFN1_W) + [meta_tokens], gather=True)
    meta_full = _cols_full(gathered[-1])

    def mat(full, n):
        g = full[n]
        return g.reshape(-1, g.shape[-1]) if n in ROW_SHARDED else _cols_full(g)

    def ffn_params(p, tag, l, full):
        p[tag + "_w_gu"] = mat(full, tag + "_w_gu")
        p[tag + "_w_down"] = mat(full, tag + "_w_down")
        p[tag + "_w_gu_t"] = p[tag + "_w_gu"].T
        p[tag + "_w_down_t"] = p[tag + "_w_down"].T
        p[tag + "_norm"] = w[tag + "_norm"][l:l + 1]

    def mix_params(p, l, full):
        p["w_in"] = _w_in_internal(mat(full, "w_in"))
        p["w_uq"] = _w_uq_internal(mat(full, "w_uq"))
        p["w_ukv"] = _w_ukv_internal(mat(full, "w_ukv"))
        p["w_pa"] = _w_pa_internal(mat(full, "w_proj_attn"))
        p["w_pr"] = mat(full, "w_proj_rec")
        p["w_out"] = mat(full, "w_out")
        for n in ("w_in", "w_uq", "w_ukv", "w_pa", "w_pr", "w_out"):
            p[n + "_t"] = p[n].T
        for n in ("mix_norm", "q_norm", "kv_norm", "hg_norm"):
            p[n] = w[n][l:l + 1]

    tabs = _rope_tables(t)
    call, call_t = _hg_tables()
    lbs = _lb_fwd(hg_lb_raw)

    pad = jnp.zeros((t - n_real, d), F32)
    h = jnp.concatenate([meta_full, x[0], pad], axis=0)
    tgt = jnp.concatenate([jnp.zeros((N_META, d), F32), loss_target[0], pad], axis=0)
    saved, params = [], []
    full = dict(zip(FFN1_W, gathered[:-1]))
    xn = _rms_fwd("first_norm", h, ffn1_norm[0:1])
    for l in range(nl):
        p = {}
        params.append(p)
        ffn_params(p, "ffn1", l, full)
        behind = [_push_plan(own_shards(0, names), gather=True) if l == 0 else None for names in GATHER_BEHIND]
        h, xn, s1, got = _ffn_fwd("ffn1", h, xn, p["ffn1_w_gu"], p["ffn1_w_down"], mix_norm[l:l + 1], behind[0:2])
        if l == 0:
            full.update(zip(GATHER_BEHIND[0], got[0]))
            full.update(zip(GATHER_BEHIND[1], got[1]))
        mix_params(p, l, full)
        push = _push_plan(own_shards(l + 1, SHARDED), gather=True) if l + 1 < nl else None
        h, xn, s2, got = _mix_fwd(h, xn, p, tabs, lbs[l:l + 1], call, (push, behind[2], behind[3]), ffn2_norm[l:l + 1])
        if l == 0:
            full.update(zip(GATHER_BEHIND[2], got[1]))
            full.update(zip(GATHER_BEHIND[3], got[2]))
        ffn_params(p, "ffn2", l, full)
        h, xn, s3, _ = _ffn_fwd("ffn2", h, xn, p["ffn2_w_gu"], p["ffn2_w_down"],
                                ffn1_norm[l + 1:l + 2] if l + 1 < nl else None)
        if push is not None:
            full = dict(zip(SHARDED, got[0]))
        saved.append((s1, s2, s3))
    dh, d_final, loss_part = _loss_head(h, final_norm.reshape(1, d), tgt, n_real)
    loss = lax.psum(loss_part[0, 0], ("x", "y", "c"))

    def shard(n, g):
        g = g.astype(BF16)
        return g.reshape(N_DEV, -1, g.shape[-1]) if n in ROW_SHARDED else _cols_shards(g)

    def scatter_plan(names, gm, layer):
        idx = [SHARDED.index(n) for n in names]
        return idx, _push_plan([shard(n, gm[n]) for n in names], gather=False, bufs=[slots[i] for i in idx], layer=layer)

    def landed(idx, pushed):
        for i, s in zip(idx, pushed):
            slots[i] = s

    per_layer = []
    slots = [jnp.zeros((N_DEV,) + w[n].shape, BF16) for n in SHARDED]
    above = None
    for l in reversed(range(nl)):
        p = params[l]
        s1, s2, s3 = saved[l]
        dh, dn2, dgu2, ddown2, _ = _ffn_bwd("ffn2", dh, s3, p["ffn2_norm"], p["ffn2_w_gu_t"], p["ffn2_w_down_t"])
        gm = dict(ffn2_norm=dn2, ffn2_w_gu=dgu2, ffn2_w_down=ddown2)
        idx, push = (None, None) if above is None else scatter_plan(SHARDED, above, l + 1)
        dh, gmix, got = _mix_bwd(dh, s2, p, tabs, lbs[l:l + 1], call, call_t, (push, None, None))
        if push is not None:
            landed(idx, got[0])
        gm.update(gmix)
        behind = [scatter_plan(names, gm, 0) if l == 0 else (None, None) for names in SCATTER_BEHIND]
        dh, dn1, dgu1, ddown1, got = _ffn_bwd("ffn1", dh, s1, p["ffn1_norm"], p["ffn1_w_gu_t"], p["ffn1_w_down_t"],
                                              [b[1] for b in behind])
        if l == 0:
            landed(behind[0][0], got[0])
            landed(behind[1][0], got[1])
        gm.update(ffn1_norm=dn1, ffn1_w_gu=dgu1, ffn1_w_down=ddown1)
        per_layer.append(gm)
        above = gm
    per_layer.reverse()
    grad_x = dh[N_META:n_real][None]

    last = [n for n in SHARDED if n not in sum(SCATTER_BEHIND, ())]
    idx = [SHARDED.index(n) for n in last]
    landed(idx, _exchange("scatter_grads", [shard(n, above[n]) for n in last], gather=False,
                          bufs=[slots[i] for i in idx], layer=0))
    grads, delta, new_m, new_v = {}, {}, {}, {}
    for n, s in zip(SHARDED, slots):
        grads[n], delta[n], new_m[n], new_v[n] = _adam_sharded("adam_" + n, s, w[n], mom[n], var[n])

    small = {n: jnp.concatenate([gm[n] for gm in per_layer], axis=0) for n in SMALL if n not in ("hg_lb_raw", "final_norm")}
    small["hg_lb_raw"] = _lb_bwd(hg_lb_raw, jnp.concatenate([gm["lb"] for gm in per_layer], axis=0))
    small["final_norm"] = d_final
    packed = jnp.concatenate([_pack_small(small), dh[:N_META].reshape(-1, LANE)], axis=0)
    summed = _sum_replicated(_exchange("gather_small", [packed], gather=True)[0])
    n_small = packed.shape[0] - N_META * d // LANE
    sd, sm, sv = _adam_small("adam_small", summed[:n_small], _pack_small(w), _pack_small(mom), _pack_small(var))
    grads.update(_unpack_small(summed[:n_small], w))
    delta.update(_unpack_small(sd, w))
    new_m.update(_unpack_small(sm, w))
    new_v.update(_unpack_small(sv, w))
    dmeta = lax.dynamic_slice_in_dim(summed[n_small:].reshape(N_META, d), me * (d // N_DEV), d // N_DEV, axis=1)
    grads["meta_tokens"] = dmeta
    delta["meta_tokens"], new_m["meta_tokens"], new_v["meta_tokens"] = _adam_small(
        "adam_meta", dmeta, meta_tokens, m_meta_tokens, v_meta_tokens)

    return (loss, grad_x, *[grads[n] for n in WEIGHTS], *[delta[n] for n in WEIGHTS], *[new_m[n] for n in WEIGHTS],
            *[new_v[n] for n in WEIGHTS])
```

```python
import functools
import math

import jax
import jax.numpy as jnp
from jax import lax
from jax.experimental import pallas as pl
from jax.experimental.pallas import tpu as pltpu

F32 = jnp.float32
BF16 = jnp.bfloat16

N_DEV = 8
N_META = 16
MLA_HEADS = 8
Q_LORA = 384
KV_LORA = 256
QK_NOPE = 64
QK_ROPE = 32
V_HEAD = 64
ROPE_THETA = 10000.0
HG_HEADS = 4
HG_DIM = 128
EPS = 1e-6
NEG_BIG = -1e30
F_MIN = 1e-20
ADAM_LR = 0.001
ADAM_B1 = 0.9
ADAM_B2 = 0.999
ADAM_EPS = 1e-08
ADAM_WD = 0.01
ADAM_STEP = 10

LANE = 128
ROW_ALIGN = 256
HG_CHUNK = 128
HG_LEVELS = (64, 32, 16, 8, 4, 2, 1)
VMEM_LIMIT = 48 * 1024 * 1024

Z_CQ, Z_CKV, Z_KPE, Z_KPESW, Z_HQ, Z_HF, Z_HI, Z_HG, Z_GA, Z_GB, Z_W = 0, 384, 640, 768, 1024, 1536, 2048, 2560, 3072, 4096, 5120
ATTN_SCALE = float((QK_NOPE + QK_ROPE) ** -0.5)
LOG2E = 1.4426950408889634
ATTN_C2 = ATTN_SCALE * LOG2E
HG_SCALE = float(HG_DIM ** -0.5)


def _cparams(sem):
    return pltpu.CompilerParams(dimension_semantics=sem, vmem_limit_bytes=VMEM_LIMIT)


def _tile(n, cap):
    if n <= cap:
        return n
    best = None
    for t in range(LANE, cap + 1, LANE):
        if n % t == 0:
            best = t
    assert best is not None, (n, cap)
    return best


def _row_tile(m):
    for t in (768, 384, 256, 128):
        if m % t == 0:
            return t
    raise ValueError(m)


def _bf(x):
    return x.astype(BF16)


def _dot(a, b):
    return jnp.dot(a, b, preferred_element_type=F32)


def _dot_nt(a, b):
    return lax.dot_general(a, b, (((1,), (1,)), ((), ())), preferred_element_type=F32)


def _dot_tn(a, b):
    return lax.dot_general(a, b, (((0,), (0,)), ((), ())), preferred_element_type=F32)


def _sigmoid(x):
    return 1.0 / (1.0 + jnp.exp(-x))


def _mm(name, a_list, pairs, extras, outs, fn, *, tm, tn, n, push=None):
    m = a_list[0].shape[0]
    na, nb, ne, no = len(a_list), len(pairs), len(extras), len(outs)

    def body(*refs):
        a_refs = refs[:na]
        b_refs = refs[na:na + nb]
        e_refs = refs[na + nb:na + nb + ne]
        o_refs = refs[na + nb + ne:]
        a_vals = [_bf(r[...]) for r in a_refs]
        accs = [_dot(a_vals[ai], b_refs[k][...]) for k, (ai, _, _) in enumerate(pairs)]
        res = fn(accs, [r[...] for r in e_refs])
        for r, v in zip(o_refs, res):
            r[...] = v.astype(r.dtype)

    def spec(block_shape, index_map):
        return pl.BlockSpec(block_shape, functools.partial(lambda j, i, im: im(i, j), im=index_map))

    in_specs = [spec((tm, a.shape[1]), lambda i, j: (i, 0)) for a in a_list]
    for _, b, off in pairs:
        in_specs.append(spec((b.shape[0], tn), functools.partial(lambda i, j, off: (0, j + off), off=off)))
    in_specs += [spec(bs, im) for _, bs, im in extras]
    out_specs = [spec(bs, im) for _, _, bs, im in outs]
    out_shape = [jax.ShapeDtypeStruct(s, d) for s, d, _, _ in outs]
    args = (*a_list, *[b for _, b, _ in pairs], *[e for e, _, _ in extras])
    if push is not None:
        return _call_carrying_push(body, push, name=name, grid=(n // tn, m // tm), in_specs=in_specs, out_specs=out_specs,
                                   out_shape=out_shape, scratch_shapes=[], args=args)
    return pl.pallas_call(
        body, name=name, grid=(n // tn, m // tm), in_specs=in_specs, out_specs=out_specs, out_shape=out_shape,
        compiler_params=_cparams(("parallel", "parallel")),
    )(*args)


def _tile_spec(tm, tn, col_off=0):
    return (tm, tn), functools.partial(lambda i, j, off: (i, j + off), off=col_off)


def _mm_tn(name, a, b, *, alpha, out_dtype):
    t, k = a.shape
    n = b.shape[1]
    tk, tn = _tile(k, 1408), _tile(n, 1408)
    tt = next(c for c in (768, 512, 256) if t % c == 0)
    nt = t // tt

    def body(a_ref, b_ref, o_ref, acc_ref):
        s = pl.program_id(2)

        @pl.when(s == 0)
        def _():
            acc_ref[...] = jnp.zeros_like(acc_ref)

        acc_ref[...] += _dot_tn(_bf(a_ref[...]), _bf(b_ref[...]))

        @pl.when(s == nt - 1)
        def _():
            o_ref[...] = (alpha * acc_ref[...]).astype(o_ref.dtype)

    return pl.pallas_call(
        body, name=name, grid=(k // tk, n // tn, nt),
        in_specs=[pl.BlockSpec((tt, tk), lambda i, j, s: (s, i)), pl.BlockSpec((tt, tn), lambda i, j, s: (s, j))],
        out_specs=pl.BlockSpec((tk, tn), lambda i, j, s: (i, j)),
        out_shape=jax.ShapeDtypeStruct((k, n), out_dtype),
        scratch_shapes=[pltpu.VMEM((tk, tn), F32)],
        compiler_params=_cparams(("parallel", "parallel", "arbitrary")),
    )(a, b)


def _rms_parts(x):
    r = lax.rsqrt(jnp.mean(x * x, axis=-1, keepdims=True) + EPS)
    return r, x * r


def _rms_bwd_math(x, w, dxn):
    r, xhat = _rms_parts(x)
    t = dxn * w
    dx = r * (t - xhat * jnp.mean(t * xhat, axis=-1, keepdims=True))
    dw = jnp.sum(dxn * xhat, axis=0, keepdims=True)
    return dx, dw


def _rms_fwd(name, h, w):
    t, d = h.shape
    tm = _row_tile(t)

    def body(h_ref, w_ref, o_ref):
        _, xhat = _rms_parts(h_ref[...])
        o_ref[...] = (xhat * w_ref[...]).astype(o_ref.dtype)

    return pl.pallas_call(
        body, name=name, grid=(t // tm,),
        in_specs=[pl.BlockSpec((tm, d), lambda i: (i, 0)), pl.BlockSpec((1, d), lambda i: (0, 0))],
        out_specs=pl.BlockSpec((tm, d), lambda i: (i, 0)),
        out_shape=jax.ShapeDtypeStruct((t, d), BF16),
        compiler_params=_cparams(("parallel",)),
    )(h, w)


def _rms_bwd(name, h, w, dxn, dh_in):
    t, d = h.shape
    tm = _row_tile(t)

    def body(h_ref, w_ref, dxn_ref, dh_ref, o_ref, dw_ref):
        dx, dw = _rms_bwd_math(h_ref[...], w_ref[...], dxn_ref[...])
        o_ref[...] = dh_ref[...] + dx

        @pl.when(pl.program_id(0) == 0)
        def _():
            dw_ref[...] = jnp.zeros_like(dw_ref)

        dw_ref[...] += dw

    row = pl.BlockSpec((tm, d), lambda i: (i, 0))
    vec = pl.BlockSpec((1, d), lambda i: (0, 0))
    return pl.pallas_call(
        body, name=name, grid=(t // tm,),
        in_specs=[row, vec, row, row],
        out_specs=[row, vec],
        out_shape=[jax.ShapeDtypeStruct((t, d), F32), jax.ShapeDtypeStruct((1, d), F32)],
        compiler_params=_cparams(("arbitrary",)),
    )(h, w, dxn, dh_in)


def _loss_head(h, w, tgt, n_real):
    t, d = h.shape
    tm = _row_tile(t)

    def body(h_ref, w_ref, t_ref, dh_ref, dw_ref, loss_ref):
        i = pl.program_id(0)
        x = h_ref[...]
        wv = w_ref[...]
        _, xhat = _rms_parts(x)
        rows = i * tm + lax.broadcasted_iota(jnp.int32, (tm, 1), 0)
        valid = (rows >= N_META) & (rows < n_real)
        e = jnp.where(valid, xhat * wv - t_ref[...], 0.0)
        dx, dw = _rms_bwd_math(x, wv, e * (1.0 / d))
        dh_ref[...] = dx

        @pl.when(i == 0)
        def _():
            dw_ref[...] = jnp.zeros_like(dw_ref)
            loss_ref[...] = jnp.zeros_like(loss_ref)

        dw_ref[...] += dw
        loss_ref[...] += (0.5 / d) * jnp.sum(jnp.sum(e * e, axis=-1, keepdims=True), axis=0, keepdims=True)

    row = pl.BlockSpec((tm, d), lambda i: (i, 0))
    vec = pl.BlockSpec((1, d), lambda i: (0, 0))
    return pl.pallas_call(
        body, name="loss_head", grid=(t // tm,),
        in_specs=[row, vec, row],
        out_specs=[row, vec, pl.BlockSpec((1, 1), lambda i: (0, 0))],
        out_shape=[jax.ShapeDtypeStruct((t, d), F32), jax.ShapeDtypeStruct((1, d), F32), jax.ShapeDtypeStruct((1, 1), F32)],
        compiler_params=_cparams(("arbitrary",)),
    )(h, w, tgt)


def _heads(x):
    return jnp.concatenate([x] * MLA_HEADS, axis=1)


def _mla_front(z, qn_w, kvn_w, tabs, w_uq, w_ukv):
    t = z.shape[0]
    tm = _attn_block(t)
    wq = MLA_HEADS * LANE

    def body(z_ref, qw_ref, kw_ref, cq_ref, ck_ref, sq_ref, wuq_ref, wukv_ref, cqn_ref, ckvn_ref, q_ref, kv_ref, vt_ref):
        zz = z_ref[...]
        _, qhat = _rms_parts(zz[:, Z_CQ:Z_CKV])
        _, khat = _rms_parts(zz[:, Z_CKV:Z_KPE])
        cqn = (qhat * qw_ref[...]).astype(BF16)
        ckvn = (khat * kw_ref[...]).astype(BF16)
        cqn_ref[...] = cqn
        ckvn_ref[...] = ckvn
        krot = zz[:, Z_KPE:Z_KPESW] * ck_ref[...] + zz[:, Z_KPESW:Z_KPESW + LANE] * sq_ref[...]
        qq = _dot(cqn, wuq_ref[...])
        q_ref[...] = (qq[:, :wq] * _heads(cq_ref[...]) + qq[:, wq:] * _heads(sq_ref[...])).astype(BF16)
        kvv = _dot(ckvn, wukv_ref[...]) + _kv_pattern(krot)
        kv_ref[...] = kvv.astype(BF16)
        for hd in range(MLA_HEADS):
            vt_ref[hd, 0] = kvv[:, 2 * LANE * hd + LANE:2 * LANE * (hd + 1)].T.astype(BF16)

    def rows(wd):
        return pl.BlockSpec((tm, wd), lambda i: (i, 0))

    def whole(a):
        return pl.BlockSpec(a.shape, lambda i: (0, 0))

    return pl.pallas_call(
        body, name="mla_front", grid=(t // tm,),
        in_specs=[rows(Z_HQ), whole(qn_w), whole(kvn_w), rows(LANE), rows(LANE), rows(LANE), whole(w_uq), whole(w_ukv)],
        out_specs=[rows(Q_LORA), rows(KV_LORA), rows(wq), rows(2 * wq),
                   pl.BlockSpec((MLA_HEADS, 1, LANE, tm), lambda i: (0, i, 0, 0))],
        out_shape=[jax.ShapeDtypeStruct((t, Q_LORA), BF16), jax.ShapeDtypeStruct((t, KV_LORA), BF16),
                   jax.ShapeDtypeStruct((t, wq), BF16), jax.ShapeDtypeStruct((t, 2 * wq), BF16),
                   jax.ShapeDtypeStruct((MLA_HEADS, t // tm, LANE, tm), BF16)],
        compiler_params=_cparams(("parallel",)),
    )(z, qn_w, kvn_w, *tabs, w_uq, w_ukv)


def _mla_back(z, qn_w, kvn_w, tabs, cqn, ckvn, dq, dkv, w_uq_t, w_ukv_t):
    t = z.shape[0]
    tm = next(c for c in (768, 512, 256) if t % c == 0)
    wq = MLA_HEADS * LANE

    def body(z_ref, qw_ref, kw_ref, cq_ref, ck_ref, sq_ref, cqn_ref, ckvn_ref, dq_ref, dkv_ref, wuqt_ref, wukvt_ref,
             dz_ref, dwuq_ref, dwukv_ref, dqw_ref, dkw_ref):
        @pl.when(pl.program_id(0) == 0)
        def _():
            for r in (dwuq_ref, dwukv_ref, dqw_ref, dkw_ref):
                r[...] = jnp.zeros_like(r)

        zz = z_ref[...]
        d = dq_ref[...] * ATTN_SCALE
        dqq = jnp.concatenate([d * _heads(cq_ref[...]), d * _heads(sq_ref[...])], axis=1).astype(BF16)
        dkv_v = dkv_ref[...]
        dwuq_ref[...] += _dot_tn(cqn_ref[...], dqq)
        dwukv_ref[...] += _dot_tn(ckvn_ref[...], dkv_v)
        dcq, dqw = _rms_bwd_math(zz[:, Z_CQ:Z_CKV], qw_ref[...], _dot(dqq, wuqt_ref[...]))
        dckv, dkw = _rms_bwd_math(zz[:, Z_CKV:Z_KPE], kw_ref[...], _dot(dkv_v, wukvt_ref[...]))
        dkr = jnp.zeros((tm, LANE), F32)
        for hd in range(MLA_HEADS):
            dkr = dkr + dkv_v[:, 2 * LANE * hd:2 * LANE * hd + LANE].astype(F32)
        dz_ref[...] = jnp.concatenate(
            [dcq, dckv, dkr * ck_ref[...], dkr * sq_ref[...], jnp.zeros((tm, Z_HQ - Z_KPESW - LANE), F32)], axis=1
        ).astype(BF16)
        dqw_ref[...] += dqw
        dkw_ref[...] += dkw

    def rows(wd):
        return pl.BlockSpec((tm, wd), lambda i: (i, 0))

    def whole(shape):
        return pl.BlockSpec(shape, lambda i: (0, 0))

    return pl.pallas_call(
        body, name="mla_back", grid=(t // tm,),
        in_specs=[rows(Z_HQ), whole(qn_w.shape), whole(kvn_w.shape), rows(LANE), rows(LANE), rows(LANE), rows(Q_LORA),
                  rows(KV_LORA), rows(wq), rows(2 * wq), whole(w_uq_t.shape), whole(w_ukv_t.shape)],
        out_specs=[rows(Z_HQ), whole((Q_LORA, 2 * wq)), whole((KV_LORA, 2 * wq)), whole((1, Q_LORA)), whole((1, KV_LORA))],
        out_shape=[jax.ShapeDtypeStruct((t, Z_HQ), BF16), jax.ShapeDtypeStruct((Q_LORA, 2 * wq), F32),
                   jax.ShapeDtypeStruct((KV_LORA, 2 * wq), F32), jax.ShapeDtypeStruct((1, Q_LORA), F32),
                   jax.ShapeDtypeStruct((1, KV_LORA), F32)],
        compiler_params=_cparams(("arbitrary",)),
    )(z, qn_w, kvn_w, *tabs, cqn, ckvn, dq, dkv, w_uq_t, w_ukv_t)


def _attn_block(t):
    for b in (768, 512, 256):
        if t % b == 0:
            return b
    raise ValueError(t)


def _call_carrying_push(body, push, *, name, grid, in_specs, out_specs, out_shape, scratch_shapes, args):
    if push is None:
        outs = pl.pallas_call(body, name=name, grid=grid, in_specs=in_specs, out_specs=out_specs, out_shape=out_shape,
                              scratch_shapes=scratch_shapes, compiler_params=_cparams(("parallel", "arbitrary")))(*args)
        return outs, None
    n_in, n_out, n_scr, n_pin, nk = len(in_specs), len(out_specs), len(scratch_shapes), len(push["ins"]), push["nk"]

    def carrying(*refs):
        o0 = n_in + n_pin
        s0 = o0 + n_out + nk
        pins, pouts, sems = refs[n_in:o0], refs[o0 + n_out:s0], refs[s0 + n_scr:]
        a, b = pl.program_id(0), pl.program_id(1)

        @pl.when((a == 0) & (b == 0))
        def _():
            for cp in _push_copies(push, pins, pouts, sems):
                cp.start()

        body(*refs[:n_in], *refs[o0:o0 + n_out], *refs[s0:s0 + n_scr])

        @pl.when((a == grid[0] - 1) & (b == grid[1] - 1))
        def _():
            for cp in _push_copies(push, pins, pouts, sems):
                cp.wait()

    outs = pl.pallas_call(
        carrying, name=name + "_push", grid=grid, in_specs=list(in_specs) + push["in_specs"],
        out_specs=list(out_specs) + push["out_specs"], out_shape=list(out_shape) + push["outs"],
        scratch_shapes=list(scratch_shapes) + push["sems"], input_output_aliases=_push_aliases(push, n_in, n_out),
        compiler_params=_cparams(("arbitrary", "arbitrary")))(*args, *push["ins"])
    return outs[:n_out], outs[n_out:]


def _attn_fwd(q, kv, v_t, push=None):
    t = q.shape[0]
    bq = bk = _attn_block(t)
    nq = t // bq

    def body(q_ref, k_ref, vt_ref, o_ref, lse_ref):
        i = pl.program_id(1)
        qv = q_ref[...]
        qpos = i * bq + lax.broadcasted_iota(jnp.int32, (1, bq), 1)

        def block(j, carry, masked):
            m, l, acc = carry
            s = _dot_nt(k_ref[pl.ds(pl.multiple_of(j * bk, bk), bk), :], qv)
            if masked:
                kpos = j * bk + lax.broadcasted_iota(jnp.int32, (bk, 1), 0)
                s = jnp.where(kpos <= qpos, s, NEG_BIG)
            m_new = jnp.maximum(m, jnp.max(s, axis=0, keepdims=True))
            p = jnp.exp2((s - m_new) * ATTN_C2)
            a = jnp.exp2((m - m_new) * ATTN_C2)
            return m_new, a * l + jnp.sum(p, axis=0, keepdims=True), a * acc + _dot(vt_ref[0, j], _bf(p))

        init = (jnp.full((1, bq), NEG_BIG, F32), jnp.zeros((1, bq), F32), jnp.zeros((LANE, bq), F32))
        carry = lax.fori_loop(0, i, functools.partial(block, masked=False), init)
        m, l, acc = block(i, carry, True)
        o_ref[...] = (acc / l).T.astype(o_ref.dtype)
        lse_ref[0, 0] = m * ATTN_C2 + jnp.log(l) * LOG2E

    return _call_carrying_push(
        body, push, name="attn_fwd", grid=(MLA_HEADS, nq),
        in_specs=[pl.BlockSpec((bq, LANE), lambda h, i: (i, h)),
                  pl.BlockSpec((t, LANE), lambda h, i: (0, 2 * h)),
                  pl.BlockSpec((1, nq, LANE, bk), lambda h, i: (h, 0, 0, 0))],
        out_specs=[pl.BlockSpec((bq, LANE), lambda h, i: (i, h)),
                   pl.BlockSpec((1, 1, 1, bq), lambda h, i: (h, i, 0, 0))],
        out_shape=[jax.ShapeDtypeStruct((t, MLA_HEADS * LANE), BF16), jax.ShapeDtypeStruct((MLA_HEADS, nq, 1, bq), F32)],
        scratch_shapes=[], args=(q, kv, v_t))


def _attn_bwd(q, kv, o, do, lse, push=None):
    t = q.shape[0]
    bk = bw = _attn_block(t)
    nk, nw = t // bk, t // bw

    def body(q_ref, o_ref, do_ref, k_ref, v_ref, lse_ref, dq_ref, dkv_ref, dl_ref):
        j = pl.program_id(1)

        @pl.when(j == 0)
        def _():
            dq_ref[...] = jnp.zeros_like(dq_ref)
            for i in range(nw):
                rows = slice(i * bw, (i + 1) * bw)
                d = jnp.sum(o_ref[rows, :].astype(F32) * do_ref[rows, :].astype(F32), axis=1, keepdims=True)
                dl_ref[i] = jnp.broadcast_to(d, (bw, LANE)).T[0:1, :]

        kb = k_ref[...]
        vb = v_ref[...]
        kpos = j * bk + lax.broadcasted_iota(jnp.int32, (bk, 1), 0)

        def block(i, carry, masked):
            dk, dv = carry
            rows = pl.ds(pl.multiple_of(i * bw, bw), bw)
            qb = q_ref[rows, :]
            dob = do_ref[rows, :]
            pt = jnp.exp2(_dot_nt(kb, qb) * ATTN_C2 - lse_ref[0, i])
            if masked:
                qpos = i * bw + lax.broadcasted_iota(jnp.int32, (1, bw), 1)
                pt = jnp.where(kpos <= qpos, pt, 0.0)
            dv = dv + _dot(_bf(pt), dob)
            dst = _bf(pt * (_dot_nt(vb, dob) - dl_ref[i]))
            dk = dk + _dot(dst, qb)
            dq_ref[rows, :] += _dot_tn(dst, kb)
            return dk, dv

        i0 = j
        carry = block(i0, (jnp.zeros((bk, LANE), F32), jnp.zeros((bk, LANE), F32)), True)
        dk, dv = lax.fori_loop(i0 + 1, nw, functools.partial(block, masked=False), carry)
        dkv_ref[...] = jnp.concatenate([dk * ATTN_SCALE, dv], axis=1).astype(dkv_ref.dtype)

    head_rows = pl.BlockSpec((t, LANE), lambda h, j: (0, h))
    return _call_carrying_push(
        body, push, name="attn_bwd", grid=(MLA_HEADS, nk),
        in_specs=[head_rows, head_rows, head_rows,
                  pl.BlockSpec((bk, LANE), lambda h, j: (j, 2 * h)), pl.BlockSpec((bk, LANE), lambda h, j: (j, 2 * h + 1)),
                  pl.BlockSpec((1, nw, 1, bw), lambda h, j: (h, 0, 0, 0))],
        out_specs=[head_rows, pl.BlockSpec((bk, 2 * LANE), lambda h, j: (j, h))],
        out_shape=[jax.ShapeDtypeStruct((t, MLA_HEADS * LANE), F32), jax.ShapeDtypeStruct((t, 2 * MLA_HEADS * LANE), BF16)],
        scratch_shapes=[pltpu.VMEM((nw, 1, bw), F32)], args=(q, o, do, kv, kv, lse))


def _hg_tables():
    c = HG_CHUNK
    tri = (jnp.arange(c)[:, None] >= jnp.arange(c)[None, :]).astype(BF16)
    return tri, tri.T


def _hg_level_ref(b, m):
    c = HG_CHUNK
    if 2 * m >= 8:
        x = b.reshape(c // (2 * m), 2 * m, c)
        return jnp.broadcast_to(x[:, m - 1:m, :], x.shape).reshape(c, c)
    row = lax.broadcasted_iota(jnp.int32, (c, 1), 0)
    if m == 2:
        pos = row & 3
        return jnp.where(pos == 0, pltpu.roll(b, c - 1, 0),
                         jnp.where(pos == 1, b, jnp.where(pos == 2, pltpu.roll(b, 1, 0), pltpu.roll(b, 2, 0))))
    return jnp.where((row & 1) == 0, b, pltpu.roll(b, 1, 0))


def _hg_level_ref_t(d, m):
    c = HG_CHUNK
    row = lax.broadcasted_iota(jnp.int32, (c, 1), 0)
    if 2 * m >= 8:
        x = d.reshape(c // (2 * m), 2 * m, c)
        s = jnp.broadcast_to(jnp.sum(x, axis=1, keepdims=True), x.shape).reshape(c, c)
        return jnp.where((row & (2 * m - 1)) == m - 1, s, 0.0)
    if m == 2:
        s = pltpu.roll(d, 1, 0) + d + pltpu.roll(d, c - 1, 0) + pltpu.roll(d, c - 2, 0)
        return jnp.where((row & 3) == 1, s, 0.0)
    return jnp.where((row & 1) == 0, d + pltpu.roll(d, c - 1, 0), 0.0)


def _table_dot(table, x):
    hi = _bf(x)
    rest = x - hi.astype(F32)
    mid = _bf(rest)
    lo = _bf(rest - mid.astype(F32))
    out = _dot(table, jnp.concatenate([hi, mid, lo], axis=1))
    n = x.shape[1]
    return out[:, 0:n] + out[:, n:2 * n] + out[:, 2 * n:3 * n]


def _hg_gates(hq, hf, lb):
    sg = _sigmoid(hf)
    sn = _sigmoid(-hf)
    f = lb + (1.0 - lb) * sg
    q = hq * _sigmoid(hq)
    g = jnp.log(jnp.maximum(f, F_MIN))
    k = (1.0 - lb) * sn
    return q, k, g, f, sg, sn


def _hg_level_masks(m):
    c = HG_CHUNK
    row = lax.broadcasted_iota(jnp.int32, (c, 1), 0)
    col = lax.broadcasted_iota(jnp.int32, (1, c), 1)
    shift = (2 * m).bit_length() - 1
    up = (row & m) != 0
    same = lax.shift_right_logical(row, shift) == lax.shift_right_logical(col, shift)
    return up, same


def _hg_level_factors(b, bref, up):
    arg = b - bref
    e = jnp.exp(jnp.where(up, arg, -arg))
    return jnp.where(up, e, 0.0), jnp.where(up, 0.0, e)


def _hg_intra(q, k, b, masks):
    c = HG_CHUNK
    row = lax.broadcasted_iota(jnp.int32, (c, 1), 0)
    col = lax.broadcasted_iota(jnp.int32, (1, c), 1)
    a = jnp.where(row == col, _dot_nt(_bf(q), _bf(k)), 0.0)
    parts = []
    for m, (up, same) in zip(HG_LEVELS, masks):
        eq, ek = _hg_level_factors(b, _hg_level_ref(b, m), up)
        qt, kt = q * eq, k * ek
        a = a + jnp.where(same, _dot_nt(_bf(qt), _bf(kt)), 0.0)
        parts.append((eq, ek, qt, kt))
    return a, parts


def _hg_chunk_fwd(hq, hf, hi, hg, lb, nw, st, call, masks):
    q, k, g, _, _, _ = _hg_gates(hq, hf, lb)
    b = _table_dot(call, g)
    a, _ = _hg_intra(q, k, b, masks)
    v16 = _bf(hi)
    o = _dot(_bf(a * HG_SCALE), v16) + _dot_nt(_bf(q * jnp.exp(b) * HG_SCALE), _bf(st))
    bl = b[HG_CHUNK - 1:HG_CHUNK]
    ke = k * jnp.exp(bl - b)
    st_new = st * jnp.exp(bl) + _dot(_bf(hi.T), _bf(ke))
    r = lax.rsqrt(jnp.mean(o * o, axis=-1, keepdims=True) + EPS)
    y = o * r * nw * (hg * _sigmoid(hg))
    return y, st_new


def _hg_chunk_bwd(hq, hf, hi, hg, lb, nw, st, call, call_t, dy, dst_new, masks):
    c = HG_CHUNK
    q, k, g, f, sg, sn = _hg_gates(hq, hf, lb)
    b = _table_dot(call, g)
    a, parts = _hg_intra(q, k, b, masks)
    v16 = _bf(hi)
    st16 = _bf(st)
    eb = jnp.exp(b)
    qe = q * eb * HG_SCALE
    a16 = _bf(a * HG_SCALE)
    o = _dot(a16, v16) + _dot_nt(_bf(qe), st16)
    bl = b[c - 1:c]
    el = jnp.exp(bl)
    x = jnp.exp(bl - b)
    ke = k * x
    r = lax.rsqrt(jnp.mean(o * o, axis=-1, keepdims=True) + EPS)
    shg = _sigmoid(hg)
    gate = hg * shg
    ohat = o * r
    don = dy * gate
    dhg = dy * ohat * nw * (shg * (1.0 + hg * (1.0 - shg)))
    dnw = jnp.sum(don * ohat, axis=0, keepdims=True)
    tt = don * nw
    do = r * (tt - ohat * jnp.mean(tt * ohat, axis=-1, keepdims=True))
    do16 = _bf(do)
    dst16 = _bf(dst_new)
    da = _dot_nt(do16, v16) * HG_SCALE
    dv = _dot(_bf(a16.astype(F32).T), do16) + _dot_nt(_bf(ke), dst16)
    dqe = _dot(do16, st16)
    dke = _dot(v16, dst16)
    dst = dst_new * el + _dot(_bf(do.T), _bf(qe))
    dbl = jnp.sum(dst_new * st, axis=0, keepdims=True) * el
    dk = dke * x
    dxa = dke * ke
    db = dqe * qe - dxa
    dbl = dbl + jnp.sum(dxa, axis=0, keepdims=True)
    dq = dqe * eb * HG_SCALE
    row = lax.broadcasted_iota(jnp.int32, (c, 1), 0)
    col = lax.broadcasted_iota(jnp.int32, (1, c), 1)
    ddiag = jnp.sum(jnp.where(row == col, da, 0.0), axis=1, keepdims=True)
    dq = dq + ddiag * k
    dk = dk + ddiag * q
    for (eq, ek, qt, kt), m, (_, same) in zip(parts, HG_LEVELS, masks):
        gm = jnp.where(same, da, 0.0)
        dqt = _dot(_bf(gm), _bf(kt))
        dkt = _dot(_bf(gm.T), _bf(qt))
        dq = dq + dqt * eq
        dk = dk + dkt * ek
        darg = dqt * qt - dkt * kt
        db = db + darg - _hg_level_ref_t(darg, m)
    db = db + jnp.where(row == c - 1, dbl, 0.0)
    dg = _table_dot(call_t, db)
    shq = _sigmoid(hq)
    dhq = dq * (shq * (1.0 + hq * (1.0 - shq)))
    df = jnp.where(f > F_MIN, dg / jnp.maximum(f, F_MIN), 0.0)
    dlb = jnp.sum(df * (1.0 - sg) - dk * sn, axis=0, keepdims=True)
    dhf = df * (1.0 - lb) * sg * (1.0 - sg) - dk * (1.0 - lb) * sn * (1.0 - sn)
    return dhq, dhf, dv, dhg, dlb, dnw, dst


def _hg_chunks_per_step(t):
    return 6 if t % (6 * HG_CHUNK) == 0 else 2


def _hg_col(group, h):
    return group // LANE + h


def _hgrn_fwd(z, lb, nw, call, push=None):
    t = z.shape[0]
    c, cs = HG_CHUNK, _hg_chunks_per_step(t)
    rows = c * cs
    nsteps = t // rows

    def body(hq_ref, hf_ref, hi_ref, hg_ref, lb_ref, nw_ref, call_ref, y_ref, sv_ref, st_ref):
        @pl.when(pl.program_id(1) == 0)
        def _():
            st_ref[...] = jnp.zeros_like(st_ref)

        masks = [_hg_level_masks(m) for m in HG_LEVELS]
        for u in range(cs):
            sl = slice(u * c, (u + 1) * c)
            st = st_ref[...]
            sv_ref[0, u] = st
            y, st_new = _hg_chunk_fwd(hq_ref[sl, :], hf_ref[sl, :], hi_ref[sl, :], hg_ref[sl, :], lb_ref[...], nw_ref[...],
                                      st, call_ref[...], masks)
            y_ref[sl, :] = y.astype(y_ref.dtype)
            st_ref[...] = st_new

    def zcol(group):
        return pl.BlockSpec((rows, LANE), functools.partial(lambda h, i, g: (i, _hg_col(g, h)), g=group))

    ncall = call.shape[0]
    return _call_carrying_push(
        body, push, name="hgrn_fwd", grid=(HG_HEADS, nsteps),
        in_specs=[zcol(Z_HQ), zcol(Z_HF), zcol(Z_HI), zcol(Z_HG),
                  pl.BlockSpec((1, LANE), lambda h, i: (0, h)), pl.BlockSpec((1, LANE), lambda h, i: (0, 0)),
                  pl.BlockSpec((ncall, c), lambda h, i: (0, 0))],
        out_specs=[pl.BlockSpec((rows, LANE), lambda h, i: (i, h)),
                   pl.BlockSpec((1, cs, c, c), lambda h, i: (h, i, 0, 0))],
        out_shape=[jax.ShapeDtypeStruct((t, HG_HEADS * LANE), BF16), jax.ShapeDtypeStruct((HG_HEADS, t // c, c, c), F32)],
        scratch_shapes=[pltpu.VMEM((c, c), F32)], args=(z, z, z, z, lb, nw, call))


def _hgrn_bwd(z, lb, nw, call, call_t, saved, dy):
    t = z.shape[0]
    c, cs = HG_CHUNK, _hg_chunks_per_step(t)
    rows = c * cs
    nsteps = t // rows

    def body(hq_ref, hf_ref, hi_ref, hg_ref, lb_ref, nw_ref, call_ref, callt_ref, sv_ref, dy_ref,
             dhq_ref, dhf_ref, dhi_ref, dhg_ref, dlb_ref, dnw_ref, dst_ref):
        h, i = pl.program_id(0), pl.program_id(1)

        @pl.when(i == 0)
        def _():
            dst_ref[...] = jnp.zeros_like(dst_ref)
            dlb_ref[...] = jnp.zeros_like(dlb_ref)

        @pl.when((i == 0) & (h == 0))
        def _():
            dnw_ref[...] = jnp.zeros_like(dnw_ref)

        masks = [_hg_level_masks(m) for m in HG_LEVELS]
        for u in reversed(range(cs)):
            sl = slice(u * c, (u + 1) * c)
            dhq, dhf, dhi, dhg, dlb, dnw, dst = _hg_chunk_bwd(
                hq_ref[sl, :], hf_ref[sl, :], hi_ref[sl, :], hg_ref[sl, :], lb_ref[...], nw_ref[...], sv_ref[0, u],
                call_ref[...], callt_ref[...], dy_ref[sl, :].astype(F32), dst_ref[...], masks)
            dhq_ref[sl, :] = dhq.astype(BF16)
            dhf_ref[sl, :] = dhf.astype(BF16)
            dhi_ref[sl, :] = dhi.astype(BF16)
            dhg_ref[sl, :] = dhg.astype(BF16)
            dlb_ref[...] += dlb
            dnw_ref[...] += dnw
            dst_ref[...] = dst

    def zcol(group):
        return pl.BlockSpec((rows, LANE), functools.partial(lambda h, i, g: (nsteps - 1 - i, _hg_col(g, h)), g=group))

    head_rows = pl.BlockSpec((rows, LANE), lambda h, i: (nsteps - 1 - i, h))
    ncall = call.shape[0]
    piece = jax.ShapeDtypeStruct((t, HG_HEADS * LANE), BF16)
    return pl.pallas_call(
        body, name="hgrn_bwd", grid=(HG_HEADS, nsteps),
        in_specs=[zcol(Z_HQ), zcol(Z_HF), zcol(Z_HI), zcol(Z_HG),
                  pl.BlockSpec((1, LANE), lambda h, i: (0, h)), pl.BlockSpec((1, LANE), lambda h, i: (0, 0)),
                  pl.BlockSpec((ncall, c), lambda h, i: (0, 0)), pl.BlockSpec((c, ncall), lambda h, i: (0, 0)),
                  pl.BlockSpec((1, cs, c, c), lambda h, i: (h, nsteps - 1 - i, 0, 0)), head_rows],
        out_specs=[head_rows, head_rows, head_rows, head_rows,
                   pl.BlockSpec((1, LANE), lambda h, i: (0, h)), pl.BlockSpec((1, LANE), lambda h, i: (0, 0))],
        out_shape=[piece, piece, piece, piece,
                   jax.ShapeDtypeStruct((1, HG_HEADS * LANE), F32), jax.ShapeDtypeStruct((1, LANE), F32)],
        scratch_shapes=[pltpu.VMEM((c, c), F32)],
        compiler_params=_cparams(("arbitrary", "arbitrary")),
    )(z, z, z, z, lb, nw, call, call_t, saved, dy)


def _lb_fwd(raw):
    nl = raw.shape[0]

    def body(r_ref, o_ref):
        x = r_ref[...]
        e = jnp.exp(x - jnp.max(x, axis=0, keepdims=True))
        p = e / jnp.sum(e, axis=0, keepdims=True)
        acc = jnp.zeros_like(p[0:1])
        for l in range(nl):
            if l > 0:
                acc = acc + p[l:l + 1]
            o_ref[l:l + 1, :] = acc

    return pl.pallas_call(body, name="lb_fwd", out_shape=jax.ShapeDtypeStruct(raw.shape, F32))(raw)


def _lb_bwd(raw, dlbs):
    nl = raw.shape[0]

    def body(r_ref, d_ref, o_ref):
        x = r_ref[...]
        e = jnp.exp(x - jnp.max(x, axis=0, keepdims=True))
        p = e / jnp.sum(e, axis=0, keepdims=True)
        d = d_ref[...]
        dps = [jnp.zeros_like(d[0:1])]
        for i in range(1, nl):
            acc = d[i:i + 1]
            for l in range(i + 1, nl):
                acc = acc + d[l:l + 1]
            dps.append(acc)
        dot = dps[0] * p[0:1]
        for i in range(1, nl):
            dot = dot + dps[i] * p[i:i + 1]
        for i in range(nl):
            o_ref[i:i + 1, :] = p[i:i + 1] * (dps[i] - dot)

    return pl.pallas_call(body, name="lb_bwd", out_shape=jax.ShapeDtypeStruct(raw.shape, F32))(raw, dlbs)


def _push_plan(srcs, gather, bufs=None, layer=None):
    nk = len(srcs)
    any_spec = pl.BlockSpec(memory_space=pl.ANY)
    if bufs is None:
        ins = list(srcs)
        outs = [jax.ShapeDtypeStruct(((N_DEV,) + s.shape) if gather else s.shape, s.dtype) for s in srcs]
    else:
        ins = list(srcs) + list(bufs)
        outs = [jax.ShapeDtypeStruct(b.shape, b.dtype) for b in bufs]
    sems = [pltpu.SemaphoreType.DMA((nk * N_DEV,)), pltpu.SemaphoreType.DMA((nk * N_DEV,)), pltpu.SemaphoreType.DMA((nk,))]
    return dict(nk=nk, gather=gather, layer=layer, ins=ins, in_specs=[any_spec] * len(ins), outs=outs,
                out_specs=[any_spec] * nk, sems=sems, alias_from=None if bufs is None else nk)


def _push_aliases(plan, first_in, first_out):
    if plan is None or plan["alias_from"] is None:
        return {}
    return {first_in + plan["alias_from"] + k: first_out + k for k in range(plan["nk"])}


def _push_copies(plan, in_refs, out_refs, sems):
    nk, gather, layer = plan["nk"], plan["gather"], plan["layer"]
    send_sems, recv_sems, local_sems = sems
    me = 4 * lax.axis_index("x") + 2 * lax.axis_index("y") + lax.axis_index("c")

    def landing(k):
        return out_refs[k].at[me] if layer is None else out_refs[k].at[me, layer]

    copies = [pltpu.make_async_copy(in_refs[k] if gather else in_refs[k].at[me], landing(k), local_sems.at[k])
              for k in range(nk)]
    for r in range(1, N_DEV):
        to = (me + r) % N_DEV
        for k in range(nk):
            copies.append(pltpu.make_async_remote_copy(
                src_ref=in_refs[k] if gather else in_refs[k].at[to], dst_ref=landing(k),
                send_sem=send_sems.at[k * N_DEV + r], recv_sem=recv_sems.at[k * N_DEV + r],
                device_id=(to // 4, (to // 2) % 2, to % 2), device_id_type=pl.DeviceIdType.MESH))
    return copies


def _exchange(name, srcs, gather, bufs=None, layer=None):
    plan = _push_plan(srcs, gather, bufs, layer)
    nin, nk = len(plan["ins"]), plan["nk"]

    def body(*refs):
        copies = _push_copies(plan, refs[:nin], refs[nin:nin + nk], refs[nin + nk:])
        for cp in copies:
            cp.start()
        for cp in copies:
            cp.wait()

    return pl.pallas_call(
        body, name=name, in_specs=plan["in_specs"], out_specs=plan["out_specs"], out_shape=plan["outs"],
        scratch_shapes=plan["sems"], input_output_aliases=_push_aliases(plan, 0, 0),
    )(*plan["ins"])


def _adam_math(g, w, m, v):
    m2 = ADAM_B1 * m + (1.0 - ADAM_B1) * g
    v2 = ADAM_B2 * v + (1.0 - ADAM_B2) * (g * g)
    m_hat = m2 / (1.0 - ADAM_B1 ** ADAM_STEP)
    v_hat = v2 / (1.0 - ADAM_B2 ** ADAM_STEP)
    return -ADAM_LR * (m_hat / (jnp.sqrt(v_hat) + ADAM_EPS) + ADAM_WD * w), m2, v2


def _sum_slots(ref):
    g = ref[0].astype(F32)
    for s in range(1, N_DEV):
        g = g + ref[s].astype(F32)
    return g


def _adam_sharded(name, slots, w, m, v):
    nl, a, b = w.shape
    ta = a
    for cand in range(8, 257, 8):
        if a % cand == 0:
            ta = cand

    def body(s_ref, w_ref, m_ref, v_ref, g_ref, d_ref, m2_ref, v2_ref):
        g = _sum_slots(s_ref)
        d, m2, v2 = _adam_math(g, w_ref[...], m_ref[...], v_ref[...])
        g_ref[...] = g
        d_ref[...] = d
        m2_ref[...] = m2
        v2_ref[...] = v2

    blk = pl.BlockSpec((1, ta, b), lambda l, i: (l, i, 0))
    sds = jax.ShapeDtypeStruct(w.shape, F32)
    return pl.pallas_call(
        body, name=name, grid=(nl, a // ta),
        in_specs=[pl.BlockSpec((N_DEV, 1, ta, b), lambda l, i: (0, l, i, 0)), blk, blk, blk],
        out_specs=[blk] * 4, out_shape=[sds] * 4,
        compiler_params=_cparams(("parallel", "parallel")),
    )(slots, w, m, v)


def _sum_replicated(slots):
    def body(s_ref, g_ref):
        g_ref[...] = _sum_slots(s_ref)

    return pl.pallas_call(body, name="sum_small", out_shape=jax.ShapeDtypeStruct(slots.shape[1:], F32))(slots)


def _adam_small(name, g, w, m, v):
    def body(g_ref, w_ref, m_ref, v_ref, d_ref, m2_ref, v2_ref):
        d, m2, v2 = _adam_math(g_ref[...], w_ref[...], m_ref[...], v_ref[...])
        d_ref[...] = d
        m2_ref[...] = m2
        v2_ref[...] = v2

    sds = jax.ShapeDtypeStruct(w.shape, F32)
    return pl.pallas_call(body, name=name, out_shape=[sds] * 3)(g, w, m, v)


def _cols_full(g):
    return jnp.transpose(g, (1, 0, 2)).reshape(g.shape[1], -1)


def _cols_shards(w):
    k = w.shape[0]
    return jnp.transpose(w.reshape(k, N_DEV, -1), (1, 0, 2))


def _swap_halves(x):
    half = x.shape[-1] // 2
    return jnp.concatenate([x[..., half:], x[..., :half]], axis=-1)


def _zeros_like_cols(x, n):
    return jnp.zeros(x.shape[:-1] + (n,), x.dtype)


def _w_in_internal(w):
    d = w.shape[0]
    kpe = w[:, 640:672]
    z64, z32 = jnp.zeros((d, 64), w.dtype), jnp.zeros((d, 32), w.dtype)
    return jnp.concatenate(
        [w[:, 0:640], z64, kpe, z32, z64, _swap_halves(kpe), z32, jnp.zeros((d, Z_HQ - Z_KPESW - LANE), w.dtype),
         w[:, 672:2720], w[:, 2720:4768]], axis=1)


def _w_in_grad(g):
    kpe = g[:, Z_KPE + 64:Z_KPE + 96] + _swap_halves(g[:, Z_KPESW + 64:Z_KPESW + 96])
    return jnp.concatenate([g[:, 0:640], kpe, g[:, Z_HQ:Z_W]], axis=1)


def _w_uq_internal(w):
    k = w.shape[0]
    w3 = w.reshape(k, MLA_HEADS, QK_NOPE + QK_ROPE)
    nope, rope = w3[..., :QK_NOPE], w3[..., QK_NOPE:]
    plain = jnp.concatenate([nope, rope, _zeros_like_cols(rope, 32)], axis=-1).reshape(k, -1)
    swapped = jnp.concatenate([_zeros_like_cols(nope, 64), _swap_halves(rope), _zeros_like_cols(rope, 32)], axis=-1).reshape(k, -1)
    return jnp.concatenate([plain, swapped], axis=1)


def _w_uq_grad(g):
    k = g.shape[0]
    half = MLA_HEADS * LANE
    g1, g2 = g[:, :half].reshape(k, MLA_HEADS, LANE), g[:, half:].reshape(k, MLA_HEADS, LANE)
    rope = g1[..., 64:96] + _swap_halves(g2[..., 64:96])
    return jnp.concatenate([g1[..., :64], rope], axis=-1).reshape(k, -1)


def _w_ukv_internal(w):
    k = w.shape[0]
    w3 = w.reshape(k, MLA_HEADS, QK_NOPE + V_HEAD)
    kn, vv = w3[..., :QK_NOPE], w3[..., QK_NOPE:]
    z = _zeros_like_cols(kn, 64)
    return jnp.concatenate([kn, z, vv, z], axis=-1).reshape(k, -1)


def _w_ukv_grad(g):
    k = g.shape[0]
    g3 = g.reshape(k, MLA_HEADS, 2 * LANE)
    return jnp.concatenate([g3[..., 0:64], g3[..., LANE:LANE + 64]], axis=-1).reshape(k, -1)


def _w_pa_internal(w):
    n = w.shape[1]
    w3 = w.reshape(MLA_HEADS, V_HEAD, n)
    return jnp.concatenate([w3, jnp.zeros_like(w3)], axis=1).reshape(-1, n)


def _w_pa_grad(g):
    n = g.shape[1]
    return g.reshape(MLA_HEADS, 2 * V_HEAD, n)[:, :V_HEAD].reshape(-1, n)


def _rope_tables(t):
    half = QK_ROPE // 2
    inv = ROPE_THETA ** (-jnp.arange(half, dtype=F32) / half)
    ang = jnp.arange(t, dtype=F32)[:, None] * inv[None, :]
    cos, sin = jnp.cos(ang), jnp.sin(ang)
    one, zero = jnp.ones((t, 64), F32), jnp.zeros((t, 64), F32)
    z32 = jnp.zeros((t, 32), F32)
    cq = jnp.concatenate([one, cos, cos, z32], axis=1)
    ck = jnp.concatenate([zero, cos, cos, z32], axis=1)
    sq = jnp.concatenate([zero, -sin, sin, z32], axis=1)
    return cq, ck, sq


def _mm_pushing(push, *args, **kwargs):
    res = _mm(*args, push=push, **kwargs)
    return res if push is not None else (res, None)


def _residual_mm(name, a, w, h, alpha, next_norm_w, push=None):
    t, d = h.shape
    tm = _row_tile(t)
    rows = _tile_spec(tm, d)
    if next_norm_w is None:
        (h2,), pushed = _mm_pushing(push, name, [a], [(0, w, 0)], [(h,) + rows], [((t, d), F32) + rows],
                                    lambda accs, ex: (ex[0] + alpha * accs[0],), tm=tm, tn=d, n=d)
        return h2, None, pushed

    def fn(accs, ex):
        h2 = ex[0] + alpha * accs[0]
        return h2, _rms_parts(h2)[1] * ex[1]

    (h2, xn2), pushed = _mm_pushing(push, name, [a], [(0, w, 0)],
                                    [(h,) + rows, (next_norm_w, (1, d), lambda i, j: (0, 0))],
                                    [((t, d), F32) + rows, ((t, d), BF16) + rows], fn, tm=tm, tn=d, n=d)
    return h2, xn2, pushed


def _ffn_fwd(tag, h, xn, w_gu, w_down, next_norm_w, pushes=(None, None)):
    t, d = h.shape
    dff = w_down.shape[0]
    tm, tn = _row_tile(t), _tile(dff, 1408)

    def act_fn(accs, _):
        g, u = accs
        return g, u, g * _sigmoid(g) * u

    spec = _tile_spec(tm, tn)
    (g, u, act), pushed0 = _mm_pushing(pushes[0], tag + "_gu", [xn], [(0, w_gu, 0), (0, w_gu, dff // tn)], [],
                                       [((t, dff), BF16) + spec] * 3, act_fn, tm=tm, tn=tn, n=dff)
    h2, xn2, pushed1 = _residual_mm(tag + "_down", act, w_down, h, 0.5, next_norm_w, pushes[1])
    return h2, xn2, (h, xn, g, u, act), (pushed0, pushed1)


def _ffn_bwd(tag, dh2, saved, nw, w_gu_t, w_down_t, pushes=(None, None)):
    h, xn, g, u, act = saved
    t, d = h.shape
    dff = act.shape[1]
    tm, tn = _row_tile(t), _tile(dff, 1408)

    def dact_fn(accs, ex):
        gg, uu = ex[0].astype(F32), ex[1].astype(F32)
        da = 0.5 * accs[0]
        sg = _sigmoid(gg)
        return da * uu * (sg * (1.0 + gg * (1.0 - sg))), da * (gg * sg)

    spec = _tile_spec(tm, tn)
    (dg, du), pushed0 = _mm_pushing(pushes[0], tag + "_dact", [dh2], [(0, w_down_t, 0)], [(g,) + spec, (u,) + spec],
                                    [((t, dff), BF16) + spec] * 2, dact_fn, tm=tm, tn=tn, n=dff)
    dw_down = _mm_tn(tag + "_dwdown", act, dh2, alpha=0.5, out_dtype=BF16)
    tn2 = _tile(d, 512)
    (dxn,), pushed1 = _mm_pushing(pushes[1], tag + "_dxn", [dg, du], [(0, w_gu_t[:dff], 0), (1, w_gu_t[dff:], 0)], [],
                                  [((t, d), F32) + _tile_spec(tm, tn2)], lambda accs, _: (accs[0] + accs[1],),
                                  tm=tm, tn=tn2, n=d)
    dw_gu = jnp.concatenate([_mm_tn(tag + "_dwg", xn, dg, alpha=1.0, out_dtype=BF16),
                             _mm_tn(tag + "_dwu", xn, du, alpha=1.0, out_dtype=BF16)], axis=1)
    dh, dnw = _rms_bwd(tag + "_dnorm", h, nw, dxn, dh2)
    return dh, dnw, dw_gu, dw_down, (pushed0, pushed1)


def _kv_pattern(kr):
    z = jnp.zeros_like(kr)
    return jnp.concatenate([kr, z] * MLA_HEADS, axis=1)


def _mix_fwd(h, u, p, tabs, lb, call, pushes, next_norm_w):
    push, hg_push, merge_push = pushes
    t, d = h.shape
    tm = _row_tile(t)
    tnz = _tile(Z_W, 1024)
    z, = _mm("mix_in", [u], [(0, p["w_in"], 0)], [], [((t, Z_GA), F32) + _tile_spec(tm, tnz)], lambda a, _: (a[0],),
             tm=tm, tn=tnz, n=Z_GA)
    gates, = _mm("mix_gates", [u], [(0, p["w_in"], Z_GA // tnz)], [], [((t, Z_W - Z_GA), BF16) + _tile_spec(tm, tnz)],
                 lambda a, _: (a[0],), tm=tm, tn=tnz, n=Z_W - Z_GA)
    cqn, ckvn, q, kv, v_t = _mla_front(z, p["q_norm"], p["kv_norm"], tabs, p["w_uq"], p["w_ukv"])
    (o_a, lse), pushed = _attn_fwd(q, kv, v_t, push)
    (o_b, st_saved), hg_pushed = _hgrn_fwd(z, lb, p["hg_norm"], call, hg_push)
    tn = _tile(d, 512)

    def merge_fn(accs, ex):
        ya, yb = accs
        return ya, yb, _sigmoid(ex[0].astype(F32)) * ya + _sigmoid(ex[1].astype(F32)) * yb

    spec = _tile_spec(tm, tn)
    (ya, yb, merged), merge_pushed = _mm_pushing(
        merge_push, "mix_merge", [o_a, o_b], [(0, p["w_pa"], 0), (1, p["w_pr"], 0)],
        [(gates,) + _tile_spec(tm, tn, 0), (gates,) + _tile_spec(tm, tn, (Z_GB - Z_GA) // tn)],
        [((t, d), BF16) + spec] * 3, merge_fn, tm=tm, tn=tn, n=d)
    h2, xn2, _ = _residual_mm("mix_out", merged, p["w_out"], h, 1.0, next_norm_w)
    return (h2, xn2, (h, u, z, cqn, ckvn, q, kv, o_a, lse, o_b, st_saved, ya, yb, merged, gates),
            (pushed, hg_pushed, merge_pushed))


def _mix_bwd(dh2, saved, p, tabs, lb, call, call_t, pushes):
    push, dmerge_push, du_push = pushes
    h, u, z, cqn, ckvn, q, kv, o_a, lse, o_b, st_saved, ya, yb, merged, gates = saved
    t, d = h.shape
    tm = _row_tile(t)
    tn = _tile(d, 512)
    spec = _tile_spec(tm, tn)

    def dmerge_fn(accs, ex):
        dm = accs[0]
        yav, ybv = ex[0].astype(F32), ex[1].astype(F32)
        sa, sb = _sigmoid(ex[2].astype(F32)), _sigmoid(ex[3].astype(F32))
        return dm * sa, dm * sb, dm * yav * sa * (1.0 - sa), dm * ybv * sb * (1.0 - sb)

    (dya, dyb, dga, dgb), dmerge_pushed = _mm_pushing(
        dmerge_push, "mix_dmerge", [dh2], [(0, p["w_out_t"], 0)],
        [(ya,) + spec, (yb,) + spec, (gates,) + _tile_spec(tm, tn, 0), (gates,) + _tile_spec(tm, tn, (Z_GB - Z_GA) // tn)],
        [((t, d), BF16) + spec] * 4, dmerge_fn, tm=tm, tn=tn, n=d)
    dw_out = _mm_tn("mix_dwout", merged, dh2, alpha=1.0, out_dtype=BF16)
    wq = MLA_HEADS * LANE
    do_a, = _mm("mix_doa", [dya], [(0, p["w_pa_t"], 0)], [], [((t, wq), BF16) + _tile_spec(tm, wq)], lambda a, _: (a[0],),
                tm=tm, tn=wq, n=wq)
    wr = HG_HEADS * LANE
    do_b, = _mm("mix_dob", [dyb], [(0, p["w_pr_t"], 0)], [], [((t, wr), BF16) + _tile_spec(tm, wr)], lambda a, _: (a[0],),
                tm=tm, tn=wr, n=wr)
    dw_pa = _mm_tn("mix_dwpa", o_a, dya, alpha=1.0, out_dtype=F32)
    dw_pr = _mm_tn("mix_dwpr", o_b, dyb, alpha=1.0, out_dtype=BF16)
    (dq, dkv), pushed = _attn_bwd(q, kv, o_a, do_a, lse, push)
    dz_mla, dw_uq, dw_ukv, dqn, dkvn = _mla_back(z, p["q_norm"], p["kv_norm"], tabs, cqn, ckvn, dq, dkv,
                                                 p["w_uq_t"], p["w_ukv_t"])
    dhq, dhf, dhi, dhg, dlb, dhgn = _hgrn_bwd(z, lb, p["hg_norm"], call, call_t, st_saved, do_b)
    dz = jnp.concatenate([dz_mla, dhq, dhf, dhi, dhg, dga, dgb], axis=1)
    (du,), du_pushed = _mm_pushing(du_push, "mix_du", [dz], [(0, p["w_in_t"], 0)], [], [((t, d), F32) + spec],
                                   lambda a, _: (a[0],), tm=tm, tn=tn, n=d)
    dw_in = _mm_tn("mix_dwin", u, dz, alpha=1.0, out_dtype=F32)
    dh, dmn = _rms_bwd("mix_dnorm", h, p["mix_norm"], du, dh2)
    grads = dict(mix_norm=dmn, q_norm=dqn, kv_norm=dkvn, hg_norm=dhgn, lb=dlb, w_in=_w_in_grad(dw_in), w_uq=_w_uq_grad(dw_uq),
                 w_ukv=_w_ukv_grad(dw_ukv), w_proj_attn=_w_pa_grad(dw_pa), w_proj_rec=dw_pr, w_out=dw_out)
    return dh, grads, (pushed, dmerge_pushed, du_pushed)


SHARDED = ("ffn1_w_gu", "ffn1_w_down", "w_in", "w_uq", "w_ukv", "w_proj_attn", "w_proj_rec", "w_out", "ffn2_w_gu", "ffn2_w_down")
ROW_SHARDED = ("ffn1_w_down", "w_out", "ffn2_w_down")
FFN1_W = ("ffn1_w_gu", "ffn1_w_down")
GATHER_BEHIND = (("w_in", "w_uq", "w_ukv"), ("w_proj_attn", "w_proj_rec", "w_out"), ("ffn2_w_gu",), ("ffn2_w_down",))
SCATTER_BEHIND = (("ffn2_w_gu",), ("ffn2_w_down", "w_in"))
SMALL = ("ffn1_norm", "mix_norm", "q_norm", "kv_norm", "hg_lb_raw", "hg_norm", "ffn2_norm", "final_norm")
WEIGHTS = ("meta_tokens", "ffn1_norm", "ffn1_w_gu", "ffn1_w_down", "mix_norm", "w_in", "q_norm", "kv_norm", "w_uq", "w_ukv",
           "hg_lb_raw", "hg_norm", "w_proj_attn", "w_proj_rec", "w_out", "ffn2_norm", "ffn2_w_gu", "ffn2_w_down", "final_norm")


def _pack_small(vals):
    flat = jnp.concatenate([vals[n].reshape(-1) for n in SMALL])
    return flat.reshape(-1, LANE)


def _unpack_small(packed, like):
    flat = packed.reshape(-1)
    out, off = {}, 0
    for n in SMALL:
        size = math.prod(like[n].shape)
        out[n] = flat[off:off + size].reshape(like[n].shape)
        off += size
    return out


def kernel(x, meta_tokens, ffn1_norm, ffn1_w_gu, ffn1_w_down, mix_norm, w_in, q_norm, kv_norm, w_uq, w_ukv, hg_lb_raw, hg_norm, w_proj_attn, w_proj_rec, w_out, ffn2_norm, ffn2_w_gu, ffn2_w_down, final_norm, loss_target, m_meta_tokens, m_ffn1_norm, m_ffn1_w_gu, m_ffn1_w_down, m_mix_norm, m_w_in, m_q_norm, m_kv_norm, m_w_uq, m_w_ukv, m_hg_lb_raw, m_hg_norm, m_w_proj_attn, m_w_proj_rec, m_w_out, m_ffn2_norm, m_ffn2_w_gu, m_ffn2_w_down, m_final_norm, v_meta_tokens, v_ffn1_norm, v_ffn1_w_gu, v_ffn1_w_down, v_mix_norm, v_w_in, v_q_norm, v_kv_norm, v_w_uq, v_w_ukv, v_hg_lb_raw, v_hg_norm, v_w_proj_attn, v_w_proj_rec, v_w_out, v_ffn2_norm, v_ffn2_w_gu, v_ffn2_w_down, v_final_norm):
    w = dict(meta_tokens=meta_tokens, ffn1_norm=ffn1_norm, ffn1_w_gu=ffn1_w_gu, ffn1_w_down=ffn1_w_down, mix_norm=mix_norm,
             w_in=w_in, q_norm=q_norm, kv_norm=kv_norm, w_uq=w_uq, w_ukv=w_ukv, hg_lb_raw=hg_lb_raw, hg_norm=hg_norm,
             w_proj_attn=w_proj_attn, w_proj_rec=w_proj_rec, w_out=w_out, ffn2_norm=ffn2_norm, ffn2_w_gu=ffn2_w_gu,
             ffn2_w_down=ffn2_w_down, final_norm=final_norm)
    mom = dict(meta_tokens=m_meta_tokens, ffn1_norm=m_ffn1_norm, ffn1_w_gu=m_ffn1_w_gu, ffn1_w_down=m_ffn1_w_down,
               mix_norm=m_mix_norm, w_in=m_w_in, q_norm=m_q_norm, kv_norm=m_kv_norm, w_uq=m_w_uq, w_ukv=m_w_ukv,
               hg_lb_raw=m_hg_lb_raw, hg_norm=m_hg_norm, w_proj_attn=m_w_proj_attn, w_proj_rec=m_w_proj_rec, w_out=m_w_out,
               ffn2_norm=m_ffn2_norm, ffn2_w_gu=m_ffn2_w_gu, ffn2_w_down=m_ffn2_w_down, final_norm=m_final_norm)
    var = dict(meta_tokens=v_meta_tokens, ffn1_norm=v_ffn1_norm, ffn1_w_gu=v_ffn1_w_gu, ffn1_w_down=v_ffn1_w_down,
               mix_norm=v_mix_norm, w_in=v_w_in, q_norm=v_q_norm, kv_norm=v_kv_norm, w_uq=v_w_uq, w_ukv=v_w_ukv,
               hg_lb_raw=v_hg_lb_raw, hg_norm=v_hg_norm, w_proj_attn=v_w_proj_attn, w_proj_rec=v_w_proj_rec, w_out=v_w_out,
               ffn2_norm=v_ffn2_norm, ffn2_w_gu=v_ffn2_w_gu, ffn2_w_down=v_ffn2_w_down, final_norm=v_final_norm)
    nl = ffn1_norm.shape[0]
    seq, d = x.shape[1], x.shape[2]
    n_real = N_META + seq
    t = -(-n_real // ROW_ALIGN) * ROW_ALIGN
    me = 4 * lax.axis_index("x") + 2 * lax.axis_index("y") + lax.axis_index("c")

    def own_shards(l, names):
        return [w[n][l].astype(BF16) for n in names]

    gathered = _exchange("gather_weights", own_shards(0, FFN1_W) + [meta_tokens], gather=True)
    meta_full = _cols_full(gathered[-1])

    def mat(full, n):
        g = full[n]
        return g.reshape(-1, g.shape[-1]) if n in ROW_SHARDED else _cols_full(g)

    def ffn_params(p, tag, l, full):
        p[tag + "_w_gu"] = mat(full, tag + "_w_gu")
        p[tag + "_w_down"] = mat(full, tag + "_w_down")
        p[tag + "_w_gu_t"] = p[tag + "_w_gu"].T
        p[tag + "_w_down_t"] = p[tag + "_w_down"].T
        p[tag + "_norm"] = w[tag + "_norm"][l:l + 1]

    def mix_params(p, l, full):
        p["w_in"] = _w_in_internal(mat(full, "w_in"))
        p["w_uq"] = _w_uq_internal(mat(full, "w_uq"))
        p["w_ukv"] = _w_ukv_internal(mat(full, "w_ukv"))
        p["w_pa"] = _w_pa_internal(mat(full, "w_proj_attn"))
        p["w_pr"] = mat(full, "w_proj_rec")
        p["w_out"] = mat(full, "w_out")
        for n in ("w_in", "w_uq", "w_ukv", "w_pa", "w_pr", "w_out"):
            p[n + "_t"] = p[n].T
        for n in ("mix_norm", "q_norm", "kv_norm", "hg_norm"):
            p[n] = w[n][l:l + 1]

    tabs = _rope_tables(t)
    call, call_t = _hg_tables()
    lbs = _lb_fwd(hg_lb_raw)

    pad = jnp.zeros((t - n_real, d), F32)
    h = jnp.concatenate([meta_full, x[0], pad], axis=0)
    tgt = jnp.concatenate([jnp.zeros((N_META, d), F32), loss_target[0], pad], axis=0)
    saved, params = [], []
    full = dict(zip(FFN1_W, gathered[:-1]))
    xn = _rms_fwd("first_norm", h, ffn1_norm[0:1])
    for l in range(nl):
        p = {}
        params.append(p)
        ffn_params(p, "ffn1", l, full)
        behind = [_push_plan(own_shards(0, names), gather=True) if l == 0 else None for names in GATHER_BEHIND]
        h, xn, s1, got = _ffn_fwd("ffn1", h, xn, p["ffn1_w_gu"], p["ffn1_w_down"], mix_norm[l:l + 1], behind[0:2])
        if l == 0:
            full.update(zip(GATHER_BEHIND[0], got[0]))
            full.update(zip(GATHER_BEHIND[1], got[1]))
        mix_params(p, l, full)
        push = _push_plan(own_shards(l + 1, SHARDED), gather=True) if l + 1 < nl else None
        h, xn, s2, got = _mix_fwd(h, xn, p, tabs, lbs[l:l + 1], call, (push, behind[2], behind[3]), ffn2_norm[l:l + 1])
        if l == 0:
            full.update(zip(GATHER_BEHIND[2], got[1]))
            full.update(zip(GATHER_BEHIND[3], got[2]))
        ffn_params(p, "ffn2", l, full)
        h, xn, s3, _ = _ffn_fwd("ffn2", h, xn, p["ffn2_w_gu"], p["ffn2_w_down"],
                                ffn1_norm[l + 1:l + 2] if l + 1 < nl else None)
        if push is not None:
            full = dict(zip(SHARDED, got[0]))
        saved.append((s1, s2, s3))
    dh, d_final, loss_part = _loss_head(h, final_norm.reshape(1, d), tgt, n_real)
    loss = lax.psum(loss_part[0, 0], ("x", "y", "c"))

    def shard(n, g):
        g = g.astype(BF16)
        return g.reshape(N_DEV, -1, g.shape[-1]) if n in ROW_SHARDED else _cols_shards(g)

    def scatter_plan(names, gm, layer):
        idx = [SHARDED.index(n) for n in names]
        return idx, _push_plan([shard(n, gm[n]) for n in names], gather=False, bufs=[slots[i] for i in idx], layer=layer)

    def landed(idx, pushed):
        for i, s in zip(idx, pushed):
            slots[i] = s

    per_layer = []
    slots = [jnp.zeros((N_DEV,) + w[n].shape, BF16) for n in SHARDED]
    above = None
    for l in reversed(range(nl)):
        p = params[l]
        s1, s2, s3 = saved[l]
        dh, dn2, dgu2, ddown2, _ = _ffn_bwd("ffn2", dh, s3, p["ffn2_norm"], p["ffn2_w_gu_t"], p["ffn2_w_down_t"])
        gm = dict(ffn2_norm=dn2, ffn2_w_gu=dgu2, ffn2_w_down=ddown2)
        idx, push = (None, None) if above is None else scatter_plan(SHARDED, above, l + 1)
        dh, gmix, got = _mix_bwd(dh, s2, p, tabs, lbs[l:l + 1], call, call_t, (push, None, None))
        if push is not None:
            landed(idx, got[0])
        gm.update(gmix)
        behind = [scatter_plan(names, gm, 0) if l == 0 else (None, None) for names in SCATTER_BEHIND]
        dh, dn1, dgu1, ddown1, got = _ffn_bwd("ffn1", dh, s1, p["ffn1_norm"], p["ffn1_w_gu_t"], p["ffn1_w_down_t"],
                                              [b[1] for b in behind])
        if l == 0:
            landed(behind[0][0], got[0])
            landed(behind[1][0], got[1])
        gm.update(ffn1_norm=dn1, ffn1_w_gu=dgu1, ffn1_w_down=ddown1)
        per_layer.append(gm)
        above = gm
    per_layer.reverse()
    grad_x = dh[N_META:n_real][None]

    last = [n for n in SHARDED if n not in sum(SCATTER_BEHIND, ())]
    idx = [SHARDED.index(n) for n in last]
    landed(idx, _exchange("scatter_grads", [shard(n, above[n]) for n in last], gather=False,
                          bufs=[slots[i] for i in idx], layer=0))
    grads, delta, new_m, new_v = {}, {}, {}, {}
    for n, s in zip(SHARDED, slots):
        grads[n], delta[n], new_m[n], new_v[n] = _adam_sharded("adam_" + n, s, w[n], mom[n], var[n])

    small = {n: jnp.concatenate([gm[n] for gm in per_layer], axis=0) for n in SMALL if n not in ("hg_lb_raw", "final_norm")}
    small["hg_lb_raw"] = _lb_bwd(hg_lb_raw, jnp.concatenate([gm["lb"] for gm in per_layer], axis=0))
    small["final_norm"] = d_final
    packed = jnp.concatenate([_pack_small(small), dh[:N_META].reshape(-1, LANE)], axis=0)
    summed = _sum_replicated(_exchange("gather_small", [packed], gather=True)[0])
    n_small = packed.shape[0] - N_META * d // LANE
    sd, sm, sv = _adam_small("adam_small", summed[:n_small], _pack_small(w), _pack_small(mom), _pack_small(var))
    grads.update(_unpack_small(summed[:n_small], w))
    delta.update(_unpack_small(sd, w))
    new_m.update(_unpack_small(sm, w))
    new_v.update(_unpack_small(sv, w))
    dmeta = lax.dynamic_slice_in_dim(summed[n_small:].reshape(N_META, d), me * (d // N_DEV), d // N_DEV, axis=1)
    grads["meta_tokens"] = dmeta
    delta["meta_tokens"], new_m["meta_tokens"], new_v["meta_tokens"] = _adam_small(
        "adam_meta", dmeta, meta_tokens, m_meta_tokens, v_meta_tokens)

    return (loss, grad_x, *[grads[n] for n in WEIGHTS], *[delta[n] for n in WEIGHTS], *[new_m[n] for n in WEIGHTS],
            *[new_v[n] for n in WEIGHTS])
```
